```python
import math
import jax, jax.numpy as jnp
from jax import lax
import numpy as np

D_MODEL = 1024
BATCH = 8
SEQ = 2048
DEPTH = 2

N_META = 16
Q_BLOCK = 128
EPS = 1e-6

SSM_WIDTH = 256
SSM_GROUP = 16
SSM_GROUPS = SSM_WIDTH // SSM_GROUP
SSM_STATE = 64
DIFF_HEADS = 4
DIFF_QK_DIM = 64
DIFF_V_DIM = 2 * DIFF_QK_DIM
DIFF_WIDTH = DIFF_HEADS * DIFF_V_DIM
FOX_HEADS = 4
FOX_HEAD_DIM = 64
FOX_WIDTH = FOX_HEADS * FOX_HEAD_DIM
MIX_WIDTH = SSM_WIDTH + DIFF_WIDTH + FOX_WIDTH
IN_SPLIT = (SSM_WIDTH, DIFF_WIDTH, DIFF_WIDTH, DIFF_WIDTH,
            FOX_WIDTH, FOX_WIDTH, FOX_WIDTH, FOX_HEADS)
IN_WIDTH = SSM_WIDTH + 3 * DIFF_WIDTH + 3 * FOX_WIDTH + FOX_HEADS
D_FF_DENSE = 2816
N_EXPERTS = 8
TOP_K = 2
D_FF_EXPERT = 3584
N_DENSE = (DEPTH + 1) // 2
N_MOE = DEPTH // 2

kernel_name = "hymba_s5_diffattn_fox_moe_trunk"


def _rmsnorm(x, g):
    xf = x.astype(jnp.float32)
    y = xf * lax.rsqrt(jnp.mean(xf * xf, axis=-1, keepdims=True) + EPS)
    return (y * g.astype(jnp.float32)).astype(x.dtype)


def _heads(t, n_heads, *tail):
    return jnp.moveaxis(t.reshape(t.shape[:2] + (n_heads,) + tail), 2, 1)


def _merge_heads(o):
    b, h, l, d = o.shape
    return jnp.moveaxis(o, 1, 2).reshape(b, l, h * d)


def _block_sweep(block_fn, q_args, n_meta):
    L = q_args[0].shape[2]
    pos = jnp.arange(L, dtype=jnp.int32)
    meta_out = block_fn(*[a[:, :, :n_meta] for a in q_args], pos[:n_meta])
    n_blk = (L - n_meta) // Q_BLOCK

    def to_blocks(a):
        r = a[:, :, n_meta:]
        r = r.reshape(r.shape[:2] + (n_blk, Q_BLOCK) + r.shape[3:])
        return jnp.moveaxis(r, 2, 0)

    blocks = tuple(to_blocks(a) for a in q_args) + (pos[n_meta:].reshape(n_blk, Q_BLOCK),)
    real = lax.map(lambda args: block_fn(*args), blocks)
    real = jnp.moveaxis(real, 0, 2)
    real = real.reshape(real.shape[:2] + (n_blk * Q_BLOCK,) + real.shape[4:])
    return jnp.concatenate([meta_out, real], axis=2)


def _complex_affine_combine(e1, e2):
    ar1, ai1, br1, bi1 = e1
    ar2, ai2, br2, bi2 = e2
    ar = ar2 * ar1 - ai2 * ai1
    ai = ar2 * ai1 + ai2 * ar1
    br = ar2 * br1 - ai2 * bi1 + br2
    bi = ar2 * bi1 + ai2 * br1 + bi2
    return (ar, ai, br, bi)


def _s5(u, lam_re, lam_im, log_dt, b_re, b_im, c_re, c_im, d_skip, w_glu):
    bsz, L, _ = u.shape
    f32 = jnp.float32
    uf = u.astype(f32).reshape(bsz, L, SSM_GROUPS, SSM_GROUP)
    dt = jnp.exp(log_dt.astype(f32))[:, None]
    lr = lam_re.astype(f32)
    li = lam_im.astype(f32)
    mag = jnp.exp(lr * dt)
    ab_re = mag * jnp.cos(li * dt)
    ab_im = mag * jnp.sin(li * dt)
    den = lr * lr + li * li
    nr = ab_re - 1.0
    ni = ab_im
    coef_re = (nr * lr + ni * li) / den
    coef_im = (ni * lr - nr * li) / den
    bu_r = jnp.einsum('blgc,gnc->blgn', uf, b_re.astype(f32))
    bu_i = jnp.einsum('blgc,gnc->blgn', uf, b_im.astype(f32))
    bu_re = coef_re * bu_r - coef_im * bu_i
    bu_im = coef_re * bu_i + coef_im * bu_r
    shape = bu_re.shape
    a_re = jnp.broadcast_to(ab_re, shape)
    a_im = jnp.broadcast_to(ab_im, shape)
    _, _, x_re, x_im = lax.associative_scan(_complex_affine_combine,
                                            (a_re, a_im, bu_re, bu_im), axis=1)
    y = (jnp.einsum('blgn,gcn->blgc', x_re, c_re.astype(f32))
         - jnp.einsum('blgn,gcn->blgc', x_im, c_im.astype(f32))
         + d_skip.astype(f32) * uf)
    y = jax.nn.gelu(y.reshape(bsz, L, SSM_WIDTH))
    g = y @ w_glu.astype(f32)
    out = g[..., :SSM_WIDTH] * jax.nn.sigmoid(g[..., SSM_WIDTH:])
    return out.astype(u.dtype)


def _diff_attention(q, k, v, lam, lam_init, subln_g):
    f32 = jnp.float32
    L = k.shape[2]
    k_pos = jnp.arange(L, dtype=jnp.int32)
    scale = DIFF_QK_DIM ** -0.5
    vf = v.astype(f32)

    def block(qb, q_pos):
        s = jnp.einsum('bhqmd,bhkmd->bhmqk', qb, k, preferred_element_type=f32) * scale
        mask = k_pos[None, :] <= q_pos[:, None]
        p = jax.nn.softmax(jnp.where(mask, s, -jnp.inf), axis=-1)
        a = p[:, :, 0] - lam * p[:, :, 1]
        return jnp.einsum('bhqk,bhkd->bhqd', a, vf)

    o = _block_sweep(block, (q,), N_META)
    o = _rmsnorm(o, subln_g) * (1.0 - lam_init)
    return _merge_heads(o)


def _forgetting_attention(q, k, v, log_f):
    f32 = jnp.float32
    L = k.shape[2]
    k_pos = jnp.arange(L, dtype=jnp.int32)
    scale = FOX_HEAD_DIM ** -0.5
    cum = jnp.cumsum(log_f, axis=-1)
    vf = v.astype(f32)

    def block(qb, cq, q_pos):
        s = jnp.einsum('bhqd,bhkd->bhqk', qb, k, preferred_element_type=f32) * scale
        s = s + cq[..., :, None] - cum[:, :, None, :]
        mask = k_pos[None, :] <= q_pos[:, None]
        p = jax.nn.softmax(jnp.where(mask, s, -jnp.inf), axis=-1)
        return jnp.einsum('bhqk,bhkd->bhqd', p, vf)

    o = _block_sweep(block, (q, cum), N_META)
    return _merge_heads(o)


def _swiglu(h, w1, w3, w2):
    return (jax.nn.silu(h @ w1) * (h @ w3)) @ w2


def _moe(h, w_router, w1, w3, w2):
    f32 = jnp.float32
    logits = (h @ w_router).astype(f32)
    top_v, top_i = lax.top_k(logits, TOP_K)
    gw = jax.nn.softmax(top_v, axis=-1)
    gate = jnp.sum(jax.nn.one_hot(top_i, N_EXPERTS, dtype=f32) * gw[..., None], axis=-2)
    y = jnp.zeros(h.shape, f32)
    for e in range(N_EXPERTS):
        y = y + gate[..., e:e + 1] * _swiglu(h, w1[e], w3[e], w2[e]).astype(f32)
    return y.astype(h.dtype)


def setup_inputs(seed: int = 0) -> dict:
    key = jax.random.key(seed)
    ks = jax.random.split(key, 32)
    f32 = jnp.float32
    nrm = lambda k, shape, s: jax.random.normal(k, shape, f32) * s
    n_idx = jnp.arange(SSM_STATE, dtype=f32)
    lam_re = -0.5 + nrm(ks[5], (DEPTH, SSM_GROUPS, SSM_STATE), 0.01)
    lam_im = math.pi * n_idx + nrm(ks[6], (DEPTH, SSM_GROUPS, SSM_STATE), 0.01)
    return {
        "x": nrm(ks[0], (BATCH, SEQ, D_MODEL), 1.0),
        "meta_tokens": nrm(ks[1], (N_META, D_MODEL), 1.0),
        "norm_mix_g": 1.0 + nrm(ks[2], (DEPTH, D_MODEL), 0.02),
        "w_in": nrm(ks[3], (DEPTH, D_MODEL, IN_WIDTH), D_MODEL ** -0.5),
        "w_out": nrm(ks[4], (DEPTH, MIX_WIDTH, D_MODEL), MIX_WIDTH ** -0.5),
        "ssm_lambda_re": lam_re,
        "ssm_lambda_im": lam_im,
        "ssm_log_dt": jax.random.uniform(ks[7], (DEPTH, SSM_GROUPS), f32,
                                         math.log(1e-3), math.log(1e-1)),
        "ssm_b_re": nrm(ks[8], (DEPTH, SSM_GROUPS, SSM_STATE, SSM_GROUP), (2 * SSM_GROUP) ** -0.5),
        "ssm_b_im": nrm(ks[9], (DEPTH, SSM_GROUPS, SSM_STATE, SSM_GROUP), (2 * SSM_GROUP) ** -0.5),
        "ssm_c_re": nrm(ks[10], (DEPTH, SSM_GROUPS, SSM_GROUP, SSM_STATE), (2 * SSM_STATE) ** -0.5),
        "ssm_c_im": nrm(ks[11], (DEPTH, SSM_GROUPS, SSM_GROUP, SSM_STATE), (2 * SSM_STATE) ** -0.5),
        "ssm_d": nrm(ks[12], (DEPTH, SSM_GROUPS, SSM_GROUP), 1.0),
        "ssm_w_glu": nrm(ks[13], (DEPTH, SSM_WIDTH, 2 * SSM_WIDTH), SSM_WIDTH ** -0.5),
        "diff_lambda_q1": nrm(ks[14], (DEPTH, DIFF_QK_DIM), 0.1),
        "diff_lambda_k1": nrm(ks[15], (DEPTH, DIFF_QK_DIM), 0.1),
        "diff_lambda_q2": nrm(ks[16], (DEPTH, DIFF_QK_DIM), 0.1),
        "diff_lambda_k2": nrm(ks[17], (DEPTH, DIFF_QK_DIM), 0.1),
        "diff_subln_g": 1.0 + nrm(ks[18], (DEPTH, DIFF_V_DIM), 0.02),
        "fox_forget_b": jax.random.uniform(ks[19], (DEPTH, FOX_HEADS), f32, 2.0, 5.0),
        "norm_ffn_g": 1.0 + nrm(ks[20], (DEPTH, D_MODEL), 0.02),
        "dense_w1": nrm(ks[21], (N_DENSE, D_MODEL, D_FF_DENSE), D_MODEL ** -0.5),
        "dense_w3": nrm(ks[22], (N_DENSE, D_MODEL, D_FF_DENSE), D_MODEL ** -0.5),
        "dense_w2": nrm(ks[23], (N_DENSE, D_FF_DENSE, D_MODEL), D_FF_DENSE ** -0.5),
        "moe_router": nrm(ks[24], (N_MOE, D_MODEL, N_EXPERTS), D_MODEL ** -0.5),
        "moe_w1": nrm(ks[25], (N_MOE, N_EXPERTS, D_MODEL, D_FF_EXPERT), D_MODEL ** -0.5),
        "moe_w3": nrm(ks[26], (N_MOE, N_EXPERTS, D_MODEL, D_FF_EXPERT), D_MODEL ** -0.5),
        "moe_w2": nrm(ks[27], (N_MOE, N_EXPERTS, D_FF_EXPERT, D_MODEL), D_FF_EXPERT ** -0.5),
        "final_norm_g": 1.0 + nrm(ks[28], (D_MODEL,), 0.02),
    }


def reference(x, meta_tokens, norm_mix_g, w_in, w_out, ssm_lambda_re, ssm_lambda_im,
              ssm_log_dt, ssm_b_re, ssm_b_im, ssm_c_re, ssm_c_im, ssm_d, ssm_w_glu,
              diff_lambda_q1, diff_lambda_k1, diff_lambda_q2, diff_lambda_k2, diff_subln_g,
              fox_forget_b, norm_ffn_g, dense_w1, dense_w3, dense_w2,
              moe_router, moe_w1, moe_w3, moe_w2, final_norm_g):
    f32 = jnp.float32
    bsz = x.shape[0]
    meta = jnp.broadcast_to(meta_tokens[None].astype(x.dtype), (bsz, N_META, D_MODEL))
    res = jnp.concatenate([meta, x], axis=1)
    offs = np.cumsum(np.array(IN_SPLIT))[:-1].tolist()
    for l in range(DEPTH):
        h = _rmsnorm(res, norm_mix_g[l])
        z = h @ w_in[l]
        u, dq, dk, dv, fq, fk, fv, fg = jnp.split(z, offs, axis=-1)

        ssm_out = _s5(u, ssm_lambda_re[l], ssm_lambda_im[l], ssm_log_dt[l],
                      ssm_b_re[l], ssm_b_im[l], ssm_c_re[l], ssm_c_im[l],
                      ssm_d[l], ssm_w_glu[l])

        lam_init = 0.8 - 0.6 * math.exp(-0.3 * l)
        lam = (jnp.exp(jnp.sum(diff_lambda_q1[l].astype(f32) * diff_lambda_k1[l].astype(f32)))
               - jnp.exp(jnp.sum(diff_lambda_q2[l].astype(f32) * diff_lambda_k2[l].astype(f32)))
               + lam_init)
        diff_out = _diff_attention(_heads(dq, DIFF_HEADS, 2, DIFF_QK_DIM),
                                   _heads(dk, DIFF_HEADS, 2, DIFF_QK_DIM),
                                   _heads(dv, DIFF_HEADS, DIFF_V_DIM),
                                   lam, lam_init, diff_subln_g[l])

        log_f = jax.nn.log_sigmoid(fg.astype(f32) + fox_forget_b[l].astype(f32))
        fox_out = _forgetting_attention(_heads(fq, FOX_HEADS, FOX_HEAD_DIM),
                                        _heads(fk, FOX_HEADS, FOX_HEAD_DIM),
                                        _heads(fv, FOX_HEADS, FOX_HEAD_DIM),
                                        jnp.moveaxis(log_f, 2, 1))

        mixed = jnp.concatenate([ssm_out.astype(res.dtype), diff_out.astype(res.dtype),
                                 fox_out.astype(res.dtype)], axis=-1)
        res = res + mixed @ w_out[l]

        h = _rmsnorm(res, norm_ffn_g[l])
        if l % 2 == 0:
            ffn = _swiglu(h, dense_w1[l // 2], dense_w3[l // 2], dense_w2[l // 2])
        else:
            ffn = _moe(h, moe_router[l // 2], moe_w1[l // 2], moe_w3[l // 2], moe_w2[l // 2])
        res = res + ffn
    return _rmsnorm(res, final_norm_g)[:, N_META:]
```

```python
import functools
import math

import jax
import jax.numpy as jnp
from jax import lax
from jax.experimental import pallas as pl
from jax.experimental.pallas import tpu as pltpu

F32 = jnp.float32
BF16 = jnp.bfloat16
EPS = 1e-6

D_MODEL = 1024
N_META = 16
NB = 8
SSM_WIDTH = 256
SSM_GROUPS = 16
SSM_GROUP = 16
SSM_STATE = 64
SSM_COLS = 2 * SSM_GROUPS * SSM_STATE
DIFF_HEADS = 4
FOX_HEADS = 4
ATT_COLS = 2304
IN_PAD = 2688
N_EXPERTS = 8
LANES = 128
MASK_VALUE = -1e30

TOK_TILE = 688
FFN_TILE = 384
ATT_TQ = 256
ATT_TK = 256
S5_CHUNK = 128
MOE_TILE = 1024
MOE_FCHUNK = 512
GATHER_TILE = 256
COMBINE_TILE = 384

VMEM_LIMIT = 56 * 1024 * 1024


def _cp(sem):
    return pltpu.CompilerParams(dimension_semantics=sem, vmem_limit_bytes=VMEM_LIMIT)


def _rms(x, g):
    return x * lax.rsqrt(jnp.mean(x * x, axis=-1, keepdims=True) + EPS) * g


def _inproj_kernel(res_ref, g_ref, w_ref, zu_ref, za_ref, zg_ref):
    h = _rms(res_ref[...], g_ref[...]).astype(BF16)
    zu_ref[...] = jnp.dot(h, w_ref[:, 0:SSM_WIDTH], preferred_element_type=F32)
    for c in range(0, ATT_COLS, 256):
        za_ref[:, c:c + 256] = jnp.dot(
            h, w_ref[:, SSM_WIDTH + c:SSM_WIDTH + c + 256],
            preferred_element_type=F32).astype(BF16)
    zg_ref[...] = jnp.dot(h, w_ref[:, SSM_WIDTH + ATT_COLS:IN_PAD],
                          preferred_element_type=F32)


def _inproj(res, g, w):
    t = res.shape[0]
    return pl.pallas_call(
        _inproj_kernel,
        grid=(t // TOK_TILE,),
        in_specs=[pl.BlockSpec((TOK_TILE, D_MODEL), lambda i: (i, 0)),
                  pl.BlockSpec((1, D_MODEL), lambda i: (0, 0)),
                  pl.BlockSpec((D_MODEL, IN_PAD), lambda i: (0, 0))],
        out_specs=[pl.BlockSpec((TOK_TILE, SSM_WIDTH), lambda i: (i, 0)),
                   pl.BlockSpec((TOK_TILE, ATT_COLS), lambda i: (i, 0)),
                   pl.BlockSpec((TOK_TILE, LANES), lambda i: (i, 0))],
        out_shape=[jax.ShapeDtypeStruct((t, SSM_WIDTH), F32),
                   jax.ShapeDtypeStruct((t, ATT_COLS), BF16),
                   jax.ShapeDtypeStruct((t, LANES), F32)],
        compiler_params=_cp(("parallel",)),
        name="inproj",
    )(res, g, w)


def _outproj_kernel(res_ref, s_ref, d_ref, f_ref, w_ref, o_ref):
    mixed = jnp.concatenate([s_ref[...], d_ref[...], f_ref[...]], axis=1)
    o_ref[...] = res_ref[...] + jnp.dot(mixed, w_ref[...], preferred_element_type=F32)


def _outproj(res, ssm, diff, fox, w):
    t = res.shape[0]
    return pl.pallas_call(
        _outproj_kernel,
        grid=(t // TOK_TILE,),
        in_specs=[pl.BlockSpec((TOK_TILE, D_MODEL), lambda i: (i, 0)),
                  pl.BlockSpec((TOK_TILE, 256), lambda i: (i, 0)),
                  pl.BlockSpec((TOK_TILE, 512), lambda i: (i, 0)),
                  pl.BlockSpec((TOK_TILE, 256), lambda i: (i, 0)),
                  pl.BlockSpec((D_MODEL, D_MODEL), lambda i: (0, 0))],
        out_specs=pl.BlockSpec((TOK_TILE, D_MODEL), lambda i: (i, 0)),
        out_shape=jax.ShapeDtypeStruct((t, D_MODEL), F32),
        compiler_params=_cp(("parallel",)),
        name="outproj",
    )(res, ssm, diff, fox, w)


def _dense_ffn_kernel(res_ref, g_ref, w1_ref, w3_ref, w2_ref, o_ref):
    x = res_ref[...]
    h = _rms(x, g_ref[...]).astype(BF16)
    a = jnp.dot(h, w1_ref[...], preferred_element_type=F32)
    b = jnp.dot(h, w3_ref[...], preferred_element_type=F32)
    hh = (a * jax.nn.sigmoid(a) * b).astype(BF16)
    o_ref[...] = x + jnp.dot(hh, w2_ref[...], preferred_element_type=F32)


def _dense_ffn(res, g, w1, w3, w2):
    t = res.shape[0]
    dff = w1.shape[1]
    once = pl.Buffered(1)
    return pl.pallas_call(
        _dense_ffn_kernel,
        grid=(t // FFN_TILE,),
        in_specs=[pl.BlockSpec((FFN_TILE, D_MODEL), lambda i: (i, 0)),
                  pl.BlockSpec((1, D_MODEL), lambda i: (0, 0)),
                  pl.BlockSpec((D_MODEL, dff), lambda i: (0, 0), pipeline_mode=once),
                  pl.BlockSpec((D_MODEL, dff), lambda i: (0, 0), pipeline_mode=once),
                  pl.BlockSpec((dff, D_MODEL), lambda i: (0, 0), pipeline_mode=once)],
        out_specs=pl.BlockSpec((FFN_TILE, D_MODEL), lambda i: (i, 0)),
        out_shape=jax.ShapeDtypeStruct((t, D_MODEL), F32),
        compiler_params=_cp(("parallel",)),
        name="dense_ffn",
    )(res, g, w1, w3, w2)


def _gelu_tanh(x):
    c = math.sqrt(2.0 / math.pi)
    return 0.5 * x * (1.0 + jnp.tanh(c * (x + 0.044715 * (x * x * x))))


def _s5_chunk(lc, get_u, state_ref, ut_ref, bu_ref, ot_ref,
              a_ref, bd_ref, cd_ref, dskip_ref, wglu_ref):
    for b in range(NB):
        ub = get_u(b)
        for s in range(2):
            ut_ref[s, pl.ds(b, lc, stride=NB), :] = ub[:, LANES * s:LANES * (s + 1)]
    u_tm = jnp.concatenate([ut_ref[0], ut_ref[1]], axis=1)
    bu_ref[...] = jnp.dot(u_tm.astype(BF16), bd_ref[...], preferred_element_type=F32)

    half = SSM_COLS // 2
    a_re = a_ref[:, :half]
    a_im = a_ref[:, half:]

    def step(t, x):
        x_re, x_im = x
        r = pl.multiple_of(t * NB, NB)
        cur = bu_ref[pl.ds(r, NB), :]
        n_re = a_re * x_re - a_im * x_im + cur[:, :half]
        n_im = a_re * x_im + a_im * x_re + cur[:, half:]
        bu_ref[pl.ds(r, NB), :] = jnp.concatenate([n_re, n_im], axis=1)
        return n_re, n_im

    x_re, x_im = lax.fori_loop(0, lc, step,
                               (state_ref[:, :half], state_ref[:, half:]), unroll=4)
    state_ref[...] = jnp.concatenate([x_re, x_im], axis=1)

    y = jnp.dot(bu_ref[...].astype(BF16), cd_ref[...], preferred_element_type=F32)
    y = _gelu_tanh(y + dskip_ref[...] * u_tm)
    g = jnp.dot(y.astype(BF16), wglu_ref[...], preferred_element_type=F32)
    o = g[:, :SSM_WIDTH] * jax.nn.sigmoid(g[:, SSM_WIDTH:])
    ot_ref[0] = o[:, :LANES]
    ot_ref[1] = o[:, LANES:]


def _s5_read_out(ot_ref, b, lc):
    return jnp.concatenate(
        [ot_ref[s, pl.ds(b, lc, stride=NB), :] for s in range(2)], axis=1)


def _s5_meta_kernel(u_ref, a_ref, bd_ref, cd_ref, dskip_ref, wglu_ref,
                    o_ref, state_out_ref, state_ref, ut_ref, bu_ref, ot_ref):
    state_ref[...] = jnp.zeros_like(state_ref)
    _s5_chunk(N_META, lambda b: u_ref[b * N_META:(b + 1) * N_META, :],
              state_ref, ut_ref, bu_ref, ot_ref, a_ref, bd_ref, cd_ref, dskip_ref, wglu_ref)
    for b in range(NB):
        o_ref[b * N_META:(b + 1) * N_META, :] = _s5_read_out(ot_ref, b, N_META).astype(BF16)
    state_out_ref[...] = state_ref[...]


def _s5_real_kernel(*refs):
    u_refs = refs[:NB]
    (state_in_ref, a_ref, bd_ref, cd_ref, dskip_ref, wglu_ref, _flat_ref,
     o_ref, state_ref, ut_ref, bu_ref, ot_ref) = refs[NB:]
    c = pl.program_id(0)
    b = pl.program_id(1)

    @pl.when((c == 0) & (b == 0))
    def _():
        state_ref[...] = state_in_ref[...]

    @pl.when(b == 0)
    def _():
        _s5_chunk(S5_CHUNK, lambda bb: u_refs[bb][...],
                  state_ref, ut_ref, bu_ref, ot_ref, a_ref, bd_ref, cd_ref, dskip_ref, wglu_ref)

    o_ref[...] = _s5_read_out(ot_ref, b, S5_CHUNK).astype(BF16)


def _s5(zu, seq, a, bd, cd, dskip, wglu):
    t = zu.shape[0]
    tr = NB * seq
    nmeta_rows = NB * N_META
    const = lambda *_: (0, 0)
    par_specs = [pl.BlockSpec((NB, SSM_COLS), const),
                 pl.BlockSpec((SSM_WIDTH, SSM_COLS), const),
                 pl.BlockSpec((SSM_COLS, SSM_WIDTH), const),
                 pl.BlockSpec((1, SSM_WIDTH), const),
                 pl.BlockSpec((SSM_WIDTH, 2 * SSM_WIDTH), const)]

    def scratch(lc):
        return [pltpu.VMEM((NB, SSM_COLS), F32),
                pltpu.VMEM((2, lc * NB, LANES), F32),
                pltpu.VMEM((lc * NB, SSM_COLS), F32),
                pltpu.VMEM((2, lc * NB, LANES), F32)]

    meta_blk = tr // nmeta_rows
    flat, state = pl.pallas_call(
        _s5_meta_kernel,
        grid=(1,),
        in_specs=[pl.BlockSpec((nmeta_rows, SSM_WIDTH), lambda i: (meta_blk, 0))] + par_specs,
        out_specs=[pl.BlockSpec((nmeta_rows, SSM_WIDTH), lambda i: (meta_blk, 0)),
                   pl.BlockSpec((NB, SSM_COLS), const)],
        out_shape=[jax.ShapeDtypeStruct((t, SSM_WIDTH), BF16),
                   jax.ShapeDtypeStruct((NB, SSM_COLS), F32)],
        scratch_shapes=scratch(N_META),
        compiler_params=_cp(("arbitrary",)),
        name="s5_meta",
    )(zu, a, bd, cd, dskip, wglu)

    nc = seq // S5_CHUNK
    u_specs = [pl.BlockSpec((S5_CHUNK, SSM_WIDTH), lambda c, b, bb=bb: (bb * nc + c, 0))
               for bb in range(NB)]
    n_in = NB + 1 + len(par_specs)
    return pl.pallas_call(
        _s5_real_kernel,
        grid=(nc, NB),
        in_specs=u_specs + [pl.BlockSpec((NB, SSM_COLS), const)] + par_specs
        + [pl.BlockSpec(memory_space=pl.ANY)],
        out_specs=pl.BlockSpec((S5_CHUNK, SSM_WIDTH), lambda c, b: (b * nc + c, 0)),
        out_shape=jax.ShapeDtypeStruct((t, SSM_WIDTH), BF16),
        scratch_shapes=scratch(S5_CHUNK),
        input_output_aliases={n_in: 0},
        compiler_params=_cp(("arbitrary", "arbitrary")),
        name="s5_real",
    )(*([zu] * NB), state, a, bd, cd, dskip, wglu, flat)


def _s5_params(lam_re, lam_im, log_dt, b_re, b_im, c_re, c_im, d_skip):
    dt = jnp.exp(log_dt)[:, None]
    mag = jnp.exp(lam_re * dt)
    ab_re = mag * jnp.cos(lam_im * dt)
    ab_im = mag * jnp.sin(lam_im * dt)
    den = lam_re * lam_re + lam_im * lam_im
    nr = ab_re - 1.0
    ni = ab_im
    coef_re = ((nr * lam_re + ni * lam_im) / den)[..., None]
    coef_im = ((ni * lam_re - nr * lam_im) / den)[..., None]
    bb_re = coef_re * b_re - coef_im * b_im
    bb_im = coef_re * b_im + coef_im * b_re
    eye = jnp.eye(SSM_GROUPS, dtype=F32)
    half = SSM_COLS // 2
    bd = jnp.concatenate(
        [jnp.einsum('gnc,gh->gchn', m, eye).reshape(SSM_WIDTH, half) for m in (bb_re, bb_im)],
        axis=1).astype(BF16)
    cd = jnp.concatenate(
        [jnp.einsum('gcn,gh->gnhc', m, eye).reshape(half, SSM_WIDTH) for m in (c_re, -c_im)],
        axis=0).astype(BF16)
    a = jnp.concatenate([ab_re.reshape(1, half), ab_im.reshape(1, half)], axis=1)
    a = jnp.broadcast_to(a, (NB, SSM_COLS))
    return a, bd, cd, d_skip.reshape(1, SSM_WIDTH)


def _nt_dot(a, b):
    return lax.dot_general(a, b, (((1,), (1,)), ((), ())), preferred_element_type=F32)


def _osm_init(s, v):
    m = jnp.max(s, axis=1, keepdims=True)
    p = jnp.exp(s - m)
    l = jnp.sum(p, axis=1, keepdims=True)
    acc = jnp.dot(p.astype(BF16), v, preferred_element_type=F32)
    return m, l, acc


def _osm_update(s, v, m, l, acc):
    m_new = jnp.maximum(m, jnp.max(s, axis=1, keepdims=True))
    alpha = jnp.exp(m - m_new)
    p = jnp.exp(s - m_new)
    l = alpha * l + jnp.sum(p, axis=1, keepdims=True)
    acc = alpha * acc + jnp.dot(p.astype(BF16), v, preferred_element_type=F32)
    return m_new, l, acc


def _causal_sweep(q_list, k_ref, v_ref, km, vm, qi):
    n = len(q_list)
    carry = []
    for q in q_list:
        carry.extend(_osm_init(_nt_dot(q, km), vm))

    def body(j, carry):
        off = pl.multiple_of(j * ATT_TK, ATT_TK)
        kb = k_ref[pl.ds(off, ATT_TK), :]
        vb = v_ref[pl.ds(off, ATT_TK), :]
        out = []
        for i in range(n):
            out.extend(_osm_update(_nt_dot(q_list[i], kb), vb, *carry[3 * i:3 * i + 3]))
        return tuple(out)

    carry = lax.fori_loop(0, qi, body, tuple(carry))
    off = pl.multiple_of(qi * ATT_TK, ATT_TK)
    kb = k_ref[pl.ds(off, ATT_TK), :]
    vb = v_ref[pl.ds(off, ATT_TK), :]
    row = lax.broadcasted_iota(jnp.int32, (ATT_TQ, ATT_TK), 0)
    col = lax.broadcasted_iota(jnp.int32, (ATT_TQ, ATT_TK), 1)
    out = []
    for i in range(n):
        s = jnp.where(col <= row, _nt_dot(q_list[i], kb), MASK_VALUE)
        m, l, acc = _osm_update(s, vb, *carry[3 * i:3 * i + 3])
        out.append((l, acc))
    return out


def _meta_attend(q, km, vm):
    n = km.shape[0]
    row = lax.broadcasted_iota(jnp.int32, (n, n), 0)
    col = lax.broadcasted_iota(jnp.int32, (n, n), 1)
    s = jnp.where(col <= row, _nt_dot(q, km), MASK_VALUE)
    _, l, acc = _osm_init(s, vm)
    return l, acc


def _diff_split(q):
    lane = lax.broadcasted_iota(jnp.int32, q.shape, 1)
    q = q * jnp.asarray(0.125, q.dtype)
    zero = jnp.zeros_like(q)
    return jnp.where(lane < 64, q, zero), jnp.where(lane >= 64, q, zero)


def _diff_finish(parts, lam, lam_init, g):
    (l1, acc1), (l2, acc2) = parts
    o = acc1 / l1 - lam * (acc2 / l2)
    return (_rms(o, g) * (1.0 - lam_init)).astype(BF16)


def _diff_kernel(lam_init, lam_ref, q_ref, k_ref, v_ref, km_ref, vm_ref, g_ref, o_ref):
    q1, q2 = _diff_split(q_ref[...])
    parts = _causal_sweep([q1, q2], k_ref, v_ref, km_ref[...], vm_ref[...], pl.program_id(2))
    o_ref[...] = _diff_finish(parts, lam_ref[0], lam_init, g_ref[...])


def _diff_meta_kernel(lam_init, lam_ref, q_ref, km_ref, vm_ref, g_ref, _flat_ref, o_ref):
    q1, q2 = _diff_split(q_ref[...])
    km = km_ref[...]
    vm = vm_ref[...]
    parts = [_meta_attend(q1, km, vm), _meta_attend(q2, km, vm)]
    o_ref[...] = _diff_finish(parts, lam_ref[0], lam_init, g_ref[...])


def _diff_attention(za, seq, lam, lam_init, g):
    t = za.shape[0]
    nq = seq // ATT_TQ
    mrow = NB * seq // N_META
    smem = pl.BlockSpec(memory_space=pltpu.SMEM)
    lam = lam.reshape(1).astype(F32)
    g = g.reshape(1, LANES)
    gspec = pl.BlockSpec((1, LANES), lambda *_: (0, 0))
    flat = pl.pallas_call(
        functools.partial(_diff_kernel, lam_init),
        grid=(NB, DIFF_HEADS, nq),
        in_specs=[smem,
                  pl.BlockSpec((ATT_TQ, LANES), lambda b, h, i: (b * nq + i, h)),
                  pl.BlockSpec((seq, LANES), lambda b, h, i: (b, 4 + h)),
                  pl.BlockSpec((seq, LANES), lambda b, h, i: (b, 8 + h)),
                  pl.BlockSpec((N_META, LANES), lambda b, h, i: (mrow + b, 4 + h)),
                  pl.BlockSpec((N_META, LANES), lambda b, h, i: (mrow + b, 8 + h)),
                  gspec],
        out_specs=pl.BlockSpec((ATT_TQ, LANES), lambda b, h, i: (b * nq + i, h)),
        out_shape=jax.ShapeDtypeStruct((t, DIFF_HEADS * LANES), BF16),
        compiler_params=_cp(("parallel", "parallel", "arbitrary")),
        name="diff_attn",
    )(lam, za, za, za, za, za, g)
    return pl.pallas_call(
        functools.partial(_diff_meta_kernel, lam_init),
        grid=(NB, DIFF_HEADS),
        in_specs=[smem,
                  pl.BlockSpec((N_META, LANES), lambda b, h: (mrow + b, h)),
                  pl.BlockSpec((N_META, LANES), lambda b, h: (mrow + b, 4 + h)),
                  pl.BlockSpec((N_META, LANES), lambda b, h: (mrow + b, 8 + h)),
                  gspec,
                  pl.BlockSpec(memory_space=pl.ANY)],
        out_specs=pl.BlockSpec((N_META, LANES), lambda b, h: (mrow + b, h)),
        out_shape=jax.ShapeDtypeStruct((t, DIFF_HEADS * LANES), BF16),
        input_output_aliases={5: 0},
        compiler_params=_cp(("parallel", "parallel")),
        name="diff_attn_meta",
    )(lam, za, za, za, g, flat)


def _split3(c):
    hi = c.astype(BF16).astype(F32)
    r1 = c - hi
    mid = r1.astype(BF16).astype(F32)
    lo = (r1 - mid).astype(BF16).astype(F32)
    return hi, mid, lo


def _cumsum_rows(tri, lf):
    parts = jnp.concatenate(_split3(lf), axis=1).astype(BF16)
    r = jnp.dot(tri, parts, preferred_element_type=F32)
    return r[:, :LANES] + r[:, LANES:2 * LANES] + r[:, 2 * LANES:]


def _log_sigmoid(x):
    return jnp.minimum(x, 0.0) - jnp.log1p(jnp.exp(-jnp.abs(x)))


def _fox_augment(fq, fk, cum, qa_ref, ka_ref):
    n = fq.shape[0]
    lane = lax.broadcasted_iota(jnp.int32, (n, LANES), 1)
    for h in range(FOX_HEADS):
        pair = slice(LANES * (h // 2), LANES * (h // 2) + LANES)
        own = (lane // 64) == (h % 2)
        e0 = 64 * (1 - h % 2)
        hi, mid, lo = _split3(jnp.broadcast_to(cum[:, h:h + 1], (n, LANES)))
        ones = (lane >= e0 + 3) & (lane < e0 + 6)
        q_extra = jnp.where(lane == e0, hi, jnp.where(lane == e0 + 1, mid, jnp.where(
            lane == e0 + 2, lo, jnp.where(ones, 1.0, 0.0))))
        ones = (lane >= e0) & (lane < e0 + 3)
        k_extra = jnp.where(lane == e0 + 3, -hi, jnp.where(lane == e0 + 4, -mid, jnp.where(
            lane == e0 + 5, -lo, jnp.where(ones, 1.0, 0.0))))
        q = fq[:, pair].astype(F32) * 0.125
        k = fk[:, pair].astype(F32)
        qa_ref[:, LANES * h:LANES * (h + 1)] = jnp.where(own, q, q_extra).astype(BF16)
        ka_ref[:, LANES * h:LANES * (h + 1)] = jnp.where(own, k, k_extra).astype(BF16)


def _fox_prep_meta_kernel(zg_ref, fq_ref, fk_ref, fb_ref, qa_ref, ka_ref, carry_ref):
    n = NB * N_META
    lf = _log_sigmoid(zg_ref[...] + fb_ref[...])
    row = lax.broadcasted_iota(jnp.int32, (n, n), 0)
    col = lax.broadcasted_iota(jnp.int32, (n, n), 1)
    tri = ((col <= row) & (col // N_META == row // N_META)).astype(BF16)
    cum = _cumsum_rows(tri, lf)
    brow = lax.broadcasted_iota(jnp.int32, (NB, n), 0)
    bcol = lax.broadcasted_iota(jnp.int32, (NB, n), 1)
    carry_ref[...] = _cumsum_rows((bcol // N_META == brow).astype(BF16), lf)
    _fox_augment(fq_ref[...], fk_ref[...], cum, qa_ref, ka_ref)


def _fox_prep_kernel(zg_ref, fq_ref, fk_ref, fb_ref, carry_in_ref, qa_ref, ka_ref, carry_ref):
    b = pl.program_id(0)
    n = zg_ref.shape[0]

    @pl.when(pl.program_id(1) == 0)
    def _():
        carry_ref[...] = carry_in_ref[pl.ds(b, 1), :]

    lf = _log_sigmoid(zg_ref[...] + fb_ref[...])
    row = lax.broadcasted_iota(jnp.int32, (n, n), 0)
    col = lax.broadcasted_iota(jnp.int32, (n, n), 1)
    cum = _cumsum_rows((col <= row).astype(BF16), lf) + carry_ref[...]
    carry_ref[...] = cum[n - 1:n, :]
    _fox_augment(fq_ref[...], fk_ref[...], cum, qa_ref, ka_ref)


FOX_PREP_TILE = 256


def _fox_prep(zg, za, seq, fb):
    tr = NB * seq
    nm = NB * N_META
    mblk = tr // nm
    fb = jnp.pad(fb.astype(F32), (0, LANES - FOX_HEADS)).reshape(1, LANES)
    fbspec = pl.BlockSpec((1, LANES), lambda *_: (0, 0))
    aug = FOX_HEADS * LANES
    qa_m, ka_m, carry = pl.pallas_call(
        _fox_prep_meta_kernel,
        grid=(1,),
        in_specs=[pl.BlockSpec((nm, LANES), lambda i: (mblk, 0)),
                  pl.BlockSpec((nm, 256), lambda i: (mblk, 6)),
                  pl.BlockSpec((nm, 256), lambda i: (mblk, 7)),
                  fbspec],
        out_specs=[pl.BlockSpec((nm, aug), lambda i: (0, 0)),
                   pl.BlockSpec((nm, aug), lambda i: (0, 0)),
                   pl.BlockSpec((NB, LANES), lambda i: (0, 0))],
        out_shape=[jax.ShapeDtypeStruct((nm, aug), BF16),
                   jax.ShapeDtypeStruct((nm, aug), BF16),
                   jax.ShapeDtypeStruct((NB, LANES), F32)],
        compiler_params=_cp(("arbitrary",)),
        name="fox_prep_meta",
    )(zg, za, za, fb)
    nc = seq // FOX_PREP_TILE
    qa, ka = pl.pallas_call(
        _fox_prep_kernel,
        grid=(NB, nc),
        in_specs=[pl.BlockSpec((FOX_PREP_TILE, LANES), lambda b, c: (b * nc + c, 0)),
                  pl.BlockSpec((FOX_PREP_TILE, 256), lambda b, c: (b * nc + c, 6)),
                  pl.BlockSpec((FOX_PREP_TILE, 256), lambda b, c: (b * nc + c, 7)),
                  fbspec,
                  pl.BlockSpec((NB, LANES), lambda b, c: (0, 0))],
        out_specs=[pl.BlockSpec((FOX_PREP_TILE, aug), lambda b, c: (b * nc + c, 0)),
                   pl.BlockSpec((FOX_PREP_TILE, aug), lambda b, c: (b * nc + c, 0))],
        out_shape=[jax.ShapeDtypeStruct((tr, aug), BF16),
                   jax.ShapeDtypeStruct((tr, aug), BF16)],
        scratch_shapes=[pltpu.VMEM((1, LANES), F32)],
        compiler_params=_cp(("parallel", "arbitrary")),
        name="fox_prep",
    )(zg, za, za, fb, carry)
    return qa, ka, qa_m, ka_m


def _fox_finish(parts):
    (l0, acc0), (l1, acc1) = parts
    lane = lax.broadcasted_iota(jnp.int32, acc0.shape, 1)
    return jnp.where(lane < 64, acc0 / l0, acc1 / l1).astype(BF16)


def _fox_kernel(q_ref, k_ref, v_ref, km_ref, vm_ref, o_ref):
    qi = pl.program_id(2)
    parts = []
    for hh in range(2):
        sl = slice(LANES * hh, LANES * (hh + 1))
        parts.extend(_causal_sweep([q_ref[:, sl]], k_ref.at[:, sl], v_ref,
                                   km_ref[:, sl], vm_ref[...], qi))
    o_ref[...] = _fox_finish(parts)


def _fox_meta_kernel(q_ref, km_ref, vm_ref, _flat_ref, o_ref):
    vm = vm_ref[...]
    parts = [_meta_attend(q_ref[:, LANES * hh:LANES * (hh + 1)],
                          km_ref[:, LANES * hh:LANES * (hh + 1)], vm) for hh in range(2)]
    o_ref[...] = _fox_finish(parts)


def _fox_attention(za, seq, qa, ka, qa_m, ka_m):
    t = za.shape[0]
    nq = seq // ATT_TQ
    mrow = NB * seq // N_META
    flat = pl.pallas_call(
        _fox_kernel,
        grid=(NB, FOX_HEADS // 2, nq),
        in_specs=[pl.BlockSpec((ATT_TQ, 256), lambda b, p, i: (b * nq + i, p)),
                  pl.BlockSpec((seq, 256), lambda b, p, i: (b, p)),
                  pl.BlockSpec((seq, LANES), lambda b, p, i: (b, 16 + p)),
                  pl.BlockSpec((N_META, 256), lambda b, p, i: (b, p)),
                  pl.BlockSpec((N_META, LANES), lambda b, p, i: (mrow + b, 16 + p))],
        out_specs=pl.BlockSpec((ATT_TQ, LANES), lambda b, p, i: (b * nq + i, p)),
        out_shape=jax.ShapeDtypeStruct((t, 256), BF16),
        compiler_params=_cp(("parallel", "parallel", "arbitrary")),
        name="fox_attn",
    )(qa, ka, za, ka_m, za)
    return pl.pallas_call(
        _fox_meta_kernel,
        grid=(NB, FOX_HEADS // 2),
        in_specs=[pl.BlockSpec((N_META, 256), lambda b, p: (b, p)),
                  pl.BlockSpec((N_META, 256), lambda b, p: (b, p)),
                  pl.BlockSpec((N_META, LANES), lambda b, p: (mrow + b, 16 + p)),
                  pl.BlockSpec(memory_space=pl.ANY)],
        out_specs=pl.BlockSpec((N_META, LANES), lambda b, p: (mrow + b, p)),
        out_shape=jax.ShapeDtypeStruct((t, 256), BF16),
        input_output_aliases={3: 0},
        compiler_params=_cp(("parallel", "parallel")),
        name="fox_attn_meta",
    )(qa_m, ka_m, za, flat)


def _router_kernel(res_ref, g_ref, wr_ref, idx_ref, gate_ref):
    h = _rms(res_ref[...], g_ref[...])
    logits = jnp.dot(h, wr_ref[...], preferred_element_type=F32,
                     precision=lax.Precision.HIGHEST)
    lane = lax.broadcasted_iota(jnp.int32, logits.shape, 1)
    logits = jnp.where(lane < N_EXPERTS, logits, -jnp.inf)
    m1 = jnp.max(logits, axis=1, keepdims=True)
    i1 = jnp.min(jnp.where(logits == m1, lane, LANES), axis=1, keepdims=True)
    rest = jnp.where(lane == i1, -jnp.inf, logits)
    m2 = jnp.max(rest, axis=1, keepdims=True)
    i2 = jnp.min(jnp.where(rest == m2, lane, LANES), axis=1, keepdims=True)
    e = jnp.exp(m2 - m1)
    g1 = 1.0 / (1.0 + e)
    g2 = e / (1.0 + e)
    idx_ref[...] = jnp.where(lane == 0, i1, jnp.where(lane == 1, i2, 0))
    gate_ref[...] = jnp.where(lane == 0, g1, jnp.where(lane == 1, g2, 0.0))


def _router(res, g, wr):
    t = res.shape[0]
    return pl.pallas_call(
        _router_kernel,
        grid=(t // TOK_TILE,),
        in_specs=[pl.BlockSpec((TOK_TILE, D_MODEL), lambda i: (i, 0)),
                  pl.BlockSpec((1, D_MODEL), lambda i: (0, 0)),
                  pl.BlockSpec((D_MODEL, LANES), lambda i: (0, 0))],
        out_specs=[pl.BlockSpec((TOK_TILE, LANES), lambda i: (i, 0)),
                   pl.BlockSpec((TOK_TILE, LANES), lambda i: (i, 0))],
        out_shape=[jax.ShapeDtypeStruct((t, LANES), jnp.int32),
                   jax.ShapeDtypeStruct((t, LANES), F32)],
        compiler_params=_cp(("parallel",)),
        name="router",
    )(res, g, wr)


def _gather_copy(src_hbm, row, dst_ref, r, sem):
    return pltpu.make_async_copy(src_hbm.at[pl.ds(row, 1), :], dst_ref.at[pl.ds(r, 1), :], sem)


def _gather_kernel(src_ref, x_hbm, o_ref, sem):
    base = pl.program_id(0) * GATHER_TILE

    def issue(r, _):
        _gather_copy(x_hbm, src_ref[base + r], o_ref, r, sem).start()
        return 0

    lax.fori_loop(0, GATHER_TILE, issue, 0)

    def drain(r, _):
        _gather_copy(x_hbm, 0, o_ref, r, sem).wait()
        return 0

    lax.fori_loop(0, GATHER_TILE, drain, 0)


def _gather_rows(src, x, n_rows):
    return pl.pallas_call(
        _gather_kernel,
        grid_spec=pltpu.PrefetchScalarGridSpec(
            num_scalar_prefetch=1,
            grid=(n_rows // GATHER_TILE,),
            in_specs=[pl.BlockSpec(memory_space=pl.ANY)],
            out_specs=pl.BlockSpec((GATHER_TILE, D_MODEL), lambda i, src: (i, 0)),
            scratch_shapes=[pltpu.SemaphoreType.DMA(())]),
        out_shape=jax.ShapeDtypeStruct((n_rows, D_MODEL), F32),
        compiler_params=_cp(("arbitrary",)),
        name="moe_gather",
    )(src, x)


def _expert_ffn_kernel(te_ref, na_ref, x_ref, g_ref, w1_ref, w3_ref, w2_ref, o_ref,
                       h_ref, acc_ref):
    r = pl.program_id(0)
    f = pl.program_id(1)
    nf = pl.num_programs(1)

    @pl.when(r < na_ref[0])
    def _():
        @pl.when(f == 0)
        def _():
            h_ref[...] = _rms(x_ref[...], g_ref[...]).astype(BF16)
            acc_ref[...] = jnp.zeros_like(acc_ref)

        h = h_ref[...]
        a = jnp.dot(h, w1_ref[...].astype(BF16), preferred_element_type=F32)
        b = jnp.dot(h, w3_ref[...].astype(BF16), preferred_element_type=F32)
        hh = (a * jax.nn.sigmoid(a) * b).astype(BF16)
        acc_ref[...] += jnp.dot(hh, w2_ref[...].astype(BF16), preferred_element_type=F32)

        @pl.when(f == nf - 1)
        def _():
            o_ref[...] = acc_ref[...]


def _expert_ffn(tile_expert, n_active, x_sorted, g, w1, w3, w2):
    p = x_sorted.shape[0]
    dff = w1.shape[2]
    nt = p // MOE_TILE
    nf = dff // MOE_FCHUNK

    def row(r, f, te, na):
        return jnp.minimum(r, na[0] - 1)

    def fch(r, f, te, na):
        return jnp.where(r < na[0], f, nf - 1)

    return pl.pallas_call(
        _expert_ffn_kernel,
        grid_spec=pltpu.PrefetchScalarGridSpec(
            num_scalar_prefetch=2,
            grid=(nt, nf),
            in_specs=[
                pl.BlockSpec((MOE_TILE, D_MODEL), lambda r, f, te, na: (row(r, f, te, na), 0)),
                pl.BlockSpec((1, D_MODEL), lambda r, f, te, na: (0, 0)),
                pl.BlockSpec((None, D_MODEL, MOE_FCHUNK),
                             lambda r, f, te, na: (te[r], 0, fch(r, f, te, na))),
                pl.BlockSpec((None, D_MODEL, MOE_FCHUNK),
                             lambda r, f, te, na: (te[r], 0, fch(r, f, te, na))),
                pl.BlockSpec((None, MOE_FCHUNK, D_MODEL),
                             lambda r, f, te, na: (te[r], fch(r, f, te, na), 0))],
            out_specs=pl.BlockSpec((MOE_TILE, D_MODEL),
                                   lambda r, f, te, na: (row(r, f, te, na), 0)),
            scratch_shapes=[pltpu.VMEM((MOE_TILE, D_MODEL), BF16),
                            pltpu.VMEM((MOE_TILE, D_MODEL), F32)]),
        out_shape=jax.ShapeDtypeStruct((p, D_MODEL), F32),
        compiler_params=_cp(("arbitrary", "arbitrary")),
        name="expert_ffn",
    )(tile_expert, n_active, x_sorted, g, w1, w3, w2)


def _combine_kernel(pos_ref, res_ref, gate_ref, y_hbm, o_ref, y1_ref, y2_ref, sem):
    base = pl.program_id(0) * COMBINE_TILE

    def issue(r, _):
        _gather_copy(y_hbm, pos_ref[2 * (base + r)], y1_ref, r, sem.at[0]).start()
        _gather_copy(y_hbm, pos_ref[2 * (base + r) + 1], y2_ref, r, sem.at[1]).start()
        return 0

    lax.fori_loop(0, COMBINE_TILE, issue, 0)

    def drain(r, _):
        _gather_copy(y_hbm, 0, y1_ref, r, sem.at[0]).wait()
        _gather_copy(y_hbm, 0, y2_ref, r, sem.at[1]).wait()
        return 0

    lax.fori_loop(0, COMBINE_TILE, drain, 0)
    gate = gate_ref[...]
    o_ref[...] = res_ref[...] + gate[:, 0:1] * y1_ref[...] + gate[:, 1:2] * y2_ref[...]


def _combine(pos, res, gates, y):
    t = res.shape[0]
    return pl.pallas_call(
        _combine_kernel,
        grid_spec=pltpu.PrefetchScalarGridSpec(
            num_scalar_prefetch=1,
            grid=(t // COMBINE_TILE,),
            in_specs=[pl.BlockSpec((COMBINE_TILE, D_MODEL), lambda i, pos: (i, 0)),
                      pl.BlockSpec((COMBINE_TILE, LANES), lambda i, pos: (i, 0)),
                      pl.BlockSpec(memory_space=pl.ANY)],
            out_specs=pl.BlockSpec((COMBINE_TILE, D_MODEL), lambda i, pos: (i, 0)),
            scratch_shapes=[pltpu.VMEM((COMBINE_TILE, D_MODEL), F32),
                            pltpu.VMEM((COMBINE_TILE, D_MODEL), F32),
                            pltpu.SemaphoreType.DMA((2,))]),
        out_shape=jax.ShapeDtypeStruct((t, D_MODEL), F32),
        compiler_params=_cp(("arbitrary",)),
        name="moe_combine",
    )(pos, res, gates, y)


def _moe(res, g, wr, w1, w3, w2):
    t = res.shape[0]
    wr = jnp.pad(wr.astype(F32), ((0, 0), (0, LANES - N_EXPERTS)))
    idx, gates = _router(res, g, wr)
    e_flat = idx[:, :2].reshape(-1)
    onehot = (e_flat[:, None] == jnp.arange(N_EXPERTS, dtype=jnp.int32)[None, :]).astype(jnp.int32)
    csum = jnp.cumsum(onehot, axis=0)
    rank = jnp.take_along_axis(csum, e_flat[:, None], axis=1)[:, 0] - 1
    counts = csum[-1]
    tiles = (counts + MOE_TILE - 1) // MOE_TILE
    tile_end = jnp.cumsum(tiles)
    starts = (tile_end - tiles) * MOE_TILE
    pos = (starts[e_flat] + rank).astype(jnp.int32)
    n_tiles = (2 * t + N_EXPERTS * (MOE_TILE - 1)) // MOE_TILE
    p = n_tiles * MOE_TILE
    src = jnp.zeros((p,), jnp.int32).at[pos].set(jnp.arange(2 * t, dtype=jnp.int32) // 2)
    n_active = tile_end[-1:].astype(jnp.int32)
    tile_ids = jnp.minimum(jnp.arange(n_tiles, dtype=jnp.int32), n_active[0] - 1)
    tile_expert = jnp.searchsorted(tile_end, tile_ids, side='right').astype(jnp.int32)

    x_sorted = _gather_rows(src, res, p)
    y = _expert_ffn(tile_expert, n_active, x_sorted, g, w1, w3, w2)
    return _combine(pos, res, gates, y)


def _final_norm_kernel(res_ref, g_ref, o_ref):
    o_ref[...] = _rms(res_ref[...], g_ref[...])


def _final_norm(res, g, n_rows):
    tile = 1024
    return pl.pallas_call(
        _final_norm_kernel,
        grid=(n_rows // tile,),
        in_specs=[pl.BlockSpec((tile, D_MODEL), lambda i: (i, 0)),
                  pl.BlockSpec((1, D_MODEL), lambda i: (0, 0))],
        out_specs=pl.BlockSpec((tile, D_MODEL), lambda i: (i, 0)),
        out_shape=jax.ShapeDtypeStruct((n_rows, D_MODEL), F32),
        compiler_params=_cp(("parallel",)),
        name="final_norm",
    )(res, g)


def kernel(x, meta_tokens, norm_mix_g, w_in, w_out, ssm_lambda_re, ssm_lambda_im, ssm_log_dt,
           ssm_b_re, ssm_b_im, ssm_c_re, ssm_c_im, ssm_d, ssm_w_glu, diff_lambda_q1,
           diff_lambda_k1, diff_lambda_q2, diff_lambda_k2, diff_subln_g, fox_forget_b,
           norm_ffn_g, dense_w1, dense_w3, dense_w2, moe_router, moe_w1, moe_w3, moe_w2,
           final_norm_g):
    bsz, seq, d = x.shape
    assert bsz == NB and d == D_MODEL and seq % ATT_TQ == 0
    depth = w_in.shape[0]
    tr = bsz * seq
    res = jnp.concatenate(
        [x.reshape(tr, d), jnp.tile(meta_tokens.astype(x.dtype), (bsz, 1))], axis=0)
    row = lambda v: v.reshape(1, -1).astype(F32)

    for l in range(depth):
        w = jnp.pad(w_in[l], ((0, 0), (0, IN_PAD - w_in.shape[2]))).astype(BF16)
        zu, za, zg = _inproj(res, row(norm_mix_g[l]), w)

        a, bd, cd, dskip = _s5_params(ssm_lambda_re[l], ssm_lambda_im[l], ssm_log_dt[l],
                                      ssm_b_re[l], ssm_b_im[l], ssm_c_re[l], ssm_c_im[l],
                                      ssm_d[l])
        ssm_out = _s5(zu, seq, a, bd, cd, dskip, ssm_w_glu[l].astype(BF16))

        lam_init = 0.8 - 0.6 * math.exp(-0.3 * l)
        lam = (jnp.exp(jnp.sum(diff_lambda_q1[l] * diff_lambda_k1[l]))
               - jnp.exp(jnp.sum(diff_lambda_q2[l] * diff_lambda_k2[l])) + lam_init)
        diff_out = _diff_attention(za, seq, lam, lam_init, diff_subln_g[l])

        qa, ka, qa_m, ka_m = _fox_prep(zg, za, seq, fox_forget_b[l])
        fox_out = _fox_attention(za, seq, qa, ka, qa_m, ka_m)

        res = _outproj(res, ssm_out, diff_out, fox_out, w_out[l].astype(BF16))

        if l % 2 == 0:
            res = _dense_ffn(res, row(norm_ffn_g[l]), dense_w1[l // 2].astype(BF16),
                             dense_w3[l // 2].astype(BF16), dense_w2[l // 2].astype(BF16))
        else:
            res = _moe(res, row(norm_ffn_g[l]), moe_router[l // 2], moe_w1[l // 2],
                       moe_w3[l // 2], moe_w2[l // 2])

    return _final_norm(res, row(final_norm_g), tr).reshape(bsz, seq, d)
```

```python
import functools
import math

import jax
import jax.numpy as jnp
from jax import lax
from jax.experimental import pallas as pl
from jax.experimental.pallas import tpu as pltpu

F32 = jnp.float32
BF16 = jnp.bfloat16
EPS = 1e-6

D_MODEL = 1024
N_META = 16
NB = 8
SSM_WIDTH = 256
SSM_GROUPS = 16
SSM_GROUP = 16
SSM_STATE = 64
SSM_COLS = 2 * SSM_GROUPS * SSM_STATE
DIFF_HEADS = 4
FOX_HEADS = 4
ATT_COLS = 2304
IN_PAD = 2688
N_EXPERTS = 8
LANES = 128
MASK_VALUE = -1e30

VT_ROWS = 768
TOK_TILE = 688
IN_TILE = 384
FFN_TILE = 384
ATT_TQ = 256
ATT_TK = 256
S5_CHUNK = 128
MOE_TILE = 1024
MOE_FCHUNK = 512
GATHER_TILE = 256
COMBINE_TILE = 384

VMEM_LIMIT = 56 * 1024 * 1024


def _cp(sem):
    return pltpu.CompilerParams(dimension_semantics=sem, vmem_limit_bytes=VMEM_LIMIT)


def _rms(x, g):
    return x * lax.rsqrt(jnp.mean(x * x, axis=-1, keepdims=True) + EPS) * g


def _inproj_kernel(res_ref, g_ref, w_ref, wvt_ref, zu_ref, za_ref, zg_ref, vt_ref):
    h = _rms(res_ref[...], g_ref[...]).astype(BF16)
    zu_ref[...] = jnp.dot(h, w_ref[:, 0:SSM_WIDTH], preferred_element_type=F32)
    for c in range(0, ATT_COLS, 256):
        za_ref[:, c:c + 256] = jnp.dot(
            h, w_ref[:, SSM_WIDTH + c:SSM_WIDTH + c + 256],
            preferred_element_type=F32).astype(BF16)
    zg_ref[...] = jnp.dot(h, w_ref[:, SSM_WIDTH + ATT_COLS:IN_PAD],
                          preferred_element_type=F32)
    vt = _nt_dot(wvt_ref[...], h).astype(BF16)
    for c in range(IN_TILE // LANES):
        vt_ref[c] = vt[:, LANES * c:LANES * (c + 1)]


def _inproj(res, g, w, wvt):
    t = res.shape[0]
    nblk = IN_TILE // LANES
    return pl.pallas_call(
        _inproj_kernel,
        grid=(t // IN_TILE,),
        in_specs=[pl.BlockSpec((IN_TILE, D_MODEL), lambda i: (i, 0)),
                  pl.BlockSpec((1, D_MODEL), lambda i: (0, 0)),
                  pl.BlockSpec((D_MODEL, IN_PAD), lambda i: (0, 0)),
                  pl.BlockSpec((VT_ROWS, D_MODEL), lambda i: (0, 0))],
        out_specs=[pl.BlockSpec((IN_TILE, SSM_WIDTH), lambda i: (i, 0)),
                   pl.BlockSpec((IN_TILE, ATT_COLS), lambda i: (i, 0)),
                   pl.BlockSpec((IN_TILE, LANES), lambda i: (i, 0)),
                   pl.BlockSpec((nblk, VT_ROWS, LANES), lambda i: (i, 0, 0))],
        out_shape=[jax.ShapeDtypeStruct((t, SSM_WIDTH), F32),
                   jax.ShapeDtypeStruct((t, ATT_COLS), BF16),
                   jax.ShapeDtypeStruct((t, LANES), F32),
                   jax.ShapeDtypeStruct((t // LANES, VT_ROWS, LANES), BF16)],
        compiler_params=_cp(("parallel",)),
        name="inproj",
    )(res, g, w, wvt)


def _outproj_kernel(res_ref, s_ref, d_ref, f_ref, w_ref, o_ref):
    mixed = jnp.concatenate([s_ref[...], d_ref[...], f_ref[...]], axis=1)
    o_ref[...] = res_ref[...] + jnp.dot(mixed, w_ref[...], preferred_element_type=F32)


def _outproj(res, ssm, diff, fox, w):
    t = res.shape[0]
    return pl.pallas_call(
        _outproj_kernel,
        grid=(t // TOK_TILE,),
        in_specs=[pl.BlockSpec((TOK_TILE, D_MODEL), lambda i: (i, 0)),
                  pl.BlockSpec((TOK_TILE, 256), lambda i: (i, 0)),
                  pl.BlockSpec((TOK_TILE, 512), lambda i: (i, 0)),
                  pl.BlockSpec((TOK_TILE, 256), lambda i: (i, 0)),
                  pl.BlockSpec((D_MODEL, D_MODEL), lambda i: (0, 0))],
        out_specs=pl.BlockSpec((TOK_TILE, D_MODEL), lambda i: (i, 0)),
        out_shape=jax.ShapeDtypeStruct((t, D_MODEL), F32),
        compiler_params=_cp(("parallel",)),
        name="outproj",
    )(res, ssm, diff, fox, w)


def _dense_ffn_kernel(res_ref, g_ref, w1_ref, w3_ref, w2_ref, o_ref):
    x = res_ref[...]
    h = _rms(x, g_ref[...]).astype(BF16)
    a = jnp.dot(h, w1_ref[...], preferred_element_type=F32)
    b = jnp.dot(h, w3_ref[...], preferred_element_type=F32)
    hh = (a * jax.nn.sigmoid(a) * b).astype(BF16)
    o_ref[...] = x + jnp.dot(hh, w2_ref[...], preferred_element_type=F32)


def _dense_ffn(res, g, w1, w3, w2):
    t = res.shape[0]
    dff = w1.shape[1]
    once = pl.Buffered(1)
    return pl.pallas_call(
        _dense_ffn_kernel,
        grid=(t // FFN_TILE,),
        in_specs=[pl.BlockSpec((FFN_TILE, D_MODEL), lambda i: (i, 0)),
                  pl.BlockSpec((1, D_MODEL), lambda i: (0, 0)),
                  pl.BlockSpec((D_MODEL, dff), lambda i: (0, 0), pipeline_mode=once),
                  pl.BlockSpec((D_MODEL, dff), lambda i: (0, 0), pipeline_mode=once),
                  pl.BlockSpec((dff, D_MODEL), lambda i: (0, 0), pipeline_mode=once)],
        out_specs=pl.BlockSpec((FFN_TILE, D_MODEL), lambda i: (i, 0)),
        out_shape=jax.ShapeDtypeStruct((t, D_MODEL), F32),
        compiler_params=_cp(("parallel",)),
        name="dense_ffn",
    )(res, g, w1, w3, w2)


def _gelu_tanh(x):
    c = math.sqrt(2.0 / math.pi)
    return 0.5 * x * (1.0 + jnp.tanh(c * (x + 0.044715 * (x * x * x))))


def _s5_chunk(lc, get_u, state_ref, ut_ref, bu_ref, ot_ref,
              a_ref, bd_ref, cd_ref, dskip_ref, wglu_ref):
    for b in range(NB):
        ub = get_u(b)
        for s in range(2):
            ut_ref[s, pl.ds(b, lc, stride=NB), :] = ub[:, LANES * s:LANES * (s + 1)]
    u_tm = jnp.concatenate([ut_ref[0], ut_ref[1]], axis=1)
    bu_ref[...] = jnp.dot(u_tm.astype(BF16), bd_ref[...], preferred_element_type=F32)

    half = SSM_COLS // 2
    a_re = a_ref[:, :half]
    a_im = a_ref[:, half:]

    def step(t, x):
        x_re, x_im = x
        r = pl.multiple_of(t * NB, NB)
        cur = bu_ref[pl.ds(r, NB), :]
        n_re = a_re * x_re - a_im * x_im + cur[:, :half]
        n_im = a_re * x_im + a_im * x_re + cur[:, half:]
        bu_ref[pl.ds(r, NB), :] = jnp.concatenate([n_re, n_im], axis=1)
        return n_re, n_im

    x_re, x_im = lax.fori_loop(0, lc, step,
                               (state_ref[:, :half], state_ref[:, half:]), unroll=4)
    state_ref[...] = jnp.concatenate([x_re, x_im], axis=1)

    y = jnp.dot(bu_ref[...].astype(BF16), cd_ref[...], preferred_element_type=F32)
    y = _gelu_tanh(y + dskip_ref[...] * u_tm)
    g = jnp.dot(y.astype(BF16), wglu_ref[...], preferred_element_type=F32)
    o = g[:, :SSM_WIDTH] * jax.nn.sigmoid(g[:, SSM_WIDTH:])
    ot_ref[0] = o[:, :LANES]
    ot_ref[1] = o[:, LANES:]


def _s5_read_out(ot_ref, b, lc):
    return jnp.concatenate(
        [ot_ref[s, pl.ds(b, lc, stride=NB), :] for s in range(2)], axis=1)


def _s5_meta_kernel(u_ref, a_ref, bd_ref, cd_ref, dskip_ref, wglu_ref,
                    o_ref, state_out_ref, state_ref, ut_ref, bu_ref, ot_ref):
    state_ref[...] = jnp.zeros_like(state_ref)
    _s5_chunk(N_META, lambda b: u_ref[b * N_META:(b + 1) * N_META, :],
              state_ref, ut_ref, bu_ref, ot_ref, a_ref, bd_ref, cd_ref, dskip_ref, wglu_ref)
    for b in range(NB):
        o_ref[b * N_META:(b + 1) * N_META, :] = _s5_read_out(ot_ref, b, N_META).astype(BF16)
    state_out_ref[...] = state_ref[...]


def _s5_real_kernel(*refs):
    u_refs = refs[:NB]
    (state_in_ref, a_ref, bd_ref, cd_ref, dskip_ref, wglu_ref, _flat_ref,
     o_ref, state_ref, ut_ref, bu_ref, ot_ref) = refs[NB:]
    c = pl.program_id(0)
    b = pl.program_id(1)

    @pl.when((c == 0) & (b == 0))
    def _():
        state_ref[...] = state_in_ref[...]

    @pl.when(b == 0)
    def _():
        _s5_chunk(S5_CHUNK, lambda bb: u_refs[bb][...],
                  state_ref, ut_ref, bu_ref, ot_ref, a_ref, bd_ref, cd_ref, dskip_ref, wglu_ref)

    o_ref[...] = _s5_read_out(ot_ref, b, S5_CHUNK).astype(BF16)


def _s5(zu, seq, a, bd, cd, dskip, wglu):
    t = zu.shape[0]
    tr = NB * seq
    nmeta_rows = NB * N_META
    const = lambda *_: (0, 0)
    par_specs = [pl.BlockSpec((NB, SSM_COLS), const),
                 pl.BlockSpec((SSM_WIDTH, SSM_COLS), const),
                 pl.BlockSpec((SSM_COLS, SSM_WIDTH), const),
                 pl.BlockSpec((1, SSM_WIDTH), const),
                 pl.BlockSpec((SSM_WIDTH, 2 * SSM_WIDTH), const)]

    def scratch(lc):
        return [pltpu.VMEM((NB, SSM_COLS), F32),
                pltpu.VMEM((2, lc * NB, LANES), F32),
                pltpu.VMEM((lc * NB, SSM_COLS), F32),
                pltpu.VMEM((2, lc * NB, LANES), F32)]

    meta_blk = tr // nmeta_rows
    flat, state = pl.pallas_call(
        _s5_meta_kernel,
        grid=(1,),
        in_specs=[pl.BlockSpec((nmeta_rows, SSM_WIDTH), lambda i: (meta_blk, 0))] + par_specs,
        out_specs=[pl.BlockSpec((nmeta_rows, SSM_WIDTH), lambda i: (meta_blk, 0)),
                   pl.BlockSpec((NB, SSM_COLS), const)],
        out_shape=[jax.ShapeDtypeStruct((t, SSM_WIDTH), BF16),
                   jax.ShapeDtypeStruct((NB, SSM_COLS), F32)],
        scratch_shapes=scratch(N_META),
        compiler_params=_cp(("arbitrary",)),
        name="s5_meta",
    )(zu, a, bd, cd, dskip, wglu)

    nc = seq // S5_CHUNK
    u_specs = [pl.BlockSpec((S5_CHUNK, SSM_WIDTH), lambda c, b, bb=bb: (bb * nc + c, 0))
               for bb in range(NB)]
    n_in = NB + 1 + len(par_specs)
    return pl.pallas_call(
        _s5_real_kernel,
        grid=(nc, NB),
        in_specs=u_specs + [pl.BlockSpec((NB, SSM_COLS), const)] + par_specs
        + [pl.BlockSpec(memory_space=pl.ANY)],
        out_specs=pl.BlockSpec((S5_CHUNK, SSM_WIDTH), lambda c, b: (b * nc + c, 0)),
        out_shape=jax.ShapeDtypeStruct((t, SSM_WIDTH), BF16),
        scratch_shapes=scratch(S5_CHUNK),
        input_output_aliases={n_in: 0},
        compiler_params=_cp(("arbitrary", "arbitrary")),
        name="s5_real",
    )(*([zu] * NB), state, a, bd, cd, dskip, wglu, flat)


def _s5_params(lam_re, lam_im, log_dt, b_re, b_im, c_re, c_im, d_skip):
    dt = jnp.exp(log_dt)[:, None]
    mag = jnp.exp(lam_re * dt)
    ab_re = mag * jnp.cos(lam_im * dt)
    ab_im = mag * jnp.sin(lam_im * dt)
    den = lam_re * lam_re + lam_im * lam_im
    nr = ab_re - 1.0
    ni = ab_im
    coef_re = ((nr * lam_re + ni * lam_im) / den)[..., None]
    coef_im = ((ni * lam_re - nr * lam_im) / den)[..., None]
    bb_re = coef_re * b_re - coef_im * b_im
    bb_im = coef_re * b_im + coef_im * b_re
    eye = jnp.eye(SSM_GROUPS, dtype=F32)
    half = SSM_COLS // 2
    bd = jnp.concatenate(
        [jnp.einsum('gnc,gh->gchn', m, eye).reshape(SSM_WIDTH, half) for m in (bb_re, bb_im)],
        axis=1).astype(BF16)
    cd = jnp.concatenate(
        [jnp.einsum('gcn,gh->gnhc', m, eye).reshape(half, SSM_WIDTH) for m in (c_re, -c_im)],
        axis=0).astype(BF16)
    a = jnp.concatenate([ab_re.reshape(1, half), ab_im.reshape(1, half)], axis=1)
    a = jnp.broadcast_to(a, (NB, SSM_COLS))
    return a, bd, cd, d_skip.reshape(1, SSM_WIDTH)


def _nt_dot(a, b):
    return lax.dot_general(a, b, (((1,), (1,)), ((), ())), preferred_element_type=F32)


def _osm_init(s, v):
    m = jnp.max(s, axis=1, keepdims=True)
    p = jnp.exp(s - m)
    l = jnp.sum(p, axis=1, keepdims=True)
    acc = jnp.dot(p.astype(BF16), v, preferred_element_type=F32)
    return m, l, acc


def _causal_sweep(qs, k_ref, k_cols, vt_ref, vt_rows, kms, vmts, qi, m_ref, l_ref, acc_ref):
    n = len(qs)
    for i in range(n):
        s = _nt_dot(kms[i], qs[i])
        m = jnp.max(s, axis=0, keepdims=True)
        p = jnp.exp(s - m)
        m_ref[i][...] = m
        l_ref[i][...] = jnp.sum(p, axis=0, keepdims=True)
        acc_ref[i][...] = jnp.dot(vmts[i], p.astype(BF16), preferred_element_type=F32)

    sub = ATT_TK // LANES

    def chunk(j, masked):
        off = pl.multiple_of(j * ATT_TK, ATT_TK)
        if masked:
            krow = lax.broadcasted_iota(jnp.int32, (ATT_TK, ATT_TQ), 0)
            qcol = lax.broadcasted_iota(jnp.int32, (ATT_TK, ATT_TQ), 1)
            visible = krow <= qcol
        scores = []
        for i in range(n):
            kb = k_ref[pl.ds(off, ATT_TK), k_cols[i]]
            s = _nt_dot(kb, qs[i])
            scores.append(jnp.where(visible, s, MASK_VALUE) if masked else s)
        probs, alphas = [], []
        for i in range(n):
            m_old = m_ref[i][...]
            m_new = jnp.maximum(m_old, jnp.max(scores[i], axis=0, keepdims=True))
            alpha = jnp.exp(m_old - m_new)
            p = jnp.exp(scores[i] - m_new)
            l_ref[i][...] = alpha * l_ref[i][...] + jnp.sum(p, axis=0, keepdims=True)
            m_ref[i][...] = m_new
            probs.append(p.astype(BF16))
            alphas.append(alpha)
        for i in range(n):
            vtb = jnp.concatenate([vt_ref[j * sub + c, vt_rows[i], :] for c in range(sub)],
                                  axis=1)
            acc_ref[i][...] = alphas[i] * acc_ref[i][...] + jnp.dot(
                vtb, probs[i], preferred_element_type=F32)

    def body(j, carry):
        chunk(j, False)
        return carry

    lax.fori_loop(0, qi, body, 0)
    chunk(qi, True)


def _meta_attend(q, km, vm):
    n = km.shape[0]
    row = lax.broadcasted_iota(jnp.int32, (n, n), 0)
    col = lax.broadcasted_iota(jnp.int32, (n, n), 1)
    s = jnp.where(col <= row, _nt_dot(q, km), MASK_VALUE)
    _, l, acc = _osm_init(s, vm)
    return l, acc


def _diff_split(q):
    lane = lax.broadcasted_iota(jnp.int32, q.shape, 1)
    q = q * jnp.asarray(0.125, q.dtype)
    zero = jnp.zeros_like(q)
    return jnp.where(lane < 64, q, zero), jnp.where(lane >= 64, q, zero)


def _diff_finish(parts, lam, lam_init, g):
    (l1, acc1), (l2, acc2) = parts
    o = acc1 / l1 - lam * (acc2 / l2)
    return (_rms(o, g) * (1.0 - lam_init)).astype(BF16)


def _transpose_bf16(x):
    return x.astype(F32).T.astype(BF16)


def _map_scratch(n_maps, dv):
    return ([pltpu.VMEM((1, ATT_TQ), F32)] * (2 * n_maps)
            + [pltpu.VMEM((dv, ATT_TQ), F32)] * n_maps)


def _split_map_scratch(scratch, n_maps):
    return scratch[:n_maps], scratch[n_maps:2 * n_maps], scratch[2 * n_maps:]


def _diff_kernel(lam_init, lam_ref, q_ref, k_ref, vt_ref, km_ref, vm_ref, g_ref, o_ref,
                 *scratch):
    m_ref, l_ref, acc_ref = _split_map_scratch(scratch, 2 * DIFF_HEADS)
    qs, k_cols, vt_rows, kms, vmts = [], [], [], [], []
    for h in range(DIFF_HEADS):
        sl = slice(LANES * h, LANES * (h + 1))
        km = km_ref[:, sl]
        vmt = _transpose_bf16(vm_ref[:, sl])
        for q in _diff_split(q_ref[:, sl]):
            qs.append(q)
            k_cols.append(sl)
            vt_rows.append(sl)
            kms.append(km)
            vmts.append(vmt)
    _causal_sweep(qs, k_ref, k_cols, vt_ref, vt_rows, kms, vmts, pl.program_id(1),
                  m_ref, l_ref, acc_ref)
    lam = lam_ref[0]
    for h in range(DIFF_HEADS):
        o = (acc_ref[2 * h][...] / l_ref[2 * h][...]
             - lam * (acc_ref[2 * h + 1][...] / l_ref[2 * h + 1][...]))
        y = o * lax.rsqrt(jnp.mean(o * o, axis=0, keepdims=True) + EPS)
        o_ref[:, LANES * h:LANES * (h + 1)] = (
            y.T * g_ref[...] * (1.0 - lam_init)).astype(BF16)


def _diff_meta_kernel(lam_init, lam_ref, q_ref, km_ref, vm_ref, g_ref, _flat_ref, o_ref):
    q1, q2 = _diff_split(q_ref[...])
    km = km_ref[...]
    vm = vm_ref[...]
    parts = [_meta_attend(q1, km, vm), _meta_attend(q2, km, vm)]
    o_ref[...] = _diff_finish(parts, lam_ref[0], lam_init, g_ref[...])


def _diff_attention(za, vt, seq, lam, lam_init, g):
    t = za.shape[0]
    nq = seq // ATT_TQ
    mrow = NB * seq // N_META
    smem = pl.BlockSpec(memory_space=pltpu.SMEM)
    lam = lam.reshape(1).astype(F32)
    g = g.reshape(1, LANES)
    gspec = pl.BlockSpec((1, LANES), lambda *_: (0, 0))
    width = DIFF_HEADS * LANES
    n_maps = 2 * DIFF_HEADS
    flat = pl.pallas_call(
        functools.partial(_diff_kernel, lam_init),
        grid=(NB, nq),
        in_specs=[smem,
                  pl.BlockSpec((ATT_TQ, width), lambda b, i: (b * nq + i, 0)),
                  pl.BlockSpec((seq, width), lambda b, i: (b, 1)),
                  pl.BlockSpec((seq // LANES, width, LANES), lambda b, i: (b, 0, 0)),
                  pl.BlockSpec((N_META, width), lambda b, i: (mrow + b, 1)),
                  pl.BlockSpec((N_META, width), lambda b, i: (mrow + b, 2)),
                  gspec],
        out_specs=pl.BlockSpec((ATT_TQ, width), lambda b, i: (b * nq + i, 0)),
        out_shape=jax.ShapeDtypeStruct((t, width), BF16),
        scratch_shapes=_map_scratch(n_maps, LANES),
        compiler_params=_cp(("parallel", "arbitrary")),
        name="diff_attn",
    )(lam, za, za, vt, za, za, g)
    return pl.pallas_call(
        functools.partial(_diff_meta_kernel, lam_init),
        grid=(NB, DIFF_HEADS),
        in_specs=[smem,
                  pl.BlockSpec((N_META, LANES), lambda b, h: (mrow + b, h)),
                  pl.BlockSpec((N_META, LANES), lambda b, h: (mrow + b, 4 + h)),
                  pl.BlockSpec((N_META, LANES), lambda b, h: (mrow + b, 8 + h)),
                  gspec,
                  pl.BlockSpec(memory_space=pl.ANY)],
        out_specs=pl.BlockSpec((N_META, LANES), lambda b, h: (mrow + b, h)),
        out_shape=jax.ShapeDtypeStruct((t, DIFF_HEADS * LANES), BF16),
        input_output_aliases={5: 0},
        compiler_params=_cp(("parallel", "parallel")),
        name="diff_attn_meta",
    )(lam, za, za, za, g, flat)


def _split3(c):
    hi = c.astype(BF16).astype(F32)
    r1 = c - hi
    mid = r1.astype(BF16).astype(F32)
    lo = (r1 - mid).astype(BF16).astype(F32)
    return hi, mid, lo


def _cumsum_rows(tri, lf):
    parts = jnp.concatenate(_split3(lf), axis=1).astype(BF16)
    r = jnp.dot(tri, parts, preferred_element_type=F32)
    return r[:, :LANES] + r[:, LANES:2 * LANES] + r[:, 2 * LANES:]


def _log_sigmoid(x):
    return jnp.minimum(x, 0.0) - jnp.log1p(jnp.exp(-jnp.abs(x)))


def _fox_augment(fq, fk, cum, qa_ref, ka_ref):
    n = fq.shape[0]
    lane = lax.broadcasted_iota(jnp.int32, (n, LANES), 1)
    for h in range(FOX_HEADS):
        pair = slice(LANES * (h // 2), LANES * (h // 2) + LANES)
        own = (lane // 64) == (h % 2)
        e0 = 64 * (1 - h % 2)
        hi, mid, lo = _split3(jnp.broadcast_to(cum[:, h:h + 1], (n, LANES)))
        ones = (lane >= e0 + 3) & (lane < e0 + 6)
        q_extra = jnp.where(lane == e0, hi, jnp.where(lane == e0 + 1, mid, jnp.where(
            lane == e0 + 2, lo, jnp.where(ones, 1.0, 0.0))))
        ones = (lane >= e0) & (lane < e0 + 3)
        k_extra = jnp.where(lane == e0 + 3, -hi, jnp.where(lane == e0 + 4, -mid, jnp.where(
            lane == e0 + 5, -lo, jnp.where(ones, 1.0, 0.0))))
        q = fq[:, pair].astype(F32) * 0.125
        k = fk[:, pair].astype(F32)
        qa_ref[:, LANES * h:LANES * (h + 1)] = jnp.where(own, q, q_extra).astype(BF16)
        ka_ref[:, LANES * h:LANES * (h + 1)] = jnp.where(own, k, k_extra).astype(BF16)


def _fox_prep_meta_kernel(zg_ref, fq_ref, fk_ref, fb_ref, qa_ref, ka_ref, carry_ref):
    n = NB * N_META
    lf = _log_sigmoid(zg_ref[...] + fb_ref[...])
    row = lax.broadcasted_iota(jnp.int32, (n, n), 0)
    col = lax.broadcasted_iota(jnp.int32, (n, n), 1)
    tri = ((col <= row) & (col // N_META == row // N_META)).astype(BF16)
    cum = _cumsum_rows(tri, lf)
    brow = lax.broadcasted_iota(jnp.int32, (NB, n), 0)
    bcol = lax.broadcasted_iota(jnp.int32, (NB, n), 1)
    carry_ref[...] = _cumsum_rows((bcol // N_META == brow).astype(BF16), lf)
    _fox_augment(fq_ref[...], fk_ref[...], cum, qa_ref, ka_ref)


def _fox_prep_kernel(zg_ref, fq_ref, fk_ref, fb_ref, carry_in_ref, qa_ref, ka_ref, carry_ref):
    b = pl.program_id(0)
    n = zg_ref.shape[0]

    @pl.when(pl.program_id(1) == 0)
    def _():
        carry_ref[...] = carry_in_ref[pl.ds(b, 1), :]

    lf = _log_sigmoid(zg_ref[...] + fb_ref[...])
    row = lax.broadcasted_iota(jnp.int32, (n, n), 0)
    col = lax.broadcasted_iota(jnp.int32, (n, n), 1)
    cum = _cumsum_rows((col <= row).astype(BF16), lf) + carry_ref[...]
    carry_ref[...] = cum[n - 1:n, :]
    _fox_augment(fq_ref[...], fk_ref[...], cum, qa_ref, ka_ref)


FOX_PREP_TILE = 256


def _fox_prep(zg, za, seq, fb):
    tr = NB * seq
    nm = NB * N_META
    mblk = tr // nm
    fb = jnp.pad(fb.astype(F32), (0, LANES - FOX_HEADS)).reshape(1, LANES)
    fbspec = pl.BlockSpec((1, LANES), lambda *_: (0, 0))
    aug = FOX_HEADS * LANES
    qa_m, ka_m, carry = pl.pallas_call(
        _fox_prep_meta_kernel,
        grid=(1,),
        in_specs=[pl.BlockSpec((nm, LANES), lambda i: (mblk, 0)),
                  pl.BlockSpec((nm, 256), lambda i: (mblk, 6)),
                  pl.BlockSpec((nm, 256), lambda i: (mblk, 7)),
                  fbspec],
        out_specs=[pl.BlockSpec((nm, aug), lambda i: (0, 0)),
                   pl.BlockSpec((nm, aug), lambda i: (0, 0)),
                   pl.BlockSpec((NB, LANES), lambda i: (0, 0))],
        out_shape=[jax.ShapeDtypeStruct((nm, aug), BF16),
                   jax.ShapeDtypeStruct((nm, aug), BF16),
                   jax.ShapeDtypeStruct((NB, LANES), F32)],
        compiler_params=_cp(("arbitrary",)),
        name="fox_prep_meta",
    )(zg, za, za, fb)
    nc = seq // FOX_PREP_TILE
    qa, ka = pl.pallas_call(
        _fox_prep_kernel,
        grid=(NB, nc),
        in_specs=[pl.BlockSpec((FOX_PREP_TILE, LANES), lambda b, c: (b * nc + c, 0)),
                  pl.BlockSpec((FOX_PREP_TILE, 256), lambda b, c: (b * nc + c, 6)),
                  pl.BlockSpec((FOX_PREP_TILE, 256), lambda b, c: (b * nc + c, 7)),
                  fbspec,
                  pl.BlockSpec((NB, LANES), lambda b, c: (0, 0))],
        out_specs=[pl.BlockSpec((FOX_PREP_TILE, aug), lambda b, c: (b * nc + c, 0)),
                   pl.BlockSpec((FOX_PREP_TILE, aug), lambda b, c: (b * nc + c, 0))],
        out_shape=[jax.ShapeDtypeStruct((tr, aug), BF16),
                   jax.ShapeDtypeStruct((tr, aug), BF16)],
        scratch_shapes=[pltpu.VMEM((1, LANES), F32)],
        compiler_params=_cp(("parallel", "arbitrary")),
        name="fox_prep",
    )(zg, za, za, fb, carry)
    return qa, ka, qa_m, ka_m


def _fox_finish(parts):
    (l0, acc0), (l1, acc1) = parts
    lane = lax.broadcasted_iota(jnp.int32, acc0.shape, 1)
    return jnp.where(lane < 64, acc0 / l0, acc1 / l1).astype(BF16)


def _fox_kernel(q_ref, k_ref, vt_ref, km_ref, vm_ref, o_ref, *scratch):
    m_ref, l_ref, acc_ref = _split_map_scratch(scratch, FOX_HEADS)
    hd = 64
    vmt = _transpose_bf16(vm_ref[...])
    qs, k_cols, vt_rows, kms, vmts = [], [], [], [], []
    for h in range(FOX_HEADS):
        sl = slice(LANES * h, LANES * (h + 1))
        qs.append(q_ref[:, sl])
        k_cols.append(sl)
        vt_rows.append(slice(hd * h, hd * (h + 1)))
        kms.append(km_ref[:, sl])
        vmts.append(vmt[hd * h:hd * (h + 1), :])
    _causal_sweep(qs, k_ref, k_cols, vt_ref, vt_rows, kms, vmts, pl.program_id(1),
                  m_ref, l_ref, acc_ref)
    for p in range(FOX_HEADS // 2):
        o = jnp.concatenate([acc_ref[2 * p][...] / l_ref[2 * p][...],
                             acc_ref[2 * p + 1][...] / l_ref[2 * p + 1][...]],
                            axis=0)
        o_ref[:, LANES * p:LANES * (p + 1)] = o.T.astype(BF16)


def _fox_meta_kernel(q_ref, km_ref, vm_ref, _flat_ref, o_ref):
    vm = vm_ref[...]
    parts = [_meta_attend(q_ref[:, LANES * hh:LANES * (hh + 1)],
                          km_ref[:, LANES * hh:LANES * (hh + 1)], vm) for hh in range(2)]
    o_ref[...] = _fox_finish(parts)


def _fox_attention(za, vt, seq, qa, ka, qa_m, ka_m):
    t = za.shape[0]
    nq = seq // ATT_TQ
    mrow = NB * seq // N_META
    aug = FOX_HEADS * LANES
    flat = pl.pallas_call(
        _fox_kernel,
        grid=(NB, nq),
        in_specs=[pl.BlockSpec((ATT_TQ, aug), lambda b, i: (b * nq + i, 0)),
                  pl.BlockSpec((seq, aug), lambda b, i: (b, 0)),
                  pl.BlockSpec((seq // LANES, 256, LANES), lambda b, i: (b, 2, 0)),
                  pl.BlockSpec((N_META, aug), lambda b, i: (b, 0)),
                  pl.BlockSpec((N_META, 256), lambda b, i: (mrow + b, 8))],
        out_specs=pl.BlockSpec((ATT_TQ, 256), lambda b, i: (b * nq + i, 0)),
        out_shape=jax.ShapeDtypeStruct((t, 256), BF16),
        scratch_shapes=_map_scratch(FOX_HEADS, 64),
        compiler_params=_cp(("parallel", "arbitrary")),
        name="fox_attn",
    )(qa, ka, vt, ka_m, za)
    return pl.pallas_call(
        _fox_meta_kernel,
        grid=(NB, FOX_HEADS // 2),
        in_specs=[pl.BlockSpec((N_META, 256), lambda b, p: (b, p)),
                  pl.BlockSpec((N_META, 256), lambda b, p: (b, p)),
                  pl.BlockSpec((N_META, LANES), lambda b, p: (mrow + b, 16 + p)),
                  pl.BlockSpec(memory_space=pl.ANY)],
        out_specs=pl.BlockSpec((N_META, LANES), lambda b, p: (mrow + b, p)),
        out_shape=jax.ShapeDtypeStruct((t, 256), BF16),
        input_output_aliases={3: 0},
        compiler_params=_cp(("parallel", "parallel")),
        name="fox_attn_meta",
    )(qa_m, ka_m, za, flat)


def _router_kernel(res_ref, g_ref, wr_ref, idx_ref, gate_ref):
    h = _rms(res_ref[...], g_ref[...])
    logits = jnp.dot(h, wr_ref[...], preferred_element_type=F32,
                     precision=lax.Precision.HIGHEST)
    lane = lax.broadcasted_iota(jnp.int32, logits.shape, 1)
    logits = jnp.where(lane < N_EXPERTS, logits, -jnp.inf)
    m1 = jnp.max(logits, axis=1, keepdims=True)
    i1 = jnp.min(jnp.where(logits == m1, lane, LANES), axis=1, keepdims=True)
    rest = jnp.where(lane == i1, -jnp.inf, logits)
    m2 = jnp.max(rest, axis=1, keepdims=True)
    i2 = jnp.min(jnp.where(rest == m2, lane, LANES), axis=1, keepdims=True)
    e = jnp.exp(m2 - m1)
    g1 = 1.0 / (1.0 + e)
    g2 = e / (1.0 + e)
    idx_ref[...] = jnp.where(lane == 0, i1, jnp.where(lane == 1, i2, 0))
    gate_ref[...] = jnp.where(lane == 0, g1, jnp.where(lane == 1, g2, 0.0))


def _router(res, g, wr):
    t = res.shape[0]
    return pl.pallas_call(
        _router_kernel,
        grid=(t // TOK_TILE,),
        in_specs=[pl.BlockSpec((TOK_TILE, D_MODEL), lambda i: (i, 0)),
                  pl.BlockSpec((1, D_MODEL), lambda i: (0, 0)),
                  pl.BlockSpec((D_MODEL, LANES), lambda i: (0, 0))],
        out_specs=[pl.BlockSpec((TOK_TILE, LANES), lambda i: (i, 0)),
                   pl.BlockSpec((TOK_TILE, LANES), lambda i: (i, 0))],
        out_shape=[jax.ShapeDtypeStruct((t, LANES), jnp.int32),
                   jax.ShapeDtypeStruct((t, LANES), F32)],
        compiler_params=_cp(("parallel",)),
        name="router",
    )(res, g, wr)


def _gather_copy(src_hbm, row, dst_ref, r, sem):
    return pltpu.make_async_copy(src_hbm.at[pl.ds(row, 1), :], dst_ref.at[pl.ds(r, 1), :], sem)


def _gather_kernel(src_ref, x_hbm, o_ref, sem):
    base = pl.program_id(0) * GATHER_TILE

    def issue(r, _):
        _gather_copy(x_hbm, src_ref[base + r], o_ref, r, sem).start()
        return 0

    lax.fori_loop(0, GATHER_TILE, issue, 0, unroll=8)

    def drain(r, _):
        _gather_copy(x_hbm, 0, o_ref, r, sem).wait()
        return 0

    lax.fori_loop(0, GATHER_TILE, drain, 0, unroll=8)


def _gather_rows(src, x, n_rows):
    return pl.pallas_call(
        _gather_kernel,
        grid_spec=pltpu.PrefetchScalarGridSpec(
            num_scalar_prefetch=1,
            grid=(n_rows // GATHER_TILE,),
            in_specs=[pl.BlockSpec(memory_space=pl.ANY)],
            out_specs=pl.BlockSpec((GATHER_TILE, D_MODEL), lambda i, src: (i, 0)),
            scratch_shapes=[pltpu.SemaphoreType.DMA(())]),
        out_shape=jax.ShapeDtypeStruct((n_rows, D_MODEL), F32),
        compiler_params=_cp(("arbitrary",)),
        name="moe_gather",
    )(src, x)


def _expert_ffn_kernel(te_ref, na_ref, x_ref, g_ref, w1_ref, w3_ref, w2_ref, o_ref,
                       h_ref, acc_ref):
    r = pl.program_id(0)
    f = pl.program_id(1)
    nf = pl.num_programs(1)

    @pl.when(r < na_ref[0])
    def _():
        @pl.when(f == 0)
        def _():
            h_ref[...] = _rms(x_ref[...], g_ref[...]).astype(BF16)
            acc_ref[...] = jnp.zeros_like(acc_ref)

        h = h_ref[...]
        a = jnp.dot(h, w1_ref[...].astype(BF16), preferred_element_type=F32)
        b = jnp.dot(h, w3_ref[...].astype(BF16), preferred_element_type=F32)
        hh = (a * jax.nn.sigmoid(a) * b).astype(BF16)
        acc_ref[...] += jnp.dot(hh, w2_ref[...].astype(BF16), preferred_element_type=F32)

        @pl.when(f == nf - 1)
        def _():
            o_ref[...] = acc_ref[...]


def _expert_ffn(tile_expert, n_active, x_sorted, g, w1, w3, w2):
    p = x_sorted.shape[0]
    dff = w1.shape[2]
    nt = p // MOE_TILE
    nf = dff // MOE_FCHUNK

    def row(r, f, te, na):
        return jnp.minimum(r, na[0] - 1)

    def fch(r, f, te, na):
        return jnp.where(r < na[0], f, nf - 1)

    return pl.pallas_call(
        _expert_ffn_kernel,
        grid_spec=pltpu.PrefetchScalarGridSpec(
            num_scalar_prefetch=2,
            grid=(nt, nf),
            in_specs=[
                pl.BlockSpec((MOE_TILE, D_MODEL), lambda r, f, te, na: (row(r, f, te, na), 0)),
                pl.BlockSpec((1, D_MODEL), lambda r, f, te, na: (0, 0)),
                pl.BlockSpec((None, D_MODEL, MOE_FCHUNK),
                             lambda r, f, te, na: (te[r], 0, fch(r, f, te, na))),
                pl.BlockSpec((None, D_MODEL, MOE_FCHUNK),
                             lambda r, f, te, na: (te[r], 0, fch(r, f, te, na))),
                pl.BlockSpec((None, MOE_FCHUNK, D_MODEL),
                             lambda r, f, te, na: (te[r], fch(r, f, te, na), 0))],
            out_specs=pl.BlockSpec((MOE_TILE, D_MODEL),
                                   lambda r, f, te, na: (row(r, f, te, na), 0)),
            scratch_shapes=[pltpu.VMEM((MOE_TILE, D_MODEL), BF16),
                            pltpu.VMEM((MOE_TILE, D_MODEL), F32)]),
        out_shape=jax.ShapeDtypeStruct((p, D_MODEL), F32),
        compiler_params=_cp(("arbitrary", "arbitrary")),
        name="expert_ffn",
    )(tile_expert, n_active, x_sorted, g, w1, w3, w2)


def _combine_kernel(pos_ref, res_ref, gate_ref, y_hbm, o_ref, y1_ref, y2_ref, sem):
    base = pl.program_id(0) * COMBINE_TILE

    def issue(r, _):
        _gather_copy(y_hbm, pos_ref[2 * (base + r)], y1_ref, r, sem.at[0]).start()
        _gather_copy(y_hbm, pos_ref[2 * (base + r) + 1], y2_ref, r, sem.at[1]).start()
        return 0

    lax.fori_loop(0, COMBINE_TILE, issue, 0, unroll=8)

    def drain(r, _):
        _gather_copy(y_hbm, 0, y1_ref, r, sem.at[0]).wait()
        _gather_copy(y_hbm, 0, y2_ref, r, sem.at[1]).wait()
        return 0

    lax.fori_loop(0, COMBINE_TILE, drain, 0, unroll=8)
    gate = gate_ref[...]
    o_ref[...] = res_ref[...] + gate[:, 0:1] * y1_ref[...] + gate[:, 1:2] * y2_ref[...]


def _combine(pos, res, gates, y):
    t = res.shape[0]
    return pl.pallas_call(
        _combine_kernel,
        grid_spec=pltpu.PrefetchScalarGridSpec(
            num_scalar_prefetch=1,
            grid=(t // COMBINE_TILE,),
            in_specs=[pl.BlockSpec((COMBINE_TILE, D_MODEL), lambda i, pos: (i, 0)),
                      pl.BlockSpec((COMBINE_TILE, LANES), lambda i, pos: (i, 0)),
                      pl.BlockSpec(memory_space=pl.ANY)],
            out_specs=pl.BlockSpec((COMBINE_TILE, D_MODEL), lambda i, pos: (i, 0)),
            scratch_shapes=[pltpu.VMEM((COMBINE_TILE, D_MODEL), F32),
                            pltpu.VMEM((COMBINE_TILE, D_MODEL), F32),
                            pltpu.SemaphoreType.DMA((2,))]),
        out_shape=jax.ShapeDtypeStruct((t, D_MODEL), F32),
        compiler_params=_cp(("arbitrary",)),
        name="moe_combine",
    )(pos, res, gates, y)


def _moe(res, g, wr, w1, w3, w2):
    t = res.shape[0]
    wr = jnp.pad(wr.astype(F32), ((0, 0), (0, LANES - N_EXPERTS)))
    idx, gates = _router(res, g, wr)
    e_flat = idx[:, :2].reshape(-1)
    onehot = (e_flat[:, None] == jnp.arange(N_EXPERTS, dtype=jnp.int32)[None, :]).astype(jnp.int32)
    csum = jnp.cumsum(onehot, axis=0)
    rank = jnp.take_along_axis(csum, e_flat[:, None], axis=1)[:, 0] - 1
    counts = csum[-1]
    tiles = (counts + MOE_TILE - 1) // MOE_TILE
    tile_end = jnp.cumsum(tiles)
    starts = (tile_end - tiles) * MOE_TILE
    pos = (starts[e_flat] + rank).astype(jnp.int32)
    n_tiles = (2 * t + N_EXPERTS * (MOE_TILE - 1)) // MOE_TILE
    p = n_tiles * MOE_TILE
    src = jnp.zeros((p,), jnp.int32).at[pos].set(jnp.arange(2 * t, dtype=jnp.int32) // 2)
    n_active = tile_end[-1:].astype(jnp.int32)
    tile_ids = jnp.minimum(jnp.arange(n_tiles, dtype=jnp.int32), n_active[0] - 1)
    tile_expert = jnp.sum(tile_ids[:, None] >= tile_end[None, :], axis=1).astype(jnp.int32)

    x_sorted = _gather_rows(src, res, p)
    y = _expert_ffn(tile_expert, n_active, x_sorted, g, w1, w3, w2)
    return _combine(pos, res, gates, y)


def _final_norm_kernel(res_ref, g_ref, o_ref):
    o_ref[...] = _rms(res_ref[...], g_ref[...])


def _final_norm(res, g, n_rows):
    tile = 1024
    return pl.pallas_call(
        _final_norm_kernel,
        grid=(n_rows // tile,),
        in_specs=[pl.BlockSpec((tile, D_MODEL), lambda i: (i, 0)),
                  pl.BlockSpec((1, D_MODEL), lambda i: (0, 0))],
        out_specs=pl.BlockSpec((tile, D_MODEL), lambda i: (i, 0)),
        out_shape=jax.ShapeDtypeStruct((n_rows, D_MODEL), F32),
        compiler_params=_cp(("parallel",)),
        name="final_norm",
    )(res, g)


def kernel(x, meta_tokens, norm_mix_g, w_in, w_out, ssm_lambda_re, ssm_lambda_im, ssm_log_dt,
           ssm_b_re, ssm_b_im, ssm_c_re, ssm_c_im, ssm_d, ssm_w_glu, diff_lambda_q1,
           diff_lambda_k1, diff_lambda_q2, diff_lambda_k2, diff_subln_g, fox_forget_b,
           norm_ffn_g, dense_w1, dense_w3, dense_w2, moe_router, moe_w1, moe_w3, moe_w2,
           final_norm_g):
    bsz, seq, d = x.shape
    assert bsz == NB and d == D_MODEL and seq % ATT_TQ == 0
    depth = w_in.shape[0]
    tr = bsz * seq
    res = jnp.concatenate(
        [x.reshape(tr, d), jnp.tile(meta_tokens.astype(x.dtype), (bsz, 1))], axis=0)
    row = lambda v: v.reshape(1, -1).astype(F32)

    for l in range(depth):
        w = jnp.pad(w_in[l], ((0, 0), (0, IN_PAD - w_in.shape[2]))).astype(BF16)
        wvt = jnp.concatenate([w_in[l][:, 1280:1792], w_in[l][:, 2304:2560]], axis=1).T
        zu, za, zg, vt = _inproj(res, row(norm_mix_g[l]), w, wvt.astype(BF16))

        a, bd, cd, dskip = _s5_params(ssm_lambda_re[l], ssm_lambda_im[l], ssm_log_dt[l],
                                      ssm_b_re[l], ssm_b_im[l], ssm_c_re[l], ssm_c_im[l],
                                      ssm_d[l])
        ssm_out = _s5(zu, seq, a, bd, cd, dskip, ssm_w_glu[l].astype(BF16))

        lam_init = 0.8 - 0.6 * math.exp(-0.3 * l)
        lam = (jnp.exp(jnp.sum(diff_lambda_q1[l] * diff_lambda_k1[l]))
               - jnp.exp(jnp.sum(diff_lambda_q2[l] * diff_lambda_k2[l])) + lam_init)
        diff_out = _diff_attention(za, vt, seq, lam, lam_init, diff_subln_g[l])

        qa, ka, qa_m, ka_m = _fox_prep(zg, za, seq, fox_forget_b[l])
        fox_out = _fox_attention(za, vt, seq, qa, ka, qa_m, ka_m)

        res = _outproj(res, ssm_out, diff_out, fox_out, w_out[l].astype(BF16))

        if l % 2 == 0:
            res = _dense_ffn(res, row(norm_ffn_g[l]), dense_w1[l // 2].astype(BF16),
                             dense_w3[l // 2].astype(BF16), dense_w2[l // 2].astype(BF16))
        else:
            res = _moe(res, row(norm_ffn_g[l]), moe_router[l // 2], moe_w1[l // 2],
                       moe_w3[l // 2], moe_w2[l // 2])

    return _final_norm(res, row(final_norm_g), tr).reshape(bsz, seq, d)
```

```python
import functools
import math

import jax
import jax.numpy as jnp
from jax import lax
from jax.experimental import pallas as pl
from jax.experimental.pallas import tpu as pltpu

F32 = jnp.float32
BF16 = jnp.bfloat16
EPS = 1e-6

D_MODEL = 1024
N_META = 16
NB = 8
SSM_WIDTH = 256
SSM_GROUPS = 16
SSM_GROUP = 16
SSM_STATE = 64
SSM_COLS = 2 * SSM_GROUPS * SSM_STATE
DIFF_HEADS = 4
FOX_HEADS = 4
ATT_COLS = 2304
IN_PAD = 2688
N_EXPERTS = 8
LANES = 128
MASK_VALUE = -1e30

VT_ROWS = 768
TOK_TILE = 688
IN_TILE = 384
FFN_TILE = 384
ATT_TQ = 256
ATT_TK = 256
S5_CHUNK = 128
MOE_TILE = 1024
MOE_FCHUNK = 512
MOE_XROWS = 1064
MOE_XBUF_ROWS = MOE_XROWS
COMBINE_TILE = 384

VMEM_LIMIT = 56 * 1024 * 1024


def _cp(sem):
    return pltpu.CompilerParams(dimension_semantics=sem, vmem_limit_bytes=VMEM_LIMIT)


def _rms(x, g):
    return x * lax.rsqrt(jnp.mean(x * x, axis=-1, keepdims=True) + EPS) * g


def _inproj_kernel(res_ref, g_ref, w_ref, wvt_ref, zu_ref, za_ref, zg_ref, vt_ref):
    h = _rms(res_ref[...], g_ref[...]).astype(BF16)
    zu_ref[...] = jnp.dot(h, w_ref[:, 0:SSM_WIDTH], preferred_element_type=F32)
    for c in range(0, ATT_COLS, 256):
        za_ref[:, c:c + 256] = jnp.dot(
            h, w_ref[:, SSM_WIDTH + c:SSM_WIDTH + c + 256],
            preferred_element_type=F32).astype(BF16)
    zg_ref[...] = jnp.dot(h, w_ref[:, SSM_WIDTH + ATT_COLS:IN_PAD],
                          preferred_element_type=F32)
    vt = _nt_dot(wvt_ref[...], h).astype(BF16)
    for c in range(IN_TILE // LANES):
        vt_ref[c] = vt[:, LANES * c:LANES * (c + 1)]


def _inproj(res, g, w, wvt):
    t = res.shape[0]
    nblk = IN_TILE // LANES
    return pl.pallas_call(
        _inproj_kernel,
        grid=(t // IN_TILE,),
        in_specs=[pl.BlockSpec((IN_TILE, D_MODEL), lambda i: (i, 0)),
                  pl.BlockSpec((1, D_MODEL), lambda i: (0, 0)),
                  pl.BlockSpec((D_MODEL, IN_PAD), lambda i: (0, 0)),
                  pl.BlockSpec((VT_ROWS, D_MODEL), lambda i: (0, 0))],
        out_specs=[pl.BlockSpec((IN_TILE, SSM_WIDTH), lambda i: (i, 0)),
                   pl.BlockSpec((IN_TILE, ATT_COLS), lambda i: (i, 0)),
                   pl.BlockSpec((IN_TILE, LANES), lambda i: (i, 0)),
                   pl.BlockSpec((nblk, VT_ROWS, LANES), lambda i: (i, 0, 0))],
        out_shape=[jax.ShapeDtypeStruct((t, SSM_WIDTH), F32),
                   jax.ShapeDtypeStruct((t, ATT_COLS), BF16),
                   jax.ShapeDtypeStruct((t, LANES), F32),
                   jax.ShapeDtypeStruct((t // LANES, VT_ROWS, LANES), BF16)],
        compiler_params=_cp(("parallel",)),
        name="inproj",
    )(res, g, w, wvt)


def _outproj_kernel(res_ref, s_ref, d_ref, f_ref, w_ref, o_ref):
    mixed = jnp.concatenate([s_ref[...], d_ref[...], f_ref[...]], axis=1)
    o_ref[...] = res_ref[...] + jnp.dot(mixed, w_ref[...], preferred_element_type=F32)


def _outproj(res, ssm, diff, fox, w):
    t = res.shape[0]
    return pl.pallas_call(
        _outproj_kernel,
        grid=(t // TOK_TILE,),
        in_specs=[pl.BlockSpec((TOK_TILE, D_MODEL), lambda i: (i, 0)),
                  pl.BlockSpec((TOK_TILE, 256), lambda i: (i, 0)),
                  pl.BlockSpec((TOK_TILE, 512), lambda i: (i, 0)),
                  pl.BlockSpec((TOK_TILE, 256), lambda i: (i, 0)),
                  pl.BlockSpec((D_MODEL, D_MODEL), lambda i: (0, 0))],
        out_specs=pl.BlockSpec((TOK_TILE, D_MODEL), lambda i: (i, 0)),
        out_shape=jax.ShapeDtypeStruct((t, D_MODEL), F32),
        compiler_params=_cp(("parallel",)),
        name="outproj",
    )(res, ssm, diff, fox, w)


def _dense_ffn_kernel(res_ref, g_ref, w1_ref, w3_ref, w2_ref, o_ref):
    x = res_ref[...]
    h = _rms(x, g_ref[...]).astype(BF16)
    a = jnp.dot(h, w1_ref[...], preferred_element_type=F32)
    b = jnp.dot(h, w3_ref[...], preferred_element_type=F32)
    hh = (a * jax.nn.sigmoid(a) * b).astype(BF16)
    o_ref[...] = x + jnp.dot(hh, w2_ref[...], preferred_element_type=F32)


def _dense_ffn(res, g, w1, w3, w2):
    t = res.shape[0]
    dff = w1.shape[1]
    once = pl.Buffered(1)
    return pl.pallas_call(
        _dense_ffn_kernel,
        grid=(t // FFN_TILE,),
        in_specs=[pl.BlockSpec((FFN_TILE, D_MODEL), lambda i: (i, 0)),
                  pl.BlockSpec((1, D_MODEL), lambda i: (0, 0)),
                  pl.BlockSpec((D_MODEL, dff), lambda i: (0, 0), pipeline_mode=once),
                  pl.BlockSpec((D_MODEL, dff), lambda i: (0, 0), pipeline_mode=once),
                  pl.BlockSpec((dff, D_MODEL), lambda i: (0, 0), pipeline_mode=once)],
        out_specs=pl.BlockSpec((FFN_TILE, D_MODEL), lambda i: (i, 0)),
        out_shape=jax.ShapeDtypeStruct((t, D_MODEL), F32),
        compiler_params=_cp(("parallel",)),
        name="dense_ffn",
    )(res, g, w1, w3, w2)


def _gelu_tanh(x):
    c = math.sqrt(2.0 / math.pi)
    return 0.5 * x * (1.0 + jnp.tanh(c * (x + 0.044715 * (x * x * x))))


def _s5_chunk(lc, get_u, state_ref, ut_ref, bu_ref, ot_ref,
              a_ref, bd_ref, cd_ref, dskip_ref, wglu_ref):
    for b in range(NB):
        ub = get_u(b)
        for s in range(2):
            ut_ref[s, pl.ds(b, lc, stride=NB), :] = ub[:, LANES * s:LANES * (s + 1)]
    u_tm = jnp.concatenate([ut_ref[0], ut_ref[1]], axis=1)
    bu_ref[...] = jnp.dot(u_tm.astype(BF16), bd_ref[...], preferred_element_type=F32)

    half = SSM_COLS // 2
    a_re = a_ref[:, :half]
    a_im = a_ref[:, half:]

    def step(t, x):
        x_re, x_im = x
        r = pl.multiple_of(t * NB, NB)
        cur = bu_ref[pl.ds(r, NB), :]
        n_re = a_re * x_re - a_im * x_im + cur[:, :half]
        n_im = a_re * x_im + a_im * x_re + cur[:, half:]
        bu_ref[pl.ds(r, NB), :] = jnp.concatenate([n_re, n_im], axis=1)
        return n_re, n_im

    x_re, x_im = lax.fori_loop(0, lc, step,
                               (state_ref[:, :half], state_ref[:, half:]), unroll=4)
    state_ref[...] = jnp.concatenate([x_re, x_im], axis=1)

    y = jnp.dot(bu_ref[...].astype(BF16), cd_ref[...], preferred_element_type=F32)
    y = _gelu_tanh(y + dskip_ref[...] * u_tm)
    g = jnp.dot(y.astype(BF16), wglu_ref[...], preferred_element_type=F32)
    o = g[:, :SSM_WIDTH] * jax.nn.sigmoid(g[:, SSM_WIDTH:])
    ot_ref[0] = o[:, :LANES]
    ot_ref[1] = o[:, LANES:]


def _s5_read_out(ot_ref, b, lc):
    return jnp.concatenate(
        [ot_ref[s, pl.ds(b, lc, stride=NB), :] for s in range(2)], axis=1)


def _s5_meta_kernel(u_ref, a_ref, bd_ref, cd_ref, dskip_ref, wglu_ref,
                    o_ref, state_out_ref, state_ref, ut_ref, bu_ref, ot_ref):
    state_ref[...] = jnp.zeros_like(state_ref)
    _s5_chunk(N_META, lambda b: u_ref[b * N_META:(b + 1) * N_META, :],
              state_ref, ut_ref, bu_ref, ot_ref, a_ref, bd_ref, cd_ref, dskip_ref, wglu_ref)
    for b in range(NB):
        o_ref[b * N_META:(b + 1) * N_META, :] = _s5_read_out(ot_ref, b, N_META).astype(BF16)
    state_out_ref[...] = state_ref[...]


def _s5_real_kernel(*refs):
    u_refs = refs[:NB]
    (state_in_ref, a_ref, bd_ref, cd_ref, dskip_ref, wglu_ref, _flat_ref,
     o_ref, state_ref, ut_ref, bu_ref, ot_ref) = refs[NB:]
    c = pl.program_id(0)
    b = pl.program_id(1)

    @pl.when((c == 0) & (b == 0))
    def _():
        state_ref[...] = state_in_ref[...]

    @pl.when(b == 0)
    def _():
        _s5_chunk(S5_CHUNK, lambda bb: u_refs[bb][...],
                  state_ref, ut_ref, bu_ref, ot_ref, a_ref, bd_ref, cd_ref, dskip_ref, wglu_ref)

    o_ref[...] = _s5_read_out(ot_ref, b, S5_CHUNK).astype(BF16)


def _s5(zu, seq, a, bd, cd, dskip, wglu):
    t = zu.shape[0]
    tr = NB * seq
    nmeta_rows = NB * N_META
    const = lambda *_: (0, 0)
    par_specs = [pl.BlockSpec((NB, SSM_COLS), const),
                 pl.BlockSpec((SSM_WIDTH, SSM_COLS), const),
                 pl.BlockSpec((SSM_COLS, SSM_WIDTH), const),
                 pl.BlockSpec((1, SSM_WIDTH), const),
                 pl.BlockSpec((SSM_WIDTH, 2 * SSM_WIDTH), const)]

    def scratch(lc):
        return [pltpu.VMEM((NB, SSM_COLS), F32),
                pltpu.VMEM((2, lc * NB, LANES), F32),
                pltpu.VMEM((lc * NB, SSM_COLS), F32),
                pltpu.VMEM((2, lc * NB, LANES), F32)]

    meta_blk = tr // nmeta_rows
    flat, state = pl.pallas_call(
        _s5_meta_kernel,
        grid=(1,),
        in_specs=[pl.BlockSpec((nmeta_rows, SSM_WIDTH), lambda i: (meta_blk, 0))] + par_specs,
        out_specs=[pl.BlockSpec((nmeta_rows, SSM_WIDTH), lambda i: (meta_blk, 0)),
                   pl.BlockSpec((NB, SSM_COLS), const)],
        out_shape=[jax.ShapeDtypeStruct((t, SSM_WIDTH), BF16),
                   jax.ShapeDtypeStruct((NB, SSM_COLS), F32)],
        scratch_shapes=scratch(N_META),
        compiler_params=_cp(("arbitrary",)),
        name="s5_meta",
    )(zu, a, bd, cd, dskip, wglu)

    nc = seq // S5_CHUNK
    u_specs = [pl.BlockSpec((S5_CHUNK, SSM_WIDTH), lambda c, b, bb=bb: (bb * nc + c, 0))
               for bb in range(NB)]
    n_in = NB + 1 + len(par_specs)
    return pl.pallas_call(
        _s5_real_kernel,
        grid=(nc, NB),
        in_specs=u_specs + [pl.BlockSpec((NB, SSM_COLS), const)] + par_specs
        + [pl.BlockSpec(memory_space=pl.ANY)],
        out_specs=pl.BlockSpec((S5_CHUNK, SSM_WIDTH), lambda c, b: (b * nc + c, 0)),
        out_shape=jax.ShapeDtypeStruct((t, SSM_WIDTH), BF16),
        scratch_shapes=scratch(S5_CHUNK),
        input_output_aliases={n_in: 0},
        compiler_params=_cp(("arbitrary", "arbitrary")),
        name="s5_real",
    )(*([zu] * NB), state, a, bd, cd, dskip, wglu, flat)


def _s5_params(lam_re, lam_im, log_dt, b_re, b_im, c_re, c_im, d_skip):
    dt = jnp.exp(log_dt)[:, None]
    mag = jnp.exp(lam_re * dt)
    ab_re = mag * jnp.cos(lam_im * dt)
    ab_im = mag * jnp.sin(lam_im * dt)
    den = lam_re * lam_re + lam_im * lam_im
    nr = ab_re - 1.0
    ni = ab_im
    coef_re = ((nr * lam_re + ni * lam_im) / den)[..., None]
    coef_im = ((ni * lam_re - nr * lam_im) / den)[..., None]
    bb_re = coef_re * b_re - coef_im * b_im
    bb_im = coef_re * b_im + coef_im * b_re
    eye = jnp.eye(SSM_GROUPS, dtype=F32)
    half = SSM_COLS // 2
    bd = jnp.concatenate(
        [jnp.einsum('gnc,gh->gchn', m, eye).reshape(SSM_WIDTH, half) for m in (bb_re, bb_im)],
        axis=1).astype(BF16)
    cd = jnp.concatenate(
        [jnp.einsum('gcn,gh->gnhc', m, eye).reshape(half, SSM_WIDTH) for m in (c_re, -c_im)],
        axis=0).astype(BF16)
    a = jnp.concatenate([ab_re.reshape(1, half), ab_im.reshape(1, half)], axis=1)
    a = jnp.broadcast_to(a, (NB, SSM_COLS))
    return a, bd, cd, d_skip.reshape(1, SSM_WIDTH)


def _nt_dot(a, b):
    return lax.dot_general(a, b, (((1,), (1,)), ((), ())), preferred_element_type=F32)


def _osm_init(s, v):
    m = jnp.max(s, axis=1, keepdims=True)
    p = jnp.exp(s - m)
    l = jnp.sum(p, axis=1, keepdims=True)
    acc = jnp.dot(p.astype(BF16), v, preferred_element_type=F32)
    return m, l, acc


def _causal_sweep(qs, k_ref, k_cols, vt_ref, vt_rows, kms, vmts, qi, m_ref, l_ref, acc_ref):
    n = len(qs)
    for i in range(n):
        s = _nt_dot(kms[i], qs[i])
        m = jnp.max(s, axis=0, keepdims=True)
        p = jnp.exp(s - m)
        m_ref[i][...] = m
        l_ref[i][...] = jnp.sum(p, axis=0, keepdims=True)
        acc_ref[i][...] = jnp.dot(vmts[i], p.astype(BF16), preferred_element_type=F32)

    sub = ATT_TK // LANES

    def chunk(j, masked):
        off = pl.multiple_of(j * ATT_TK, ATT_TK)
        if masked:
            krow = lax.broadcasted_iota(jnp.int32, (ATT_TK, ATT_TQ), 0)
            qcol = lax.broadcasted_iota(jnp.int32, (ATT_TK, ATT_TQ), 1)
            visible = krow <= qcol
        scores = []
        for i in range(n):
            kb = k_ref[pl.ds(off, ATT_TK), k_cols[i]]
            s = _nt_dot(kb, qs[i])
            scores.append(jnp.where(visible, s, MASK_VALUE) if masked else s)
        probs, alphas = [], []
        for i in range(n):
            m_old = m_ref[i][...]
            m_new = jnp.maximum(m_old, jnp.max(scores[i], axis=0, keepdims=True))
            alpha = jnp.exp(m_old - m_new)
            p = jnp.exp(scores[i] - m_new)
            l_ref[i][...] = alpha * l_ref[i][...] + jnp.sum(p, axis=0, keepdims=True)
            m_ref[i][...] = m_new
            probs.append(p.astype(BF16))
            alphas.append(alpha)
        for i in range(n):
            vtb = jnp.concatenate([vt_ref[j * sub + c, vt_rows[i], :] for c in range(sub)],
                                  axis=1)
            acc_ref[i][...] = alphas[i] * acc_ref[i][...] + jnp.dot(
                vtb, probs[i], preferred_element_type=F32)

    def body(j, carry):
        chunk(j, False)
        return carry

    lax.fori_loop(0, qi, body, 0)
    chunk(qi, True)


def _meta_attend(q, km, vm):
    n = km.shape[0]
    row = lax.broadcasted_iota(jnp.int32, (n, n), 0)
    col = lax.broadcasted_iota(jnp.int32, (n, n), 1)
    s = jnp.where(col <= row, _nt_dot(q, km), MASK_VALUE)
    _, l, acc = _osm_init(s, vm)
    return l, acc


def _diff_split(q):
    lane = lax.broadcasted_iota(jnp.int32, q.shape, 1)
    q = q * jnp.asarray(0.125, q.dtype)
    zero = jnp.zeros_like(q)
    return jnp.where(lane < 64, q, zero), jnp.where(lane >= 64, q, zero)


def _diff_finish(parts, lam, lam_init, g):
    (l1, acc1), (l2, acc2) = parts
    o = acc1 / l1 - lam * (acc2 / l2)
    return (_rms(o, g) * (1.0 - lam_init)).astype(BF16)


def _transpose_bf16(x):
    return x.astype(F32).T.astype(BF16)


def _map_scratch(n_maps, dv):
    return ([pltpu.VMEM((1, ATT_TQ), F32)] * (2 * n_maps)
            + [pltpu.VMEM((dv, ATT_TQ), F32)] * n_maps)


def _split_map_scratch(scratch, n_maps):
    return scratch[:n_maps], scratch[n_maps:2 * n_maps], scratch[2 * n_maps:]


def _diff_kernel(lam_init, lam_ref, q_ref, k_ref, vt_ref, km_ref, vm_ref, g_ref, o_ref,
                 *scratch):
    m_ref, l_ref, acc_ref = _split_map_scratch(scratch, 2 * DIFF_HEADS)
    qs, k_cols, vt_rows, kms, vmts = [], [], [], [], []
    for h in range(DIFF_HEADS):
        sl = slice(LANES * h, LANES * (h + 1))
        km = km_ref[:, sl]
        vmt = _transpose_bf16(vm_ref[:, sl])
        for q in _diff_split(q_ref[:, sl]):
            qs.append(q)
            k_cols.append(sl)
            vt_rows.append(sl)
            kms.append(km)
            vmts.append(vmt)
    _causal_sweep(qs, k_ref, k_cols, vt_ref, vt_rows, kms, vmts, pl.program_id(1),
                  m_ref, l_ref, acc_ref)
    lam = lam_ref[0]
    for h in range(DIFF_HEADS):
        o = (acc_ref[2 * h][...] / l_ref[2 * h][...]
             - lam * (acc_ref[2 * h + 1][...] / l_ref[2 * h + 1][...]))
        y = o * lax.rsqrt(jnp.mean(o * o, axis=0, keepdims=True) + EPS)
        o_ref[:, LANES * h:LANES * (h + 1)] = (
            y.T * g_ref[...] * (1.0 - lam_init)).astype(BF16)


def _diff_meta_kernel(lam_init, lam_ref, q_ref, km_ref, vm_ref, g_ref, _flat_ref, o_ref):
    q1, q2 = _diff_split(q_ref[...])
    km = km_ref[...]
    vm = vm_ref[...]
    parts = [_meta_attend(q1, km, vm), _meta_attend(q2, km, vm)]
    o_ref[...] = _diff_finish(parts, lam_ref[0], lam_init, g_ref[...])


def _diff_attention(za, vt, seq, lam, lam_init, g):
    t = za.shape[0]
    nq = seq // ATT_TQ
    mrow = NB * seq // N_META
    smem = pl.BlockSpec(memory_space=pltpu.SMEM)
    lam = lam.reshape(1).astype(F32)
    g = g.reshape(1, LANES)
    gspec = pl.BlockSpec((1, LANES), lambda *_: (0, 0))
    width = DIFF_HEADS * LANES
    n_maps = 2 * DIFF_HEADS
    flat = pl.pallas_call(
        functools.partial(_diff_kernel, lam_init),
        grid=(NB, nq),
        in_specs=[smem,
                  pl.BlockSpec((ATT_TQ, width), lambda b, i: (b * nq + i, 0)),
                  pl.BlockSpec((seq, width), lambda b, i: (b, 1)),
                  pl.BlockSpec((seq // LANES, width, LANES), lambda b, i: (b, 0, 0)),
                  pl.BlockSpec((N_META, width), lambda b, i: (mrow + b, 1)),
                  pl.BlockSpec((N_META, width), lambda b, i: (mrow + b, 2)),
                  gspec],
        out_specs=pl.BlockSpec((ATT_TQ, width), lambda b, i: (b * nq + i, 0)),
        out_shape=jax.ShapeDtypeStruct((t, width), BF16),
        scratch_shapes=_map_scratch(n_maps, LANES),
        compiler_params=_cp(("parallel", "arbitrary")),
        name="diff_attn",
    )(lam, za, za, vt, za, za, g)
    return pl.pallas_call(
        functools.partial(_diff_meta_kernel, lam_init),
        grid=(NB, DIFF_HEADS),
        in_specs=[smem,
                  pl.BlockSpec((N_META, LANES), lambda b, h: (mrow + b, h)),
                  pl.BlockSpec((N_META, LANES), lambda b, h: (mrow + b, 4 + h)),
                  pl.BlockSpec((N_META, LANES), lambda b, h: (mrow + b, 8 + h)),
                  gspec,
                  pl.BlockSpec(memory_space=pl.ANY)],
        out_specs=pl.BlockSpec((N_META, LANES), lambda b, h: (mrow + b, h)),
        out_shape=jax.ShapeDtypeStruct((t, DIFF_HEADS * LANES), BF16),
        input_output_aliases={5: 0},
        compiler_params=_cp(("parallel", "parallel")),
        name="diff_attn_meta",
    )(lam, za, za, za, g, flat)


def _split3(c):
    hi = c.astype(BF16).astype(F32)
    r1 = c - hi
    mid = r1.astype(BF16).astype(F32)
    lo = (r1 - mid).astype(BF16).astype(F32)
    return hi, mid, lo


def _cumsum_rows(tri, lf):
    parts = jnp.concatenate(_split3(lf), axis=1).astype(BF16)
    r = jnp.dot(tri, parts, preferred_element_type=F32)
    return r[:, :LANES] + r[:, LANES:2 * LANES] + r[:, 2 * LANES:]


def _log_sigmoid(x):
    return jnp.minimum(x, 0.0) - jnp.log1p(jnp.exp(-jnp.abs(x)))


def _fox_augment(fq, fk, cum, qa_ref, ka_ref):
    n = fq.shape[0]
    lane = lax.broadcasted_iota(jnp.int32, (n, LANES), 1)
    for h in range(FOX_HEADS):
        pair = slice(LANES * (h // 2), LANES * (h // 2) + LANES)
        own = (lane // 64) == (h % 2)
        e0 = 64 * (1 - h % 2)
        hi, mid, lo = _split3(jnp.broadcast_to(cum[:, h:h + 1], (n, LANES)))
        ones = (lane >= e0 + 3) & (lane < e0 + 6)
        q_extra = jnp.where(lane == e0, hi, jnp.where(lane == e0 + 1, mid, jnp.where(
            lane == e0 + 2, lo, jnp.where(ones, 1.0, 0.0))))
        ones = (lane >= e0) & (lane < e0 + 3)
        k_extra = jnp.where(lane == e0 + 3, -hi, jnp.where(lane == e0 + 4, -mid, jnp.where(
            lane == e0 + 5, -lo, jnp.where(ones, 1.0, 0.0))))
        q = fq[:, pair].astype(F32) * 0.125
        k = fk[:, pair].astype(F32)
        qa_ref[:, LANES * h:LANES * (h + 1)] = jnp.where(own, q, q_extra).astype(BF16)
        ka_ref[:, LANES * h:LANES * (h + 1)] = jnp.where(own, k, k_extra).astype(BF16)


def _fox_prep_meta_kernel(zg_ref, fq_ref, fk_ref, fb_ref, qa_ref, ka_ref, carry_ref):
    n = NB * N_META
    lf = _log_sigmoid(zg_ref[...] + fb_ref[...])
    row = lax.broadcasted_iota(jnp.int32, (n, n), 0)
    col = lax.broadcasted_iota(jnp.int32, (n, n), 1)
    tri = ((col <= row) & (col // N_META == row // N_META)).astype(BF16)
    cum = _cumsum_rows(tri, lf)
    brow = lax.broadcasted_iota(jnp.int32, (NB, n), 0)
    bcol = lax.broadcasted_iota(jnp.int32, (NB, n), 1)
    carry_ref[...] = _cumsum_rows((bcol // N_META == brow).astype(BF16), lf)
    _fox_augment(fq_ref[...], fk_ref[...], cum, qa_ref, ka_ref)


def _fox_prep_kernel(zg_ref, fq_ref, fk_ref, fb_ref, carry_in_ref, qa_ref, ka_ref, carry_ref):
    b = pl.program_id(0)
    n = zg_ref.shape[0]

    @pl.when(pl.program_id(1) == 0)
    def _():
        carry_ref[...] = carry_in_ref[pl.ds(b, 1), :]

    lf = _log_sigmoid(zg_ref[...] + fb_ref[...])
    row = lax.broadcasted_iota(jnp.int32, (n, n), 0)
    col = lax.broadcasted_iota(jnp.int32, (n, n), 1)
    cum = _cumsum_rows((col <= row).astype(BF16), lf) + carry_ref[...]
    carry_ref[...] = cum[n - 1:n, :]
    _fox_augment(fq_ref[...], fk_ref[...], cum, qa_ref, ka_ref)


FOX_PREP_TILE = 256


def _fox_prep(zg, za, seq, fb):
    tr = NB * seq
    nm = NB * N_META
    mblk = tr // nm
    fb = jnp.pad(fb.astype(F32), (0, LANES - FOX_HEADS)).reshape(1, LANES)
    fbspec = pl.BlockSpec((1, LANES), lambda *_: (0, 0))
    aug = FOX_HEADS * LANES
    qa_m, ka_m, carry = pl.pallas_call(
        _fox_prep_meta_kernel,
        grid=(1,),
        in_specs=[pl.BlockSpec((nm, LANES), lambda i: (mblk, 0)),
                  pl.BlockSpec((nm, 256), lambda i: (mblk, 6)),
                  pl.BlockSpec((nm, 256), lambda i: (mblk, 7)),
                  fbspec],
        out_specs=[pl.BlockSpec((nm, aug), lambda i: (0, 0)),
                   pl.BlockSpec((nm, aug), lambda i: (0, 0)),
                   pl.BlockSpec((NB, LANES), lambda i: (0, 0))],
        out_shape=[jax.ShapeDtypeStruct((nm, aug), BF16),
                   jax.ShapeDtypeStruct((nm, aug), BF16),
                   jax.ShapeDtypeStruct((NB, LANES), F32)],
        compiler_params=_cp(("arbitrary",)),
        name="fox_prep_meta",
    )(zg, za, za, fb)
    nc = seq // FOX_PREP_TILE
    qa, ka = pl.pallas_call(
        _fox_prep_kernel,
        grid=(NB, nc),
        in_specs=[pl.BlockSpec((FOX_PREP_TILE, LANES), lambda b, c: (b * nc + c, 0)),
                  pl.BlockSpec((FOX_PREP_TILE, 256), lambda b, c: (b * nc + c, 6)),
                  pl.BlockSpec((FOX_PREP_TILE, 256), lambda b, c: (b * nc + c, 7)),
                  fbspec,
                  pl.BlockSpec((NB, LANES), lambda b, c: (0, 0))],
        out_specs=[pl.BlockSpec((FOX_PREP_TILE, aug), lambda b, c: (b * nc + c, 0)),
                   pl.BlockSpec((FOX_PREP_TILE, aug), lambda b, c: (b * nc + c, 0))],
        out_shape=[jax.ShapeDtypeStruct((tr, aug), BF16),
                   jax.ShapeDtypeStruct((tr, aug), BF16)],
        scratch_shapes=[pltpu.VMEM((1, LANES), F32)],
        compiler_params=_cp(("parallel", "arbitrary")),
        name="fox_prep",
    )(zg, za, za, fb, carry)
    return qa, ka, qa_m, ka_m


def _fox_finish(parts):
    (l0, acc0), (l1, acc1) = parts
    lane = lax.broadcasted_iota(jnp.int32, acc0.shape, 1)
    return jnp.where(lane < 64, acc0 / l0, acc1 / l1).astype(BF16)


def _fox_kernel(q_ref, k_ref, vt_ref, km_ref, vm_ref, o_ref, *scratch):
    m_ref, l_ref, acc_ref = _split_map_scratch(scratch, FOX_HEADS)
    hd = 64
    vmt = _transpose_bf16(vm_ref[...])
    qs, k_cols, vt_rows, kms, vmts = [], [], [], [], []
    for h in range(FOX_HEADS):
        sl = slice(LANES * h, LANES * (h + 1))
        qs.append(q_ref[:, sl])
        k_cols.append(sl)
        vt_rows.append(slice(hd * h, hd * (h + 1)))
        kms.append(km_ref[:, sl])
        vmts.append(vmt[hd * h:hd * (h + 1), :])
    _causal_sweep(qs, k_ref, k_cols, vt_ref, vt_rows, kms, vmts, pl.program_id(1),
                  m_ref, l_ref, acc_ref)
    for p in range(FOX_HEADS // 2):
        o = jnp.concatenate([acc_ref[2 * p][...] / l_ref[2 * p][...],
                             acc_ref[2 * p + 1][...] / l_ref[2 * p + 1][...]],
                            axis=0)
        o_ref[:, LANES * p:LANES * (p + 1)] = o.T.astype(BF16)


def _fox_meta_kernel(q_ref, km_ref, vm_ref, _flat_ref, o_ref):
    vm = vm_ref[...]
    parts = [_meta_attend(q_ref[:, LANES * hh:LANES * (hh + 1)],
                          km_ref[:, LANES * hh:LANES * (hh + 1)], vm) for hh in range(2)]
    o_ref[...] = _fox_finish(parts)


def _fox_attention(za, vt, seq, qa, ka, qa_m, ka_m):
    t = za.shape[0]
    nq = seq // ATT_TQ
    mrow = NB * seq // N_META
    aug = FOX_HEADS * LANES
    flat = pl.pallas_call(
        _fox_kernel,
        grid=(NB, nq),
        in_specs=[pl.BlockSpec((ATT_TQ, aug), lambda b, i: (b * nq + i, 0)),
                  pl.BlockSpec((seq, aug), lambda b, i: (b, 0)),
                  pl.BlockSpec((seq // LANES, 256, LANES), lambda b, i: (b, 2, 0)),
                  pl.BlockSpec((N_META, aug), lambda b, i: (b, 0)),
                  pl.BlockSpec((N_META, 256), lambda b, i: (mrow + b, 8))],
        out_specs=pl.BlockSpec((ATT_TQ, 256), lambda b, i: (b * nq + i, 0)),
        out_shape=jax.ShapeDtypeStruct((t, 256), BF16),
        scratch_shapes=_map_scratch(FOX_HEADS, 64),
        compiler_params=_cp(("parallel", "arbitrary")),
        name="fox_attn",
    )(qa, ka, vt, ka_m, za)
    return pl.pallas_call(
        _fox_meta_kernel,
        grid=(NB, FOX_HEADS // 2),
        in_specs=[pl.BlockSpec((N_META, 256), lambda b, p: (b, p)),
                  pl.BlockSpec((N_META, 256), lambda b, p: (b, p)),
                  pl.BlockSpec((N_META, LANES), lambda b, p: (mrow + b, 16 + p)),
                  pl.BlockSpec(memory_space=pl.ANY)],
        out_specs=pl.BlockSpec((N_META, LANES), lambda b, p: (mrow + b, p)),
        out_shape=jax.ShapeDtypeStruct((t, 256), BF16),
        input_output_aliases={3: 0},
        compiler_params=_cp(("parallel", "parallel")),
        name="fox_attn_meta",
    )(qa_m, ka_m, za, flat)


def _router_kernel(res_ref, g_ref, wr_ref, idx_ref, gate_ref):
    h = _rms(res_ref[...], g_ref[...])
    logits = jnp.dot(h, wr_ref[...], preferred_element_type=F32,
                     precision=lax.Precision.HIGHEST)
    lane = lax.broadcasted_iota(jnp.int32, logits.shape, 1)
    logits = jnp.where(lane < N_EXPERTS, logits, -jnp.inf)
    m1 = jnp.max(logits, axis=1, keepdims=True)
    i1 = jnp.min(jnp.where(logits == m1, lane, LANES), axis=1, keepdims=True)
    rest = jnp.where(lane == i1, -jnp.inf, logits)
    m2 = jnp.max(rest, axis=1, keepdims=True)
    i2 = jnp.min(jnp.where(rest == m2, lane, LANES), axis=1, keepdims=True)
    e = jnp.exp(m2 - m1)
    g1 = 1.0 / (1.0 + e)
    g2 = e / (1.0 + e)
    idx_ref[...] = jnp.where(lane == 0, i1, jnp.where(lane == 1, i2, 0))
    gate_ref[...] = jnp.where(lane == 0, g1, jnp.where(lane == 1, g2, 0.0))


def _router(res, g, wr):
    t = res.shape[0]
    return pl.pallas_call(
        _router_kernel,
        grid=(t // TOK_TILE,),
        in_specs=[pl.BlockSpec((TOK_TILE, D_MODEL), lambda i: (i, 0)),
                  pl.BlockSpec((1, D_MODEL), lambda i: (0, 0)),
                  pl.BlockSpec((D_MODEL, LANES), lambda i: (0, 0))],
        out_specs=[pl.BlockSpec((TOK_TILE, LANES), lambda i: (i, 0)),
                   pl.BlockSpec((TOK_TILE, LANES), lambda i: (i, 0))],
        out_shape=[jax.ShapeDtypeStruct((t, LANES), jnp.int32),
                   jax.ShapeDtypeStruct((t, LANES), F32)],
        compiler_params=_cp(("parallel",)),
        name="router",
    )(res, g, wr)


def _gather_copy(src_hbm, row, dst_ref, r, sem):
    return pltpu.make_async_copy(src_hbm.at[pl.ds(row, 1), :], dst_ref.at[pl.ds(r, 1), :], sem)


def _expert_ffn_kernel(nf, te_ref, na_ref, src_ref, x_hbm, g_ref, w1_ref, w3_ref, w2_ref,
                       o_ref, xbuf_ref, h_ref, acc_ref, sem):
    r = pl.program_id(0)
    f = pl.program_id(1)
    n_act = na_ref[0]
    per_step = MOE_XROWS // nf

    def tile_copy(slot):
        return pltpu.make_async_copy(x_hbm.at[pl.ds(0, MOE_XROWS), :],
                                     xbuf_ref.at[slot, pl.ds(0, MOE_XROWS), :], sem.at[slot])

    def start_row(tile, slot, k):
        _gather_copy(x_hbm, src_ref[tile * MOE_XROWS + k], xbuf_ref.at[slot], k,
                     sem.at[slot]).start()

    @pl.when(r < n_act)
    def _():
        slot = lax.rem(r, 2)

        @pl.when((r == 0) & (f == 0))
        def _():
            def first(k, _):
                start_row(0, 0, k)
                return 0
            lax.fori_loop(0, MOE_XROWS, first, 0, unroll=8)

        @pl.when(f == 0)
        def _():
            tile_copy(slot).wait()
            h_ref[...] = _rms(xbuf_ref[slot, pl.ds(0, MOE_TILE), :], g_ref[...]).astype(BF16)
            acc_ref[...] = jnp.zeros_like(acc_ref)

        nxt = jnp.minimum(r + 1, pl.num_programs(0) - 1)
        for k in range(per_step):
            start_row(nxt, 1 - slot, f * per_step + k)

        h = h_ref[...]
        a = jnp.dot(h, w1_ref[...].astype(BF16), preferred_element_type=F32)
        b = jnp.dot(h, w3_ref[...].astype(BF16), preferred_element_type=F32)
        hh = (a * jax.nn.sigmoid(a) * b).astype(BF16)
        acc_ref[...] += jnp.dot(hh, w2_ref[...].astype(BF16), preferred_element_type=F32)

        @pl.when(f == nf - 1)
        def _():
            o_ref[...] = acc_ref[...]

        @pl.when((f == nf - 1) & (r == n_act - 1))
        def _():
            tile_copy(1 - slot).wait()


def _expert_ffn(tile_expert, n_active, src, x, g, w1, w3, w2):
    dff = w1.shape[2]
    nt = src.shape[0] // MOE_XROWS
    nf = dff // MOE_FCHUNK
    assert MOE_XROWS % nf == 0 and MOE_XROWS >= MOE_TILE

    def row(r, f, te, na, src):
        return jnp.minimum(r, na[0] - 1)

    def fch(r, f, te, na, src):
        return jnp.where(r < na[0], f, nf - 1)

    return pl.pallas_call(
        functools.partial(_expert_ffn_kernel, nf),
        grid_spec=pltpu.PrefetchScalarGridSpec(
            num_scalar_prefetch=3,
            grid=(nt, nf),
            in_specs=[
                pl.BlockSpec(memory_space=pl.ANY),
                pl.BlockSpec((1, D_MODEL), lambda r, f, te, na, src: (0, 0)),
                pl.BlockSpec((None, D_MODEL, MOE_FCHUNK),
                             lambda r, f, te, na, src: (te[r], 0, fch(r, f, te, na, src))),
                pl.BlockSpec((None, D_MODEL, MOE_FCHUNK),
                             lambda r, f, te, na, src: (te[r], 0, fch(r, f, te, na, src))),
                pl.BlockSpec((None, MOE_FCHUNK, D_MODEL),
                             lambda r, f, te, na, src: (te[r], fch(r, f, te, na, src), 0))],
            out_specs=pl.BlockSpec((MOE_TILE, D_MODEL),
                                   lambda r, f, te, na, src: (row(r, f, te, na, src), 0)),
            scratch_shapes=[pltpu.VMEM((2, MOE_XBUF_ROWS, D_MODEL), F32),
                            pltpu.VMEM((MOE_TILE, D_MODEL), BF16),
                            pltpu.VMEM((MOE_TILE, D_MODEL), F32),
                            pltpu.SemaphoreType.DMA((2,))]),
        out_shape=jax.ShapeDtypeStruct((nt * MOE_TILE, D_MODEL), F32),
        compiler_params=_cp(("arbitrary", "arbitrary")),
        name="expert_ffn",
    )(tile_expert, n_active, src, x, g, w1, w3, w2)


def _combine_kernel(tile, final, pos_ref, res_ref, gate_ref, gf_ref, y_hbm, o_ref, ybuf_ref, sem):
    i = pl.program_id(0)
    n = pl.num_programs(0)
    slot = lax.rem(i, 2)

    def issue_tile(step, s):
        base = step * tile

        def issue(r, _):
            for k in range(2):
                _gather_copy(y_hbm, pos_ref[2 * (base + r) + k], ybuf_ref.at[s, k], r,
                             sem.at[s]).start()
            return 0

        lax.fori_loop(0, tile, issue, 0, unroll=8)

    @pl.when(i == 0)
    def _():
        issue_tile(0, 0)

    @pl.when(i + 1 < n)
    def _():
        issue_tile(i + 1, 1 - slot)

    pltpu.make_async_copy(ybuf_ref.at[slot], ybuf_ref.at[slot], sem.at[slot]).wait()
    gate = gate_ref[...]
    out = (res_ref[...] + gate[:, 0:1] * ybuf_ref[slot, 0] + gate[:, 1:2] * ybuf_ref[slot, 1])
    o_ref[...] = _rms(out, gf_ref[...]) if final else out


def _combine(pos, res, gates, y, final_g=None):
    final = final_g is not None
    t = res.shape[0]
    n_rows = t - NB * N_META if final else t
    tile = 256 if final else COMBINE_TILE
    if not final:
        final_g = jnp.ones((1, D_MODEL), F32)
    return pl.pallas_call(
        functools.partial(_combine_kernel, tile, final),
        grid_spec=pltpu.PrefetchScalarGridSpec(
            num_scalar_prefetch=1,
            grid=(n_rows // tile,),
            in_specs=[pl.BlockSpec((tile, D_MODEL), lambda i, pos: (i, 0)),
                      pl.BlockSpec((tile, LANES), lambda i, pos: (i, 0)),
                      pl.BlockSpec((1, D_MODEL), lambda i, pos: (0, 0)),
                      pl.BlockSpec(memory_space=pl.ANY)],
            out_specs=pl.BlockSpec((tile, D_MODEL), lambda i, pos: (i, 0)),
            scratch_shapes=[pltpu.VMEM((2, 2, tile, D_MODEL), F32),
                            pltpu.SemaphoreType.DMA((2,))]),
        out_shape=jax.ShapeDtypeStruct((n_rows, D_MODEL), F32),
        compiler_params=_cp(("arbitrary",)),
        name="moe_combine",
    )(pos, res, gates, final_g, y)


def _moe(res, g, wr, w1, w3, w2, final_g=None):
    t = res.shape[0]
    wr = jnp.pad(wr.astype(F32), ((0, 0), (0, LANES - N_EXPERTS)))
    idx, gates = _router(res, g, wr)
    e_flat = idx[:, :2].reshape(-1)
    onehot = (e_flat[:, None] == jnp.arange(N_EXPERTS, dtype=jnp.int32)[None, :]).astype(jnp.int32)
    csum = jnp.cumsum(onehot, axis=0)
    rank = jnp.take_along_axis(csum, e_flat[:, None], axis=1)[:, 0] - 1
    counts = csum[-1]
    tiles = (counts + MOE_TILE - 1) // MOE_TILE
    tile_end = jnp.cumsum(tiles)
    starts = (tile_end - tiles) * MOE_TILE
    pos = (starts[e_flat] + rank).astype(jnp.int32)
    n_tiles = (2 * t + N_EXPERTS * (MOE_TILE - 1)) // MOE_TILE
    spos = (pos // MOE_TILE) * MOE_XROWS + pos % MOE_TILE
    src = jnp.zeros((n_tiles * MOE_XROWS,), jnp.int32).at[spos].set(
        jnp.arange(2 * t, dtype=jnp.int32) // 2, unique_indices=True)
    n_active = tile_end[-1:].astype(jnp.int32)
    tile_ids = jnp.minimum(jnp.arange(n_tiles, dtype=jnp.int32), n_active[0] - 1)
    tile_expert = jnp.sum(tile_ids[:, None] >= tile_end[None, :], axis=1).astype(jnp.int32)

    y = _expert_ffn(tile_expert, n_active, src, res, g, w1, w3, w2)
    return _combine(pos, res, gates, y, final_g)


def _final_norm_kernel(res_ref, g_ref, o_ref):
    o_ref[...] = _rms(res_ref[...], g_ref[...])


def _final_norm(res, g, n_rows):
    tile = 1024
    return pl.pallas_call(
        _final_norm_kernel,
        grid=(n_rows // tile,),
        in_specs=[pl.BlockSpec((tile, D_MODEL), lambda i: (i, 0)),
                  pl.BlockSpec((1, D_MODEL), lambda i: (0, 0))],
        out_specs=pl.BlockSpec((tile, D_MODEL), lambda i: (i, 0)),
        out_shape=jax.ShapeDtypeStruct((n_rows, D_MODEL), F32),
        compiler_params=_cp(("parallel",)),
        name="final_norm",
    )(res, g)


def kernel(x, meta_tokens, norm_mix_g, w_in, w_out, ssm_lambda_re, ssm_lambda_im, ssm_log_dt,
           ssm_b_re, ssm_b_im, ssm_c_re, ssm_c_im, ssm_d, ssm_w_glu, diff_lambda_q1,
           diff_lambda_k1, diff_lambda_q2, diff_lambda_k2, diff_subln_g, fox_forget_b,
           norm_ffn_g, dense_w1, dense_w3, dense_w2, moe_router, moe_w1, moe_w3, moe_w2,
           final_norm_g):
    bsz, seq, d = x.shape
    assert bsz == NB and d == D_MODEL and seq % ATT_TQ == 0
    depth = w_in.shape[0]
    tr = bsz * seq
    res = jnp.concatenate(
        [x.reshape(tr, d), jnp.tile(meta_tokens.astype(x.dtype), (bsz, 1))], axis=0)
    row = lambda v: v.reshape(1, -1).astype(F32)

    for l in range(depth):
        w = jnp.pad(w_in[l], ((0, 0), (0, IN_PAD - w_in.shape[2]))).astype(BF16)
        wvt = jnp.concatenate([w_in[l][:, 1280:1792], w_in[l][:, 2304:2560]], axis=1).T
        zu, za, zg, vt = _inproj(res, row(norm_mix_g[l]), w, wvt.astype(BF16))

        a, bd, cd, dskip = _s5_params(ssm_lambda_re[l], ssm_lambda_im[l], ssm_log_dt[l],
                                      ssm_b_re[l], ssm_b_im[l], ssm_c_re[l], ssm_c_im[l],
                                      ssm_d[l])
        ssm_out = _s5(zu, seq, a, bd, cd, dskip, ssm_w_glu[l].astype(BF16))

        lam_init = 0.8 - 0.6 * math.exp(-0.3 * l)
        lam = (jnp.exp(jnp.sum(diff_lambda_q1[l] * diff_lambda_k1[l]))
               - jnp.exp(jnp.sum(diff_lambda_q2[l] * diff_lambda_k2[l])) + lam_init)
        diff_out = _diff_attention(za, vt, seq, lam, lam_init, diff_subln_g[l])

        qa, ka, qa_m, ka_m = _fox_prep(zg, za, seq, fox_forget_b[l])
        fox_out = _fox_attention(za, vt, seq, qa, ka, qa_m, ka_m)

        res = _outproj(res, ssm_out, diff_out, fox_out, w_out[l].astype(BF16))

        if l % 2 == 0:
            res = _dense_ffn(res, row(norm_ffn_g[l]), dense_w1[l // 2].astype(BF16),
                             dense_w3[l // 2].astype(BF16), dense_w2[l // 2].astype(BF16))
        else:
            res = _moe(res, row(norm_ffn_g[l]), moe_router[l // 2], moe_w1[l // 2],
                       moe_w3[l // 2], moe_w2[l // 2],
                       final_g=row(final_norm_g) if l == depth - 1 else None)

    if depth % 2 == 1:
        res = _final_norm(res, row(final_norm_g), tr)
    return res.reshape(bsz, seq, d)
```

```python
import functools
import math

import jax
import jax.numpy as jnp
from jax import lax
from jax.experimental import pallas as pl
from jax.experimental.pallas import tpu as pltpu

F32 = jnp.float32
BF16 = jnp.bfloat16
EPS = 1e-6

D_MODEL = 1024
N_META = 16
NB = 8
SSM_WIDTH = 256
SSM_GROUPS = 16
SSM_GROUP = 16
SSM_STATE = 64
SSM_COLS = 2 * SSM_GROUPS * SSM_STATE
DIFF_HEADS = 4
FOX_HEADS = 4
ATT_COLS = 2304
IN_PAD = 2688
N_EXPERTS = 8
LANES = 128
MASK_VALUE = -1e30

VT_ROWS = 768
TOK_TILE = 688
IN_TILE = 384
FFN_TILE = 384
ATT_TQ = 256
ATT_TK = 256
S5_CHUNK = 128
MOE_TILE = 1024
MOE_FCHUNK = 512
MOE_XROWS = 1029
ROW_BLOCKS = D_MODEL // LANES
COMBINE_TILE = 384

VMEM_LIMIT = 56 * 1024 * 1024


def _cp(sem):
    return pltpu.CompilerParams(dimension_semantics=sem, vmem_limit_bytes=VMEM_LIMIT)


def _rms(x, g):
    return x * lax.rsqrt(jnp.mean(x * x, axis=-1, keepdims=True) + EPS) * g


def _inproj_kernel(res_ref, g_ref, w_ref, wvt_ref, zu_ref, za_ref, zg_ref, vt_ref):
    h = _rms(res_ref[...], g_ref[...]).astype(BF16)
    zu_ref[...] = jnp.dot(h, w_ref[:, 0:SSM_WIDTH], preferred_element_type=F32)
    for c in range(0, ATT_COLS, 256):
        za_ref[:, c:c + 256] = jnp.dot(
            h, w_ref[:, SSM_WIDTH + c:SSM_WIDTH + c + 256],
            preferred_element_type=F32).astype(BF16)
    zg_ref[...] = jnp.dot(h, w_ref[:, SSM_WIDTH + ATT_COLS:IN_PAD],
                          preferred_element_type=F32)
    vt = _nt_dot(wvt_ref[...], h).astype(BF16)
    for c in range(IN_TILE // LANES):
        vt_ref[c] = vt[:, LANES * c:LANES * (c + 1)]


def _inproj(res, g, w, wvt):
    t = res.shape[0]
    nblk = IN_TILE // LANES
    return pl.pallas_call(
        _inproj_kernel,
        grid=(t // IN_TILE,),
        in_specs=[pl.BlockSpec((IN_TILE, D_MODEL), lambda i: (i, 0)),
                  pl.BlockSpec((1, D_MODEL), lambda i: (0, 0)),
                  pl.BlockSpec((D_MODEL, IN_PAD), lambda i: (0, 0)),
                  pl.BlockSpec((VT_ROWS, D_MODEL), lambda i: (0, 0))],
        out_specs=[pl.BlockSpec((IN_TILE, SSM_WIDTH), lambda i: (i, 0)),
                   pl.BlockSpec((IN_TILE, ATT_COLS), lambda i: (i, 0)),
                   pl.BlockSpec((IN_TILE, LANES), lambda i: (i, 0)),
                   pl.BlockSpec((nblk, VT_ROWS, LANES), lambda i: (i, 0, 0))],
        out_shape=[jax.ShapeDtypeStruct((t, SSM_WIDTH), F32),
                   jax.ShapeDtypeStruct((t, ATT_COLS), BF16),
                   jax.ShapeDtypeStruct((t, LANES), F32),
                   jax.ShapeDtypeStruct((t // LANES, VT_ROWS, LANES), BF16)],
        compiler_params=_cp(("parallel",)),
        name="inproj",
    )(res, g, w, wvt)


def _store_row_tiles(ref, x):
    n = x.shape[0]
    for j in range(ROW_BLOCKS):
        ref[pl.ds(j, n, stride=ROW_BLOCKS), :] = x[:, LANES * j:LANES * (j + 1)]


def _load_row_tiles(ref, n):
    return jnp.concatenate(
        [ref[pl.ds(j, n, stride=ROW_BLOCKS), :] for j in range(ROW_BLOCKS)], axis=1)


def _outproj_kernel(res_ref, s_ref, d_ref, f_ref, w_ref, o_ref, *tiled_ref):
    mixed = jnp.concatenate([s_ref[...], d_ref[...], f_ref[...]], axis=1)
    out = res_ref[...] + jnp.dot(mixed, w_ref[...], preferred_element_type=F32)
    o_ref[...] = out
    if tiled_ref:
        _store_row_tiles(tiled_ref[0], out)


def _outproj(res, ssm, diff, fox, w, with_row_tiles):
    t = res.shape[0]
    out_specs = [pl.BlockSpec((TOK_TILE, D_MODEL), lambda i: (i, 0))]
    out_shape = [jax.ShapeDtypeStruct((t, D_MODEL), F32)]
    if with_row_tiles:
        out_specs.append(pl.BlockSpec((TOK_TILE * ROW_BLOCKS, LANES), lambda i: (i, 0)))
        out_shape.append(jax.ShapeDtypeStruct((t * ROW_BLOCKS, LANES), F32))
    return pl.pallas_call(
        _outproj_kernel,
        grid=(t // TOK_TILE,),
        in_specs=[pl.BlockSpec((TOK_TILE, D_MODEL), lambda i: (i, 0)),
                  pl.BlockSpec((TOK_TILE, 256), lambda i: (i, 0)),
                  pl.BlockSpec((TOK_TILE, 512), lambda i: (i, 0)),
                  pl.BlockSpec((TOK_TILE, 256), lambda i: (i, 0)),
                  pl.BlockSpec((D_MODEL, D_MODEL), lambda i: (0, 0))],
        out_specs=out_specs,
        out_shape=out_shape,
        compiler_params=_cp(("parallel",)),
        name="outproj",
    )(res, ssm, diff, fox, w)


def _dense_ffn_kernel(res_ref, g_ref, w1_ref, w3_ref, w2_ref, o_ref):
    x = res_ref[...]
    h = _rms(x, g_ref[...]).astype(BF16)
    a = jnp.dot(h, w1_ref[...], preferred_element_type=F32)
    b = jnp.dot(h, w3_ref[...], preferred_element_type=F32)
    hh = (a * jax.nn.sigmoid(a) * b).astype(BF16)
    o_ref[...] = x + jnp.dot(hh, w2_ref[...], preferred_element_type=F32)


def _dense_ffn(res, g, w1, w3, w2):
    t = res.shape[0]
    dff = w1.shape[1]
    once = pl.Buffered(1)
    return pl.pallas_call(
        _dense_ffn_kernel,
        grid=(t // FFN_TILE,),
        in_specs=[pl.BlockSpec((FFN_TILE, D_MODEL), lambda i: (i, 0)),
                  pl.BlockSpec((1, D_MODEL), lambda i: (0, 0)),
                  pl.BlockSpec((D_MODEL, dff), lambda i: (0, 0), pipeline_mode=once),
                  pl.BlockSpec((D_MODEL, dff), lambda i: (0, 0), pipeline_mode=once),
                  pl.BlockSpec((dff, D_MODEL), lambda i: (0, 0), pipeline_mode=once)],
        out_specs=pl.BlockSpec((FFN_TILE, D_MODEL), lambda i: (i, 0)),
        out_shape=jax.ShapeDtypeStruct((t, D_MODEL), F32),
        compiler_params=_cp(("parallel",)),
        name="dense_ffn",
    )(res, g, w1, w3, w2)


def _gelu_tanh(x):
    c = math.sqrt(2.0 / math.pi)
    return 0.5 * x * (1.0 + jnp.tanh(c * (x + 0.044715 * (x * x * x))))


def _s5_chunk(lc, get_u, state_ref, ut_ref, bu_ref, ot_ref,
              a_ref, bd_ref, cd_ref, dskip_ref, wglu_ref):
    for b in range(NB):
        ub = get_u(b)
        for s in range(2):
            ut_ref[s, pl.ds(b, lc, stride=NB), :] = ub[:, LANES * s:LANES * (s + 1)]
    u_tm = jnp.concatenate([ut_ref[0], ut_ref[1]], axis=1)
    bu_ref[...] = jnp.dot(u_tm.astype(BF16), bd_ref[...], preferred_element_type=F32)

    half = SSM_COLS // 2
    a_re = a_ref[:, :half]
    a_im = a_ref[:, half:]

    def step(t, x):
        x_re, x_im = x
        r = pl.multiple_of(t * NB, NB)
        cur = bu_ref[pl.ds(r, NB), :]
        n_re = a_re * x_re - a_im * x_im + cur[:, :half]
        n_im = a_re * x_im + a_im * x_re + cur[:, half:]
        bu_ref[pl.ds(r, NB), :] = jnp.concatenate([n_re, n_im], axis=1)
        return n_re, n_im

    x_re, x_im = lax.fori_loop(0, lc, step,
                               (state_ref[:, :half], state_ref[:, half:]), unroll=4)
    state_ref[...] = jnp.concatenate([x_re, x_im], axis=1)

    y = jnp.dot(bu_ref[...].astype(BF16), cd_ref[...], preferred_element_type=F32)
    y = _gelu_tanh(y + dskip_ref[...] * u_tm)
    g = jnp.dot(y.astype(BF16), wglu_ref[...], preferred_element_type=F32)
    o = g[:, :SSM_WIDTH] * jax.nn.sigmoid(g[:, SSM_WIDTH:])
    ot_ref[0] = o[:, :LANES]
    ot_ref[1] = o[:, LANES:]


def _s5_read_out(ot_ref, b, lc):
    return jnp.concatenate(
        [ot_ref[s, pl.ds(b, lc, stride=NB), :] for s in range(2)], axis=1)


def _s5_meta_kernel(u_ref, a_ref, bd_ref, cd_ref, dskip_ref, wglu_ref,
                    o_ref, state_out_ref, state_ref, ut_ref, bu_ref, ot_ref):
    state_ref[...] = jnp.zeros_like(state_ref)
    _s5_chunk(N_META, lambda b: u_ref[b * N_META:(b + 1) * N_META, :],
              state_ref, ut_ref, bu_ref, ot_ref, a_ref, bd_ref, cd_ref, dskip_ref, wglu_ref)
    for b in range(NB):
        o_ref[b * N_META:(b + 1) * N_META, :] = _s5_read_out(ot_ref, b, N_META).astype(BF16)
    state_out_ref[...] = state_ref[...]


def _s5_real_kernel(*refs):
    u_refs = refs[:NB]
    (state_in_ref, a_ref, bd_ref, cd_ref, dskip_ref, wglu_ref, _flat_ref,
     o_ref, state_ref, ut_ref, bu_ref, ot_ref) = refs[NB:]
    c = pl.program_id(0)
    b = pl.program_id(1)

    @pl.when((c == 0) & (b == 0))
    def _():
        state_ref[...] = state_in_ref[...]

    @pl.when(b == 0)
    def _():
        _s5_chunk(S5_CHUNK, lambda bb: u_refs[bb][...],
                  state_ref, ut_ref, bu_ref, ot_ref, a_ref, bd_ref, cd_ref, dskip_ref, wglu_ref)

    o_ref[...] = _s5_read_out(ot_ref, b, S5_CHUNK).astype(BF16)


def _s5(zu, seq, a, bd, cd, dskip, wglu):
    t = zu.shape[0]
    tr = NB * seq
    nmeta_rows = NB * N_META
    const = lambda *_: (0, 0)
    par_specs = [pl.BlockSpec((NB, SSM_COLS), const),
                 pl.BlockSpec((SSM_WIDTH, SSM_COLS), const),
                 pl.BlockSpec((SSM_COLS, SSM_WIDTH), const),
                 pl.BlockSpec((1, SSM_WIDTH), const),
                 pl.BlockSpec((SSM_WIDTH, 2 * SSM_WIDTH), const)]

    def scratch(lc):
        return [pltpu.VMEM((NB, SSM_COLS), F32),
                pltpu.VMEM((2, lc * NB, LANES), F32),
                pltpu.VMEM((lc * NB, SSM_COLS), F32),
                pltpu.VMEM((2, lc * NB, LANES), F32)]

    meta_blk = tr // nmeta_rows
    flat, state = pl.pallas_call(
        _s5_meta_kernel,
        grid=(1,),
        in_specs=[pl.BlockSpec((nmeta_rows, SSM_WIDTH), lambda i: (meta_blk, 0))] + par_specs,
        out_specs=[pl.BlockSpec((nmeta_rows, SSM_WIDTH), lambda i: (meta_blk, 0)),
                   pl.BlockSpec((NB, SSM_COLS), const)],
        out_shape=[jax.ShapeDtypeStruct((t, SSM_WIDTH), BF16),
                   jax.ShapeDtypeStruct((NB, SSM_COLS), F32)],
        scratch_shapes=scratch(N_META),
        compiler_params=_cp(("arbitrary",)),
        name="s5_meta",
    )(zu, a, bd, cd, dskip, wglu)

    nc = seq // S5_CHUNK
    u_specs = [pl.BlockSpec((S5_CHUNK, SSM_WIDTH), lambda c, b, bb=bb: (bb * nc + c, 0))
               for bb in range(NB)]
    n_in = NB + 1 + len(par_specs)
    return pl.pallas_call(
        _s5_real_kernel,
        grid=(nc, NB),
        in_specs=u_specs + [pl.BlockSpec((NB, SSM_COLS), const)] + par_specs
        + [pl.BlockSpec(memory_space=pl.ANY)],
        out_specs=pl.BlockSpec((S5_CHUNK, SSM_WIDTH), lambda c, b: (b * nc + c, 0)),
        out_shape=jax.ShapeDtypeStruct((t, SSM_WIDTH), BF16),
        scratch_shapes=scratch(S5_CHUNK),
        input_output_aliases={n_in: 0},
        compiler_params=_cp(("arbitrary", "arbitrary")),
        name="s5_real",
    )(*([zu] * NB), state, a, bd, cd, dskip, wglu, flat)


def _s5_params(lam_re, lam_im, log_dt, b_re, b_im, c_re, c_im, d_skip):
    dt = jnp.exp(log_dt)[:, None]
    mag = jnp.exp(lam_re * dt)
    ab_re = mag * jnp.cos(lam_im * dt)
    ab_im = mag * jnp.sin(lam_im * dt)
    den = lam_re * lam_re + lam_im * lam_im
    nr = ab_re - 1.0
    ni = ab_im
    coef_re = ((nr * lam_re + ni * lam_im) / den)[..., None]
    coef_im = ((ni * lam_re - nr * lam_im) / den)[..., None]
    bb_re = coef_re * b_re - coef_im * b_im
    bb_im = coef_re * b_im + coef_im * b_re
    eye = jnp.eye(SSM_GROUPS, dtype=F32)
    half = SSM_COLS // 2
    bd = jnp.concatenate(
        [jnp.einsum('gnc,gh->gchn', m, eye).reshape(SSM_WIDTH, half) for m in (bb_re, bb_im)],
        axis=1).astype(BF16)
    cd = jnp.concatenate(
        [jnp.einsum('gcn,gh->gnhc', m, eye).reshape(half, SSM_WIDTH) for m in (c_re, -c_im)],
        axis=0).astype(BF16)
    a = jnp.concatenate([ab_re.reshape(1, half), ab_im.reshape(1, half)], axis=1)
    a = jnp.broadcast_to(a, (NB, SSM_COLS))
    return a, bd, cd, d_skip.reshape(1, SSM_WIDTH)


def _nt_dot(a, b):
    return lax.dot_general(a, b, (((1,), (1,)), ((), ())), preferred_element_type=F32)


def _osm_init(s, v):
    m = jnp.max(s, axis=1, keepdims=True)
    p = jnp.exp(s - m)
    l = jnp.sum(p, axis=1, keepdims=True)
    acc = jnp.dot(p.astype(BF16), v, preferred_element_type=F32)
    return m, l, acc


def _causal_sweep(qs, k_ref, k_cols, vt_ref, vt_rows, kms, vmts, qi, m_ref, l_ref, acc_ref):
    n = len(qs)
    for i in range(n):
        s = _nt_dot(kms[i], qs[i])
        m = jnp.max(s, axis=0, keepdims=True)
        p = jnp.exp(s - m)
        m_ref[i][...] = m
        l_ref[i][...] = jnp.sum(p, axis=0, keepdims=True)
        acc_ref[i][...] = jnp.dot(vmts[i], p.astype(BF16), preferred_element_type=F32)

    sub = ATT_TK // LANES

    def chunk(j, masked):
        off = pl.multiple_of(j * ATT_TK, ATT_TK)
        if masked:
            krow = lax.broadcasted_iota(jnp.int32, (ATT_TK, ATT_TQ), 0)
            qcol = lax.broadcasted_iota(jnp.int32, (ATT_TK, ATT_TQ), 1)
            visible = krow <= qcol
        scores = []
        for i in range(n):
            kb = k_ref[pl.ds(off, ATT_TK), k_cols[i]]
            s = _nt_dot(kb, qs[i])
            scores.append(jnp.where(visible, s, MASK_VALUE) if masked else s)
        probs, alphas = [], []
        for i in range(n):
            m_old = m_ref[i][...]
            m_new = jnp.maximum(m_old, jnp.max(scores[i], axis=0, keepdims=True))
            alpha = jnp.exp(m_old - m_new)
            p = jnp.exp(scores[i] - m_new)
            l_ref[i][...] = alpha * l_ref[i][...] + jnp.sum(p, axis=0, keepdims=True)
            m_ref[i][...] = m_new
            probs.append(p.astype(BF16))
            alphas.append(alpha)
        for i in range(n):
            vtb = jnp.concatenate([vt_ref[j * sub + c, vt_rows[i], :] for c in range(sub)],
                                  axis=1)
            acc_ref[i][...] = alphas[i] * acc_ref[i][...] + jnp.dot(
                vtb, probs[i], preferred_element_type=F32)

    def body(j, carry):
        chunk(j, False)
        return carry

    lax.fori_loop(0, qi, body, 0)
    chunk(qi, True)


def _meta_attend(q, km, vm):
    n = km.shape[0]
    row = lax.broadcasted_iota(jnp.int32, (n, n), 0)
    col = lax.broadcasted_iota(jnp.int32, (n, n), 1)
    s = jnp.where(col <= row, _nt_dot(q, km), MASK_VALUE)
    _, l, acc = _osm_init(s, vm)
    return l, acc


def _diff_split(q):
    lane = lax.broadcasted_iota(jnp.int32, q.shape, 1)
    q = q * jnp.asarray(0.125, q.dtype)
    zero = jnp.zeros_like(q)
    return jnp.where(lane < 64, q, zero), jnp.where(lane >= 64, q, zero)


def _diff_finish(parts, lam, lam_init, g):
    (l1, acc1), (l2, acc2) = parts
    o = acc1 / l1 - lam * (acc2 / l2)
    return (_rms(o, g) * (1.0 - lam_init)).astype(BF16)


def _transpose_bf16(x):
    return x.astype(F32).T.astype(BF16)


def _map_scratch(n_maps, dv):
    return ([pltpu.VMEM((1, ATT_TQ), F32)] * (2 * n_maps)
            + [pltpu.VMEM((dv, ATT_TQ), F32)] * n_maps)


def _split_map_scratch(scratch, n_maps):
    return scratch[:n_maps], scratch[n_maps:2 * n_maps], scratch[2 * n_maps:]


def _diff_kernel(lam_init, lam_ref, q_ref, k_ref, vt_ref, km_ref, vm_ref, g_ref, o_ref,
                 *scratch):
    m_ref, l_ref, acc_ref = _split_map_scratch(scratch, 2 * DIFF_HEADS)
    qs, k_cols, vt_rows, kms, vmts = [], [], [], [], []
    for h in range(DIFF_HEADS):
        sl = slice(LANES * h, LANES * (h + 1))
        km = km_ref[:, sl]
        vmt = _transpose_bf16(vm_ref[:, sl])
        for q in _diff_split(q_ref[:, sl]):
            qs.append(q)
            k_cols.append(sl)
            vt_rows.append(sl)
            kms.append(km)
            vmts.append(vmt)
    _causal_sweep(qs, k_ref, k_cols, vt_ref, vt_rows, kms, vmts, pl.program_id(1),
                  m_ref, l_ref, acc_ref)
    lam = lam_ref[0]
    for h in range(DIFF_HEADS):
        o = (acc_ref[2 * h][...] / l_ref[2 * h][...]
             - lam * (acc_ref[2 * h + 1][...] / l_ref[2 * h + 1][...]))
        y = o * lax.rsqrt(jnp.mean(o * o, axis=0, keepdims=True) + EPS)
        o_ref[:, LANES * h:LANES * (h + 1)] = (
            y.T * g_ref[...] * (1.0 - lam_init)).astype(BF16)


def _diff_meta_kernel(lam_init, lam_ref, q_ref, km_ref, vm_ref, g_ref, _flat_ref, o_ref):
    q1, q2 = _diff_split(q_ref[...])
    km = km_ref[...]
    vm = vm_ref[...]
    parts = [_meta_attend(q1, km, vm), _meta_attend(q2, km, vm)]
    o_ref[...] = _diff_finish(parts, lam_ref[0], lam_init, g_ref[...])


def _diff_attention(za, vt, seq, lam, lam_init, g):
    t = za.shape[0]
    nq = seq // ATT_TQ
    mrow = NB * seq // N_META
    smem = pl.BlockSpec(memory_space=pltpu.SMEM)
    lam = lam.reshape(1).astype(F32)
    g = g.reshape(1, LANES)
    gspec = pl.BlockSpec((1, LANES), lambda *_: (0, 0))
    width = DIFF_HEADS * LANES
    n_maps = 2 * DIFF_HEADS
    flat = pl.pallas_call(
        functools.partial(_diff_kernel, lam_init),
        grid=(NB, nq),
        in_specs=[smem,
                  pl.BlockSpec((ATT_TQ, width), lambda b, i: (b * nq + i, 0)),
                  pl.BlockSpec((seq, width), lambda b, i: (b, 1)),
                  pl.BlockSpec((seq // LANES, width, LANES), lambda b, i: (b, 0, 0)),
                  pl.BlockSpec((N_META, width), lambda b, i: (mrow + b, 1)),
                  pl.BlockSpec((N_META, width), lambda b, i: (mrow + b, 2)),
                  gspec],
        out_specs=pl.BlockSpec((ATT_TQ, width), lambda b, i: (b * nq + i, 0)),
        out_shape=jax.ShapeDtypeStruct((t, width), BF16),
        scratch_shapes=_map_scratch(n_maps, LANES),
        compiler_params=_cp(("parallel", "arbitrary")),
        name="diff_attn",
    )(lam, za, za, vt, za, za, g)
    return pl.pallas_call(
        functools.partial(_diff_meta_kernel, lam_init),
        grid=(NB, DIFF_HEADS),
        in_specs=[smem,
                  pl.BlockSpec((N_META, LANES), lambda b, h: (mrow + b, h)),
                  pl.BlockSpec((N_META, LANES), lambda b, h: (mrow + b, 4 + h)),
                  pl.BlockSpec((N_META, LANES), lambda b, h: (mrow + b, 8 + h)),
                  gspec,
                  pl.BlockSpec(memory_space=pl.ANY)],
        out_specs=pl.BlockSpec((N_META, LANES), lambda b, h: (mrow + b, h)),
        out_shape=jax.ShapeDtypeStruct((t, DIFF_HEADS * LANES), BF16),
        input_output_aliases={5: 0},
        compiler_params=_cp(("parallel", "parallel")),
        name="diff_attn_meta",
    )(lam, za, za, za, g, flat)


def _split3(c):
    hi = c.astype(BF16).astype(F32)
    r1 = c - hi
    mid = r1.astype(BF16).astype(F32)
    lo = (r1 - mid).astype(BF16).astype(F32)
    return hi, mid, lo


def _cumsum_rows(tri, lf):
    parts = jnp.concatenate(_split3(lf), axis=1).astype(BF16)
    r = jnp.dot(tri, parts, preferred_element_type=F32)
    return r[:, :LANES] + r[:, LANES:2 * LANES] + r[:, 2 * LANES:]


def _log_sigmoid(x):
    return jnp.minimum(x, 0.0) - jnp.log1p(jnp.exp(-jnp.abs(x)))


def _fox_augment(fq, fk, cum, qa_ref, ka_ref):
    n = fq.shape[0]
    lane = lax.broadcasted_iota(jnp.int32, (n, LANES), 1)
    for h in range(FOX_HEADS):
        pair = slice(LANES * (h // 2), LANES * (h // 2) + LANES)
        own = (lane // 64) == (h % 2)
        e0 = 64 * (1 - h % 2)
        hi, mid, lo = _split3(jnp.broadcast_to(cum[:, h:h + 1], (n, LANES)))
        ones = (lane >= e0 + 3) & (lane < e0 + 6)
        q_extra = jnp.where(lane == e0, hi, jnp.where(lane == e0 + 1, mid, jnp.where(
            lane == e0 + 2, lo, jnp.where(ones, 1.0, 0.0))))
        ones = (lane >= e0) & (lane < e0 + 3)
        k_extra = jnp.where(lane == e0 + 3, -hi, jnp.where(lane == e0 + 4, -mid, jnp.where(
            lane == e0 + 5, -lo, jnp.where(ones, 1.0, 0.0))))
        q = fq[:, pair].astype(F32) * 0.125
        k = fk[:, pair].astype(F32)
        qa_ref[:, LANES * h:LANES * (h + 1)] = jnp.where(own, q, q_extra).astype(BF16)
        ka_ref[:, LANES * h:LANES * (h + 1)] = jnp.where(own, k, k_extra).astype(BF16)


def _fox_prep_meta_kernel(zg_ref, fq_ref, fk_ref, fb_ref, qa_ref, ka_ref, carry_ref):
    n = NB * N_META
    lf = _log_sigmoid(zg_ref[...] + fb_ref[...])
    row = lax.broadcasted_iota(jnp.int32, (n, n), 0)
    col = lax.broadcasted_iota(jnp.int32, (n, n), 1)
    tri = ((col <= row) & (col // N_META == row // N_META)).astype(BF16)
    cum = _cumsum_rows(tri, lf)
    brow = lax.broadcasted_iota(jnp.int32, (NB, n), 0)
    bcol = lax.broadcasted_iota(jnp.int32, (NB, n), 1)
    carry_ref[...] = _cumsum_rows((bcol // N_META == brow).astype(BF16), lf)
    _fox_augment(fq_ref[...], fk_ref[...], cum, qa_ref, ka_ref)


def _fox_prep_kernel(zg_ref, fq_ref, fk_ref, fb_ref, carry_in_ref, qa_ref, ka_ref, carry_ref):
    b = pl.program_id(0)
    n = zg_ref.shape[0]

    @pl.when(pl.program_id(1) == 0)
    def _():
        carry_ref[...] = carry_in_ref[pl.ds(b, 1), :]

    lf = _log_sigmoid(zg_ref[...] + fb_ref[...])
    row = lax.broadcasted_iota(jnp.int32, (n, n), 0)
    col = lax.broadcasted_iota(jnp.int32, (n, n), 1)
    cum = _cumsum_rows((col <= row).astype(BF16), lf) + carry_ref[...]
    carry_ref[...] = cum[n - 1:n, :]
    _fox_augment(fq_ref[...], fk_ref[...], cum, qa_ref, ka_ref)


FOX_PREP_TILE = 256


def _fox_prep(zg, za, seq, fb):
    tr = NB * seq
    nm = NB * N_META
    mblk = tr // nm
    fb = jnp.pad(fb.astype(F32), (0, LANES - FOX_HEADS)).reshape(1, LANES)
    fbspec = pl.BlockSpec((1, LANES), lambda *_: (0, 0))
    aug = FOX_HEADS * LANES
    qa_m, ka_m, carry = pl.pallas_call(
        _fox_prep_meta_kernel,
        grid=(1,),
        in_specs=[pl.BlockSpec((nm, LANES), lambda i: (mblk, 0)),
                  pl.BlockSpec((nm, 256), lambda i: (mblk, 6)),
                  pl.BlockSpec((nm, 256), lambda i: (mblk, 7)),
                  fbspec],
        out_specs=[pl.BlockSpec((nm, aug), lambda i: (0, 0)),
                   pl.BlockSpec((nm, aug), lambda i: (0, 0)),
                   pl.BlockSpec((NB, LANES), lambda i: (0, 0))],
        out_shape=[jax.ShapeDtypeStruct((nm, aug), BF16),
                   jax.ShapeDtypeStruct((nm, aug), BF16),
                   jax.ShapeDtypeStruct((NB, LANES), F32)],
        compiler_params=_cp(("arbitrary",)),
        name="fox_prep_meta",
    )(zg, za, za, fb)
    nc = seq // FOX_PREP_TILE
    qa, ka = pl.pallas_call(
        _fox_prep_kernel,
        grid=(NB, nc),
        in_specs=[pl.BlockSpec((FOX_PREP_TILE, LANES), lambda b, c: (b * nc + c, 0)),
                  pl.BlockSpec((FOX_PREP_TILE, 256), lambda b, c: (b * nc + c, 6)),
                  pl.BlockSpec((FOX_PREP_TILE, 256), lambda b, c: (b * nc + c, 7)),
                  fbspec,
                  pl.BlockSpec((NB, LANES), lambda b, c: (0, 0))],
        out_specs=[pl.BlockSpec((FOX_PREP_TILE, aug), lambda b, c: (b * nc + c, 0)),
                   pl.BlockSpec((FOX_PREP_TILE, aug), lambda b, c: (b * nc + c, 0))],
        out_shape=[jax.ShapeDtypeStruct((tr, aug), BF16),
                   jax.ShapeDtypeStruct((tr, aug), BF16)],
        scratch_shapes=[pltpu.VMEM((1, LANES), F32)],
        compiler_params=_cp(("parallel", "arbitrary")),
        name="fox_prep",
    )(zg, za, za, fb, carry)
    return qa, ka, qa_m, ka_m


def _fox_finish(parts):
    (l0, acc0), (l1, acc1) = parts
    lane = lax.broadcasted_iota(jnp.int32, acc0.shape, 1)
    return jnp.where(lane < 64, acc0 / l0, acc1 / l1).astype(BF16)


def _fox_kernel(q_ref, k_ref, vt_ref, km_ref, vm_ref, o_ref, *scratch):
    m_ref, l_ref, acc_ref = _split_map_scratch(scratch, FOX_HEADS)
    hd = 64
    vmt = _transpose_bf16(vm_ref[...])
    qs, k_cols, vt_rows, kms, vmts = [], [], [], [], []
    for h in range(FOX_HEADS):
        sl = slice(LANES * h, LANES * (h + 1))
        qs.append(q_ref[:, sl])
        k_cols.append(sl)
        vt_rows.append(slice(hd * h, hd * (h + 1)))
        kms.append(km_ref[:, sl])
        vmts.append(vmt[hd * h:hd * (h + 1), :])
    _causal_sweep(qs, k_ref, k_cols, vt_ref, vt_rows, kms, vmts, pl.program_id(1),
                  m_ref, l_ref, acc_ref)
    for p in range(FOX_HEADS // 2):
        o = jnp.concatenate([acc_ref[2 * p][...] / l_ref[2 * p][...],
                             acc_ref[2 * p + 1][...] / l_ref[2 * p + 1][...]],
                            axis=0)
        o_ref[:, LANES * p:LANES * (p + 1)] = o.T.astype(BF16)


def _fox_meta_kernel(q_ref, km_ref, vm_ref, _flat_ref, o_ref):
    vm = vm_ref[...]
    parts = [_meta_attend(q_ref[:, LANES * hh:LANES * (hh + 1)],
                          km_ref[:, LANES * hh:LANES * (hh + 1)], vm) for hh in range(2)]
    o_ref[...] = _fox_finish(parts)


def _fox_attention(za, vt, seq, qa, ka, qa_m, ka_m):
    t = za.shape[0]
    nq = seq // ATT_TQ
    mrow = NB * seq // N_META
    aug = FOX_HEADS * LANES
    flat = pl.pallas_call(
        _fox_kernel,
        grid=(NB, nq),
        in_specs=[pl.BlockSpec((ATT_TQ, aug), lambda b, i: (b * nq + i, 0)),
                  pl.BlockSpec((seq, aug), lambda b, i: (b, 0)),
                  pl.BlockSpec((seq // LANES, 256, LANES), lambda b, i: (b, 2, 0)),
                  pl.BlockSpec((N_META, aug), lambda b, i: (b, 0)),
                  pl.BlockSpec((N_META, 256), lambda b, i: (mrow + b, 8))],
        out_specs=pl.BlockSpec((ATT_TQ, 256), lambda b, i: (b * nq + i, 0)),
        out_shape=jax.ShapeDtypeStruct((t, 256), BF16),
        scratch_shapes=_map_scratch(FOX_HEADS, 64),
        compiler_params=_cp(("parallel", "arbitrary")),
        name="fox_attn",
    )(qa, ka, vt, ka_m, za)
    return pl.pallas_call(
        _fox_meta_kernel,
        grid=(NB, FOX_HEADS // 2),
        in_specs=[pl.BlockSpec((N_META, 256), lambda b, p: (b, p)),
                  pl.BlockSpec((N_META, 256), lambda b, p: (b, p)),
                  pl.BlockSpec((N_META, LANES), lambda b, p: (mrow + b, 16 + p)),
                  pl.BlockSpec(memory_space=pl.ANY)],
        out_specs=pl.BlockSpec((N_META, LANES), lambda b, p: (mrow + b, p)),
        out_shape=jax.ShapeDtypeStruct((t, 256), BF16),
        input_output_aliases={3: 0},
        compiler_params=_cp(("parallel", "parallel")),
        name="fox_attn_meta",
    )(qa_m, ka_m, za, flat)


def _router_kernel(res_ref, g_ref, wr_ref, idx_ref, gate_ref):
    h = _rms(res_ref[...], g_ref[...])
    logits = jnp.dot(h, wr_ref[...], preferred_element_type=F32,
                     precision=lax.Precision.HIGHEST)
    lane = lax.broadcasted_iota(jnp.int32, logits.shape, 1)
    logits = jnp.where(lane < N_EXPERTS, logits, -jnp.inf)
    m1 = jnp.max(logits, axis=1, keepdims=True)
    i1 = jnp.min(jnp.where(logits == m1, lane, LANES), axis=1, keepdims=True)
    rest = jnp.where(lane == i1, -jnp.inf, logits)
    m2 = jnp.max(rest, axis=1, keepdims=True)
    i2 = jnp.min(jnp.where(rest == m2, lane, LANES), axis=1, keepdims=True)
    e = jnp.exp(m2 - m1)
    g1 = 1.0 / (1.0 + e)
    g2 = e / (1.0 + e)
    idx_ref[...] = jnp.where(lane == 0, i1, jnp.where(lane == 1, i2, 0))
    gate_ref[...] = jnp.where(lane == 0, g1, jnp.where(lane == 1, g2, 0.0))


def _router(res, g, wr):
    t = res.shape[0]
    return pl.pallas_call(
        _router_kernel,
        grid=(t // TOK_TILE,),
        in_specs=[pl.BlockSpec((TOK_TILE, D_MODEL), lambda i: (i, 0)),
                  pl.BlockSpec((1, D_MODEL), lambda i: (0, 0)),
                  pl.BlockSpec((D_MODEL, LANES), lambda i: (0, 0))],
        out_specs=[pl.BlockSpec((TOK_TILE, LANES), lambda i: (i, 0)),
                   pl.BlockSpec((TOK_TILE, LANES), lambda i: (i, 0))],
        out_shape=[jax.ShapeDtypeStruct((t, LANES), jnp.int32),
                   jax.ShapeDtypeStruct((t, LANES), F32)],
        compiler_params=_cp(("parallel",)),
        name="router",
    )(res, g, wr)


def _gather_copy(src_hbm, first, dst_ref, r, sem):
    first = pl.multiple_of(first, ROW_BLOCKS)
    dst = pl.multiple_of(r * ROW_BLOCKS, ROW_BLOCKS)
    return pltpu.make_async_copy(src_hbm.at[pl.ds(first, ROW_BLOCKS), :],
                                 dst_ref.at[pl.ds(dst, ROW_BLOCKS), :], sem)


def _expert_ffn_kernel(nf, te_ref, na_ref, src_ref, x_hbm, g_ref, w1_ref, w3_ref, w2_ref,
                       o_ref, xbuf_ref, h_ref, acc_ref, sem):
    r = pl.program_id(0)
    f = pl.program_id(1)
    n_act = na_ref[0]
    per_step = MOE_XROWS // nf

    def tile_copy(slot):
        return pltpu.make_async_copy(x_hbm.at[pl.ds(0, MOE_XROWS * ROW_BLOCKS), :],
                                     xbuf_ref.at[slot], sem.at[slot])

    def start_row(tile, slot, k):
        _gather_copy(x_hbm, src_ref[tile * MOE_XROWS + k], xbuf_ref.at[slot], k,
                     sem.at[slot]).start()

    @pl.when(r < n_act)
    def _():
        slot = lax.rem(r, 2)

        @pl.when((r == 0) & (f == 0))
        def _():
            def first(k, _):
                start_row(0, 0, k)
                return 0
            lax.fori_loop(0, MOE_XROWS, first, 0, unroll=8)

        @pl.when(f == 0)
        def _():
            tile_copy(slot).wait()
            x = _load_row_tiles(xbuf_ref.at[slot], MOE_TILE)
            h_ref[...] = _rms(x, g_ref[...]).astype(BF16)
            acc_ref[...] = jnp.zeros_like(acc_ref)

        nxt = jnp.minimum(r + 1, pl.num_programs(0) - 1)
        for k in range(per_step):
            start_row(nxt, 1 - slot, f * per_step + k)

        h = h_ref[...]
        a = jnp.dot(h, w1_ref[...].astype(BF16), preferred_element_type=F32)
        b = jnp.dot(h, w3_ref[...].astype(BF16), preferred_element_type=F32)
        hh = (a * jax.nn.sigmoid(a) * b).astype(BF16)
        acc_ref[...] += jnp.dot(hh, w2_ref[...].astype(BF16), preferred_element_type=F32)

        @pl.when(f == nf - 1)
        def _():
            _store_row_tiles(o_ref, acc_ref[...])

        @pl.when((f == nf - 1) & (r == n_act - 1))
        def _():
            tile_copy(1 - slot).wait()


def _expert_ffn(tile_expert, n_active, src, x, g, w1, w3, w2):
    dff = w1.shape[2]
    nt = src.shape[0] // MOE_XROWS
    nf = dff // MOE_FCHUNK
    assert MOE_XROWS % nf == 0 and MOE_XROWS >= MOE_TILE

    def row(r, f, te, na, src):
        return jnp.minimum(r, na[0] - 1)

    def fch(r, f, te, na, src):
        return jnp.where(r < na[0], f, nf - 1)

    return pl.pallas_call(
        functools.partial(_expert_ffn_kernel, nf),
        grid_spec=pltpu.PrefetchScalarGridSpec(
            num_scalar_prefetch=3,
            grid=(nt, nf),
            in_specs=[
                pl.BlockSpec(memory_space=pl.ANY),
                pl.BlockSpec((1, D_MODEL), lambda r, f, te, na, src: (0, 0)),
                pl.BlockSpec((None, D_MODEL, MOE_FCHUNK),
                             lambda r, f, te, na, src: (te[r], 0, fch(r, f, te, na, src))),
                pl.BlockSpec((None, D_MODEL, MOE_FCHUNK),
                             lambda r, f, te, na, src: (te[r], 0, fch(r, f, te, na, src))),
                pl.BlockSpec((None, MOE_FCHUNK, D_MODEL),
                             lambda r, f, te, na, src: (te[r], fch(r, f, te, na, src), 0))],
            out_specs=pl.BlockSpec((MOE_TILE * ROW_BLOCKS, LANES),
                                   lambda r, f, te, na, src: (row(r, f, te, na, src), 0)),
            scratch_shapes=[pltpu.VMEM((2, MOE_XROWS * ROW_BLOCKS, LANES), F32),
                            pltpu.VMEM((MOE_TILE, D_MODEL), BF16),
                            pltpu.VMEM((MOE_TILE, D_MODEL), F32),
                            pltpu.SemaphoreType.DMA((2,))]),
        out_shape=jax.ShapeDtypeStruct((nt * MOE_TILE * ROW_BLOCKS, LANES), F32),
        compiler_params=_cp(("arbitrary", "arbitrary")),
        name="expert_ffn",
    )(tile_expert, n_active, src, x, g, w1, w3, w2)


def _combine_kernel(tile, final, pos_ref, res_ref, gate_ref, gf_ref, y_hbm, o_ref, ybuf_ref, sem):
    i = pl.program_id(0)
    n = pl.num_programs(0)
    slot = lax.rem(i, 2)

    def issue_tile(step, s):
        base = step * tile

        def issue(r, _):
            for k in range(2):
                _gather_copy(y_hbm, pos_ref[2 * (base + r) + k], ybuf_ref.at[s, k], r,
                             sem.at[s]).start()
            return 0

        lax.fori_loop(0, tile, issue, 0, unroll=8)

    @pl.when(i == 0)
    def _():
        issue_tile(0, 0)

    @pl.when(i + 1 < n)
    def _():
        issue_tile(i + 1, 1 - slot)

    pltpu.make_async_copy(ybuf_ref.at[slot], ybuf_ref.at[slot], sem.at[slot]).wait()
    gate = gate_ref[...]
    out = (res_ref[...] + gate[:, 0:1] * _load_row_tiles(ybuf_ref.at[slot, 0], tile)
           + gate[:, 1:2] * _load_row_tiles(ybuf_ref.at[slot, 1], tile))
    o_ref[...] = _rms(out, gf_ref[...]) if final else out


def _combine(pos, res, gates, y, final_g=None):
    final = final_g is not None
    t = res.shape[0]
    n_rows = t - NB * N_META if final else t
    tile = 256 if final else COMBINE_TILE
    if not final:
        final_g = jnp.ones((1, D_MODEL), F32)
    return pl.pallas_call(
        functools.partial(_combine_kernel, tile, final),
        grid_spec=pltpu.PrefetchScalarGridSpec(
            num_scalar_prefetch=1,
            grid=(n_rows // tile,),
            in_specs=[pl.BlockSpec((tile, D_MODEL), lambda i, pos: (i, 0)),
                      pl.BlockSpec((tile, LANES), lambda i, pos: (i, 0)),
                      pl.BlockSpec((1, D_MODEL), lambda i, pos: (0, 0)),
                      pl.BlockSpec(memory_space=pl.ANY)],
            out_specs=pl.BlockSpec((tile, D_MODEL), lambda i, pos: (i, 0)),
            scratch_shapes=[pltpu.VMEM((2, 2, tile * ROW_BLOCKS, LANES), F32),
                            pltpu.SemaphoreType.DMA((2,))]),
        out_shape=jax.ShapeDtypeStruct((n_rows, D_MODEL), F32),
        compiler_params=_cp(("arbitrary",)),
        name="moe_combine",
    )(pos, res, gates, final_g, y)


def _moe(res, res_tiles, g, wr, w1, w3, w2, final_g=None):
    t = res.shape[0]
    wr = jnp.pad(wr.astype(F32), ((0, 0), (0, LANES - N_EXPERTS)))
    idx, gates = _router(res, g, wr)
    e_flat = idx[:, :2].reshape(-1)
    onehot = (e_flat[:, None] == jnp.arange(N_EXPERTS, dtype=jnp.int32)[None, :]).astype(jnp.int32)
    csum = jnp.cumsum(onehot, axis=0)
    rank = jnp.take_along_axis(csum, e_flat[:, None], axis=1)[:, 0] - 1
    counts = csum[-1]
    tiles = (counts + MOE_TILE - 1) // MOE_TILE
    tile_end = jnp.cumsum(tiles)
    starts = (tile_end - tiles) * MOE_TILE
    pos = (starts[e_flat] + rank).astype(jnp.int32)
    n_tiles = (2 * t + N_EXPERTS * (MOE_TILE - 1)) // MOE_TILE
    spos = (pos // MOE_TILE) * MOE_XROWS + pos % MOE_TILE
    src = jnp.zeros((n_tiles * MOE_XROWS,), jnp.int32).at[spos].set(
        (jnp.arange(2 * t, dtype=jnp.int32) // 2) * ROW_BLOCKS, unique_indices=True)
    n_active = tile_end[-1:].astype(jnp.int32)
    tile_ids = jnp.minimum(jnp.arange(n_tiles, dtype=jnp.int32), n_active[0] - 1)
    tile_expert = jnp.sum(tile_ids[:, None] >= tile_end[None, :], axis=1).astype(jnp.int32)

    y = _expert_ffn(tile_expert, n_active, src, res_tiles, g, w1, w3, w2)
    return _combine(pos * ROW_BLOCKS, res, gates, y, final_g)


def _final_norm_kernel(res_ref, g_ref, o_ref):
    o_ref[...] = _rms(res_ref[...], g_ref[...])


def _final_norm(res, g, n_rows):
    tile = 1024
    return pl.pallas_call(
        _final_norm_kernel,
        grid=(n_rows // tile,),
        in_specs=[pl.BlockSpec((tile, D_MODEL), lambda i: (i, 0)),
                  pl.BlockSpec((1, D_MODEL), lambda i: (0, 0))],
        out_specs=pl.BlockSpec((tile, D_MODEL), lambda i: (i, 0)),
        out_shape=jax.ShapeDtypeStruct((n_rows, D_MODEL), F32),
        compiler_params=_cp(("parallel",)),
        name="final_norm",
    )(res, g)


def kernel(x, meta_tokens, norm_mix_g, w_in, w_out, ssm_lambda_re, ssm_lambda_im, ssm_log_dt,
           ssm_b_re, ssm_b_im, ssm_c_re, ssm_c_im, ssm_d, ssm_w_glu, diff_lambda_q1,
           diff_lambda_k1, diff_lambda_q2, diff_lambda_k2, diff_subln_g, fox_forget_b,
           norm_ffn_g, dense_w1, dense_w3, dense_w2, moe_router, moe_w1, moe_w3, moe_w2,
           final_norm_g):
    bsz, seq, d = x.shape
    assert bsz == NB and d == D_MODEL and seq % ATT_TQ == 0
    depth = w_in.shape[0]
    tr = bsz * seq
    res = jnp.concatenate(
        [x.reshape(tr, d), jnp.tile(meta_tokens.astype(x.dtype), (bsz, 1))], axis=0)
    row = lambda v: v.reshape(1, -1).astype(F32)

    for l in range(depth):
        w = jnp.pad(w_in[l], ((0, 0), (0, IN_PAD - w_in.shape[2]))).astype(BF16)
        wvt = jnp.concatenate([w_in[l][:, 1280:1792], w_in[l][:, 2304:2560]], axis=1).T
        zu, za, zg, vt = _inproj(res, row(norm_mix_g[l]), w, wvt.astype(BF16))

        a, bd, cd, dskip = _s5_params(ssm_lambda_re[l], ssm_lambda_im[l], ssm_log_dt[l],
                                      ssm_b_re[l], ssm_b_im[l], ssm_c_re[l], ssm_c_im[l],
                                      ssm_d[l])
        ssm_out = _s5(zu, seq, a, bd, cd, dskip, ssm_w_glu[l].astype(BF16))

        lam_init = 0.8 - 0.6 * math.exp(-0.3 * l)
        lam = (jnp.exp(jnp.sum(diff_lambda_q1[l] * diff_lambda_k1[l]))
               - jnp.exp(jnp.sum(diff_lambda_q2[l] * diff_lambda_k2[l])) + lam_init)
        diff_out = _diff_attention(za, vt, seq, lam, lam_init, diff_subln_g[l])

        qa, ka, qa_m, ka_m = _fox_prep(zg, za, seq, fox_forget_b[l])
        fox_out = _fox_attention(za, vt, seq, qa, ka, qa_m, ka_m)

        is_moe = l % 2 == 1
        res, *res_tiles = _outproj(res, ssm_out, diff_out, fox_out, w_out[l].astype(BF16),
                                   with_row_tiles=is_moe)

        if not is_moe:
            res = _dense_ffn(res, row(norm_ffn_g[l]), dense_w1[l // 2].astype(BF16),
                             dense_w3[l // 2].astype(BF16), dense_w2[l // 2].astype(BF16))
        else:
            res = _moe(res, res_tiles[0], row(norm_ffn_g[l]), moe_router[l // 2], moe_w1[l // 2],
                       moe_w3[l // 2], moe_w2[l // 2],
                       final_g=row(final_norm_g) if l == depth - 1 else None)

    if depth % 2 == 1:
        res = _final_norm(res, row(final_norm_g), tr)
    return res.reshape(bsz, seq, d)
```

```python
import functools
import math

import jax
import jax.numpy as jnp
from jax import lax
from jax.experimental import pallas as pl
from jax.experimental.pallas import tpu as pltpu

F32 = jnp.float32
BF16 = jnp.bfloat16
EPS = 1e-6

D_MODEL = 1024
N_META = 16
NB = 8
SSM_WIDTH = 256
SSM_GROUPS = 16
SSM_GROUP = 16
SSM_STATE = 64
SSM_COLS = 2 * SSM_GROUPS * SSM_STATE
DIFF_HEADS = 4
FOX_HEADS = 4
ATT_COLS = 2304
IN_PAD = 2688
N_EXPERTS = 8
LANES = 128
MASK_VALUE = -1e30

VT_ROWS = 768
TOK_TILE = 688
IN_TILE = 384
FFN_TILE = 384
ATT_TQ = 256
ATT_TK = 256
S5_CHUNK = 128
MOE_TILE = 1024
MOE_FCHUNK = 512
MOE_XROWS = 1029
ROW_BLOCKS = D_MODEL // LANES
COMBINE_TILE = 384

VMEM_LIMIT = 56 * 1024 * 1024


def _cp(sem):
    return pltpu.CompilerParams(dimension_semantics=sem, vmem_limit_bytes=VMEM_LIMIT)


def _rms(x, g):
    return x * lax.rsqrt(jnp.mean(x * x, axis=-1, keepdims=True) + EPS) * g


def _inproj_kernel(res_ref, g_ref, w_ref, wvt_ref, zu_ref, za_ref, zg_ref, vt_ref):
    h = _rms(res_ref[...], g_ref[...]).astype(BF16)
    zu_ref[...] = jnp.dot(h, w_ref[:, 0:SSM_WIDTH], preferred_element_type=F32)
    for c in range(0, ATT_COLS, 256):
        za_ref[:, c:c + 256] = jnp.dot(
            h, w_ref[:, SSM_WIDTH + c:SSM_WIDTH + c + 256],
            preferred_element_type=F32).astype(BF16)
    zg_ref[...] = jnp.dot(h, w_ref[:, SSM_WIDTH + ATT_COLS:IN_PAD],
                          preferred_element_type=F32)
    vt = _nt_dot(wvt_ref[...], h).astype(BF16)
    for c in range(IN_TILE // LANES):
        vt_ref[c] = vt[:, LANES * c:LANES * (c + 1)]


def _inproj(res, g, w, wvt):
    t = res.shape[0]
    nblk = IN_TILE // LANES
    return pl.pallas_call(
        _inproj_kernel,
        grid=(t // IN_TILE,),
        in_specs=[pl.BlockSpec((IN_TILE, D_MODEL), lambda i: (i, 0)),
                  pl.BlockSpec((1, D_MODEL), lambda i: (0, 0)),
                  pl.BlockSpec((D_MODEL, IN_PAD), lambda i: (0, 0)),
                  pl.BlockSpec((VT_ROWS, D_MODEL), lambda i: (0, 0))],
        out_specs=[pl.BlockSpec((IN_TILE, SSM_WIDTH), lambda i: (i, 0)),
                   pl.BlockSpec((IN_TILE, ATT_COLS), lambda i: (i, 0)),
                   pl.BlockSpec((IN_TILE, LANES), lambda i: (i, 0)),
                   pl.BlockSpec((nblk, VT_ROWS, LANES), lambda i: (i, 0, 0))],
        out_shape=[jax.ShapeDtypeStruct((t, SSM_WIDTH), F32),
                   jax.ShapeDtypeStruct((t, ATT_COLS), BF16),
                   jax.ShapeDtypeStruct((t, LANES), F32),
                   jax.ShapeDtypeStruct((t // LANES, VT_ROWS, LANES), BF16)],
        compiler_params=_cp(("parallel",)),
        name="inproj",
    )(res, g, w, wvt)


def _store_row_tiles(ref, x):
    n = x.shape[0]
    for j in range(ROW_BLOCKS):
        ref[pl.ds(j, n, stride=ROW_BLOCKS), :] = x[:, LANES * j:LANES * (j + 1)]


def _load_row_tiles(ref, n):
    return jnp.concatenate(
        [ref[pl.ds(j, n, stride=ROW_BLOCKS), :] for j in range(ROW_BLOCKS)], axis=1)


def _outproj_kernel(res_ref, s_ref, d_ref, f_ref, w_ref, o_ref, *tiled_ref):
    mixed = jnp.concatenate([s_ref[...], d_ref[...], f_ref[...]], axis=1)
    out = res_ref[...] + jnp.dot(mixed, w_ref[...], preferred_element_type=F32)
    o_ref[...] = out
    if tiled_ref:
        _store_row_tiles(tiled_ref[0], out)


def _outproj(res, ssm, diff, fox, w, with_row_tiles):
    t = res.shape[0]
    out_specs = [pl.BlockSpec((TOK_TILE, D_MODEL), lambda i: (i, 0))]
    out_shape = [jax.ShapeDtypeStruct((t, D_MODEL), F32)]
    if with_row_tiles:
        out_specs.append(pl.BlockSpec((TOK_TILE * ROW_BLOCKS, LANES), lambda i: (i, 0)))
        out_shape.append(jax.ShapeDtypeStruct((t * ROW_BLOCKS, LANES), F32))
    return pl.pallas_call(
        _outproj_kernel,
        grid=(t // TOK_TILE,),
        in_specs=[pl.BlockSpec((TOK_TILE, D_MODEL), lambda i: (i, 0)),
                  pl.BlockSpec((TOK_TILE, 256), lambda i: (i, 0)),
                  pl.BlockSpec((TOK_TILE, 512), lambda i: (i, 0)),
                  pl.BlockSpec((TOK_TILE, 256), lambda i: (i, 0)),
                  pl.BlockSpec((D_MODEL, D_MODEL), lambda i: (0, 0))],
        out_specs=out_specs,
        out_shape=out_shape,
        compiler_params=_cp(("parallel",)),
        name="outproj",
    )(res, ssm, diff, fox, w)


def _dense_ffn_kernel(res_ref, g_ref, w1_ref, w3_ref, w2_ref, o_ref):
    x = res_ref[...]
    h = _rms(x, g_ref[...]).astype(BF16)
    a = jnp.dot(h, w1_ref[...], preferred_element_type=F32)
    b = jnp.dot(h, w3_ref[...], preferred_element_type=F32)
    hh = (a * jax.nn.sigmoid(a) * b).astype(BF16)
    o_ref[...] = x + jnp.dot(hh, w2_ref[...], preferred_element_type=F32)


def _dense_ffn(res, g, w1, w3, w2):
    t = res.shape[0]
    dff = w1.shape[1]
    once = pl.Buffered(1)
    return pl.pallas_call(
        _dense_ffn_kernel,
        grid=(t // FFN_TILE,),
        in_specs=[pl.BlockSpec((FFN_TILE, D_MODEL), lambda i: (i, 0)),
                  pl.BlockSpec((1, D_MODEL), lambda i: (0, 0)),
                  pl.BlockSpec((D_MODEL, dff), lambda i: (0, 0), pipeline_mode=once),
                  pl.BlockSpec((D_MODEL, dff), lambda i: (0, 0), pipeline_mode=once),
                  pl.BlockSpec((dff, D_MODEL), lambda i: (0, 0), pipeline_mode=once)],
        out_specs=pl.BlockSpec((FFN_TILE, D_MODEL), lambda i: (i, 0)),
        out_shape=jax.ShapeDtypeStruct((t, D_MODEL), F32),
        compiler_params=_cp(("parallel",)),
        name="dense_ffn",
    )(res, g, w1, w3, w2)


def _gelu_tanh(x):
    c = math.sqrt(2.0 / math.pi)
    return 0.5 * x * (1.0 + jnp.tanh(c * (x + 0.044715 * (x * x * x))))


def _s5_chunk(lc, get_u, state_ref, ut_ref, bu_ref, ot_ref,
              a_ref, bd_ref, cd_ref, dskip_ref, wglu_ref):
    for b in range(NB):
        ub = get_u(b)
        for s in range(2):
            ut_ref[s, pl.ds(b, lc, stride=NB), :] = ub[:, LANES * s:LANES * (s + 1)]
    u_tm = jnp.concatenate([ut_ref[0], ut_ref[1]], axis=1)
    bu_ref[...] = jnp.dot(u_tm.astype(BF16), bd_ref[...], preferred_element_type=F32)

    half = SSM_COLS // 2
    a_re = a_ref[:, :half]
    a_im = a_ref[:, half:]

    def step(t, x):
        x_re, x_im = x
        r = pl.multiple_of(t * NB, NB)
        cur = bu_ref[pl.ds(r, NB), :]
        n_re = a_re * x_re - a_im * x_im + cur[:, :half]
        n_im = a_re * x_im + a_im * x_re + cur[:, half:]
        bu_ref[pl.ds(r, NB), :] = jnp.concatenate([n_re, n_im], axis=1)
        return n_re, n_im

    x_re, x_im = lax.fori_loop(0, lc, step,
                               (state_ref[:, :half], state_ref[:, half:]), unroll=4)
    state_ref[...] = jnp.concatenate([x_re, x_im], axis=1)

    y = jnp.dot(bu_ref[...].astype(BF16), cd_ref[...], preferred_element_type=F32)
    y = _gelu_tanh(y + dskip_ref[...] * u_tm)
    g = jnp.dot(y.astype(BF16), wglu_ref[...], preferred_element_type=F32)
    o = g[:, :SSM_WIDTH] * jax.nn.sigmoid(g[:, SSM_WIDTH:])
    ot_ref[0] = o[:, :LANES]
    ot_ref[1] = o[:, LANES:]


def _s5_read_out(ot_ref, b, lc):
    return jnp.concatenate(
        [ot_ref[s, pl.ds(b, lc, stride=NB), :] for s in range(2)], axis=1)


def _s5_meta_kernel(u_ref, a_ref, bd_ref, cd_ref, dskip_ref, wglu_ref,
                    o_ref, state_out_ref, state_ref, ut_ref, bu_ref, ot_ref):
    state_ref[...] = jnp.zeros_like(state_ref)
    _s5_chunk(N_META, lambda b: u_ref[b * N_META:(b + 1) * N_META, :],
              state_ref, ut_ref, bu_ref, ot_ref, a_ref, bd_ref, cd_ref, dskip_ref, wglu_ref)
    for b in range(NB):
        o_ref[b * N_META:(b + 1) * N_META, :] = _s5_read_out(ot_ref, b, N_META).astype(BF16)
    state_out_ref[...] = state_ref[...]


def _s5_real_kernel(*refs):
    u_refs = refs[:NB]
    (state_in_ref, a_ref, bd_ref, cd_ref, dskip_ref, wglu_ref, _flat_ref,
     o_ref, state_ref, ut_ref, bu_ref, ot_ref) = refs[NB:]
    c = pl.program_id(0)
    b = pl.program_id(1)

    @pl.when((c == 0) & (b == 0))
    def _():
        state_ref[...] = state_in_ref[...]

    @pl.when(b == 0)
    def _():
        _s5_chunk(S5_CHUNK, lambda bb: u_refs[bb][...],
                  state_ref, ut_ref, bu_ref, ot_ref, a_ref, bd_ref, cd_ref, dskip_ref, wglu_ref)

    o_ref[...] = _s5_read_out(ot_ref, b, S5_CHUNK).astype(BF16)


def _s5(zu, seq, a, bd, cd, dskip, wglu):
    t = zu.shape[0]
    tr = NB * seq
    nmeta_rows = NB * N_META
    const = lambda *_: (0, 0)
    par_specs = [pl.BlockSpec((NB, SSM_COLS), const),
                 pl.BlockSpec((SSM_WIDTH, SSM_COLS), const),
                 pl.BlockSpec((SSM_COLS, SSM_WIDTH), const),
                 pl.BlockSpec((1, SSM_WIDTH), const),
                 pl.BlockSpec((SSM_WIDTH, 2 * SSM_WIDTH), const)]

    def scratch(lc):
        return [pltpu.VMEM((NB, SSM_COLS), F32),
                pltpu.VMEM((2, lc * NB, LANES), F32),
                pltpu.VMEM((lc * NB, SSM_COLS), F32),
                pltpu.VMEM((2, lc * NB, LANES), F32)]

    meta_blk = tr // nmeta_rows
    flat, state = pl.pallas_call(
        _s5_meta_kernel,
        grid=(1,),
        in_specs=[pl.BlockSpec((nmeta_rows, SSM_WIDTH), lambda i: (meta_blk, 0))] + par_specs,
        out_specs=[pl.BlockSpec((nmeta_rows, SSM_WIDTH), lambda i: (meta_blk, 0)),
                   pl.BlockSpec((NB, SSM_COLS), const)],
        out_shape=[jax.ShapeDtypeStruct((t, SSM_WIDTH), BF16),
                   jax.ShapeDtypeStruct((NB, SSM_COLS), F32)],
        scratch_shapes=scratch(N_META),
        compiler_params=_cp(("arbitrary",)),
        name="s5_meta",
    )(zu, a, bd, cd, dskip, wglu)

    nc = seq // S5_CHUNK
    u_specs = [pl.BlockSpec((S5_CHUNK, SSM_WIDTH), lambda c, b, bb=bb: (bb * nc + c, 0))
               for bb in range(NB)]
    n_in = NB + 1 + len(par_specs)
    return pl.pallas_call(
        _s5_real_kernel,
        grid=(nc, NB),
        in_specs=u_specs + [pl.BlockSpec((NB, SSM_COLS), const)] + par_specs
        + [pl.BlockSpec(memory_space=pl.ANY)],
        out_specs=pl.BlockSpec((S5_CHUNK, SSM_WIDTH), lambda c, b: (b * nc + c, 0)),
        out_shape=jax.ShapeDtypeStruct((t, SSM_WIDTH), BF16),
        scratch_shapes=scratch(S5_CHUNK),
        input_output_aliases={n_in: 0},
        compiler_params=_cp(("arbitrary", "arbitrary")),
        name="s5_real",
    )(*([zu] * NB), state, a, bd, cd, dskip, wglu, flat)


def _s5_params(lam_re, lam_im, log_dt, b_re, b_im, c_re, c_im, d_skip):
    dt = jnp.exp(log_dt)[:, None]
    mag = jnp.exp(lam_re * dt)
    ab_re = mag * jnp.cos(lam_im * dt)
    ab_im = mag * jnp.sin(lam_im * dt)
    den = lam_re * lam_re + lam_im * lam_im
    nr = ab_re - 1.0
    ni = ab_im
    coef_re = ((nr * lam_re + ni * lam_im) / den)[..., None]
    coef_im = ((ni * lam_re - nr * lam_im) / den)[..., None]
    bb_re = coef_re * b_re - coef_im * b_im
    bb_im = coef_re * b_im + coef_im * b_re
    eye = jnp.eye(SSM_GROUPS, dtype=F32)
    half = SSM_COLS // 2
    bd = jnp.concatenate(
        [jnp.einsum('gnc,gh->gchn', m, eye).reshape(SSM_WIDTH, half) for m in (bb_re, bb_im)],
        axis=1).astype(BF16)
    cd = jnp.concatenate(
        [jnp.einsum('gcn,gh->gnhc', m, eye).reshape(half, SSM_WIDTH) for m in (c_re, -c_im)],
        axis=0).astype(BF16)
    a = jnp.concatenate([ab_re.reshape(1, half), ab_im.reshape(1, half)], axis=1)
    a = jnp.broadcast_to(a, (NB, SSM_COLS))
    return a, bd, cd, d_skip.reshape(1, SSM_WIDTH)


def _nt_dot(a, b):
    return lax.dot_general(a, b, (((1,), (1,)), ((), ())), preferred_element_type=F32)


def _osm_init(s, v):
    m = jnp.max(s, axis=1, keepdims=True)
    p = jnp.exp(s - m)
    l = jnp.sum(p, axis=1, keepdims=True)
    acc = jnp.dot(p.astype(BF16), v, preferred_element_type=F32)
    return m, l, acc


def _causal_sweep(qts, k_ref, k_cols, vt_ref, vt_rows, kms, vmts, qi,
                  m_ref, l_ref, acc_ref, al_ref, p_ref):
    n = len(qts)
    for i in range(n):
        s = jnp.dot(kms[i], qts[i], preferred_element_type=F32)
        m = jnp.max(s, axis=0, keepdims=True)
        p = jnp.exp(s - m)
        m_ref[i][...] = m
        l_ref[i][...] = jnp.sum(p, axis=0, keepdims=True)
        acc_ref[i][...] = jnp.dot(vmts[i], p.astype(BF16), preferred_element_type=F32)
        al_ref[i][...] = jnp.ones_like(al_ref[i])
        p_ref[i][...] = jnp.zeros_like(p_ref[i])

    sub = ATT_TK // LANES

    def pending_pv(jp):
        for i in range(n):
            vtb = jnp.concatenate([vt_ref[jp * sub + c, vt_rows[i], :] for c in range(sub)],
                                  axis=1)
            acc_ref[i][...] = al_ref[i][...] * acc_ref[i][...] + jnp.dot(
                vtb, p_ref[i][...], preferred_element_type=F32)

    def chunk(j, masked):
        off = pl.multiple_of(j * ATT_TK, ATT_TK)
        if masked:
            krow = lax.broadcasted_iota(jnp.int32, (ATT_TK, ATT_TQ), 0)
            qcol = lax.broadcasted_iota(jnp.int32, (ATT_TK, ATT_TQ), 1)
            visible = krow <= qcol
        scores = []
        for i in range(n):
            kb = k_ref[pl.ds(off, ATT_TK), k_cols[i]]
            s = jnp.dot(kb, qts[i], preferred_element_type=F32)
            scores.append(jnp.where(visible, s, MASK_VALUE) if masked else s)
        pending_pv(jnp.maximum(j - 1, 0))
        for i in range(n):
            m_old = m_ref[i][...]
            m_new = jnp.maximum(m_old, jnp.max(scores[i], axis=0, keepdims=True))
            alpha = jnp.exp(m_old - m_new)
            p = jnp.exp(scores[i] - m_new)
            l_ref[i][...] = alpha * l_ref[i][...] + jnp.sum(p, axis=0, keepdims=True)
            m_ref[i][...] = m_new
            al_ref[i][...] = alpha
            p_ref[i][...] = p.astype(BF16)

    def body(j, carry):
        chunk(j, False)
        return carry

    lax.fori_loop(0, qi, body, 0)
    chunk(qi, True)
    pending_pv(qi)


def _meta_attend(q, km, vm):
    n = km.shape[0]
    row = lax.broadcasted_iota(jnp.int32, (n, n), 0)
    col = lax.broadcasted_iota(jnp.int32, (n, n), 1)
    s = jnp.where(col <= row, _nt_dot(q, km), MASK_VALUE)
    _, l, acc = _osm_init(s, vm)
    return l, acc


def _diff_split(q):
    lane = lax.broadcasted_iota(jnp.int32, q.shape, 1)
    q = q * jnp.asarray(0.125, q.dtype)
    zero = jnp.zeros_like(q)
    return jnp.where(lane < 64, q, zero), jnp.where(lane >= 64, q, zero)


def _diff_finish(parts, lam, lam_init, g):
    (l1, acc1), (l2, acc2) = parts
    o = acc1 / l1 - lam * (acc2 / l2)
    return (_rms(o, g) * (1.0 - lam_init)).astype(BF16)


def _transpose_bf16(x):
    return x.astype(F32).T.astype(BF16)


def _map_scratch(n_maps, dv):
    return ([pltpu.VMEM((1, ATT_TQ), F32)] * (2 * n_maps)
            + [pltpu.VMEM((dv, ATT_TQ), F32)] * n_maps
            + [pltpu.VMEM((1, ATT_TQ), F32)] * n_maps
            + [pltpu.VMEM((ATT_TK, ATT_TQ), BF16)] * n_maps)


def _split_map_scratch(scratch, n_maps):
    return [scratch[k * n_maps:(k + 1) * n_maps] for k in range(5)]


def _diff_kernel(lam_init, lam_ref, q_ref, k_ref, vt_ref, km_ref, vm_ref, g_ref, o_ref,
                 *scratch):
    m_ref, l_ref, acc_ref, al_ref, p_ref = _split_map_scratch(scratch, 2 * DIFF_HEADS)
    qts, k_cols, vt_rows, kms, vmts = [], [], [], [], []
    row = lax.broadcasted_iota(jnp.int32, (LANES, ATT_TQ), 0)
    for h in range(DIFF_HEADS):
        sl = slice(LANES * h, LANES * (h + 1))
        km = km_ref[:, sl]
        vmt = _transpose_bf16(vm_ref[:, sl])
        qt = _transpose_bf16(q_ref[:, sl] * jnp.asarray(0.125, BF16))
        zero = jnp.zeros_like(qt)
        for qm in (jnp.where(row < 64, qt, zero), jnp.where(row >= 64, qt, zero)):
            qts.append(qm)
            k_cols.append(sl)
            vt_rows.append(sl)
            kms.append(km)
            vmts.append(vmt)
    _causal_sweep(qts, k_ref, k_cols, vt_ref, vt_rows, kms, vmts, pl.program_id(1),
                  m_ref, l_ref, acc_ref, al_ref, p_ref)
    lam = lam_ref[0]
    for h in range(DIFF_HEADS):
        o = (acc_ref[2 * h][...] / l_ref[2 * h][...]
             - lam * (acc_ref[2 * h + 1][...] / l_ref[2 * h + 1][...]))
        y = o * lax.rsqrt(jnp.mean(o * o, axis=0, keepdims=True) + EPS)
        o_ref[:, LANES * h:LANES * (h + 1)] = (
            y.T * g_ref[...] * (1.0 - lam_init)).astype(BF16)


def _diff_meta_kernel(lam_init, lam_ref, q_ref, km_ref, vm_ref, g_ref, _flat_ref, o_ref):
    q1, q2 = _diff_split(q_ref[...])
    km = km_ref[...]
    vm = vm_ref[...]
    parts = [_meta_attend(q1, km, vm), _meta_attend(q2, km, vm)]
    o_ref[...] = _diff_finish(parts, lam_ref[0], lam_init, g_ref[...])


def _diff_attention(za, vt, seq, lam, lam_init, g):
    t = za.shape[0]
    nq = seq // ATT_TQ
    mrow = NB * seq // N_META
    smem = pl.BlockSpec(memory_space=pltpu.SMEM)
    lam = lam.reshape(1).astype(F32)
    g = g.reshape(1, LANES)
    gspec = pl.BlockSpec((1, LANES), lambda *_: (0, 0))
    width = DIFF_HEADS * LANES
    n_maps = 2 * DIFF_HEADS
    flat = pl.pallas_call(
        functools.partial(_diff_kernel, lam_init),
        grid=(NB, nq),
        in_specs=[smem,
                  pl.BlockSpec((ATT_TQ, width), lambda b, i: (b * nq + i, 0)),
                  pl.BlockSpec((seq, width), lambda b, i: (b, 1)),
                  pl.BlockSpec((seq // LANES, width, LANES), lambda b, i: (b, 0, 0)),
                  pl.BlockSpec((N_META, width), lambda b, i: (mrow + b, 1)),
                  pl.BlockSpec((N_META, width), lambda b, i: (mrow + b, 2)),
                  gspec],
        out_specs=pl.BlockSpec((ATT_TQ, width), lambda b, i: (b * nq + i, 0)),
        out_shape=jax.ShapeDtypeStruct((t, width), BF16),
        scratch_shapes=_map_scratch(n_maps, LANES),
        compiler_params=_cp(("parallel", "arbitrary")),
        name="diff_attn",
    )(lam, za, za, vt, za, za, g)
    return pl.pallas_call(
        functools.partial(_diff_meta_kernel, lam_init),
        grid=(NB, DIFF_HEADS),
        in_specs=[smem,
                  pl.BlockSpec((N_META, LANES), lambda b, h: (mrow + b, h)),
                  pl.BlockSpec((N_META, LANES), lambda b, h: (mrow + b, 4 + h)),
                  pl.BlockSpec((N_META, LANES), lambda b, h: (mrow + b, 8 + h)),
                  gspec,
                  pl.BlockSpec(memory_space=pl.ANY)],
        out_specs=pl.BlockSpec((N_META, LANES), lambda b, h: (mrow + b, h)),
        out_shape=jax.ShapeDtypeStruct((t, DIFF_HEADS * LANES), BF16),
        input_output_aliases={5: 0},
        compiler_params=_cp(("parallel", "parallel")),
        name="diff_attn_meta",
    )(lam, za, za, za, g, flat)


def _split3(c):
    hi = c.astype(BF16).astype(F32)
    r1 = c - hi
    mid = r1.astype(BF16).astype(F32)
    lo = (r1 - mid).astype(BF16).astype(F32)
    return hi, mid, lo


def _cumsum_rows(tri, lf):
    parts = jnp.concatenate(_split3(lf), axis=1).astype(BF16)
    r = jnp.dot(tri, parts, preferred_element_type=F32)
    return r[:, :LANES] + r[:, LANES:2 * LANES] + r[:, 2 * LANES:]


def _log_sigmoid(x):
    return jnp.minimum(x, 0.0) - jnp.log1p(jnp.exp(-jnp.abs(x)))


def _fox_augment(fq, fk, cum, qa_ref, ka_ref):
    n = fq.shape[0]
    lane = lax.broadcasted_iota(jnp.int32, (n, LANES), 1)
    for h in range(FOX_HEADS):
        pair = slice(LANES * (h // 2), LANES * (h // 2) + LANES)
        own = (lane // 64) == (h % 2)
        e0 = 64 * (1 - h % 2)
        hi, mid, lo = _split3(jnp.broadcast_to(cum[:, h:h + 1], (n, LANES)))
        ones = (lane >= e0 + 3) & (lane < e0 + 6)
        q_extra = jnp.where(lane == e0, hi, jnp.where(lane == e0 + 1, mid, jnp.where(
            lane == e0 + 2, lo, jnp.where(ones, 1.0, 0.0))))
        ones = (lane >= e0) & (lane < e0 + 3)
        k_extra = jnp.where(lane == e0 + 3, -hi, jnp.where(lane == e0 + 4, -mid, jnp.where(
            lane == e0 + 5, -lo, jnp.where(ones, 1.0, 0.0))))
        q = fq[:, pair].astype(F32) * 0.125
        k = fk[:, pair].astype(F32)
        qa_ref[:, LANES * h:LANES * (h + 1)] = jnp.where(own, q, q_extra).astype(BF16)
        ka_ref[:, LANES * h:LANES * (h + 1)] = jnp.where(own, k, k_extra).astype(BF16)


def _fox_prep_meta_kernel(zg_ref, fq_ref, fk_ref, fb_ref, qa_ref, ka_ref, carry_ref):
    n = NB * N_META
    lf = _log_sigmoid(zg_ref[...] + fb_ref[...])
    row = lax.broadcasted_iota(jnp.int32, (n, n), 0)
    col = lax.broadcasted_iota(jnp.int32, (n, n), 1)
    tri = ((col <= row) & (col // N_META == row // N_META)).astype(BF16)
    cum = _cumsum_rows(tri, lf)
    brow = lax.broadcasted_iota(jnp.int32, (NB, n), 0)
    bcol = lax.broadcasted_iota(jnp.int32, (NB, n), 1)
    carry_ref[...] = _cumsum_rows((bcol // N_META == brow).astype(BF16), lf)
    _fox_augment(fq_ref[...], fk_ref[...], cum, qa_ref, ka_ref)


def _fox_prep_kernel(zg_ref, fq_ref, fk_ref, fb_ref, carry_in_ref, qa_ref, ka_ref, carry_ref):
    b = pl.program_id(0)
    n = zg_ref.shape[0]

    @pl.when(pl.program_id(1) == 0)
    def _():
        carry_ref[...] = carry_in_ref[pl.ds(b, 1), :]

    lf = _log_sigmoid(zg_ref[...] + fb_ref[...])
    row = lax.broadcasted_iota(jnp.int32, (n, n), 0)
    col = lax.broadcasted_iota(jnp.int32, (n, n), 1)
    cum = _cumsum_rows((col <= row).astype(BF16), lf) + carry_ref[...]
    carry_ref[...] = cum[n - 1:n, :]
    _fox_augment(fq_ref[...], fk_ref[...], cum, qa_ref, ka_ref)


FOX_PREP_TILE = 256


def _fox_prep(zg, za, seq, fb):
    tr = NB * seq
    nm = NB * N_META
    mblk = tr // nm
    fb = jnp.pad(fb.astype(F32), (0, LANES - FOX_HEADS)).reshape(1, LANES)
    fbspec = pl.BlockSpec((1, LANES), lambda *_: (0, 0))
    aug = FOX_HEADS * LANES
    qa_m, ka_m, carry = pl.pallas_call(
        _fox_prep_meta_kernel,
        grid=(1,),
        in_specs=[pl.BlockSpec((nm, LANES), lambda i: (mblk, 0)),
                  pl.BlockSpec((nm, 256), lambda i: (mblk, 6)),
                  pl.BlockSpec((nm, 256), lambda i: (mblk, 7)),
                  fbspec],
        out_specs=[pl.BlockSpec((nm, aug), lambda i: (0, 0)),
                   pl.BlockSpec((nm, aug), lambda i: (0, 0)),
                   pl.BlockSpec((NB, LANES), lambda i: (0, 0))],
        out_shape=[jax.ShapeDtypeStruct((nm, aug), BF16),
                   jax.ShapeDtypeStruct((nm, aug), BF16),
                   jax.ShapeDtypeStruct((NB, LANES), F32)],
        compiler_params=_cp(("arbitrary",)),
        name="fox_prep_meta",
    )(zg, za, za, fb)
    nc = seq // FOX_PREP_TILE
    qa, ka = pl.pallas_call(
        _fox_prep_kernel,
        grid=(NB, nc),
        in_specs=[pl.BlockSpec((FOX_PREP_TILE, LANES), lambda b, c: (b * nc + c, 0)),
                  pl.BlockSpec((FOX_PREP_TILE, 256), lambda b, c: (b * nc + c, 6)),
                  pl.BlockSpec((FOX_PREP_TILE, 256), lambda b, c: (b * nc + c, 7)),
                  fbspec,
                  pl.BlockSpec((NB, LANES), lambda b, c: (0, 0))],
        out_specs=[pl.BlockSpec((FOX_PREP_TILE, aug), lambda b, c: (b * nc + c, 0)),
                   pl.BlockSpec((FOX_PREP_TILE, aug), lambda b, c: (b * nc + c, 0))],
        out_shape=[jax.ShapeDtypeStruct((tr, aug), BF16),
                   jax.ShapeDtypeStruct((tr, aug), BF16)],
        scratch_shapes=[pltpu.VMEM((1, LANES), F32)],
        compiler_params=_cp(("parallel", "arbitrary")),
        name="fox_prep",
    )(zg, za, za, fb, carry)
    return qa, ka, qa_m, ka_m


def _fox_finish(parts):
    (l0, acc0), (l1, acc1) = parts
    lane = lax.broadcasted_iota(jnp.int32, acc0.shape, 1)
    return jnp.where(lane < 64, acc0 / l0, acc1 / l1).astype(BF16)


def _fox_kernel(q_ref, k_ref, vt_ref, km_ref, vm_ref, o_ref, *scratch):
    m_ref, l_ref, acc_ref, al_ref, p_ref = _split_map_scratch(scratch, FOX_HEADS)
    hd = 64
    vmt = _transpose_bf16(vm_ref[...])
    qts, k_cols, vt_rows, kms, vmts = [], [], [], [], []
    for h in range(FOX_HEADS):
        sl = slice(LANES * h, LANES * (h + 1))
        qts.append(_transpose_bf16(q_ref[:, sl]))
        k_cols.append(sl)
        vt_rows.append(slice(hd * h, hd * (h + 1)))
        kms.append(km_ref[:, sl])
        vmts.append(vmt[hd * h:hd * (h + 1), :])
    _causal_sweep(qts, k_ref, k_cols, vt_ref, vt_rows, kms, vmts, pl.program_id(1),
                  m_ref, l_ref, acc_ref, al_ref, p_ref)
    for p in range(FOX_HEADS // 2):
        o = jnp.concatenate([acc_ref[2 * p][...] / l_ref[2 * p][...],
                             acc_ref[2 * p + 1][...] / l_ref[2 * p + 1][...]],
                            axis=0)
        o_ref[:, LANES * p:LANES * (p + 1)] = o.T.astype(BF16)


def _fox_meta_kernel(q_ref, km_ref, vm_ref, _flat_ref, o_ref):
    vm = vm_ref[...]
    parts = [_meta_attend(q_ref[:, LANES * hh:LANES * (hh + 1)],
                          km_ref[:, LANES * hh:LANES * (hh + 1)], vm) for hh in range(2)]
    o_ref[...] = _fox_finish(parts)


def _fox_attention(za, vt, seq, qa, ka, qa_m, ka_m):
    t = za.shape[0]
    nq = seq // ATT_TQ
    mrow = NB * seq // N_META
    aug = FOX_HEADS * LANES
    flat = pl.pallas_call(
        _fox_kernel,
        grid=(NB, nq),
        in_specs=[pl.BlockSpec((ATT_TQ, aug), lambda b, i: (b * nq + i, 0)),
                  pl.BlockSpec((seq, aug), lambda b, i: (b, 0)),
                  pl.BlockSpec((seq // LANES, 256, LANES), lambda b, i: (b, 2, 0)),
                  pl.BlockSpec((N_META, aug), lambda b, i: (b, 0)),
                  pl.BlockSpec((N_META, 256), lambda b, i: (mrow + b, 8))],
        out_specs=pl.BlockSpec((ATT_TQ, 256), lambda b, i: (b * nq + i, 0)),
        out_shape=jax.ShapeDtypeStruct((t, 256), BF16),
        scratch_shapes=_map_scratch(FOX_HEADS, 64),
        compiler_params=_cp(("parallel", "arbitrary")),
        name="fox_attn",
    )(qa, ka, vt, ka_m, za)
    return pl.pallas_call(
        _fox_meta_kernel,
        grid=(NB, FOX_HEADS // 2),
        in_specs=[pl.BlockSpec((N_META, 256), lambda b, p: (b, p)),
                  pl.BlockSpec((N_META, 256), lambda b, p: (b, p)),
                  pl.BlockSpec((N_META, LANES), lambda b, p: (mrow + b, 16 + p)),
                  pl.BlockSpec(memory_space=pl.ANY)],
        out_specs=pl.BlockSpec((N_META, LANES), lambda b, p: (mrow + b, p)),
        out_shape=jax.ShapeDtypeStruct((t, 256), BF16),
        input_output_aliases={3: 0},
        compiler_params=_cp(("parallel", "parallel")),
        name="fox_attn_meta",
    )(qa_m, ka_m, za, flat)


def _router_kernel(res_ref, g_ref, wr_ref, idx_ref, gate_ref):
    h = _rms(res_ref[...], g_ref[...])
    logits = jnp.dot(h, wr_ref[...], preferred_element_type=F32,
                     precision=lax.Precision.HIGHEST)
    lane = lax.broadcasted_iota(jnp.int32, logits.shape, 1)
    logits = jnp.where(lane < N_EXPERTS, logits, -jnp.inf)
    m1 = jnp.max(logits, axis=1, keepdims=True)
    i1 = jnp.min(jnp.where(logits == m1, lane, LANES), axis=1, keepdims=True)
    rest = jnp.where(lane == i1, -jnp.inf, logits)
    m2 = jnp.max(rest, axis=1, keepdims=True)
    i2 = jnp.min(jnp.where(rest == m2, lane, LANES), axis=1, keepdims=True)
    e = jnp.exp(m2 - m1)
    g1 = 1.0 / (1.0 + e)
    g2 = e / (1.0 + e)
    idx_ref[...] = jnp.where(lane == 0, i1, jnp.where(lane == 1, i2, 0))
    gate_ref[...] = jnp.where(lane == 0, g1, jnp.where(lane == 1, g2, 0.0))


def _router(res, g, wr):
    t = res.shape[0]
    return pl.pallas_call(
        _router_kernel,
        grid=(t // TOK_TILE,),
        in_specs=[pl.BlockSpec((TOK_TILE, D_MODEL), lambda i: (i, 0)),
                  pl.BlockSpec((1, D_MODEL), lambda i: (0, 0)),
                  pl.BlockSpec((D_MODEL, LANES), lambda i: (0, 0))],
        out_specs=[pl.BlockSpec((TOK_TILE, LANES), lambda i: (i, 0)),
                   pl.BlockSpec((TOK_TILE, LANES), lambda i: (i, 0))],
        out_shape=[jax.ShapeDtypeStruct((t, LANES), jnp.int32),
                   jax.ShapeDtypeStruct((t, LANES), F32)],
        compiler_params=_cp(("parallel",)),
        name="router",
    )(res, g, wr)


def _gather_copy(src_hbm, first, dst_ref, r, sem):
    first = pl.multiple_of(first, ROW_BLOCKS)
    dst = pl.multiple_of(r * ROW_BLOCKS, ROW_BLOCKS)
    return pltpu.make_async_copy(src_hbm.at[pl.ds(first, ROW_BLOCKS), :],
                                 dst_ref.at[pl.ds(dst, ROW_BLOCKS), :], sem)


def _expert_ffn_kernel(nf, te_ref, na_ref, src_ref, x_hbm, g_ref, w1_ref, w3_ref, w2_ref,
                       o_ref, xbuf_ref, h_ref, acc_ref, sem):
    r = pl.program_id(0)
    f = pl.program_id(1)
    n_act = na_ref[0]
    per_step = MOE_XROWS // nf

    def tile_copy(slot):
        return pltpu.make_async_copy(x_hbm.at[pl.ds(0, MOE_XROWS * ROW_BLOCKS), :],
                                     xbuf_ref.at[slot], sem.at[slot])

    def start_row(tile, slot, k, priority=0):
        _gather_copy(x_hbm, src_ref[tile * MOE_XROWS + k], xbuf_ref.at[slot], k,
                     sem.at[slot]).start(priority=priority)

    @pl.when(r < n_act)
    def _():
        slot = lax.rem(r, 2)

        @pl.when((r == 0) & (f == 0))
        def _():
            def first(k, _):
                start_row(0, 0, k)
                return 0
            lax.fori_loop(0, MOE_XROWS, first, 0, unroll=8)

        @pl.when(f == 0)
        def _():
            tile_copy(slot).wait()
            x = _load_row_tiles(xbuf_ref.at[slot], MOE_TILE)
            h_ref[...] = _rms(x, g_ref[...]).astype(BF16)
            acc_ref[...] = jnp.zeros_like(acc_ref)

        nxt = jnp.minimum(r + 1, pl.num_programs(0) - 1)
        for k in range(per_step):
            start_row(nxt, 1 - slot, f * per_step + k, priority=k % 2)

        h = h_ref[...]
        a = jnp.dot(h, w1_ref[...].astype(BF16), preferred_element_type=F32)
        b = jnp.dot(h, w3_ref[...].astype(BF16), preferred_element_type=F32)
        hh = (a * jax.nn.sigmoid(a) * b).astype(BF16)
        acc_ref[...] += jnp.dot(hh, w2_ref[...].astype(BF16), preferred_element_type=F32)

        @pl.when(f == nf - 1)
        def _():
            _store_row_tiles(o_ref, acc_ref[...])

        @pl.when((f == nf - 1) & (r == n_act - 1))
        def _():
            tile_copy(1 - slot).wait()


def _expert_ffn(tile_expert, n_active, src, x, g, w1, w3, w2):
    dff = w1.shape[2]
    nt = src.shape[0] // MOE_XROWS
    nf = dff // MOE_FCHUNK
    assert MOE_XROWS % nf == 0 and MOE_XROWS >= MOE_TILE

    def row(r, f, te, na, src):
        return jnp.minimum(r, na[0] - 1)

    def fch(r, f, te, na, src):
        return jnp.where(r < na[0], f, nf - 1)

    return pl.pallas_call(
        functools.partial(_expert_ffn_kernel, nf),
        grid_spec=pltpu.PrefetchScalarGridSpec(
            num_scalar_prefetch=3,
            grid=(nt, nf),
            in_specs=[
                pl.BlockSpec(memory_space=pl.ANY),
                pl.BlockSpec((1, D_MODEL), lambda r, f, te, na, src: (0, 0)),
                pl.BlockSpec((None, D_MODEL, MOE_FCHUNK),
                             lambda r, f, te, na, src: (te[r], 0, fch(r, f, te, na, src))),
                pl.BlockSpec((None, D_MODEL, MOE_FCHUNK),
                             lambda r, f, te, na, src: (te[r], 0, fch(r, f, te, na, src))),
                pl.BlockSpec((None, MOE_FCHUNK, D_MODEL),
                             lambda r, f, te, na, src: (te[r], fch(r, f, te, na, src), 0))],
            out_specs=pl.BlockSpec((MOE_TILE * ROW_BLOCKS, LANES),
                                   lambda r, f, te, na, src: (row(r, f, te, na, src), 0)),
            scratch_shapes=[pltpu.VMEM((2, MOE_XROWS * ROW_BLOCKS, LANES), F32),
                            pltpu.VMEM((MOE_TILE, D_MODEL), BF16),
                            pltpu.VMEM((MOE_TILE, D_MODEL), F32),
                            pltpu.SemaphoreType.DMA((2,))]),
        out_shape=jax.ShapeDtypeStruct((nt * MOE_TILE * ROW_BLOCKS, LANES), F32),
        compiler_params=_cp(("arbitrary", "arbitrary")),
        name="expert_ffn",
    )(tile_expert, n_active, src, x, g, w1, w3, w2)


def _combine_kernel(tile, final, pos_ref, res_ref, gate_ref, gf_ref, y_hbm, o_ref, ybuf_ref, sem):
    i = pl.program_id(0)
    n = pl.num_programs(0)
    slot = lax.rem(i, 2)

    def issue_tile(step, s):
        base = step * tile

        def issue(r, _):
            for k in range(2):
                _gather_copy(y_hbm, pos_ref[2 * (base + r) + k], ybuf_ref.at[s, k], r,
                             sem.at[s]).start(priority=k)
            return 0

        lax.fori_loop(0, tile, issue, 0, unroll=8)

    @pl.when(i == 0)
    def _():
        issue_tile(0, 0)

    @pl.when(i + 1 < n)
    def _():
        issue_tile(i + 1, 1 - slot)

    pltpu.make_async_copy(ybuf_ref.at[slot], ybuf_ref.at[slot], sem.at[slot]).wait()
    gate = gate_ref[...]
    out = (res_ref[...] + gate[:, 0:1] * _load_row_tiles(ybuf_ref.at[slot, 0], tile)
           + gate[:, 1:2] * _load_row_tiles(ybuf_ref.at[slot, 1], tile))
    o_ref[...] = _rms(out, gf_ref[...]) if final else out


def _combine(pos, res, gates, y, final_g=None):
    final = final_g is not None
    t = res.shape[0]
    n_rows = t - NB * N_META if final else t
    tile = 256 if final else COMBINE_TILE
    if not final:
        final_g = jnp.ones((1, D_MODEL), F32)
    return pl.pallas_call(
        functools.partial(_combine_kernel, tile, final),
        grid_spec=pltpu.PrefetchScalarGridSpec(
            num_scalar_prefetch=1,
            grid=(n_rows // tile,),
            in_specs=[pl.BlockSpec((tile, D_MODEL), lambda i, pos: (i, 0)),
                      pl.BlockSpec((tile, LANES), lambda i, pos: (i, 0)),
                      pl.BlockSpec((1, D_MODEL), lambda i, pos: (0, 0)),
                      pl.BlockSpec(memory_space=pl.ANY)],
            out_specs=pl.BlockSpec((tile, D_MODEL), lambda i, pos: (i, 0)),
            scratch_shapes=[pltpu.VMEM((2, 2, tile * ROW_BLOCKS, LANES), F32),
                            pltpu.SemaphoreType.DMA((2,))]),
        out_shape=jax.ShapeDtypeStruct((n_rows, D_MODEL), F32),
        compiler_params=_cp(("arbitrary",)),
        name="moe_combine",
    )(pos, res, gates, final_g, y)


def _moe(res, res_tiles, g, wr, w1, w3, w2, final_g=None):
    t = res.shape[0]
    wr = jnp.pad(wr.astype(F32), ((0, 0), (0, LANES - N_EXPERTS)))
    idx, gates = _router(res, g, wr)
    e_flat = idx[:, :2].reshape(-1)
    onehot = (e_flat[:, None] == jnp.arange(N_EXPERTS, dtype=jnp.int32)[None, :]).astype(jnp.int32)
    csum = jnp.cumsum(onehot, axis=0)
    rank = jnp.take_along_axis(csum, e_flat[:, None], axis=1)[:, 0] - 1
    counts = csum[-1]
    tiles = (counts + MOE_TILE - 1) // MOE_TILE
    tile_end = jnp.cumsum(tiles)
    starts = (tile_end - tiles) * MOE_TILE
    pos = (starts[e_flat] + rank).astype(jnp.int32)
    n_tiles = (2 * t + N_EXPERTS * (MOE_TILE - 1)) // MOE_TILE
    spos = (pos // MOE_TILE) * MOE_XROWS + pos % MOE_TILE
    src = jnp.zeros((n_tiles * MOE_XROWS,), jnp.int32).at[spos].set(
        (jnp.arange(2 * t, dtype=jnp.int32) // 2) * ROW_BLOCKS, unique_indices=True)
    n_active = tile_end[-1:].astype(jnp.int32)
    tile_ids = jnp.minimum(jnp.arange(n_tiles, dtype=jnp.int32), n_active[0] - 1)
    tile_expert = jnp.sum(tile_ids[:, None] >= tile_end[None, :], axis=1).astype(jnp.int32)

    y = _expert_ffn(tile_expert, n_active, src, res_tiles, g, w1, w3, w2)
    return _combine(pos * ROW_BLOCKS, res, gates, y, final_g)


def _final_norm_kernel(res_ref, g_ref, o_ref):
    o_ref[...] = _rms(res_ref[...], g_ref[...])


def _final_norm(res, g, n_rows):
    tile = 1024
    return pl.pallas_call(
        _final_norm_kernel,
        grid=(n_rows // tile,),
        in_specs=[pl.BlockSpec((tile, D_MODEL), lambda i: (i, 0)),
                  pl.BlockSpec((1, D_MODEL), lambda i: (0, 0))],
        out_specs=pl.BlockSpec((tile, D_MODEL), lambda i: (i, 0)),
        out_shape=jax.ShapeDtypeStruct((n_rows, D_MODEL), F32),
        compiler_params=_cp(("parallel",)),
        name="final_norm",
    )(res, g)


def kernel(x, meta_tokens, norm_mix_g, w_in, w_out, ssm_lambda_re, ssm_lambda_im, ssm_log_dt,
           ssm_b_re, ssm_b_im, ssm_c_re, ssm_c_im, ssm_d, ssm_w_glu, diff_lambda_q1,
           diff_lambda_k1, diff_lambda_q2, diff_lambda_k2, diff_subln_g, fox_forget_b,
           norm_ffn_g, dense_w1, dense_w3, dense_w2, moe_router, moe_w1, moe_w3, moe_w2,
           final_norm_g):
    bsz, seq, d = x.shape
    assert bsz == NB and d == D_MODEL and seq % ATT_TQ == 0
    depth = w_in.shape[0]
    tr = bsz * seq
    res = jnp.concatenate(
        [x.reshape(tr, d), jnp.tile(meta_tokens.astype(x.dtype), (bsz, 1))], axis=0)
    row = lambda v: v.reshape(1, -1).astype(F32)

    for l in range(depth):
        w = jnp.pad(w_in[l], ((0, 0), (0, IN_PAD - w_in.shape[2]))).astype(BF16)
        wvt = jnp.concatenate([w_in[l][:, 1280:1792], w_in[l][:, 2304:2560]], axis=1).T
        zu, za, zg, vt = _inproj(res, row(norm_mix_g[l]), w, wvt.astype(BF16))

        a, bd, cd, dskip = _s5_params(ssm_lambda_re[l], ssm_lambda_im[l], ssm_log_dt[l],
                                      ssm_b_re[l], ssm_b_im[l], ssm_c_re[l], ssm_c_im[l],
                                      ssm_d[l])
        ssm_out = _s5(zu, seq, a, bd, cd, dskip, ssm_w_glu[l].astype(BF16))

        lam_init = 0.8 - 0.6 * math.exp(-0.3 * l)
        lam = (jnp.exp(jnp.sum(diff_lambda_q1[l] * diff_lambda_k1[l]))
               - jnp.exp(jnp.sum(diff_lambda_q2[l] * diff_lambda_k2[l])) + lam_init)
        diff_out = _diff_attention(za, vt, seq, lam, lam_init, diff_subln_g[l])

        qa, ka, qa_m, ka_m = _fox_prep(zg, za, seq, fox_forget_b[l])
        fox_out = _fox_attention(za, vt, seq, qa, ka, qa_m, ka_m)

        is_moe = l % 2 == 1
        res, *res_tiles = _outproj(res, ssm_out, diff_out, fox_out, w_out[l].astype(BF16),
                                   with_row_tiles=is_moe)

        if not is_moe:
            res = _dense_ffn(res, row(norm_ffn_g[l]), dense_w1[l // 2].astype(BF16),
                             dense_w3[l // 2].astype(BF16), dense_w2[l // 2].astype(BF16))
        else:
            res = _moe(res, res_tiles[0], row(norm_ffn_g[l]), moe_router[l // 2], moe_w1[l // 2],
                       moe_w3[l // 2], moe_w2[l // 2],
                       final_g=row(final_norm_g) if l == depth - 1 else None)

    if depth % 2 == 1:
        res = _final_norm(res, row(final_norm_g), tr)
    return res.reshape(bsz, seq, d)
```

```python
import functools
import math

import jax
import jax.numpy as jnp
from jax import lax
from jax.experimental import pallas as pl
from jax.experimental.pallas import tpu as pltpu

F32 = jnp.float32
BF16 = jnp.bfloat16
EPS = 1e-6

D_MODEL = 1024
N_META = 16
NB = 8
SSM_WIDTH = 256
SSM_GROUPS = 16
SSM_GROUP = 16
SSM_STATE = 64
SSM_COLS = 2 * SSM_GROUPS * SSM_STATE
DIFF_HEADS = 4
FOX_HEADS = 4
ATT_COLS = 1536
IN_PAD = 1920
N_EXPERTS = 8
LANES = 128
MASK_VALUE = -1e30

VT_ROWS = 768
TOK_TILE = 688
IN_TILE = 384
FFN_TILE = 384
ATT_TQ = 256
ATT_TK = 256
S5_CHUNK = 256
MOE_TILE = 1024
MOE_FCHUNK = 512
MOE_XROWS = 1029
ROW_BLOCKS = D_MODEL // LANES
COMBINE_TILE = 384

VMEM_LIMIT = 56 * 1024 * 1024


def _cp(sem):
    return pltpu.CompilerParams(dimension_semantics=sem, vmem_limit_bytes=VMEM_LIMIT)


def _rms(x, g):
    return x * lax.rsqrt(jnp.mean(x * x, axis=-1, keepdims=True) + EPS) * g


def _inproj_kernel(res_ref, g_ref, w_ref, wvt_ref, zu_ref, za_ref, zg_ref, vt_ref, vtm_ref):
    h = _rms(res_ref[...], g_ref[...]).astype(BF16)
    zu_ref[...] = jnp.dot(h, w_ref[:, 0:SSM_WIDTH], preferred_element_type=F32)
    for c in range(0, ATT_COLS, 256):
        za_ref[:, c:c + 256] = jnp.dot(
            h, w_ref[:, SSM_WIDTH + c:SSM_WIDTH + c + 256],
            preferred_element_type=F32).astype(BF16)
    zg_ref[...] = jnp.dot(h, w_ref[:, SSM_WIDTH + ATT_COLS:IN_PAD],
                          preferred_element_type=F32)
    vt = _nt_dot(wvt_ref[...], h).astype(BF16)
    for c in range(IN_TILE // LANES):
        vt_ref[c] = vt[:, LANES * c:LANES * (c + 1)]

    @pl.when(pl.program_id(0) == pl.num_programs(0) - 1)
    def _():
        first = IN_TILE - NB * N_META
        for b in range(NB):
            vtm_ref[b] = vt[:, first + N_META * b:first + N_META * (b + 1)]


def _inproj(res, g, w, wvt):
    t = res.shape[0]
    nblk = IN_TILE // LANES
    assert t % IN_TILE == 0 and NB * N_META <= IN_TILE
    return pl.pallas_call(
        _inproj_kernel,
        grid=(t // IN_TILE,),
        in_specs=[pl.BlockSpec((IN_TILE, D_MODEL), lambda i: (i, 0)),
                  pl.BlockSpec((1, D_MODEL), lambda i: (0, 0)),
                  pl.BlockSpec((D_MODEL, IN_PAD), lambda i: (0, 0)),
                  pl.BlockSpec((VT_ROWS, D_MODEL), lambda i: (0, 0))],
        out_specs=[pl.BlockSpec((IN_TILE, SSM_WIDTH), lambda i: (i, 0)),
                   pl.BlockSpec((IN_TILE, ATT_COLS), lambda i: (i, 0)),
                   pl.BlockSpec((IN_TILE, LANES), lambda i: (i, 0)),
                   pl.BlockSpec((nblk, VT_ROWS, LANES), lambda i: (i, 0, 0)),
                   pl.BlockSpec((NB, VT_ROWS, N_META), lambda i: (0, 0, 0))],
        out_shape=[jax.ShapeDtypeStruct((t, SSM_WIDTH), F32),
                   jax.ShapeDtypeStruct((t, ATT_COLS), BF16),
                   jax.ShapeDtypeStruct((t, LANES), F32),
                   jax.ShapeDtypeStruct((t // LANES, VT_ROWS, LANES), BF16),
                   jax.ShapeDtypeStruct((NB, VT_ROWS, N_META), BF16)],
        compiler_params=_cp(("arbitrary",)),
        name="inproj",
    )(res, g, w, wvt)


def _store_row_tiles(ref, x):
    n = x.shape[0]
    for j in range(ROW_BLOCKS):
        ref[pl.ds(j, n, stride=ROW_BLOCKS), :] = x[:, LANES * j:LANES * (j + 1)]


def _load_row_tiles(ref, n):
    return jnp.concatenate(
        [ref[pl.ds(j, n, stride=ROW_BLOCKS), :] for j in range(ROW_BLOCKS)], axis=1)


def _outproj_kernel(res_ref, s_ref, d_ref, f_ref, w_ref, o_ref, *tiled_ref):
    mixed = jnp.concatenate([s_ref[...], d_ref[...], f_ref[...]], axis=1)
    out = res_ref[...] + jnp.dot(mixed, w_ref[...], preferred_element_type=F32)
    o_ref[...] = out
    if tiled_ref:
        _store_row_tiles(tiled_ref[0], out)


def _outproj(res, ssm, diff, fox, w, with_row_tiles):
    t = res.shape[0]
    out_specs = [pl.BlockSpec((TOK_TILE, D_MODEL), lambda i: (i, 0))]
    out_shape = [jax.ShapeDtypeStruct((t, D_MODEL), F32)]
    if with_row_tiles:
        out_specs.append(pl.BlockSpec((TOK_TILE * ROW_BLOCKS, LANES), lambda i: (i, 0)))
        out_shape.append(jax.ShapeDtypeStruct((t * ROW_BLOCKS, LANES), F32))
    return pl.pallas_call(
        _outproj_kernel,
        grid=(t // TOK_TILE,),
        in_specs=[pl.BlockSpec((TOK_TILE, D_MODEL), lambda i: (i, 0)),
                  pl.BlockSpec((TOK_TILE, 256), lambda i: (i, 0)),
                  pl.BlockSpec((TOK_TILE, 512), lambda i: (i, 0)),
                  pl.BlockSpec((TOK_TILE, 256), lambda i: (i, 0)),
                  pl.BlockSpec((D_MODEL, D_MODEL), lambda i: (0, 0))],
        out_specs=out_specs,
        out_shape=out_shape,
        compiler_params=_cp(("parallel",)),
        name="outproj",
    )(res, ssm, diff, fox, w)


def _dense_ffn_kernel(res_ref, g_ref, w1_ref, w3_ref, w2_ref, o_ref):
    x = res_ref[...]
    h = _rms(x, g_ref[...]).astype(BF16)
    a = jnp.dot(h, w1_ref[...], preferred_element_type=F32)
    b = jnp.dot(h, w3_ref[...], preferred_element_type=F32)
    hh = (a * jax.nn.sigmoid(a) * b).astype(BF16)
    o_ref[...] = x + jnp.dot(hh, w2_ref[...], preferred_element_type=F32)


def _dense_ffn(res, g, w1, w3, w2):
    t = res.shape[0]
    dff = w1.shape[1]
    once = pl.Buffered(1)
    return pl.pallas_call(
        _dense_ffn_kernel,
        grid=(t // FFN_TILE,),
        in_specs=[pl.BlockSpec((FFN_TILE, D_MODEL), lambda i: (i, 0)),
                  pl.BlockSpec((1, D_MODEL), lambda i: (0, 0)),
                  pl.BlockSpec((D_MODEL, dff), lambda i: (0, 0), pipeline_mode=once),
                  pl.BlockSpec((D_MODEL, dff), lambda i: (0, 0), pipeline_mode=once),
                  pl.BlockSpec((dff, D_MODEL), lambda i: (0, 0), pipeline_mode=once)],
        out_specs=pl.BlockSpec((FFN_TILE, D_MODEL), lambda i: (i, 0)),
        out_shape=jax.ShapeDtypeStruct((t, D_MODEL), F32),
        compiler_params=_cp(("parallel",)),
        name="dense_ffn",
    )(res, g, w1, w3, w2)


def _gelu_tanh(x):
    c = math.sqrt(2.0 / math.pi)
    return 0.5 * x * (1.0 + jnp.tanh(c * (x + 0.044715 * (x * x * x))))


def _s5_chunk(lc, get_u, state_ref, ut_ref, bu_ref, ot_ref,
              a_ref, bd_ref, cd_ref, dskip_ref, wglu_ref):
    for b in range(NB):
        ub = get_u(b)
        for s in range(2):
            ut_ref[s, pl.ds(b, lc, stride=NB), :] = ub[:, LANES * s:LANES * (s + 1)]
    u_tm = jnp.concatenate([ut_ref[0], ut_ref[1]], axis=1)
    bu_ref[...] = jnp.dot(u_tm.astype(BF16), bd_ref[...], preferred_element_type=F32)

    half = SSM_COLS // 2
    a_re = a_ref[:, :half]
    a_im = a_ref[:, half:]

    def step(t, x):
        x_re, x_im = x
        r = pl.multiple_of(t * NB, NB)
        cur = bu_ref[pl.ds(r, NB), :]
        n_re = a_re * x_re - a_im * x_im + cur[:, :half]
        n_im = a_re * x_im + a_im * x_re + cur[:, half:]
        bu_ref[pl.ds(r, NB), :] = jnp.concatenate([n_re, n_im], axis=1)
        return n_re, n_im

    x_re, x_im = lax.fori_loop(0, lc, step,
                               (state_ref[:, :half], state_ref[:, half:]), unroll=4)
    state_ref[...] = jnp.concatenate([x_re, x_im], axis=1)

    y = jnp.dot(bu_ref[...].astype(BF16), cd_ref[...], preferred_element_type=F32)
    y = _gelu_tanh(y + dskip_ref[...] * u_tm)
    g = jnp.dot(y.astype(BF16), wglu_ref[...], preferred_element_type=F32)
    o = g[:, :SSM_WIDTH] * jax.nn.sigmoid(g[:, SSM_WIDTH:])
    ot_ref[0] = o[:, :LANES]
    ot_ref[1] = o[:, LANES:]


def _s5_read_out(ot_ref, b, lc):
    return jnp.concatenate(
        [ot_ref[s, pl.ds(b, lc, stride=NB), :] for s in range(2)], axis=1)


def _s5_meta_kernel(u_ref, a_ref, bd_ref, cd_ref, dskip_ref, wglu_ref,
                    o_ref, state_out_ref, state_ref, ut_ref, bu_ref, ot_ref):
    state_ref[...] = jnp.zeros_like(state_ref)
    _s5_chunk(N_META, lambda b: u_ref[b * N_META:(b + 1) * N_META, :],
              state_ref, ut_ref, bu_ref, ot_ref, a_ref, bd_ref, cd_ref, dskip_ref, wglu_ref)
    for b in range(NB):
        o_ref[b * N_META:(b + 1) * N_META, :] = _s5_read_out(ot_ref, b, N_META).astype(BF16)
    state_out_ref[...] = state_ref[...]


def _s5_real_kernel(*refs):
    u_refs = refs[:NB]
    (state_in_ref, a_ref, bd_ref, cd_ref, dskip_ref, wglu_ref, _flat_ref,
     o_ref, state_ref, ut_ref, bu_ref, ot_ref) = refs[NB:]
    c = pl.program_id(0)
    b = pl.program_id(1)

    @pl.when((c == 0) & (b == 0))
    def _():
        state_ref[...] = state_in_ref[...]

    @pl.when(b == 0)
    def _():
        _s5_chunk(S5_CHUNK, lambda bb: u_refs[bb][...],
                  state_ref, ut_ref, bu_ref, ot_ref, a_ref, bd_ref, cd_ref, dskip_ref, wglu_ref)

    o_ref[...] = _s5_read_out(ot_ref, b, S5_CHUNK).astype(BF16)


def _s5(zu, seq, a, bd, cd, dskip, wglu):
    t = zu.shape[0]
    tr = NB * seq
    nmeta_rows = NB * N_META
    const = lambda *_: (0, 0)
    par_specs = [pl.BlockSpec((NB, SSM_COLS), const),
                 pl.BlockSpec((SSM_WIDTH, SSM_COLS), const),
                 pl.BlockSpec((SSM_COLS, SSM_WIDTH), const),
                 pl.BlockSpec((1, SSM_WIDTH), const),
                 pl.BlockSpec((SSM_WIDTH, 2 * SSM_WIDTH), const)]

    def scratch(lc):
        return [pltpu.VMEM((NB, SSM_COLS), F32),
                pltpu.VMEM((2, lc * NB, LANES), F32),
                pltpu.VMEM((lc * NB, SSM_COLS), F32),
                pltpu.VMEM((2, lc * NB, LANES), F32)]

    meta_blk = tr // nmeta_rows
    flat, state = pl.pallas_call(
        _s5_meta_kernel,
        grid=(1,),
        in_specs=[pl.BlockSpec((nmeta_rows, SSM_WIDTH), lambda i: (meta_blk, 0))] + par_specs,
        out_specs=[pl.BlockSpec((nmeta_rows, SSM_WIDTH), lambda i: (meta_blk, 0)),
                   pl.BlockSpec((NB, SSM_COLS), const)],
        out_shape=[jax.ShapeDtypeStruct((t, SSM_WIDTH), BF16),
                   jax.ShapeDtypeStruct((NB, SSM_COLS), F32)],
        scratch_shapes=scratch(N_META),
        compiler_params=_cp(("arbitrary",)),
        name="s5_meta",
    )(zu, a, bd, cd, dskip, wglu)

    nc = seq // S5_CHUNK
    u_specs = [pl.BlockSpec((S5_CHUNK, SSM_WIDTH), lambda c, b, bb=bb: (bb * nc + c, 0))
               for bb in range(NB)]
    n_in = NB + 1 + len(par_specs)
    return pl.pallas_call(
        _s5_real_kernel,
        grid=(nc, NB),
        in_specs=u_specs + [pl.BlockSpec((NB, SSM_COLS), const)] + par_specs
        + [pl.BlockSpec(memory_space=pl.ANY)],
        out_specs=pl.BlockSpec((S5_CHUNK, SSM_WIDTH), lambda c, b: (b * nc + c, 0)),
        out_shape=jax.ShapeDtypeStruct((t, SSM_WIDTH), BF16),
        scratch_shapes=scratch(S5_CHUNK),
        input_output_aliases={n_in: 0},
        compiler_params=_cp(("arbitrary", "arbitrary")),
        name="s5_real",
    )(*([zu] * NB), state, a, bd, cd, dskip, wglu, flat)


def _s5_params(lam_re, lam_im, log_dt, b_re, b_im, c_re, c_im, d_skip):
    dt = jnp.exp(log_dt)[:, None]
    mag = jnp.exp(lam_re * dt)
    ab_re = mag * jnp.cos(lam_im * dt)
    ab_im = mag * jnp.sin(lam_im * dt)
    den = lam_re * lam_re + lam_im * lam_im
    nr = ab_re - 1.0
    ni = ab_im
    coef_re = ((nr * lam_re + ni * lam_im) / den)[..., None]
    coef_im = ((ni * lam_re - nr * lam_im) / den)[..., None]
    bb_re = coef_re * b_re - coef_im * b_im
    bb_im = coef_re * b_im + coef_im * b_re
    eye = jnp.eye(SSM_GROUPS, dtype=F32)
    half = SSM_COLS // 2
    bd = jnp.concatenate(
        [jnp.einsum('gnc,gh->gchn', m, eye).reshape(SSM_WIDTH, half) for m in (bb_re, bb_im)],
        axis=1).astype(BF16)
    cd = jnp.concatenate(
        [jnp.einsum('gcn,gh->gnhc', m, eye).reshape(half, SSM_WIDTH) for m in (c_re, -c_im)],
        axis=0).astype(BF16)
    a = jnp.concatenate([ab_re.reshape(1, half), ab_im.reshape(1, half)], axis=1)
    a = jnp.broadcast_to(a, (NB, SSM_COLS))
    return a, bd, cd, d_skip.reshape(1, SSM_WIDTH)


def _nt_dot(a, b):
    return lax.dot_general(a, b, (((1,), (1,)), ((), ())), preferred_element_type=F32)


def _osm_init(s, vt):
    m = jnp.max(s, axis=1, keepdims=True)
    p = jnp.exp(s - m)
    l = jnp.sum(p, axis=1, keepdims=True)
    acc = _nt_dot(p.astype(BF16), vt)
    return m, l, acc


def _causal_sweep(qts, k_ref, k_cols, vt_ref, vt_rows, kms, vmts, qi,
                  m_ref, l_ref, acc_ref, al_ref, p_ref):
    n = len(qts)
    for i in range(n):
        s = jnp.dot(kms[i], qts[i], preferred_element_type=F32)
        m = jnp.max(s, axis=0, keepdims=True)
        p = jnp.exp(s - m)
        m_ref[i][...] = m
        l_ref[i][...] = jnp.sum(p, axis=0, keepdims=True)
        acc_ref[i][...] = jnp.dot(vmts[i], p.astype(BF16), preferred_element_type=F32)
        al_ref[i][...] = jnp.ones_like(al_ref[i])
        p_ref[i][...] = jnp.zeros_like(p_ref[i])

    sub = ATT_TK // LANES

    def pending_pv(jp):
        for i in range(n):
            vtb = jnp.concatenate([vt_ref[jp * sub + c, vt_rows[i], :] for c in range(sub)],
                                  axis=1)
            acc_ref[i][...] = al_ref[i][...] * acc_ref[i][...] + jnp.dot(
                vtb, p_ref[i][...], preferred_element_type=F32)

    def chunk(j, masked):
        off = pl.multiple_of(j * ATT_TK, ATT_TK)
        if masked:
            krow = lax.broadcasted_iota(jnp.int32, (ATT_TK, ATT_TQ), 0)
            qcol = lax.broadcasted_iota(jnp.int32, (ATT_TK, ATT_TQ), 1)
            visible = krow <= qcol
        scores = []
        for i in range(n):
            kb = k_ref[pl.ds(off, ATT_TK), k_cols[i]]
            s = jnp.dot(kb, qts[i], preferred_element_type=F32)
            scores.append(jnp.where(visible, s, MASK_VALUE) if masked else s)
        pending_pv(jnp.maximum(j - 1, 0))
        for i in range(n):
            m_old = m_ref[i][...]
            m_new = jnp.maximum(m_old, jnp.max(scores[i], axis=0, keepdims=True))
            alpha = jnp.exp(m_old - m_new)
            p = jnp.exp(scores[i] - m_new)
            l_ref[i][...] = alpha * l_ref[i][...] + jnp.sum(p, axis=0, keepdims=True)
            m_ref[i][...] = m_new
            al_ref[i][...] = alpha
            p_ref[i][...] = p.astype(BF16)

    def body(j, carry):
        chunk(j, False)
        return carry

    lax.fori_loop(0, qi, body, 0)
    chunk(qi, True)
    pending_pv(qi)


def _meta_attend(q, km, vmt):
    n = km.shape[0]
    row = lax.broadcasted_iota(jnp.int32, (n, n), 0)
    col = lax.broadcasted_iota(jnp.int32, (n, n), 1)
    s = jnp.where(col <= row, _nt_dot(q, km), MASK_VALUE)
    _, l, acc = _osm_init(s, vmt)
    return l, acc


def _diff_split(q):
    lane = lax.broadcasted_iota(jnp.int32, q.shape, 1)
    q = q * jnp.asarray(0.125, q.dtype)
    zero = jnp.zeros_like(q)
    return jnp.where(lane < 64, q, zero), jnp.where(lane >= 64, q, zero)


def _diff_finish(parts, lam, lam_init, g):
    (l1, acc1), (l2, acc2) = parts
    o = acc1 / l1 - lam * (acc2 / l2)
    return (_rms(o, g) * (1.0 - lam_init)).astype(BF16)


def _transpose_bf16(x):
    return x.astype(F32).T.astype(BF16)


def _map_scratch(n_maps, dv):
    return ([pltpu.VMEM((1, ATT_TQ), F32)] * (2 * n_maps)
            + [pltpu.VMEM((dv, ATT_TQ), F32)] * n_maps
            + [pltpu.VMEM((1, ATT_TQ), F32)] * n_maps
            + [pltpu.VMEM((ATT_TK, ATT_TQ), BF16)] * n_maps)


def _split_map_scratch(scratch, n_maps):
    return [scratch[k * n_maps:(k + 1) * n_maps] for k in range(5)]


def _diff_kernel(lam_init, lam_ref, q_ref, k_ref, vt_ref, km_ref, vmt_ref, g_ref, o_ref,
                 *scratch):
    m_ref, l_ref, acc_ref, al_ref, p_ref = _split_map_scratch(scratch, 2 * DIFF_HEADS)
    qts, k_cols, vt_rows, kms, vmts = [], [], [], [], []
    row = lax.broadcasted_iota(jnp.int32, (LANES, ATT_TQ), 0)
    for h in range(DIFF_HEADS):
        sl = slice(LANES * h, LANES * (h + 1))
        km = km_ref[:, sl]
        vmt = vmt_ref[sl, :]
        qt = _transpose_bf16(q_ref[:, sl] * jnp.asarray(0.125, BF16))
        zero = jnp.zeros_like(qt)
        for qm in (jnp.where(row < 64, qt, zero), jnp.where(row >= 64, qt, zero)):
            qts.append(qm)
            k_cols.append(sl)
            vt_rows.append(sl)
            kms.append(km)
            vmts.append(vmt)
    _causal_sweep(qts, k_ref, k_cols, vt_ref, vt_rows, kms, vmts, pl.program_id(1),
                  m_ref, l_ref, acc_ref, al_ref, p_ref)
    lam = lam_ref[0]
    for h in range(DIFF_HEADS):
        o = (acc_ref[2 * h][...] / l_ref[2 * h][...]
             - lam * (acc_ref[2 * h + 1][...] / l_ref[2 * h + 1][...]))
        y = o * lax.rsqrt(jnp.mean(o * o, axis=0, keepdims=True) + EPS)
        o_ref[:, LANES * h:LANES * (h + 1)] = (
            y.T * g_ref[...] * (1.0 - lam_init)).astype(BF16)


def _diff_meta_kernel(lam_init, lam_ref, q_ref, km_ref, vmt_ref, g_ref, _flat_ref, o_ref):
    q1, q2 = _diff_split(q_ref[...])
    km = km_ref[...]
    vmt = vmt_ref[...]
    parts = [_meta_attend(q1, km, vmt), _meta_attend(q2, km, vmt)]
    o_ref[...] = _diff_finish(parts, lam_ref[0], lam_init, g_ref[...])


def _diff_attention(za, vt, vtm, seq, lam, lam_init, g):
    t = za.shape[0]
    nq = seq // ATT_TQ
    mrow = NB * seq // N_META
    smem = pl.BlockSpec(memory_space=pltpu.SMEM)
    lam = lam.reshape(1).astype(F32)
    g = g.reshape(1, LANES)
    gspec = pl.BlockSpec((1, LANES), lambda *_: (0, 0))
    width = DIFF_HEADS * LANES
    n_maps = 2 * DIFF_HEADS
    flat = pl.pallas_call(
        functools.partial(_diff_kernel, lam_init),
        grid=(NB, nq),
        in_specs=[smem,
                  pl.BlockSpec((ATT_TQ, width), lambda b, i: (b * nq + i, 0)),
                  pl.BlockSpec((seq, width), lambda b, i: (b, 1)),
                  pl.BlockSpec((seq // LANES, width, LANES), lambda b, i: (b, 0, 0)),
                  pl.BlockSpec((N_META, width), lambda b, i: (mrow + b, 1)),
                  pl.BlockSpec((None, width, N_META), lambda b, i: (b, 0, 0)),
                  gspec],
        out_specs=pl.BlockSpec((ATT_TQ, width), lambda b, i: (b * nq + i, 0)),
        out_shape=jax.ShapeDtypeStruct((t, width), BF16),
        scratch_shapes=_map_scratch(n_maps, LANES),
        compiler_params=_cp(("parallel", "arbitrary")),
        name="diff_attn",
    )(lam, za, za, vt, za, vtm, g)
    return pl.pallas_call(
        functools.partial(_diff_meta_kernel, lam_init),
        grid=(NB, DIFF_HEADS),
        in_specs=[smem,
                  pl.BlockSpec((N_META, LANES), lambda b, h: (mrow + b, h)),
                  pl.BlockSpec((N_META, LANES), lambda b, h: (mrow + b, 4 + h)),
                  pl.BlockSpec((None, LANES, N_META), lambda b, h: (b, h, 0)),
                  gspec,
                  pl.BlockSpec(memory_space=pl.ANY)],
        out_specs=pl.BlockSpec((N_META, LANES), lambda b, h: (mrow + b, h)),
        out_shape=jax.ShapeDtypeStruct((t, DIFF_HEADS * LANES), BF16),
        input_output_aliases={5: 0},
        compiler_params=_cp(("parallel", "parallel")),
        name="diff_attn_meta",
    )(lam, za, za, vtm, g, flat)


def _split3(c):
    hi = c.astype(BF16).astype(F32)
    r1 = c - hi
    mid = r1.astype(BF16).astype(F32)
    lo = (r1 - mid).astype(BF16).astype(F32)
    return hi, mid, lo


def _cumsum_rows(tri, lf):
    parts = jnp.concatenate(_split3(lf), axis=1).astype(BF16)
    r = jnp.dot(tri, parts, preferred_element_type=F32)
    return r[:, :LANES] + r[:, LANES:2 * LANES] + r[:, 2 * LANES:]


def _log_sigmoid(x):
    return jnp.minimum(x, 0.0) - jnp.log1p(jnp.exp(-jnp.abs(x)))


def _fox_augment(fq, fk, cum, qa_ref, ka_ref):
    n = fq.shape[0]
    lane = lax.broadcasted_iota(jnp.int32, (n, LANES), 1)
    for h in range(FOX_HEADS):
        pair = slice(LANES * (h // 2), LANES * (h // 2) + LANES)
        own = (lane // 64) == (h % 2)
        e0 = 64 * (1 - h % 2)
        hi, mid, lo = _split3(jnp.broadcast_to(cum[:, h:h + 1], (n, LANES)))
        ones = (lane >= e0 + 3) & (lane < e0 + 6)
        q_extra = jnp.where(lane == e0, hi, jnp.where(lane == e0 + 1, mid, jnp.where(
            lane == e0 + 2, lo, jnp.where(ones, 1.0, 0.0))))
        ones = (lane >= e0) & (lane < e0 + 3)
        k_extra = jnp.where(lane == e0 + 3, -hi, jnp.where(lane == e0 + 4, -mid, jnp.where(
            lane == e0 + 5, -lo, jnp.where(ones, 1.0, 0.0))))
        q = fq[:, pair].astype(F32) * 0.125
        k = fk[:, pair].astype(F32)
        qa_ref[:, LANES * h:LANES * (h + 1)] = jnp.where(own, q, q_extra).astype(BF16)
        ka_ref[:, LANES * h:LANES * (h + 1)] = jnp.where(own, k, k_extra).astype(BF16)


def _fox_prep_meta_kernel(zg_ref, fq_ref, fk_ref, fb_ref, qa_ref, ka_ref, carry_ref):
    n = NB * N_META
    lf = _log_sigmoid(zg_ref[...] + fb_ref[...])
    row = lax.broadcasted_iota(jnp.int32, (n, n), 0)
    col = lax.broadcasted_iota(jnp.int32, (n, n), 1)
    tri = ((col <= row) & (col // N_META == row // N_META)).astype(BF16)
    cum = _cumsum_rows(tri, lf)
    brow = lax.broadcasted_iota(jnp.int32, (NB, n), 0)
    bcol = lax.broadcasted_iota(jnp.int32, (NB, n), 1)
    carry_ref[...] = _cumsum_rows((bcol // N_META == brow).astype(BF16), lf)
    _fox_augment(fq_ref[...], fk_ref[...], cum, qa_ref, ka_ref)


def _fox_prep_kernel(zg_ref, fq_ref, fk_ref, fb_ref, carry_in_ref, qa_ref, ka_ref, carry_ref):
    b = pl.program_id(0)
    n = zg_ref.shape[0]

    @pl.when(pl.program_id(1) == 0)
    def _():
        carry_ref[...] = carry_in_ref[pl.ds(b, 1), :]

    lf = _log_sigmoid(zg_ref[...] + fb_ref[...])
    row = lax.broadcasted_iota(jnp.int32, (n, n), 0)
    col = lax.broadcasted_iota(jnp.int32, (n, n), 1)
    cum = _cumsum_rows((col <= row).astype(BF16), lf) + carry_ref[...]
    carry_ref[...] = cum[n - 1:n, :]
    _fox_augment(fq_ref[...], fk_ref[...], cum, qa_ref, ka_ref)


FOX_PREP_TILE = 1024


def _fox_prep(zg, za, seq, fb):
    tr = NB * seq
    nm = NB * N_META
    mblk = tr // nm
    fb = jnp.pad(fb.astype(F32), (0, LANES - FOX_HEADS)).reshape(1, LANES)
    fbspec = pl.BlockSpec((1, LANES), lambda *_: (0, 0))
    aug = FOX_HEADS * LANES
    qa_m, ka_m, carry = pl.pallas_call(
        _fox_prep_meta_kernel,
        grid=(1,),
        in_specs=[pl.BlockSpec((nm, LANES), lambda i: (mblk, 0)),
                  pl.BlockSpec((nm, 256), lambda i: (mblk, 4)),
                  pl.BlockSpec((nm, 256), lambda i: (mblk, 5)),
                  fbspec],
        out_specs=[pl.BlockSpec((nm, aug), lambda i: (0, 0)),
                   pl.BlockSpec((nm, aug), lambda i: (0, 0)),
                   pl.BlockSpec((NB, LANES), lambda i: (0, 0))],
        out_shape=[jax.ShapeDtypeStruct((nm, aug), BF16),
                   jax.ShapeDtypeStruct((nm, aug), BF16),
                   jax.ShapeDtypeStruct((NB, LANES), F32)],
        compiler_params=_cp(("arbitrary",)),
        name="fox_prep_meta",
    )(zg, za, za, fb)
    nc = seq // FOX_PREP_TILE
    qa, ka = pl.pallas_call(
        _fox_prep_kernel,
        grid=(NB, nc),
        in_specs=[pl.BlockSpec((FOX_PREP_TILE, LANES), lambda b, c: (b * nc + c, 0)),
                  pl.BlockSpec((FOX_PREP_TILE, 256), lambda b, c: (b * nc + c, 4)),
                  pl.BlockSpec((FOX_PREP_TILE, 256), lambda b, c: (b * nc + c, 5)),
                  fbspec,
                  pl.BlockSpec((NB, LANES), lambda b, c: (0, 0))],
        out_specs=[pl.BlockSpec((FOX_PREP_TILE, aug), lambda b, c: (b * nc + c, 0)),
                   pl.BlockSpec((FOX_PREP_TILE, aug), lambda b, c: (b * nc + c, 0))],
        out_shape=[jax.ShapeDtypeStruct((tr, aug), BF16),
                   jax.ShapeDtypeStruct((tr, aug), BF16)],
        scratch_shapes=[pltpu.VMEM((1, LANES), F32)],
        compiler_params=_cp(("parallel", "arbitrary")),
        name="fox_prep",
    )(zg, za, za, fb, carry)
    return qa, ka, qa_m, ka_m


def _fox_finish(parts):
    (l0, acc0), (l1, acc1) = parts
    lane = lax.broadcasted_iota(jnp.int32, acc0.shape, 1)
    return jnp.where(lane < 64, acc0 / l0, acc1 / l1).astype(BF16)


def _fox_kernel(q_ref, k_ref, vt_ref, km_ref, vmt_ref, o_ref, *scratch):
    m_ref, l_ref, acc_ref, al_ref, p_ref = _split_map_scratch(scratch, FOX_HEADS)
    hd = 64
    vmt = vmt_ref[...]
    qts, k_cols, vt_rows, kms, vmts = [], [], [], [], []
    for h in range(FOX_HEADS):
        sl = slice(LANES * h, LANES * (h + 1))
        qts.append(_transpose_bf16(q_ref[:, sl]))
        k_cols.append(sl)
        vt_rows.append(slice(hd * h, hd * (h + 1)))
        kms.append(km_ref[:, sl])
        vmts.append(vmt[hd * h:hd * (h + 1), :])
    _causal_sweep(qts, k_ref, k_cols, vt_ref, vt_rows, kms, vmts, pl.program_id(1),
                  m_ref, l_ref, acc_ref, al_ref, p_ref)
    for p in range(FOX_HEADS // 2):
        o = jnp.concatenate([acc_ref[2 * p][...] / l_ref[2 * p][...],
                             acc_ref[2 * p + 1][...] / l_ref[2 * p + 1][...]],
                            axis=0)
        o_ref[:, LANES * p:LANES * (p + 1)] = o.T.astype(BF16)


def _fox_meta_kernel(q_ref, km_ref, vmt_ref, _flat_ref, o_ref):
    vmt = vmt_ref[...]
    parts = [_meta_attend(q_ref[:, LANES * hh:LANES * (hh + 1)],
                          km_ref[:, LANES * hh:LANES * (hh + 1)], vmt) for hh in range(2)]
    o_ref[...] = _fox_finish(parts)


def _fox_attention(t, vt, vtm, seq, qa, ka, qa_m, ka_m):
    nq = seq // ATT_TQ
    mrow = NB * seq // N_META
    aug = FOX_HEADS * LANES
    flat = pl.pallas_call(
        _fox_kernel,
        grid=(NB, nq),
        in_specs=[pl.BlockSpec((ATT_TQ, aug), lambda b, i: (b * nq + i, 0)),
                  pl.BlockSpec((seq, aug), lambda b, i: (b, 0)),
                  pl.BlockSpec((seq // LANES, 256, LANES), lambda b, i: (b, 2, 0)),
                  pl.BlockSpec((N_META, aug), lambda b, i: (b, 0)),
                  pl.BlockSpec((None, 256, N_META), lambda b, i: (b, 2, 0))],
        out_specs=pl.BlockSpec((ATT_TQ, 256), lambda b, i: (b * nq + i, 0)),
        out_shape=jax.ShapeDtypeStruct((t, 256), BF16),
        scratch_shapes=_map_scratch(FOX_HEADS, 64),
        compiler_params=_cp(("parallel", "arbitrary")),
        name="fox_attn",
    )(qa, ka, vt, ka_m, vtm)
    return pl.pallas_call(
        _fox_meta_kernel,
        grid=(NB, FOX_HEADS // 2),
        in_specs=[pl.BlockSpec((N_META, 256), lambda b, p: (b, p)),
                  pl.BlockSpec((N_META, 256), lambda b, p: (b, p)),
                  pl.BlockSpec((None, LANES, N_META), lambda b, p: (b, 4 + p, 0)),
                  pl.BlockSpec(memory_space=pl.ANY)],
        out_specs=pl.BlockSpec((N_META, LANES), lambda b, p: (mrow + b, p)),
        out_shape=jax.ShapeDtypeStruct((t, 256), BF16),
        input_output_aliases={3: 0},
        compiler_params=_cp(("parallel", "parallel")),
        name="fox_attn_meta",
    )(qa_m, ka_m, vtm, flat)


def _router_kernel(res_ref, g_ref, wr_ref, idx_ref, gate_ref):
    h = _rms(res_ref[...], g_ref[...])
    w = wr_ref[...]
    hi = h.astype(BF16)
    lo = (h - hi.astype(F32)).astype(BF16)
    whi = w.astype(BF16)
    wlo = (w - whi.astype(F32)).astype(BF16)
    logits = (jnp.dot(hi, whi, preferred_element_type=F32)
              + jnp.dot(lo, whi, preferred_element_type=F32)
              + jnp.dot(hi, wlo, preferred_element_type=F32))
    lane = lax.broadcasted_iota(jnp.int32, logits.shape, 1)
    logits = jnp.where(lane < N_EXPERTS, logits, -jnp.inf)
    m1 = jnp.max(logits, axis=1, keepdims=True)
    i1 = jnp.min(jnp.where(logits == m1, lane, LANES), axis=1, keepdims=True)
    rest = jnp.where(lane == i1, -jnp.inf, logits)
    m2 = jnp.max(rest, axis=1, keepdims=True)
    i2 = jnp.min(jnp.where(rest == m2, lane, LANES), axis=1, keepdims=True)
    e = jnp.exp(m2 - m1)
    g1 = 1.0 / (1.0 + e)
    g2 = e / (1.0 + e)
    idx_ref[...] = jnp.where(lane == 0, i1, jnp.where(lane == 1, i2, 0))
    gate_ref[...] = jnp.where(lane == 0, g1, jnp.where(lane == 1, g2, 0.0))


def _router(res, g, wr):
    t = res.shape[0]
    return pl.pallas_call(
        _router_kernel,
        grid=(t // TOK_TILE,),
        in_specs=[pl.BlockSpec((TOK_TILE, D_MODEL), lambda i: (i, 0)),
                  pl.BlockSpec((1, D_MODEL), lambda i: (0, 0)),
                  pl.BlockSpec((D_MODEL, LANES), lambda i: (0, 0))],
        out_specs=[pl.BlockSpec((TOK_TILE, LANES), lambda i: (i, 0)),
                   pl.BlockSpec((TOK_TILE, LANES), lambda i: (i, 0))],
        out_shape=[jax.ShapeDtypeStruct((t, LANES), jnp.int32),
                   jax.ShapeDtypeStruct((t, LANES), F32)],
        compiler_params=_cp(("parallel",)),
        name="router",
    )(res, g, wr)


def _gather_copy(src_hbm, first, dst_ref, r, sem):
    first = pl.multiple_of(first, ROW_BLOCKS)
    dst = pl.multiple_of(r * ROW_BLOCKS, ROW_BLOCKS)
    return pltpu.make_async_copy(src_hbm.at[pl.ds(first, ROW_BLOCKS), :],
                                 dst_ref.at[pl.ds(dst, ROW_BLOCKS), :], sem)


def _expert_ffn_kernel(nf, te_ref, na_ref, src_ref, x_hbm, g_ref, w1_ref, w3_ref, w2_ref,
                       o_ref, xbuf_ref, h_ref, acc_ref, sem):
    r = pl.program_id(0)
    f = pl.program_id(1)
    n_act = na_ref[0]
    per_step = MOE_XROWS // nf

    def tile_copy(slot):
        return pltpu.make_async_copy(x_hbm.at[pl.ds(0, MOE_XROWS * ROW_BLOCKS), :],
                                     xbuf_ref.at[slot], sem.at[slot])

    def start_row(tile, slot, k, priority=0):
        _gather_copy(x_hbm, src_ref[tile * MOE_XROWS + k], xbuf_ref.at[slot], k,
                     sem.at[slot]).start(priority=priority)

    @pl.when(r < n_act)
    def _():
        slot = lax.rem(r, 2)

        @pl.when((r == 0) & (f == 0))
        def _():
            def first(k, _):
                start_row(0, 0, k)
                return 0
            lax.fori_loop(0, MOE_XROWS, first, 0, unroll=8)

        @pl.when(f == 0)
        def _():
            tile_copy(slot).wait()
            x = _load_row_tiles(xbuf_ref.at[slot], MOE_TILE)
            h_ref[...] = _rms(x, g_ref[...]).astype(BF16)
            acc_ref[...] = jnp.zeros_like(acc_ref)

        nxt = jnp.minimum(r + 1, pl.num_programs(0) - 1)
        for k in range(per_step):
            start_row(nxt, 1 - slot, f * per_step + k, priority=1)

        h = h_ref[...]
        a = jnp.dot(h, w1_ref[...].astype(BF16), preferred_element_type=F32)
        b = jnp.dot(h, w3_ref[...].astype(BF16), preferred_element_type=F32)
        hh = (a * jax.nn.sigmoid(a) * b).astype(BF16)
        acc_ref[...] += jnp.dot(hh, w2_ref[...].astype(BF16), preferred_element_type=F32)

        @pl.when(f == nf - 1)
        def _():
            _store_row_tiles(o_ref, acc_ref[...])

        @pl.when((f == nf - 1) & (r == n_act - 1))
        def _():
            tile_copy(1 - slot).wait()


def _expert_ffn(tile_expert, n_active, src, x, g, w1, w3, w2):
    dff = w1.shape[2]
    nt = src.shape[0] // MOE_XROWS
    nf = dff // MOE_FCHUNK
    assert MOE_XROWS % nf == 0 and MOE_XROWS >= MOE_TILE

    def row(r, f, te, na, src):
        return jnp.minimum(r, na[0] - 1)

    def fch(r, f, te, na, src):
        return jnp.where(r < na[0], f, nf - 1)

    return pl.pallas_call(
        functools.partial(_expert_ffn_kernel, nf),
        grid_spec=pltpu.PrefetchScalarGridSpec(
            num_scalar_prefetch=3,
            grid=(nt, nf),
            in_specs=[
                pl.BlockSpec(memory_space=pl.ANY),
                pl.BlockSpec((1, D_MODEL), lambda r, f, te, na, src: (0, 0)),
                pl.BlockSpec((None, D_MODEL, MOE_FCHUNK),
                             lambda r, f, te, na, src: (te[r], 0, fch(r, f, te, na, src))),
                pl.BlockSpec((None, D_MODEL, MOE_FCHUNK),
                             lambda r, f, te, na, src: (te[r], 0, fch(r, f, te, na, src))),
                pl.BlockSpec((None, MOE_FCHUNK, D_MODEL),
                             lambda r, f, te, na, src: (te[r], fch(r, f, te, na, src), 0))],
            out_specs=pl.BlockSpec((MOE_TILE * ROW_BLOCKS, LANES),
                                   lambda r, f, te, na, src: (row(r, f, te, na, src), 0)),
            scratch_shapes=[pltpu.VMEM((2, MOE_XROWS * ROW_BLOCKS, LANES), F32),
                            pltpu.VMEM((MOE_TILE, D_MODEL), BF16),
                            pltpu.VMEM((MOE_TILE, D_MODEL), F32),
                            pltpu.SemaphoreType.DMA((2,))]),
        out_shape=jax.ShapeDtypeStruct((nt * MOE_TILE * ROW_BLOCKS, LANES), F32),
        compiler_params=_cp(("arbitrary", "arbitrary")),
        name="expert_ffn",
    )(tile_expert, n_active, src, x, g, w1, w3, w2)


def _combine_kernel(tile, final, pos_ref, res_ref, gate_ref, gf_ref, y_hbm, o_ref, ybuf_ref, sem):
    i = pl.program_id(0)
    n = pl.num_programs(0)
    slot = lax.rem(i, 2)

    def issue_tile(step, s):
        base = step * tile

        def issue(r, _):
            for k in range(2):
                _gather_copy(y_hbm, pos_ref[2 * (base + r) + k], ybuf_ref.at[s, k], r,
                             sem.at[s]).start(priority=k)
            return 0

        lax.fori_loop(0, tile, issue, 0, unroll=8)

    @pl.when(i == 0)
    def _():
        issue_tile(0, 0)

    @pl.when(i + 1 < n)
    def _():
        issue_tile(i + 1, 1 - slot)

    pltpu.make_async_copy(ybuf_ref.at[slot], ybuf_ref.at[slot], sem.at[slot]).wait()
    gate = gate_ref[...]
    out = (res_ref[...] + gate[:, 0:1] * _load_row_tiles(ybuf_ref.at[slot, 0], tile)
           + gate[:, 1:2] * _load_row_tiles(ybuf_ref.at[slot, 1], tile))
    o_ref[...] = _rms(out, gf_ref[...]) if final else out


def _combine(pos, res, gates, y, final_g=None):
    final = final_g is not None
    t = res.shape[0]
    n_rows = t - NB * N_META if final else t
    tile = 256 if final else COMBINE_TILE
    if not final:
        final_g = jnp.ones((1, D_MODEL), F32)
    return pl.pallas_call(
        functools.partial(_combine_kernel, tile, final),
        grid_spec=pltpu.PrefetchScalarGridSpec(
            num_scalar_prefetch=1,
            grid=(n_rows // tile,),
            in_specs=[pl.BlockSpec((tile, D_MODEL), lambda i, pos: (i, 0)),
                      pl.BlockSpec((tile, LANES), lambda i, pos: (i, 0)),
                      pl.BlockSpec((1, D_MODEL), lambda i, pos: (0, 0)),
                      pl.BlockSpec(memory_space=pl.ANY)],
            out_specs=pl.BlockSpec((tile, D_MODEL), lambda i, pos: (i, 0)),
            scratch_shapes=[pltpu.VMEM((2, 2, tile * ROW_BLOCKS, LANES), F32),
                            pltpu.SemaphoreType.DMA((2,))]),
        out_shape=jax.ShapeDtypeStruct((n_rows, D_MODEL), F32),
        compiler_params=_cp(("arbitrary",)),
        name="moe_combine",
    )(pos, res, gates, final_g, y)


def _moe(res, res_tiles, g, wr, w1, w3, w2, final_g=None):
    t = res.shape[0]
    wr = jnp.pad(wr.astype(F32), ((0, 0), (0, LANES - N_EXPERTS)))
    idx, gates = _router(res, g, wr)
    e_flat = idx[:, :2].reshape(-1)
    onehot = (e_flat[:, None] == jnp.arange(N_EXPERTS, dtype=jnp.int32)[None, :]).astype(jnp.int32)
    csum = jnp.cumsum(onehot, axis=0)
    rank = jnp.take_along_axis(csum, e_flat[:, None], axis=1)[:, 0] - 1
    counts = csum[-1]
    tiles = (counts + MOE_TILE - 1) // MOE_TILE
    tile_end = jnp.cumsum(tiles)
    starts = (tile_end - tiles) * MOE_TILE
    pos = (starts[e_flat] + rank).astype(jnp.int32)
    n_tiles = (2 * t + N_EXPERTS * (MOE_TILE - 1)) // MOE_TILE
    spos = (pos // MOE_TILE) * MOE_XROWS + pos % MOE_TILE
    src = jnp.zeros((n_tiles * MOE_XROWS,), jnp.int32).at[spos].set(
        (jnp.arange(2 * t, dtype=jnp.int32) // 2) * ROW_BLOCKS, unique_indices=True)
    n_active = tile_end[-1:].astype(jnp.int32)
    tile_ids = jnp.minimum(jnp.arange(n_tiles, dtype=jnp.int32), n_active[0] - 1)
    tile_expert = jnp.sum(tile_ids[:, None] >= tile_end[None, :], axis=1).astype(jnp.int32)

    y = _expert_ffn(tile_expert, n_active, src, res_tiles, g, w1, w3, w2)
    return _combine(pos * ROW_BLOCKS, res, gates, y, final_g)


def _final_norm_kernel(res_ref, g_ref, o_ref):
    o_ref[...] = _rms(res_ref[...], g_ref[...])


def _final_norm(res, g, n_rows):
    tile = 1024
    return pl.pallas_call(
        _final_norm_kernel,
        grid=(n_rows // tile,),
        in_specs=[pl.BlockSpec((tile, D_MODEL), lambda i: (i, 0)),
                  pl.BlockSpec((1, D_MODEL), lambda i: (0, 0))],
        out_specs=pl.BlockSpec((tile, D_MODEL), lambda i: (i, 0)),
        out_shape=jax.ShapeDtypeStruct((n_rows, D_MODEL), F32),
        compiler_params=_cp(("parallel",)),
        name="final_norm",
    )(res, g)


def kernel(x, meta_tokens, norm_mix_g, w_in, w_out, ssm_lambda_re, ssm_lambda_im, ssm_log_dt,
           ssm_b_re, ssm_b_im, ssm_c_re, ssm_c_im, ssm_d, ssm_w_glu, diff_lambda_q1,
           diff_lambda_k1, diff_lambda_q2, diff_lambda_k2, diff_subln_g, fox_forget_b,
           norm_ffn_g, dense_w1, dense_w3, dense_w2, moe_router, moe_w1, moe_w3, moe_w2,
           final_norm_g):
    bsz, seq, d = x.shape
    assert bsz == NB and d == D_MODEL and seq % ATT_TQ == 0
    depth = w_in.shape[0]
    tr = bsz * seq
    res = jnp.concatenate(
        [x.reshape(tr, d), jnp.tile(meta_tokens.astype(x.dtype), (bsz, 1))], axis=0)
    row = lambda v: v.reshape(1, -1).astype(F32)

    for l in range(depth):
        wl = w_in[l]
        w = jnp.concatenate([wl[:, :1280], wl[:, 1792:2304],
                             jnp.pad(wl[:, 2560:], ((0, 0), (0, LANES - FOX_HEADS)))], axis=1)
        wvt = jnp.concatenate([wl[:, 1280:1792], wl[:, 2304:2560]], axis=1).T
        zu, za, zg, vt, vtm = _inproj(res, row(norm_mix_g[l]), w.astype(BF16),
                                      wvt.astype(BF16))

        a, bd, cd, dskip = _s5_params(ssm_lambda_re[l], ssm_lambda_im[l], ssm_log_dt[l],
                                      ssm_b_re[l], ssm_b_im[l], ssm_c_re[l], ssm_c_im[l],
                                      ssm_d[l])
        ssm_out = _s5(zu, seq, a, bd, cd, dskip, ssm_w_glu[l].astype(BF16))

        lam_init = 0.8 - 0.6 * math.exp(-0.3 * l)
        lam = (jnp.exp(jnp.sum(diff_lambda_q1[l] * diff_lambda_k1[l]))
               - jnp.exp(jnp.sum(diff_lambda_q2[l] * diff_lambda_k2[l])) + lam_init)
        diff_out = _diff_attention(za, vt, vtm, seq, lam, lam_init, diff_subln_g[l])

        qa, ka, qa_m, ka_m = _fox_prep(zg, za, seq, fox_forget_b[l])
        fox_out = _fox_attention(za.shape[0], vt, vtm, seq, qa, ka, qa_m, ka_m)

        is_moe = l % 2 == 1
        res, *res_tiles = _outproj(res, ssm_out, diff_out, fox_out, w_out[l].astype(BF16),
                                   with_row_tiles=is_moe)

        if not is_moe:
            res = _dense_ffn(res, row(norm_ffn_g[l]), dense_w1[l // 2].astype(BF16),
                             dense_w3[l // 2].astype(BF16), dense_w2[l // 2].astype(BF16))
        else:
            res = _moe(res, res_tiles[0], row(norm_ffn_g[l]), moe_router[l // 2], moe_w1[l // 2],
                       moe_w3[l // 2], moe_w2[l // 2],
                       final_g=row(final_norm_g) if l == depth - 1 else None)

    if depth % 2 == 1:
        res = _final_norm(res, row(final_norm_g), tr)
    return res.reshape(bsz, seq, d)
```

```python
import functools
import math

import jax
import jax.numpy as jnp
from jax import lax
from jax.experimental import pallas as pl
from jax.experimental.pallas import tpu as pltpu

F32 = jnp.float32
BF16 = jnp.bfloat16
EPS = 1e-6

D_MODEL = 1024
N_META = 16
NB = 8
SSM_WIDTH = 256
SSM_GROUPS = 16
SSM_GROUP = 16
SSM_STATE = 64
SSM_COLS = 2 * SSM_GROUPS * SSM_STATE
DIFF_HEADS = 4
FOX_HEADS = 4
ATT_COLS = 1536
IN_PAD = 1920
N_EXPERTS = 8
LANES = 128
MASK_VALUE = -1e30
LOG2E = math.log2(math.e)
QK_SCALE = 64 ** -0.5 * LOG2E

VT_ROWS = 768
TOK_TILE = 688
IN_TILE = 384
FFN_TILE = 384
ATT_TQ = 256
ATT_TK = 256
S5_CHUNK = 256
MOE_TILE = 1024
MOE_FCHUNK = 512
MOE_XROWS = 1029
MOE_XSLOTS = 3
ROW_BLOCKS = D_MODEL // LANES
COMBINE_TILE = 384

VMEM_LIMIT = 56 * 1024 * 1024


def _cp(sem):
    return pltpu.CompilerParams(dimension_semantics=sem, vmem_limit_bytes=VMEM_LIMIT)


def _rms(x, g):
    return x * lax.rsqrt(jnp.mean(x * x, axis=-1, keepdims=True) + EPS) * g


def _inproj_kernel(res_ref, g_ref, w_ref, wvt_ref, zu_ref, za_ref, zg_ref, vt_ref, vtm_ref):
    h = _rms(res_ref[...], g_ref[...]).astype(BF16)
    zu_ref[...] = jnp.dot(h, w_ref[:, 0:SSM_WIDTH], preferred_element_type=F32)
    for c in range(0, ATT_COLS, 256):
        za_ref[:, c:c + 256] = jnp.dot(
            h, w_ref[:, SSM_WIDTH + c:SSM_WIDTH + c + 256],
            preferred_element_type=F32).astype(BF16)
    zg_ref[...] = jnp.dot(h, w_ref[:, SSM_WIDTH + ATT_COLS:IN_PAD],
                          preferred_element_type=F32)
    vt = _nt_dot(wvt_ref[...], h).astype(BF16)
    for c in range(IN_TILE // LANES):
        vt_ref[c] = vt[:, LANES * c:LANES * (c + 1)]

    @pl.when(pl.program_id(0) == pl.num_programs(0) - 1)
    def _():
        first = IN_TILE - NB * N_META
        for b in range(NB):
            vtm_ref[b] = vt[:, first + N_META * b:first + N_META * (b + 1)]


def _inproj(res, g, w, wvt):
    t = res.shape[0]
    nblk = IN_TILE // LANES
    assert t % IN_TILE == 0 and NB * N_META <= IN_TILE
    return pl.pallas_call(
        _inproj_kernel,
        grid=(t // IN_TILE,),
        in_specs=[pl.BlockSpec((IN_TILE, D_MODEL), lambda i: (i, 0)),
                  pl.BlockSpec((1, D_MODEL), lambda i: (0, 0)),
                  pl.BlockSpec((D_MODEL, IN_PAD), lambda i: (0, 0)),
                  pl.BlockSpec((VT_ROWS, D_MODEL), lambda i: (0, 0))],
        out_specs=[pl.BlockSpec((IN_TILE, SSM_WIDTH), lambda i: (i, 0)),
                   pl.BlockSpec((IN_TILE, ATT_COLS), lambda i: (i, 0)),
                   pl.BlockSpec((IN_TILE, LANES), lambda i: (i, 0)),
                   pl.BlockSpec((nblk, VT_ROWS, LANES), lambda i: (i, 0, 0)),
                   pl.BlockSpec((NB, VT_ROWS, N_META), lambda i: (0, 0, 0))],
        out_shape=[jax.ShapeDtypeStruct((t, SSM_WIDTH), F32),
                   jax.ShapeDtypeStruct((t, ATT_COLS), BF16),
                   jax.ShapeDtypeStruct((t, LANES), F32),
                   jax.ShapeDtypeStruct((t // LANES, VT_ROWS, LANES), BF16),
                   jax.ShapeDtypeStruct((NB, VT_ROWS, N_META), BF16)],
        compiler_params=_cp(("arbitrary",)),
        name="inproj",
    )(res, g, w, wvt)


def _store_row_tiles(ref, x):
    n = x.shape[0]
    for j in range(ROW_BLOCKS):
        ref[pl.ds(j, n, stride=ROW_BLOCKS), :] = x[:, LANES * j:LANES * (j + 1)]


def _load_row_tiles(ref, n):
    return jnp.concatenate(
        [ref[pl.ds(j, n, stride=ROW_BLOCKS), :] for j in range(ROW_BLOCKS)], axis=1)


def _outproj_kernel(res_ref, s_ref, d_ref, f_ref, w_ref, o_ref, *tiled_ref):
    mixed = jnp.concatenate([s_ref[...], d_ref[...], f_ref[...]], axis=1)
    out = res_ref[...] + jnp.dot(mixed, w_ref[...], preferred_element_type=F32)
    o_ref[...] = out
    if tiled_ref:
        _store_row_tiles(tiled_ref[0], out)


def _outproj(res, ssm, diff, fox, w, with_row_tiles):
    t = res.shape[0]
    out_specs = [pl.BlockSpec((TOK_TILE, D_MODEL), lambda i: (i, 0))]
    out_shape = [jax.ShapeDtypeStruct((t, D_MODEL), F32)]
    if with_row_tiles:
        out_specs.append(pl.BlockSpec((TOK_TILE * ROW_BLOCKS, LANES), lambda i: (i, 0)))
        out_shape.append(jax.ShapeDtypeStruct((t * ROW_BLOCKS, LANES), F32))
    return pl.pallas_call(
        _outproj_kernel,
        grid=(t // TOK_TILE,),
        in_specs=[pl.BlockSpec((TOK_TILE, D_MODEL), lambda i: (i, 0)),
                  pl.BlockSpec((TOK_TILE, 256), lambda i: (i, 0)),
                  pl.BlockSpec((TOK_TILE, 512), lambda i: (i, 0)),
                  pl.BlockSpec((TOK_TILE, 256), lambda i: (i, 0)),
                  pl.BlockSpec((D_MODEL, D_MODEL), lambda i: (0, 0))],
        out_specs=out_specs,
        out_shape=out_shape,
        compiler_params=_cp(("parallel",)),
        name="outproj",
    )(res, ssm, diff, fox, w)


def _dense_ffn_kernel(res_ref, g_ref, w1_ref, w3_ref, w2_ref, o_ref):
    x = res_ref[...]
    h = _rms(x, g_ref[...]).astype(BF16)
    a = jnp.dot(h, w1_ref[...], preferred_element_type=F32)
    b = jnp.dot(h, w3_ref[...], preferred_element_type=F32)
    hh = (a * jax.nn.sigmoid(a) * b).astype(BF16)
    o_ref[...] = x + jnp.dot(hh, w2_ref[...], preferred_element_type=F32)


def _dense_ffn(res, g, w1, w3, w2):
    t = res.shape[0]
    dff = w1.shape[1]
    once = pl.Buffered(1)
    return pl.pallas_call(
        _dense_ffn_kernel,
        grid=(t // FFN_TILE,),
        in_specs=[pl.BlockSpec((FFN_TILE, D_MODEL), lambda i: (i, 0)),
                  pl.BlockSpec((1, D_MODEL), lambda i: (0, 0)),
                  pl.BlockSpec((D_MODEL, dff), lambda i: (0, 0), pipeline_mode=once),
                  pl.BlockSpec((D_MODEL, dff), lambda i: (0, 0), pipeline_mode=once),
                  pl.BlockSpec((dff, D_MODEL), lambda i: (0, 0), pipeline_mode=once)],
        out_specs=pl.BlockSpec((FFN_TILE, D_MODEL), lambda i: (i, 0)),
        out_shape=jax.ShapeDtypeStruct((t, D_MODEL), F32),
        compiler_params=_cp(("parallel",)),
        name="dense_ffn",
    )(res, g, w1, w3, w2)


def _gelu_tanh(x):
    c = math.sqrt(2.0 / math.pi)
    return 0.5 * x * (1.0 + jnp.tanh(c * (x + 0.044715 * (x * x * x))))


def _s5_chunk(lc, get_u, state_ref, ut_ref, bu_ref, ot_ref,
              a_ref, bd_ref, cd_ref, dskip_ref, wglu_ref):
    for b in range(NB):
        ub = get_u(b)
        for s in range(2):
            ut_ref[s, pl.ds(b, lc, stride=NB), :] = ub[:, LANES * s:LANES * (s + 1)]
    u_tm = jnp.concatenate([ut_ref[0], ut_ref[1]], axis=1)
    bu_ref[...] = jnp.dot(u_tm.astype(BF16), bd_ref[...], preferred_element_type=F32)

    half = SSM_COLS // 2
    a_re = a_ref[:, :half]
    a_im = a_ref[:, half:]

    def step(t, x):
        x_re, x_im = x
        r = pl.multiple_of(t * NB, NB)
        cur = bu_ref[pl.ds(r, NB), :]
        n_re = a_re * x_re - a_im * x_im + cur[:, :half]
        n_im = a_re * x_im + a_im * x_re + cur[:, half:]
        bu_ref[pl.ds(r, NB), :] = jnp.concatenate([n_re, n_im], axis=1)
        return n_re, n_im

    x_re, x_im = lax.fori_loop(0, lc, step,
                               (state_ref[:, :half], state_ref[:, half:]), unroll=4)
    state_ref[...] = jnp.concatenate([x_re, x_im], axis=1)

    y = jnp.dot(bu_ref[...].astype(BF16), cd_ref[...], preferred_element_type=F32)
    y = _gelu_tanh(y + dskip_ref[...] * u_tm)
    g = jnp.dot(y.astype(BF16), wglu_ref[...], preferred_element_type=F32)
    o = g[:, :SSM_WIDTH] * jax.nn.sigmoid(g[:, SSM_WIDTH:])
    ot_ref[0] = o[:, :LANES]
    ot_ref[1] = o[:, LANES:]


def _s5_read_out(ot_ref, b, lc):
    return jnp.concatenate(
        [ot_ref[s, pl.ds(b, lc, stride=NB), :] for s in range(2)], axis=1)


def _s5_meta_kernel(u_ref, a_ref, bd_ref, cd_ref, dskip_ref, wglu_ref,
                    o_ref, state_out_ref, state_ref, ut_ref, bu_ref, ot_ref):
    state_ref[...] = jnp.zeros_like(state_ref)
    _s5_chunk(N_META, lambda b: u_ref[b * N_META:(b + 1) * N_META, :],
              state_ref, ut_ref, bu_ref, ot_ref, a_ref, bd_ref, cd_ref, dskip_ref, wglu_ref)
    for b in range(NB):
        o_ref[b * N_META:(b + 1) * N_META, :] = _s5_read_out(ot_ref, b, N_META).astype(BF16)
    state_out_ref[...] = state_ref[...]


def _s5_real_kernel(*refs):
    u_refs = refs[:NB]
    (state_in_ref, a_ref, bd_ref, cd_ref, dskip_ref, wglu_ref, _flat_ref,
     o_ref, state_ref, ut_ref, bu_ref, ot_ref) = refs[NB:]
    c = pl.program_id(0)
    b = pl.program_id(1)

    @pl.when((c == 0) & (b == 0))
    def _():
        state_ref[...] = state_in_ref[...]

    @pl.when(b == 0)
    def _():
        _s5_chunk(S5_CHUNK, lambda bb: u_refs[bb][...],
                  state_ref, ut_ref, bu_ref, ot_ref, a_ref, bd_ref, cd_ref, dskip_ref, wglu_ref)

    o_ref[...] = _s5_read_out(ot_ref, b, S5_CHUNK).astype(BF16)


def _s5(zu, seq, a, bd, cd, dskip, wglu):
    t = zu.shape[0]
    tr = NB * seq
    nmeta_rows = NB * N_META
    const = lambda *_: (0, 0)
    par_specs = [pl.BlockSpec((NB, SSM_COLS), const),
                 pl.BlockSpec((SSM_WIDTH, SSM_COLS), const),
                 pl.BlockSpec((SSM_COLS, SSM_WIDTH), const),
                 pl.BlockSpec((1, SSM_WIDTH), const),
                 pl.BlockSpec((SSM_WIDTH, 2 * SSM_WIDTH), const)]

    def scratch(lc):
        return [pltpu.VMEM((NB, SSM_COLS), F32),
                pltpu.VMEM((2, lc * NB, LANES), F32),
                pltpu.VMEM((lc * NB, SSM_COLS), F32),
                pltpu.VMEM((2, lc * NB, LANES), F32)]

    meta_blk = tr // nmeta_rows
    flat, state = pl.pallas_call(
        _s5_meta_kernel,
        grid=(1,),
        in_specs=[pl.BlockSpec((nmeta_rows, SSM_WIDTH), lambda i: (meta_blk, 0))] + par_specs,
        out_specs=[pl.BlockSpec((nmeta_rows, SSM_WIDTH), lambda i: (meta_blk, 0)),
                   pl.BlockSpec((NB, SSM_COLS), const)],
        out_shape=[jax.ShapeDtypeStruct((t, SSM_WIDTH), BF16),
                   jax.ShapeDtypeStruct((NB, SSM_COLS), F32)],
        scratch_shapes=scratch(N_META),
        compiler_params=_cp(("arbitrary",)),
        name="s5_meta",
    )(zu, a, bd, cd, dskip, wglu)

    nc = seq // S5_CHUNK
    u_specs = [pl.BlockSpec((S5_CHUNK, SSM_WIDTH), lambda c, b, bb=bb: (bb * nc + c, 0))
               for bb in range(NB)]
    n_in = NB + 1 + len(par_specs)
    return pl.pallas_call(
        _s5_real_kernel,
        grid=(nc, NB),
        in_specs=u_specs + [pl.BlockSpec((NB, SSM_COLS), const)] + par_specs
        + [pl.BlockSpec(memory_space=pl.ANY)],
        out_specs=pl.BlockSpec((S5_CHUNK, SSM_WIDTH), lambda c, b: (b * nc + c, 0)),
        out_shape=jax.ShapeDtypeStruct((t, SSM_WIDTH), BF16),
        scratch_shapes=scratch(S5_CHUNK),
        input_output_aliases={n_in: 0},
        compiler_params=_cp(("arbitrary", "arbitrary")),
        name="s5_real",
    )(*([zu] * NB), state, a, bd, cd, dskip, wglu, flat)


def _s5_params(lam_re, lam_im, log_dt, b_re, b_im, c_re, c_im, d_skip):
    dt = jnp.exp(log_dt)[:, None]
    mag = jnp.exp(lam_re * dt)
    ab_re = mag * jnp.cos(lam_im * dt)
    ab_im = mag * jnp.sin(lam_im * dt)
    den = lam_re * lam_re + lam_im * lam_im
    nr = ab_re - 1.0
    ni = ab_im
    coef_re = ((nr * lam_re + ni * lam_im) / den)[..., None]
    coef_im = ((ni * lam_re - nr * lam_im) / den)[..., None]
    bb_re = coef_re * b_re - coef_im * b_im
    bb_im = coef_re * b_im + coef_im * b_re
    eye = jnp.eye(SSM_GROUPS, dtype=F32)
    half = SSM_COLS // 2
    bd = jnp.concatenate(
        [jnp.einsum('gnc,gh->gchn', m, eye).reshape(SSM_WIDTH, half) for m in (bb_re, bb_im)],
        axis=1).astype(BF16)
    cd = jnp.concatenate(
        [jnp.einsum('gcn,gh->gnhc', m, eye).reshape(half, SSM_WIDTH) for m in (c_re, -c_im)],
        axis=0).astype(BF16)
    a = jnp.concatenate([ab_re.reshape(1, half), ab_im.reshape(1, half)], axis=1)
    a = jnp.broadcast_to(a, (NB, SSM_COLS))
    return a, bd, cd, d_skip.reshape(1, SSM_WIDTH)


def _nt_dot(a, b):
    return lax.dot_general(a, b, (((1,), (1,)), ((), ())), preferred_element_type=F32)


def _osm_init(s, vt):
    m = jnp.max(s, axis=1, keepdims=True)
    p = jnp.exp2(s - m)
    l = jnp.sum(p, axis=1, keepdims=True)
    acc = _nt_dot(p.astype(BF16), vt)
    return m, l, acc


def _causal_sweep(qts, k_ref, k_cols, vt_ref, vt_rows, kms, vmts, qi,
                  m_ref, l_ref, acc_ref, al_ref, p_ref):
    n = len(qts)
    for i in range(n):
        s = jnp.dot(kms[i], qts[i], preferred_element_type=F32)
        m = jnp.max(s, axis=0, keepdims=True)
        p = jnp.exp2(s - m)
        m_ref[i][...] = m
        l_ref[i][...] = jnp.sum(p, axis=0, keepdims=True)
        acc_ref[i][...] = jnp.dot(vmts[i], p.astype(BF16), preferred_element_type=F32)
        al_ref[i][...] = jnp.ones_like(al_ref[i])
        p_ref[i][...] = jnp.zeros_like(p_ref[i])

    sub = ATT_TK // LANES

    def pending_pv(jp):
        for i in range(n):
            vtb = jnp.concatenate([vt_ref[jp * sub + c, vt_rows[i], :] for c in range(sub)],
                                  axis=1)
            acc_ref[i][...] = al_ref[i][...] * acc_ref[i][...] + jnp.dot(
                vtb, p_ref[i][...], preferred_element_type=F32)

    def chunk(j, masked):
        off = pl.multiple_of(j * ATT_TK, ATT_TK)
        if masked:
            krow = lax.broadcasted_iota(jnp.int32, (ATT_TK, ATT_TQ), 0)
            qcol = lax.broadcasted_iota(jnp.int32, (ATT_TK, ATT_TQ), 1)
            visible = krow <= qcol
        scores = []
        for i in range(n):
            kb = k_ref[pl.ds(off, ATT_TK), k_cols[i]]
            s = jnp.dot(kb, qts[i], preferred_element_type=F32)
            scores.append(jnp.where(visible, s, MASK_VALUE) if masked else s)
        pending_pv(jnp.maximum(j - 1, 0))
        for i in range(n):
            m_old = m_ref[i][...]
            m_new = jnp.maximum(m_old, jnp.max(scores[i], axis=0, keepdims=True))
            alpha = jnp.exp2(m_old - m_new)
            p = jnp.exp2(scores[i] - m_new)
            l_ref[i][...] = alpha * l_ref[i][...] + jnp.sum(p, axis=0, keepdims=True)
            m_ref[i][...] = m_new
            al_ref[i][...] = alpha
            p_ref[i][...] = p.astype(BF16)

    def body(j, carry):
        chunk(j, False)
        return carry

    lax.fori_loop(0, qi, body, 0)
    chunk(qi, True)
    pending_pv(qi)


def _meta_attend(q, km, vmt):
    n = km.shape[0]
    row = lax.broadcasted_iota(jnp.int32, (n, n), 0)
    col = lax.broadcasted_iota(jnp.int32, (n, n), 1)
    s = jnp.where(col <= row, _nt_dot(q, km), MASK_VALUE)
    _, l, acc = _osm_init(s, vmt)
    return l, acc


def _diff_split(q):
    lane = lax.broadcasted_iota(jnp.int32, q.shape, 1)
    q = (q.astype(F32) * QK_SCALE).astype(BF16)
    zero = jnp.zeros_like(q)
    return jnp.where(lane < 64, q, zero), jnp.where(lane >= 64, q, zero)


def _diff_finish(parts, lam, lam_init, g):
    (l1, acc1), (l2, acc2) = parts
    o = acc1 / l1 - lam * (acc2 / l2)
    return (_rms(o, g) * (1.0 - lam_init)).astype(BF16)


def _transpose_bf16(x):
    return x.astype(F32).T.astype(BF16)


def _map_scratch(n_maps, dv):
    return ([pltpu.VMEM((1, ATT_TQ), F32)] * (2 * n_maps)
            + [pltpu.VMEM((dv, ATT_TQ), F32)] * n_maps
            + [pltpu.VMEM((1, ATT_TQ), F32)] * n_maps
            + [pltpu.VMEM((ATT_TK, ATT_TQ), BF16)] * n_maps)


def _split_map_scratch(scratch, n_maps):
    return [scratch[k * n_maps:(k + 1) * n_maps] for k in range(5)]


def _diff_kernel(lam_init, lam_ref, q_ref, k_ref, vt_ref, km_ref, vmt_ref, g_ref, o_ref,
                 *scratch):
    m_ref, l_ref, acc_ref, al_ref, p_ref = _split_map_scratch(scratch, 2 * DIFF_HEADS)
    qts, k_cols, vt_rows, kms, vmts = [], [], [], [], []
    row = lax.broadcasted_iota(jnp.int32, (LANES, ATT_TQ), 0)
    for h in range(DIFF_HEADS):
        sl = slice(LANES * h, LANES * (h + 1))
        km = km_ref[:, sl]
        vmt = vmt_ref[sl, :]
        qt = (q_ref[:, sl].astype(F32) * QK_SCALE).T.astype(BF16)
        zero = jnp.zeros_like(qt)
        for qm in (jnp.where(row < 64, qt, zero), jnp.where(row >= 64, qt, zero)):
            qts.append(qm)
            k_cols.append(sl)
            vt_rows.append(sl)
            kms.append(km)
            vmts.append(vmt)
    _causal_sweep(qts, k_ref, k_cols, vt_ref, vt_rows, kms, vmts, pl.program_id(1),
                  m_ref, l_ref, acc_ref, al_ref, p_ref)
    lam = lam_ref[0]
    for h in range(DIFF_HEADS):
        o = (acc_ref[2 * h][...] / l_ref[2 * h][...]
             - lam * (acc_ref[2 * h + 1][...] / l_ref[2 * h + 1][...]))
        y = o * lax.rsqrt(jnp.mean(o * o, axis=0, keepdims=True) + EPS)
        o_ref[:, LANES * h:LANES * (h + 1)] = (
            y.T * g_ref[...] * (1.0 - lam_init)).astype(BF16)


def _diff_meta_kernel(lam_init, lam_ref, q_ref, km_ref, vmt_ref, g_ref, _flat_ref, o_ref):
    q1, q2 = _diff_split(q_ref[...])
    km = km_ref[...]
    vmt = vmt_ref[...]
    parts = [_meta_attend(q1, km, vmt), _meta_attend(q2, km, vmt)]
    o_ref[...] = _diff_finish(parts, lam_ref[0], lam_init, g_ref[...])


def _diff_attention(za, vt, vtm, seq, lam, lam_init, g):
    t = za.shape[0]
    nq = seq // ATT_TQ
    mrow = NB * seq // N_META
    smem = pl.BlockSpec(memory_space=pltpu.SMEM)
    lam = lam.reshape(1).astype(F32)
    g = g.reshape(1, LANES)
    gspec = pl.BlockSpec((1, LANES), lambda *_: (0, 0))
    width = DIFF_HEADS * LANES
    n_maps = 2 * DIFF_HEADS
    flat = pl.pallas_call(
        functools.partial(_diff_kernel, lam_init),
        grid=(NB, nq),
        in_specs=[smem,
                  pl.BlockSpec((ATT_TQ, width), lambda b, i: (b * nq + i, 0)),
                  pl.BlockSpec((seq, width), lambda b, i: (b, 1)),
                  pl.BlockSpec((seq // LANES, width, LANES), lambda b, i: (b, 0, 0)),
                  pl.BlockSpec((N_META, width), lambda b, i: (mrow + b, 1)),
                  pl.BlockSpec((None, width, N_META), lambda b, i: (b, 0, 0)),
                  gspec],
        out_specs=pl.BlockSpec((ATT_TQ, width), lambda b, i: (b * nq + i, 0)),
        out_shape=jax.ShapeDtypeStruct((t, width), BF16),
        scratch_shapes=_map_scratch(n_maps, LANES),
        compiler_params=_cp(("parallel", "arbitrary")),
        name="diff_attn",
    )(lam, za, za, vt, za, vtm, g)
    return pl.pallas_call(
        functools.partial(_diff_meta_kernel, lam_init),
        grid=(NB, DIFF_HEADS),
        in_specs=[smem,
                  pl.BlockSpec((N_META, LANES), lambda b, h: (mrow + b, h)),
                  pl.BlockSpec((N_META, LANES), lambda b, h: (mrow + b, 4 + h)),
                  pl.BlockSpec((None, LANES, N_META), lambda b, h: (b, h, 0)),
                  gspec,
                  pl.BlockSpec(memory_space=pl.ANY)],
        out_specs=pl.BlockSpec((N_META, LANES), lambda b, h: (mrow + b, h)),
        out_shape=jax.ShapeDtypeStruct((t, DIFF_HEADS * LANES), BF16),
        input_output_aliases={5: 0},
        compiler_params=_cp(("parallel", "parallel")),
        name="diff_attn_meta",
    )(lam, za, za, vtm, g, flat)


def _split3(c):
    hi = c.astype(BF16).astype(F32)
    r1 = c - hi
    mid = r1.astype(BF16).astype(F32)
    lo = (r1 - mid).astype(BF16).astype(F32)
    return hi, mid, lo


def _cumsum_rows(tri, lf):
    parts = jnp.concatenate(_split3(lf), axis=1).astype(BF16)
    r = jnp.dot(tri, parts, preferred_element_type=F32)
    return r[:, :LANES] + r[:, LANES:2 * LANES] + r[:, 2 * LANES:]


def _log_sigmoid(x):
    return jnp.minimum(x, 0.0) - jnp.log1p(jnp.exp(-jnp.abs(x)))


def _fox_augment(fq, fk, cum, qa_ref, ka_ref):
    n = fq.shape[0]
    lane = lax.broadcasted_iota(jnp.int32, (n, LANES), 1)
    for h in range(FOX_HEADS):
        pair = slice(LANES * (h // 2), LANES * (h // 2) + LANES)
        own = (lane // 64) == (h % 2)
        e0 = 64 * (1 - h % 2)
        hi, mid, lo = _split3(jnp.broadcast_to(cum[:, h:h + 1] * LOG2E, (n, LANES)))
        ones = (lane >= e0 + 3) & (lane < e0 + 6)
        q_extra = jnp.where(lane == e0, hi, jnp.where(lane == e0 + 1, mid, jnp.where(
            lane == e0 + 2, lo, jnp.where(ones, 1.0, 0.0))))
        ones = (lane >= e0) & (lane < e0 + 3)
        k_extra = jnp.where(lane == e0 + 3, -hi, jnp.where(lane == e0 + 4, -mid, jnp.where(
            lane == e0 + 5, -lo, jnp.where(ones, 1.0, 0.0))))
        q = fq[:, pair].astype(F32) * QK_SCALE
        k = fk[:, pair].astype(F32)
        qa_ref[:, LANES * h:LANES * (h + 1)] = jnp.where(own, q, q_extra).astype(BF16)
        ka_ref[:, LANES * h:LANES * (h + 1)] = jnp.where(own, k, k_extra).astype(BF16)


def _fox_prep_meta_kernel(zg_ref, fq_ref, fk_ref, fb_ref, qa_ref, ka_ref, carry_ref):
    n = NB * N_META
    lf = _log_sigmoid(zg_ref[...] + fb_ref[...])
    row = lax.broadcasted_iota(jnp.int32, (n, n), 0)
    col = lax.broadcasted_iota(jnp.int32, (n, n), 1)
    tri = ((col <= row) & (col // N_META == row // N_META)).astype(BF16)
    cum = _cumsum_rows(tri, lf)
    brow = lax.broadcasted_iota(jnp.int32, (NB, n), 0)
    bcol = lax.broadcasted_iota(jnp.int32, (NB, n), 1)
    carry_ref[...] = _cumsum_rows((bcol // N_META == brow).astype(BF16), lf)
    _fox_augment(fq_ref[...], fk_ref[...], cum, qa_ref, ka_ref)


def _fox_prep_kernel(zg_ref, fq_ref, fk_ref, fb_ref, carry_in_ref, qa_ref, ka_ref, carry_ref):
    b = pl.program_id(0)
    n = zg_ref.shape[0]

    @pl.when(pl.program_id(1) == 0)
    def _():
        carry_ref[...] = carry_in_ref[pl.ds(b, 1), :]

    lf = _log_sigmoid(zg_ref[...] + fb_ref[...])
    row = lax.broadcasted_iota(jnp.int32, (n, n), 0)
    col = lax.broadcasted_iota(jnp.int32, (n, n), 1)
    cum = _cumsum_rows((col <= row).astype(BF16), lf) + carry_ref[...]
    carry_ref[...] = cum[n - 1:n, :]
    _fox_augment(fq_ref[...], fk_ref[...], cum, qa_ref, ka_ref)


FOX_PREP_TILE = 1024


def _fox_prep(zg, za, seq, fb):
    tr = NB * seq
    nm = NB * N_META
    mblk = tr // nm
    fb = jnp.pad(fb.astype(F32), (0, LANES - FOX_HEADS)).reshape(1, LANES)
    fbspec = pl.BlockSpec((1, LANES), lambda *_: (0, 0))
    aug = FOX_HEADS * LANES
    qa_m, ka_m, carry = pl.pallas_call(
        _fox_prep_meta_kernel,
        grid=(1,),
        in_specs=[pl.BlockSpec((nm, LANES), lambda i: (mblk, 0)),
                  pl.BlockSpec((nm, 256), lambda i: (mblk, 4)),
                  pl.BlockSpec((nm, 256), lambda i: (mblk, 5)),
                  fbspec],
        out_specs=[pl.BlockSpec((nm, aug), lambda i: (0, 0)),
                   pl.BlockSpec((nm, aug), lambda i: (0, 0)),
                   pl.BlockSpec((NB, LANES), lambda i: (0, 0))],
        out_shape=[jax.ShapeDtypeStruct((nm, aug), BF16),
                   jax.ShapeDtypeStruct((nm, aug), BF16),
                   jax.ShapeDtypeStruct((NB, LANES), F32)],
        compiler_params=_cp(("arbitrary",)),
        name="fox_prep_meta",
    )(zg, za, za, fb)
    nc = seq // FOX_PREP_TILE
    qa, ka = pl.pallas_call(
        _fox_prep_kernel,
        grid=(NB, nc),
        in_specs=[pl.BlockSpec((FOX_PREP_TILE, LANES), lambda b, c: (b * nc + c, 0)),
                  pl.BlockSpec((FOX_PREP_TILE, 256), lambda b, c: (b * nc + c, 4)),
                  pl.BlockSpec((FOX_PREP_TILE, 256), lambda b, c: (b * nc + c, 5)),
                  fbspec,
                  pl.BlockSpec((NB, LANES), lambda b, c: (0, 0))],
        out_specs=[pl.BlockSpec((FOX_PREP_TILE, aug), lambda b, c: (b * nc + c, 0)),
                   pl.BlockSpec((FOX_PREP_TILE, aug), lambda b, c: (b * nc + c, 0))],
        out_shape=[jax.ShapeDtypeStruct((tr, aug), BF16),
                   jax.ShapeDtypeStruct((tr, aug), BF16)],
        scratch_shapes=[pltpu.VMEM((1, LANES), F32)],
        compiler_params=_cp(("parallel", "arbitrary")),
        name="fox_prep",
    )(zg, za, za, fb, carry)
    return qa, ka, qa_m, ka_m


def _fox_finish(parts):
    (l0, acc0), (l1, acc1) = parts
    lane = lax.broadcasted_iota(jnp.int32, acc0.shape, 1)
    return jnp.where(lane < 64, acc0 / l0, acc1 / l1).astype(BF16)


def _fox_kernel(q_ref, k_ref, vt_ref, km_ref, vmt_ref, o_ref, *scratch):
    m_ref, l_ref, acc_ref, al_ref, p_ref = _split_map_scratch(scratch, FOX_HEADS)
    hd = 64
    vmt = vmt_ref[...]
    qts, k_cols, vt_rows, kms, vmts = [], [], [], [], []
    for h in range(FOX_HEADS):
        sl = slice(LANES * h, LANES * (h + 1))
        qts.append(_transpose_bf16(q_ref[:, sl]))
        k_cols.append(sl)
        vt_rows.append(slice(hd * h, hd * (h + 1)))
        kms.append(km_ref[:, sl])
        vmts.append(vmt[hd * h:hd * (h + 1), :])
    _causal_sweep(qts, k_ref, k_cols, vt_ref, vt_rows, kms, vmts, pl.program_id(1),
                  m_ref, l_ref, acc_ref, al_ref, p_ref)
    for p in range(FOX_HEADS // 2):
        o = jnp.concatenate([acc_ref[2 * p][...] / l_ref[2 * p][...],
                             acc_ref[2 * p + 1][...] / l_ref[2 * p + 1][...]],
                            axis=0)
        o_ref[:, LANES * p:LANES * (p + 1)] = o.T.astype(BF16)


def _fox_meta_kernel(q_ref, km_ref, vmt_ref, _flat_ref, o_ref):
    vmt = vmt_ref[...]
    parts = [_meta_attend(q_ref[:, LANES * hh:LANES * (hh + 1)],
                          km_ref[:, LANES * hh:LANES * (hh + 1)], vmt) for hh in range(2)]
    o_ref[...] = _fox_finish(parts)


def _fox_attention(t, vt, vtm, seq, qa, ka, qa_m, ka_m):
    nq = seq // ATT_TQ
    mrow = NB * seq // N_META
    aug = FOX_HEADS * LANES
    flat = pl.pallas_call(
        _fox_kernel,
        grid=(NB, nq),
        in_specs=[pl.BlockSpec((ATT_TQ, aug), lambda b, i: (b * nq + i, 0)),
                  pl.BlockSpec((seq, aug), lambda b, i: (b, 0)),
                  pl.BlockSpec((seq // LANES, 256, LANES), lambda b, i: (b, 2, 0)),
                  pl.BlockSpec((N_META, aug), lambda b, i: (b, 0)),
                  pl.BlockSpec((None, 256, N_META), lambda b, i: (b, 2, 0))],
        out_specs=pl.BlockSpec((ATT_TQ, 256), lambda b, i: (b * nq + i, 0)),
        out_shape=jax.ShapeDtypeStruct((t, 256), BF16),
        scratch_shapes=_map_scratch(FOX_HEADS, 64),
        compiler_params=_cp(("parallel", "arbitrary")),
        name="fox_attn",
    )(qa, ka, vt, ka_m, vtm)
    return pl.pallas_call(
        _fox_meta_kernel,
        grid=(NB, FOX_HEADS // 2),
        in_specs=[pl.BlockSpec((N_META, 256), lambda b, p: (b, p)),
                  pl.BlockSpec((N_META, 256), lambda b, p: (b, p)),
                  pl.BlockSpec((None, LANES, N_META), lambda b, p: (b, 4 + p, 0)),
                  pl.BlockSpec(memory_space=pl.ANY)],
        out_specs=pl.BlockSpec((N_META, LANES), lambda b, p: (mrow + b, p)),
        out_shape=jax.ShapeDtypeStruct((t, 256), BF16),
        input_output_aliases={3: 0},
        compiler_params=_cp(("parallel", "parallel")),
        name="fox_attn_meta",
    )(qa_m, ka_m, vtm, flat)


def _router_kernel(res_ref, g_ref, wr_ref, idx_ref, gate_ref):
    h = _rms(res_ref[...], g_ref[...])
    w = wr_ref[...]
    hi = h.astype(BF16)
    lo = (h - hi.astype(F32)).astype(BF16)
    whi = w.astype(BF16)
    wlo = (w - whi.astype(F32)).astype(BF16)
    logits = (jnp.dot(hi, whi, preferred_element_type=F32)
              + jnp.dot(lo, whi, preferred_element_type=F32)
              + jnp.dot(hi, wlo, preferred_element_type=F32))
    lane = lax.broadcasted_iota(jnp.int32, logits.shape, 1)
    logits = jnp.where(lane < N_EXPERTS, logits, -jnp.inf)
    m1 = jnp.max(logits, axis=1, keepdims=True)
    i1 = jnp.min(jnp.where(logits == m1, lane, LANES), axis=1, keepdims=True)
    rest = jnp.where(lane == i1, -jnp.inf, logits)
    m2 = jnp.max(rest, axis=1, keepdims=True)
    i2 = jnp.min(jnp.where(rest == m2, lane, LANES), axis=1, keepdims=True)
    e = jnp.exp(m2 - m1)
    g1 = 1.0 / (1.0 + e)
    g2 = e / (1.0 + e)
    idx_ref[...] = jnp.where(lane == 0, i1, jnp.where(lane == 1, i2, 0))
    gate_ref[...] = jnp.where(lane == 0, g1, jnp.where(lane == 1, g2, 0.0))


def _router(res, g, wr):
    t = res.shape[0]
    return pl.pallas_call(
        _router_kernel,
        grid=(t // TOK_TILE,),
        in_specs=[pl.BlockSpec((TOK_TILE, D_MODEL), lambda i: (i, 0)),
                  pl.BlockSpec((1, D_MODEL), lambda i: (0, 0)),
                  pl.BlockSpec((D_MODEL, LANES), lambda i: (0, 0))],
        out_specs=[pl.BlockSpec((TOK_TILE, LANES), lambda i: (i, 0)),
                   pl.BlockSpec((TOK_TILE, LANES), lambda i: (i, 0))],
        out_shape=[jax.ShapeDtypeStruct((t, LANES), jnp.int32),
                   jax.ShapeDtypeStruct((t, LANES), F32)],
        compiler_params=_cp(("parallel",)),
        name="router",
    )(res, g, wr)


def _gather_copy(src_hbm, first, dst_ref, r, sem):
    first = pl.multiple_of(first, ROW_BLOCKS)
    dst = pl.multiple_of(r * ROW_BLOCKS, ROW_BLOCKS)
    return pltpu.make_async_copy(src_hbm.at[pl.ds(first, ROW_BLOCKS), :],
                                 dst_ref.at[pl.ds(dst, ROW_BLOCKS), :], sem)


def _expert_ffn_kernel(nf, te_ref, na_ref, src_ref, x_hbm, g_ref, w1_ref, w3_ref, w2_ref,
                       o_ref, xbuf_ref, h_ref, acc_ref, sem):
    r = pl.program_id(0)
    f = pl.program_id(1)
    n_act = na_ref[0]
    per_step = MOE_XROWS // nf
    last_tile = pl.num_programs(0) - 1

    def tile_copy(slot):
        return pltpu.make_async_copy(x_hbm.at[pl.ds(0, MOE_XROWS * ROW_BLOCKS), :],
                                     xbuf_ref.at[slot], sem.at[slot])

    def start_row(tile, slot, k):
        _gather_copy(x_hbm, src_ref[tile * MOE_XROWS + k], xbuf_ref.at[slot], k,
                     sem.at[slot]).start()

    @pl.when(r < n_act)
    def _():
        slot = lax.rem(r, MOE_XSLOTS)
        slot1 = lax.rem(r + 1, MOE_XSLOTS)
        slot2 = lax.rem(r + 2, MOE_XSLOTS)

        @pl.when((r == 0) & (f == 0))
        def _():
            def first(k, _):
                start_row(0, 0, k)
                start_row(jnp.minimum(1, last_tile), 1, k)
                return 0
            lax.fori_loop(0, MOE_XROWS, first, 0, unroll=8)

        @pl.when(f == 0)
        def _():
            tile_copy(slot).wait()
            x = _load_row_tiles(xbuf_ref.at[slot], MOE_TILE)
            h_ref[...] = _rms(x, g_ref[...]).astype(BF16)
            acc_ref[...] = jnp.zeros_like(acc_ref)

        ahead = jnp.minimum(r + 2, last_tile)
        for k in range(per_step):
            start_row(ahead, slot2, f * per_step + k)

        h = h_ref[...]
        a = jnp.dot(h, w1_ref[...].astype(BF16), preferred_element_type=F32)
        b = jnp.dot(h, w3_ref[...].astype(BF16), preferred_element_type=F32)
        hh = (a * jax.nn.sigmoid(a) * b).astype(BF16)
        acc_ref[...] += jnp.dot(hh, w2_ref[...].astype(BF16), preferred_element_type=F32)

        @pl.when(f == nf - 1)
        def _():
            _store_row_tiles(o_ref, acc_ref[...])

        @pl.when((f == nf - 1) & (r == n_act - 1))
        def _():
            tile_copy(slot1).wait()
            tile_copy(slot2).wait()


def _expert_ffn(tile_expert, n_active, src, x, g, w1, w3, w2):
    dff = w1.shape[2]
    nt = src.shape[0] // MOE_XROWS
    nf = dff // MOE_FCHUNK
    assert MOE_XROWS % nf == 0 and MOE_XROWS >= MOE_TILE

    def row(r, f, te, na, src):
        return jnp.minimum(r, na[0] - 1)

    def fch(r, f, te, na, src):
        return jnp.where(r < na[0], f, nf - 1)

    return pl.pallas_call(
        functools.partial(_expert_ffn_kernel, nf),
        grid_spec=pltpu.PrefetchScalarGridSpec(
            num_scalar_prefetch=3,
            grid=(nt, nf),
            in_specs=[
                pl.BlockSpec(memory_space=pl.ANY),
                pl.BlockSpec((1, D_MODEL), lambda r, f, te, na, src: (0, 0)),
                pl.BlockSpec((None, D_MODEL, MOE_FCHUNK),
                             lambda r, f, te, na, src: (te[r], 0, fch(r, f, te, na, src))),
                pl.BlockSpec((None, D_MODEL, MOE_FCHUNK),
                             lambda r, f, te, na, src: (te[r], 0, fch(r, f, te, na, src))),
                pl.BlockSpec((None, MOE_FCHUNK, D_MODEL),
                             lambda r, f, te, na, src: (te[r], fch(r, f, te, na, src), 0))],
            out_specs=pl.BlockSpec((MOE_TILE * ROW_BLOCKS, LANES),
                                   lambda r, f, te, na, src: (row(r, f, te, na, src), 0)),
            scratch_shapes=[pltpu.VMEM((MOE_XSLOTS, MOE_XROWS * ROW_BLOCKS, LANES), F32),
                            pltpu.VMEM((MOE_TILE, D_MODEL), BF16),
                            pltpu.VMEM((MOE_TILE, D_MODEL), F32),
                            pltpu.SemaphoreType.DMA((MOE_XSLOTS,))]),
        out_shape=jax.ShapeDtypeStruct((nt * MOE_TILE * ROW_BLOCKS, LANES), F32),
        compiler_params=_cp(("arbitrary", "arbitrary")),
        name="expert_ffn",
    )(tile_expert, n_active, src, x, g, w1, w3, w2)


def _combine_kernel(tile, final, pos_ref, res_ref, gate_ref, gf_ref, y_hbm, o_ref, ybuf_ref, sem):
    i = pl.program_id(0)
    n = pl.num_programs(0)
    slot = lax.rem(i, 2)

    def issue_tile(step, s):
        base = step * tile

        def issue(r, _):
            for k in range(2):
                _gather_copy(y_hbm, pos_ref[2 * (base + r) + k], ybuf_ref.at[s, k], r,
                             sem.at[s]).start(priority=k)
            return 0

        lax.fori_loop(0, tile, issue, 0, unroll=8)

    @pl.when(i == 0)
    def _():
        issue_tile(0, 0)

    @pl.when(i + 1 < n)
    def _():
        issue_tile(i + 1, 1 - slot)

    pltpu.make_async_copy(ybuf_ref.at[slot], ybuf_ref.at[slot], sem.at[slot]).wait()
    gate = gate_ref[...]
    out = (res_ref[...] + gate[:, 0:1] * _load_row_tiles(ybuf_ref.at[slot, 0], tile)
           + gate[:, 1:2] * _load_row_tiles(ybuf_ref.at[slot, 1], tile))
    o_ref[...] = _rms(out, gf_ref[...]) if final else out


def _combine(pos, res, gates, y, final_g=None):
    final = final_g is not None
    t = res.shape[0]
    n_rows = t - NB * N_META if final else t
    tile = 256 if final else COMBINE_TILE
    if not final:
        final_g = jnp.ones((1, D_MODEL), F32)
    return pl.pallas_call(
        functools.partial(_combine_kernel, tile, final),
        grid_spec=pltpu.PrefetchScalarGridSpec(
            num_scalar_prefetch=1,
            grid=(n_rows // tile,),
            in_specs=[pl.BlockSpec((tile, D_MODEL), lambda i, pos: (i, 0)),
                      pl.BlockSpec((tile, LANES), lambda i, pos: (i, 0)),
                      pl.BlockSpec((1, D_MODEL), lambda i, pos: (0, 0)),
                      pl.BlockSpec(memory_space=pl.ANY)],
            out_specs=pl.BlockSpec((tile, D_MODEL), lambda i, pos: (i, 0)),
            scratch_shapes=[pltpu.VMEM((2, 2, tile * ROW_BLOCKS, LANES), F32),
                            pltpu.SemaphoreType.DMA((2,))]),
        out_shape=jax.ShapeDtypeStruct((n_rows, D_MODEL), F32),
        compiler_params=_cp(("arbitrary",)),
        name="moe_combine",
    )(pos, res, gates, final_g, y)


def _moe(res, res_tiles, g, wr, w1, w3, w2, final_g=None):
    t = res.shape[0]
    wr = jnp.pad(wr.astype(F32), ((0, 0), (0, LANES - N_EXPERTS)))
    idx, gates = _router(res, g, wr)
    e_flat = idx[:, :2].reshape(-1)
    onehot = (e_flat[:, None] == jnp.arange(N_EXPERTS, dtype=jnp.int32)[None, :]).astype(jnp.int32)
    csum = jnp.cumsum(onehot, axis=0)
    rank = jnp.take_along_axis(csum, e_flat[:, None], axis=1)[:, 0] - 1
    counts = csum[-1]
    tiles = (counts + MOE_TILE - 1) // MOE_TILE
    tile_end = jnp.cumsum(tiles)
    starts = (tile_end - tiles) * MOE_TILE
    pos = (starts[e_flat] + rank).astype(jnp.int32)
    n_tiles = (2 * t + N_EXPERTS * (MOE_TILE - 1)) // MOE_TILE
    spos = (pos // MOE_TILE) * MOE_XROWS + pos % MOE_TILE
    src = jnp.zeros((n_tiles * MOE_XROWS,), jnp.int32).at[spos].set(
        (jnp.arange(2 * t, dtype=jnp.int32) // 2) * ROW_BLOCKS, unique_indices=True)
    n_active = tile_end[-1:].astype(jnp.int32)
    tile_ids = jnp.minimum(jnp.arange(n_tiles, dtype=jnp.int32), n_active[0] - 1)
    tile_expert = jnp.sum(tile_ids[:, None] >= tile_end[None, :], axis=1).astype(jnp.int32)

    y = _expert_ffn(tile_expert, n_active, src, res_tiles, g, w1, w3, w2)
    return _combine(pos * ROW_BLOCKS, res, gates, y, final_g)


def _final_norm_kernel(res_ref, g_ref, o_ref):
    o_ref[...] = _rms(res_ref[...], g_ref[...])


def _final_norm(res, g, n_rows):
    tile = 1024
    return pl.pallas_call(
        _final_norm_kernel,
        grid=(n_rows // tile,),
        in_specs=[pl.BlockSpec((tile, D_MODEL), lambda i: (i, 0)),
                  pl.BlockSpec((1, D_MODEL), lambda i: (0, 0))],
        out_specs=pl.BlockSpec((tile, D_MODEL), lambda i: (i, 0)),
        out_shape=jax.ShapeDtypeStruct((n_rows, D_MODEL), F32),
        compiler_params=_cp(("parallel",)),
        name="final_norm",
    )(res, g)


def kernel(x, meta_tokens, norm_mix_g, w_in, w_out, ssm_lambda_re, ssm_lambda_im, ssm_log_dt,
           ssm_b_re, ssm_b_im, ssm_c_re, ssm_c_im, ssm_d, ssm_w_glu, diff_lambda_q1,
           diff_lambda_k1, diff_lambda_q2, diff_lambda_k2, diff_subln_g, fox_forget_b,
           norm_ffn_g, dense_w1, dense_w3, dense_w2, moe_router, moe_w1, moe_w3, moe_w2,
           final_norm_g):
    bsz, seq, d = x.shape
    assert bsz == NB and d == D_MODEL and seq % ATT_TQ == 0
    depth = w_in.shape[0]
    tr = bsz * seq
    res = jnp.concatenate(
        [x.reshape(tr, d), jnp.tile(meta_tokens.astype(x.dtype), (bsz, 1))], axis=0)
    row = lambda v: v.reshape(1, -1).astype(F32)

    for l in range(depth):
        wl = w_in[l]
        w = jnp.concatenate([wl[:, :1280], wl[:, 1792:2304],
                             jnp.pad(wl[:, 2560:], ((0, 0), (0, LANES - FOX_HEADS)))], axis=1)
        wvt = jnp.concatenate([wl[:, 1280:1792], wl[:, 2304:2560]], axis=1).T
        zu, za, zg, vt, vtm = _inproj(res, row(norm_mix_g[l]), w.astype(BF16),
                                      wvt.astype(BF16))

        a, bd, cd, dskip = _s5_params(ssm_lambda_re[l], ssm_lambda_im[l], ssm_log_dt[l],
                                      ssm_b_re[l], ssm_b_im[l], ssm_c_re[l], ssm_c_im[l],
                                      ssm_d[l])
        ssm_out = _s5(zu, seq, a, bd, cd, dskip, ssm_w_glu[l].astype(BF16))

        lam_init = 0.8 - 0.6 * math.exp(-0.3 * l)
        lam = (jnp.exp(jnp.sum(diff_lambda_q1[l] * diff_lambda_k1[l]))
               - jnp.exp(jnp.sum(diff_lambda_q2[l] * diff_lambda_k2[l])) + lam_init)
        diff_out = _diff_attention(za, vt, vtm, seq, lam, lam_init, diff_subln_g[l])

        qa, ka, qa_m, ka_m = _fox_prep(zg, za, seq, fox_forget_b[l])
        fox_out = _fox_attention(za.shape[0], vt, vtm, seq, qa, ka, qa_m, ka_m)

        is_moe = l % 2 == 1
        res, *res_tiles = _outproj(res, ssm_out, diff_out, fox_out, w_out[l].astype(BF16),
                                   with_row_tiles=is_moe)

        if not is_moe:
            res = _dense_ffn(res, row(norm_ffn_g[l]), dense_w1[l // 2].astype(BF16),
                             dense_w3[l // 2].astype(BF16), dense_w2[l // 2].astype(BF16))
        else:
            res = _moe(res, res_tiles[0], row(norm_ffn_g[l]), moe_router[l // 2], moe_w1[l // 2],
                       moe_w3[l // 2], moe_w2[l // 2],
                       final_g=row(final_norm_g) if l == depth - 1 else None)

    if depth % 2 == 1:
        res = _final_norm(res, row(final_norm_g), tr)
    return res.reshape(bsz, seq, d)
```

```python
import functools
import math

import jax
import jax.numpy as jnp
from jax import lax
from jax.experimental import pallas as pl
from jax.experimental.pallas import tpu as pltpu

F32 = jnp.float32
BF16 = jnp.bfloat16
EPS = 1e-6

D_MODEL = 1024
N_META = 16
NB = 8
SSM_WIDTH = 256
SSM_GROUPS = 16
SSM_GROUP = 16
SSM_STATE = 64
SSM_COLS = 2 * SSM_GROUPS * SSM_STATE
DIFF_HEADS = 4
FOX_HEADS = 4
ATT_COLS = 1536
IN_PAD = 1920
N_EXPERTS = 8
LANES = 128
MASK_VALUE = -1e30
LOG2E = math.log2(math.e)
QK_SCALE = 64 ** -0.5 * LOG2E

VT_ROWS = 768
TOK_TILE = 688
IN_TILE = 384
FFN_TILE = 384
ATT_TQ = 256
ATT_TK = 256
S5_CHUNK = 256
MOE_TILE = 1024
MOE_FCHUNK = 512
MOE_XROWS = 1029
MOE_XSLOTS = 3
ROW_BLOCKS = D_MODEL // LANES
COMBINE_TILE = 384

VMEM_LIMIT = 56 * 1024 * 1024


def _cp(sem):
    return pltpu.CompilerParams(dimension_semantics=sem, vmem_limit_bytes=VMEM_LIMIT)


def _rms(x, g):
    return x * lax.rsqrt(jnp.mean(x * x, axis=-1, keepdims=True) + EPS) * g


def _inproj_kernel(res_ref, g_ref, w_ref, wvt_ref, zu_ref, za_ref, zg_ref, vt_ref, vtm_ref):
    h = _rms(res_ref[...], g_ref[...]).astype(BF16)
    zu_ref[...] = jnp.dot(h, w_ref[:, 0:SSM_WIDTH], preferred_element_type=F32)
    for c in range(0, ATT_COLS, 256):
        za_ref[:, c:c + 256] = jnp.dot(
            h, w_ref[:, SSM_WIDTH + c:SSM_WIDTH + c + 256],
            preferred_element_type=F32).astype(BF16)
    zg_ref[...] = jnp.dot(h, w_ref[:, SSM_WIDTH + ATT_COLS:IN_PAD],
                          preferred_element_type=F32)
    vt = _nt_dot(wvt_ref[...], h).astype(BF16)
    for c in range(IN_TILE // LANES):
        vt_ref[c] = vt[:, LANES * c:LANES * (c + 1)]

    @pl.when(pl.program_id(0) == pl.num_programs(0) - 1)
    def _():
        first = IN_TILE - NB * N_META
        for b in range(NB):
            vtm_ref[b] = vt[:, first + N_META * b:first + N_META * (b + 1)]


def _inproj(res, g, w, wvt):
    t = res.shape[0]
    nblk = IN_TILE // LANES
    assert t % IN_TILE == 0 and NB * N_META <= IN_TILE
    return pl.pallas_call(
        _inproj_kernel,
        grid=(t // IN_TILE,),
        in_specs=[pl.BlockSpec((IN_TILE, D_MODEL), lambda i: (i, 0)),
                  pl.BlockSpec((1, D_MODEL), lambda i: (0, 0)),
                  pl.BlockSpec((D_MODEL, IN_PAD), lambda i: (0, 0)),
                  pl.BlockSpec((VT_ROWS, D_MODEL), lambda i: (0, 0))],
        out_specs=[pl.BlockSpec((IN_TILE, SSM_WIDTH), lambda i: (i, 0)),
                   pl.BlockSpec((IN_TILE, ATT_COLS), lambda i: (i, 0)),
                   pl.BlockSpec((IN_TILE, LANES), lambda i: (i, 0)),
                   pl.BlockSpec((nblk, VT_ROWS, LANES), lambda i: (i, 0, 0)),
                   pl.BlockSpec((NB, VT_ROWS, N_META), lambda i: (0, 0, 0))],
        out_shape=[jax.ShapeDtypeStruct((t, SSM_WIDTH), F32),
                   jax.ShapeDtypeStruct((t, ATT_COLS), BF16),
                   jax.ShapeDtypeStruct((t, LANES), F32),
                   jax.ShapeDtypeStruct((t // LANES, VT_ROWS, LANES), BF16),
                   jax.ShapeDtypeStruct((NB, VT_ROWS, N_META), BF16)],
        compiler_params=_cp(("arbitrary",)),
        name="inproj",
    )(res, g, w, wvt)


def _store_row_tiles(ref, x):
    n = x.shape[0]
    for j in range(ROW_BLOCKS):
        ref[pl.ds(j, n, stride=ROW_BLOCKS), :] = x[:, LANES * j:LANES * (j + 1)]


def _load_row_tiles(ref, n):
    return jnp.concatenate(
        [ref[pl.ds(j, n, stride=ROW_BLOCKS), :] for j in range(ROW_BLOCKS)], axis=1)


def _outproj_kernel(res_ref, s_ref, d_ref, f_ref, w_ref, o_ref, *tiled_ref):
    mixed = jnp.concatenate([s_ref[...], d_ref[...], f_ref[...]], axis=1)
    out = res_ref[...] + jnp.dot(mixed, w_ref[...], preferred_element_type=F32)
    o_ref[...] = out
    if tiled_ref:
        _store_row_tiles(tiled_ref[0], out)


def _outproj(res, ssm, diff, fox, w, with_row_tiles):
    t = res.shape[0]
    out_specs = [pl.BlockSpec((TOK_TILE, D_MODEL), lambda i: (i, 0))]
    out_shape = [jax.ShapeDtypeStruct((t, D_MODEL), F32)]
    if with_row_tiles:
        out_specs.append(pl.BlockSpec((TOK_TILE * ROW_BLOCKS, LANES), lambda i: (i, 0)))
        out_shape.append(jax.ShapeDtypeStruct((t * ROW_BLOCKS, LANES), F32))
    return pl.pallas_call(
        _outproj_kernel,
        grid=(t // TOK_TILE,),
        in_specs=[pl.BlockSpec((TOK_TILE, D_MODEL), lambda i: (i, 0)),
                  pl.BlockSpec((TOK_TILE, 256), lambda i: (i, 0)),
                  pl.BlockSpec((TOK_TILE, 512), lambda i: (i, 0)),
                  pl.BlockSpec((TOK_TILE, 256), lambda i: (i, 0)),
                  pl.BlockSpec((D_MODEL, D_MODEL), lambda i: (0, 0))],
        out_specs=out_specs,
        out_shape=out_shape,
        compiler_params=_cp(("parallel",)),
        name="outproj",
    )(res, ssm, diff, fox, w)


def _dense_ffn_kernel(res_ref, g_ref, w1_ref, w3_ref, w2_ref, o_ref):
    x = res_ref[...]
    h = _rms(x, g_ref[...]).astype(BF16)
    a = jnp.dot(h, w1_ref[...], preferred_element_type=F32)
    b = jnp.dot(h, w3_ref[...], preferred_element_type=F32)
    hh = (a * jax.nn.sigmoid(a) * b).astype(BF16)
    o_ref[...] = x + jnp.dot(hh, w2_ref[...], preferred_element_type=F32)


def _dense_ffn(res, g, w1, w3, w2):
    t = res.shape[0]
    dff = w1.shape[1]
    once = pl.Buffered(1)
    return pl.pallas_call(
        _dense_ffn_kernel,
        grid=(t // FFN_TILE,),
        in_specs=[pl.BlockSpec((FFN_TILE, D_MODEL), lambda i: (i, 0)),
                  pl.BlockSpec((1, D_MODEL), lambda i: (0, 0)),
                  pl.BlockSpec((D_MODEL, dff), lambda i: (0, 0), pipeline_mode=once),
                  pl.BlockSpec((D_MODEL, dff), lambda i: (0, 0), pipeline_mode=once),
                  pl.BlockSpec((dff, D_MODEL), lambda i: (0, 0), pipeline_mode=once)],
        out_specs=pl.BlockSpec((FFN_TILE, D_MODEL), lambda i: (i, 0)),
        out_shape=jax.ShapeDtypeStruct((t, D_MODEL), F32),
        compiler_params=_cp(("parallel",)),
        name="dense_ffn",
    )(res, g, w1, w3, w2)


def _gelu_tanh(x):
    c = math.sqrt(2.0 / math.pi)
    return 0.5 * x * (1.0 + jnp.tanh(c * (x + 0.044715 * (x * x * x))))


def _s5_chunk(lc, get_u, state_ref, ut_ref, bu_ref, ot_ref,
              a_ref, bd_ref, cd_ref, dskip_ref, wglu_ref):
    for b in range(NB):
        ub = get_u(b)
        for s in range(2):
            ut_ref[s, pl.ds(b, lc, stride=NB), :] = ub[:, LANES * s:LANES * (s + 1)]
    u_tm = jnp.concatenate([ut_ref[0], ut_ref[1]], axis=1)
    bu_ref[...] = jnp.dot(u_tm.astype(BF16), bd_ref[...], preferred_element_type=F32)

    half = SSM_COLS // 2
    a_re = a_ref[:, :half]
    a_im = a_ref[:, half:]

    def step(t, x):
        x_re, x_im = x
        r = pl.multiple_of(t * NB, NB)
        cur = bu_ref[pl.ds(r, NB), :]
        n_re = a_re * x_re - a_im * x_im + cur[:, :half]
        n_im = a_re * x_im + a_im * x_re + cur[:, half:]
        bu_ref[pl.ds(r, NB), :] = jnp.concatenate([n_re, n_im], axis=1)
        return n_re, n_im

    x_re, x_im = lax.fori_loop(0, lc, step,
                               (state_ref[:, :half], state_ref[:, half:]), unroll=4)
    state_ref[...] = jnp.concatenate([x_re, x_im], axis=1)

    y = jnp.dot(bu_ref[...].astype(BF16), cd_ref[...], preferred_element_type=F32)
    y = _gelu_tanh(y + dskip_ref[...] * u_tm)
    g = jnp.dot(y.astype(BF16), wglu_ref[...], preferred_element_type=F32)
    o = g[:, :SSM_WIDTH] * jax.nn.sigmoid(g[:, SSM_WIDTH:])
    ot_ref[0] = o[:, :LANES]
    ot_ref[1] = o[:, LANES:]


def _s5_read_out(ot_ref, b, lc):
    return jnp.concatenate(
        [ot_ref[s, pl.ds(b, lc, stride=NB), :] for s in range(2)], axis=1)


def _s5_meta_kernel(u_ref, a_ref, bd_ref, cd_ref, dskip_ref, wglu_ref,
                    o_ref, state_out_ref, state_ref, ut_ref, bu_ref, ot_ref):
    state_ref[...] = jnp.zeros_like(state_ref)
    _s5_chunk(N_META, lambda b: u_ref[b * N_META:(b + 1) * N_META, :],
              state_ref, ut_ref, bu_ref, ot_ref, a_ref, bd_ref, cd_ref, dskip_ref, wglu_ref)
    for b in range(NB):
        o_ref[b * N_META:(b + 1) * N_META, :] = _s5_read_out(ot_ref, b, N_META).astype(BF16)
    state_out_ref[...] = state_ref[...]


def _s5_real_kernel(*refs):
    u_refs = refs[:NB]
    (state_in_ref, a_ref, bd_ref, cd_ref, dskip_ref, wglu_ref, _flat_ref,
     o_ref, state_ref, ut_ref, bu_ref, ot_ref) = refs[NB:]
    c = pl.program_id(0)
    b = pl.program_id(1)

    @pl.when((c == 0) & (b == 0))
    def _():
        state_ref[...] = state_in_ref[...]

    @pl.when(b == 0)
    def _():
        _s5_chunk(S5_CHUNK, lambda bb: u_refs[bb][...],
                  state_ref, ut_ref, bu_ref, ot_ref, a_ref, bd_ref, cd_ref, dskip_ref, wglu_ref)

    o_ref[...] = _s5_read_out(ot_ref, b, S5_CHUNK).astype(BF16)


def _s5(zu, seq, a, bd, cd, dskip, wglu):
    t = zu.shape[0]
    tr = NB * seq
    nmeta_rows = NB * N_META
    const = lambda *_: (0, 0)
    par_specs = [pl.BlockSpec((NB, SSM_COLS), const),
                 pl.BlockSpec((SSM_WIDTH, SSM_COLS), const),
                 pl.BlockSpec((SSM_COLS, SSM_WIDTH), const),
                 pl.BlockSpec((1, SSM_WIDTH), const),
                 pl.BlockSpec((SSM_WIDTH, 2 * SSM_WIDTH), const)]

    def scratch(lc):
        return [pltpu.VMEM((NB, SSM_COLS), F32),
                pltpu.VMEM((2, lc * NB, LANES), F32),
                pltpu.VMEM((lc * NB, SSM_COLS), F32),
                pltpu.VMEM((2, lc * NB, LANES), F32)]

    meta_blk = tr // nmeta_rows
    flat, state = pl.pallas_call(
        _s5_meta_kernel,
        grid=(1,),
        in_specs=[pl.BlockSpec((nmeta_rows, SSM_WIDTH), lambda i: (meta_blk, 0))] + par_specs,
        out_specs=[pl.BlockSpec((nmeta_rows, SSM_WIDTH), lambda i: (meta_blk, 0)),
                   pl.BlockSpec((NB, SSM_COLS), const)],
        out_shape=[jax.ShapeDtypeStruct((t, SSM_WIDTH), BF16),
                   jax.ShapeDtypeStruct((NB, SSM_COLS), F32)],
        scratch_shapes=scratch(N_META),
        compiler_params=_cp(("arbitrary",)),
        name="s5_meta",
    )(zu, a, bd, cd, dskip, wglu)

    nc = seq // S5_CHUNK
    u_specs = [pl.BlockSpec((S5_CHUNK, SSM_WIDTH), lambda c, b, bb=bb: (bb * nc + c, 0))
               for bb in range(NB)]
    n_in = NB + 1 + len(par_specs)
    return pl.pallas_call(
        _s5_real_kernel,
        grid=(nc, NB),
        in_specs=u_specs + [pl.BlockSpec((NB, SSM_COLS), const)] + par_specs
        + [pl.BlockSpec(memory_space=pl.ANY)],
        out_specs=pl.BlockSpec((S5_CHUNK, SSM_WIDTH), lambda c, b: (b * nc + c, 0)),
        out_shape=jax.ShapeDtypeStruct((t, SSM_WIDTH), BF16),
        scratch_shapes=scratch(S5_CHUNK),
        input_output_aliases={n_in: 0},
        compiler_params=_cp(("arbitrary", "arbitrary")),
        name="s5_real",
    )(*([zu] * NB), state, a, bd, cd, dskip, wglu, flat)


def _s5_params(lam_re, lam_im, log_dt, b_re, b_im, c_re, c_im, d_skip):
    dt = jnp.exp(log_dt)[:, None]
    mag = jnp.exp(lam_re * dt)
    ab_re = mag * jnp.cos(lam_im * dt)
    ab_im = mag * jnp.sin(lam_im * dt)
    den = lam_re * lam_re + lam_im * lam_im
    nr = ab_re - 1.0
    ni = ab_im
    coef_re = ((nr * lam_re + ni * lam_im) / den)[..., None]
    coef_im = ((ni * lam_re - nr * lam_im) / den)[..., None]
    bb_re = coef_re * b_re - coef_im * b_im
    bb_im = coef_re * b_im + coef_im * b_re
    eye = jnp.eye(SSM_GROUPS, dtype=F32)
    half = SSM_COLS // 2
    bd = jnp.concatenate(
        [jnp.einsum('gnc,gh->gchn', m, eye).reshape(SSM_WIDTH, half) for m in (bb_re, bb_im)],
        axis=1).astype(BF16)
    cd = jnp.concatenate(
        [jnp.einsum('gcn,gh->gnhc', m, eye).reshape(half, SSM_WIDTH) for m in (c_re, -c_im)],
        axis=0).astype(BF16)
    a = jnp.concatenate([ab_re.reshape(1, half), ab_im.reshape(1, half)], axis=1)
    a = jnp.broadcast_to(a, (NB, SSM_COLS))
    return a, bd, cd, d_skip.reshape(1, SSM_WIDTH)


def _nt_dot(a, b):
    return lax.dot_general(a, b, (((1,), (1,)), ((), ())), preferred_element_type=F32)


def _osm_init(s, vt):
    m = jnp.max(s, axis=1, keepdims=True)
    p = jnp.exp2(s - m)
    l = jnp.sum(p, axis=1, keepdims=True)
    acc = _nt_dot(p.astype(BF16), vt)
    return m, l, acc


def _causal_sweep(qts, k_ref, k_cols, vt_ref, vt_rows, kms, vmts, qi,
                  m_ref, l_ref, acc_ref, al_ref, p_ref):
    n = len(qts)
    meta_scores = [jnp.dot(kms[i], qts[i], preferred_element_type=F32) for i in range(n)]
    meta_p = []
    for i in range(n):
        s = meta_scores[i]
        m = jnp.max(s, axis=0, keepdims=True)
        p = jnp.exp2(s - m)
        m_ref[i][...] = m
        l_ref[i][...] = jnp.sum(p, axis=0, keepdims=True)
        meta_p.append(p.astype(BF16))
        al_ref[i][...] = jnp.ones_like(al_ref[i])
        p_ref[i][...] = jnp.zeros_like(p_ref[i])
    for i in range(n):
        acc_ref[i][...] = jnp.dot(vmts[i], meta_p[i], preferred_element_type=F32)

    sub = ATT_TK // LANES

    def pending_pv(jp):
        for i in range(n):
            vtb = jnp.concatenate([vt_ref[jp * sub + c, vt_rows[i], :] for c in range(sub)],
                                  axis=1)
            acc_ref[i][...] = al_ref[i][...] * acc_ref[i][...] + jnp.dot(
                vtb, p_ref[i][...], preferred_element_type=F32)

    def chunk(j, masked):
        off = pl.multiple_of(j * ATT_TK, ATT_TK)
        if masked:
            krow = lax.broadcasted_iota(jnp.int32, (ATT_TK, ATT_TQ), 0)
            qcol = lax.broadcasted_iota(jnp.int32, (ATT_TK, ATT_TQ), 1)
            visible = krow <= qcol
        scores = []
        for i in range(n):
            kb = k_ref[pl.ds(off, ATT_TK), k_cols[i]]
            s = jnp.dot(kb, qts[i], preferred_element_type=F32)
            scores.append(jnp.where(visible, s, MASK_VALUE) if masked else s)
        pending_pv(jnp.maximum(j - 1, 0))
        for i in range(n):
            m_old = m_ref[i][...]
            m_new = jnp.maximum(m_old, jnp.max(scores[i], axis=0, keepdims=True))
            alpha = jnp.exp2(m_old - m_new)
            p = jnp.exp2(scores[i] - m_new)
            l_ref[i][...] = alpha * l_ref[i][...] + jnp.sum(p, axis=0, keepdims=True)
            m_ref[i][...] = m_new
            al_ref[i][...] = alpha
            p_ref[i][...] = p.astype(BF16)

    def body(j, carry):
        chunk(j, False)
        return carry

    lax.fori_loop(0, qi, body, 0)
    chunk(qi, True)
    pending_pv(qi)


def _meta_attend(q, km, vmt):
    n = km.shape[0]
    row = lax.broadcasted_iota(jnp.int32, (n, n), 0)
    col = lax.broadcasted_iota(jnp.int32, (n, n), 1)
    s = jnp.where(col <= row, _nt_dot(q, km), MASK_VALUE)
    _, l, acc = _osm_init(s, vmt)
    return l, acc


def _diff_split(q):
    lane = lax.broadcasted_iota(jnp.int32, q.shape, 1)
    q = (q.astype(F32) * QK_SCALE).astype(BF16)
    zero = jnp.zeros_like(q)
    return jnp.where(lane < 64, q, zero), jnp.where(lane >= 64, q, zero)


def _diff_finish(parts, lam, lam_init, g):
    (l1, acc1), (l2, acc2) = parts
    o = acc1 / l1 - lam * (acc2 / l2)
    return (_rms(o, g) * (1.0 - lam_init)).astype(BF16)


def _transpose_bf16(x):
    return x.astype(F32).T.astype(BF16)


def _map_scratch(n_maps, dv):
    return ([pltpu.VMEM((1, ATT_TQ), F32)] * (2 * n_maps)
            + [pltpu.VMEM((dv, ATT_TQ), F32)] * n_maps
            + [pltpu.VMEM((1, ATT_TQ), F32)] * n_maps
            + [pltpu.VMEM((ATT_TK, ATT_TQ), BF16)] * n_maps)


def _split_map_scratch(scratch, n_maps):
    return [scratch[k * n_maps:(k + 1) * n_maps] for k in range(5)]


def _diff_kernel(lam_init, lam_ref, q_ref, k_ref, vt_ref, km_ref, vmt_ref, g_ref, o_ref,
                 *scratch):
    m_ref, l_ref, acc_ref, al_ref, p_ref = _split_map_scratch(scratch, 2 * DIFF_HEADS)
    qts, k_cols, vt_rows, kms, vmts = [], [], [], [], []
    row = lax.broadcasted_iota(jnp.int32, (LANES, ATT_TQ), 0)
    for h in range(DIFF_HEADS):
        sl = slice(LANES * h, LANES * (h + 1))
        km = km_ref[:, sl]
        vmt = vmt_ref[sl, :]
        qt = (q_ref[:, sl].astype(F32) * QK_SCALE).T.astype(BF16)
        zero = jnp.zeros_like(qt)
        for qm in (jnp.where(row < 64, qt, zero), jnp.where(row >= 64, qt, zero)):
            qts.append(qm)
            k_cols.append(sl)
            vt_rows.append(sl)
            kms.append(km)
            vmts.append(vmt)
    _causal_sweep(qts, k_ref, k_cols, vt_ref, vt_rows, kms, vmts, pl.program_id(1),
                  m_ref, l_ref, acc_ref, al_ref, p_ref)
    lam = lam_ref[0]
    for h in range(DIFF_HEADS):
        o = (acc_ref[2 * h][...] / l_ref[2 * h][...]
             - lam * (acc_ref[2 * h + 1][...] / l_ref[2 * h + 1][...]))
        y = o * lax.rsqrt(jnp.mean(o * o, axis=0, keepdims=True) + EPS)
        o_ref[:, LANES * h:LANES * (h + 1)] = (
            y.T * g_ref[...] * (1.0 - lam_init)).astype(BF16)


def _diff_meta_kernel(lam_init, lam_ref, q_ref, km_ref, vmt_ref, g_ref, _flat_ref, o_ref):
    for h in range(DIFF_HEADS):
        sl = slice(LANES * h, LANES * (h + 1))
        q1, q2 = _diff_split(q_ref[:, sl])
        km = km_ref[:, sl]
        vmt = vmt_ref[sl, :]
        parts = [_meta_attend(q1, km, vmt), _meta_attend(q2, km, vmt)]
        o_ref[:, sl] = _diff_finish(parts, lam_ref[0], lam_init, g_ref[...])


def _diff_attention(za, vt, vtm, seq, lam, lam_init, g):
    t = za.shape[0]
    nq = seq // ATT_TQ
    mrow = NB * seq // N_META
    smem = pl.BlockSpec(memory_space=pltpu.SMEM)
    lam = lam.reshape(1).astype(F32)
    g = g.reshape(1, LANES)
    gspec = pl.BlockSpec((1, LANES), lambda *_: (0, 0))
    width = DIFF_HEADS * LANES
    n_maps = 2 * DIFF_HEADS
    flat = pl.pallas_call(
        functools.partial(_diff_kernel, lam_init),
        grid=(NB, nq),
        in_specs=[smem,
                  pl.BlockSpec((ATT_TQ, width), lambda b, i: (b * nq + i, 0)),
                  pl.BlockSpec((seq, width), lambda b, i: (b, 1)),
                  pl.BlockSpec((seq // LANES, width, LANES), lambda b, i: (b, 0, 0)),
                  pl.BlockSpec((N_META, width), lambda b, i: (mrow + b, 1)),
                  pl.BlockSpec((None, width, N_META), lambda b, i: (b, 0, 0)),
                  gspec],
        out_specs=pl.BlockSpec((ATT_TQ, width), lambda b, i: (b * nq + i, 0)),
        out_shape=jax.ShapeDtypeStruct((t, width), BF16),
        scratch_shapes=_map_scratch(n_maps, LANES),
        compiler_params=_cp(("parallel", "arbitrary")),
        name="diff_attn",
    )(lam, za, za, vt, za, vtm, g)
    return pl.pallas_call(
        functools.partial(_diff_meta_kernel, lam_init),
        grid=(NB,),
        in_specs=[smem,
                  pl.BlockSpec((N_META, width), lambda b: (mrow + b, 0)),
                  pl.BlockSpec((N_META, width), lambda b: (mrow + b, 1)),
                  pl.BlockSpec((None, width, N_META), lambda b: (b, 0, 0)),
                  gspec,
                  pl.BlockSpec(memory_space=pl.ANY)],
        out_specs=pl.BlockSpec((N_META, width), lambda b: (mrow + b, 0)),
        out_shape=jax.ShapeDtypeStruct((t, width), BF16),
        input_output_aliases={5: 0},
        compiler_params=_cp(("parallel",)),
        name="diff_attn_meta",
    )(lam, za, za, vtm, g, flat)


def _split3(c):
    hi = c.astype(BF16).astype(F32)
    r1 = c - hi
    mid = r1.astype(BF16).astype(F32)
    lo = (r1 - mid).astype(BF16).astype(F32)
    return hi, mid, lo


def _cumsum_rows(tri, lf):
    parts = jnp.concatenate(_split3(lf), axis=1).astype(BF16)
    r = jnp.dot(tri, parts, preferred_element_type=F32)
    return r[:, :LANES] + r[:, LANES:2 * LANES] + r[:, 2 * LANES:]


def _log_sigmoid(x):
    return jnp.minimum(x, 0.0) - jnp.log1p(jnp.exp(-jnp.abs(x)))


def _fox_augment(fq, fk, cum, qa_ref, ka_ref):
    n = fq.shape[0]
    lane = lax.broadcasted_iota(jnp.int32, (n, LANES), 1)
    for h in range(FOX_HEADS):
        pair = slice(LANES * (h // 2), LANES * (h // 2) + LANES)
        own = (lane // 64) == (h % 2)
        e0 = 64 * (1 - h % 2)
        hi, mid, lo = _split3(jnp.broadcast_to(cum[:, h:h + 1] * LOG2E, (n, LANES)))
        ones = (lane >= e0 + 3) & (lane < e0 + 6)
        q_extra = jnp.where(lane == e0, hi, jnp.where(lane == e0 + 1, mid, jnp.where(
            lane == e0 + 2, lo, jnp.where(ones, 1.0, 0.0))))
        ones = (lane >= e0) & (lane < e0 + 3)
        k_extra = jnp.where(lane == e0 + 3, -hi, jnp.where(lane == e0 + 4, -mid, jnp.where(
            lane == e0 + 5, -lo, jnp.where(ones, 1.0, 0.0))))
        q = fq[:, pair].astype(F32) * QK_SCALE
        k = fk[:, pair].astype(F32)
        qa_ref[:, LANES * h:LANES * (h + 1)] = jnp.where(own, q, q_extra).astype(BF16)
        ka_ref[:, LANES * h:LANES * (h + 1)] = jnp.where(own, k, k_extra).astype(BF16)


def _fox_prep_meta_kernel(zg_ref, fq_ref, fk_ref, fb_ref, qa_ref, ka_ref, carry_ref):
    n = NB * N_META
    lf = _log_sigmoid(zg_ref[...] + fb_ref[...])
    row = lax.broadcasted_iota(jnp.int32, (n, n), 0)
    col = lax.broadcasted_iota(jnp.int32, (n, n), 1)
    tri = ((col <= row) & (col // N_META == row // N_META)).astype(BF16)
    cum = _cumsum_rows(tri, lf)
    brow = lax.broadcasted_iota(jnp.int32, (NB, n), 0)
    bcol = lax.broadcasted_iota(jnp.int32, (NB, n), 1)
    carry_ref[...] = _cumsum_rows((bcol // N_META == brow).astype(BF16), lf)
    _fox_augment(fq_ref[...], fk_ref[...], cum, qa_ref, ka_ref)


def _fox_prep_kernel(zg_ref, fq_ref, fk_ref, fb_ref, carry_in_ref, qa_ref, ka_ref, carry_ref):
    b = pl.program_id(0)
    n = zg_ref.shape[0]

    @pl.when(pl.program_id(1) == 0)
    def _():
        carry_ref[...] = carry_in_ref[pl.ds(b, 1), :]

    lf = _log_sigmoid(zg_ref[...] + fb_ref[...])
    row = lax.broadcasted_iota(jnp.int32, (n, n), 0)
    col = lax.broadcasted_iota(jnp.int32, (n, n), 1)
    cum = _cumsum_rows((col <= row).astype(BF16), lf) + carry_ref[...]
    carry_ref[...] = cum[n - 1:n, :]
    _fox_augment(fq_ref[...], fk_ref[...], cum, qa_ref, ka_ref)


FOX_PREP_TILE = 1024


def _fox_prep(zg, za, seq, fb):
    tr = NB * seq
    nm = NB * N_META
    mblk = tr // nm
    fb = jnp.pad(fb.astype(F32), (0, LANES - FOX_HEADS)).reshape(1, LANES)
    fbspec = pl.BlockSpec((1, LANES), lambda *_: (0, 0))
    aug = FOX_HEADS * LANES
    qa_m, ka_m, carry = pl.pallas_call(
        _fox_prep_meta_kernel,
        grid=(1,),
        in_specs=[pl.BlockSpec((nm, LANES), lambda i: (mblk, 0)),
                  pl.BlockSpec((nm, 256), lambda i: (mblk, 4)),
                  pl.BlockSpec((nm, 256), lambda i: (mblk, 5)),
                  fbspec],
        out_specs=[pl.BlockSpec((nm, aug), lambda i: (0, 0)),
                   pl.BlockSpec((nm, aug), lambda i: (0, 0)),
                   pl.BlockSpec((NB, LANES), lambda i: (0, 0))],
        out_shape=[jax.ShapeDtypeStruct((nm, aug), BF16),
                   jax.ShapeDtypeStruct((nm, aug), BF16),
                   jax.ShapeDtypeStruct((NB, LANES), F32)],
        compiler_params=_cp(("arbitrary",)),
        name="fox_prep_meta",
    )(zg, za, za, fb)
    nc = seq // FOX_PREP_TILE
    qa, ka = pl.pallas_call(
        _fox_prep_kernel,
        grid=(NB, nc),
        in_specs=[pl.BlockSpec((FOX_PREP_TILE, LANES), lambda b, c: (b * nc + c, 0)),
                  pl.BlockSpec((FOX_PREP_TILE, 256), lambda b, c: (b * nc + c, 4)),
                  pl.BlockSpec((FOX_PREP_TILE, 256), lambda b, c: (b * nc + c, 5)),
                  fbspec,
                  pl.BlockSpec((NB, LANES), lambda b, c: (0, 0))],
        out_specs=[pl.BlockSpec((FOX_PREP_TILE, aug), lambda b, c: (b * nc + c, 0)),
                   pl.BlockSpec((FOX_PREP_TILE, aug), lambda b, c: (b * nc + c, 0))],
        out_shape=[jax.ShapeDtypeStruct((tr, aug), BF16),
                   jax.ShapeDtypeStruct((tr, aug), BF16)],
        scratch_shapes=[pltpu.VMEM((1, LANES), F32)],
        compiler_params=_cp(("parallel", "arbitrary")),
        name="fox_prep",
    )(zg, za, za, fb, carry)
    return qa, ka, qa_m, ka_m


def _fox_finish(parts):
    (l0, acc0), (l1, acc1) = parts
    lane = lax.broadcasted_iota(jnp.int32, acc0.shape, 1)
    return jnp.where(lane < 64, acc0 / l0, acc1 / l1).astype(BF16)


def _fox_kernel(q_ref, k_ref, vt_ref, km_ref, vmt_ref, o_ref, *scratch):
    m_ref, l_ref, acc_ref, al_ref, p_ref = _split_map_scratch(scratch, FOX_HEADS)
    hd = 64
    vmt = vmt_ref[...]
    qts, k_cols, vt_rows, kms, vmts = [], [], [], [], []
    for h in range(FOX_HEADS):
        sl = slice(LANES * h, LANES * (h + 1))
        qts.append(_transpose_bf16(q_ref[:, sl]))
        k_cols.append(sl)
        vt_rows.append(slice(hd * h, hd * (h + 1)))
        kms.append(km_ref[:, sl])
        vmts.append(vmt[hd * h:hd * (h + 1), :])
    _causal_sweep(qts, k_ref, k_cols, vt_ref, vt_rows, kms, vmts, pl.program_id(1),
                  m_ref, l_ref, acc_ref, al_ref, p_ref)
    for p in range(FOX_HEADS // 2):
        o = jnp.concatenate([acc_ref[2 * p][...] / l_ref[2 * p][...],
                             acc_ref[2 * p + 1][...] / l_ref[2 * p + 1][...]],
                            axis=0)
        o_ref[:, LANES * p:LANES * (p + 1)] = o.T.astype(BF16)


def _fox_meta_kernel(q_ref, km_ref, vmt_ref, _flat_ref, o_ref):
    for p in range(FOX_HEADS // 2):
        vmt = vmt_ref[LANES * p:LANES * (p + 1), :]
        parts = []
        for h in (2 * p, 2 * p + 1):
            sl = slice(LANES * h, LANES * (h + 1))
            parts.append(_meta_attend(q_ref[:, sl], km_ref[:, sl], vmt))
        o_ref[:, LANES * p:LANES * (p + 1)] = _fox_finish(parts)


def _fox_attention(t, vt, vtm, seq, qa, ka, qa_m, ka_m):
    nq = seq // ATT_TQ
    mrow = NB * seq // N_META
    aug = FOX_HEADS * LANES
    flat = pl.pallas_call(
        _fox_kernel,
        grid=(NB, nq),
        in_specs=[pl.BlockSpec((ATT_TQ, aug), lambda b, i: (b * nq + i, 0)),
                  pl.BlockSpec((seq, aug), lambda b, i: (b, 0)),
                  pl.BlockSpec((seq // LANES, 256, LANES), lambda b, i: (b, 2, 0)),
                  pl.BlockSpec((N_META, aug), lambda b, i: (b, 0)),
                  pl.BlockSpec((None, 256, N_META), lambda b, i: (b, 2, 0))],
        out_specs=pl.BlockSpec((ATT_TQ, 256), lambda b, i: (b * nq + i, 0)),
        out_shape=jax.ShapeDtypeStruct((t, 256), BF16),
        scratch_shapes=_map_scratch(FOX_HEADS, 64),
        compiler_params=_cp(("parallel", "arbitrary")),
        name="fox_attn",
    )(qa, ka, vt, ka_m, vtm)
    return pl.pallas_call(
        _fox_meta_kernel,
        grid=(NB,),
        in_specs=[pl.BlockSpec((N_META, aug), lambda b: (b, 0)),
                  pl.BlockSpec((N_META, aug), lambda b: (b, 0)),
                  pl.BlockSpec((None, 256, N_META), lambda b: (b, 2, 0)),
                  pl.BlockSpec(memory_space=pl.ANY)],
        out_specs=pl.BlockSpec((N_META, 256), lambda b: (mrow + b, 0)),
        out_shape=jax.ShapeDtypeStruct((t, 256), BF16),
        input_output_aliases={3: 0},
        compiler_params=_cp(("parallel",)),
        name="fox_attn_meta",
    )(qa_m, ka_m, vtm, flat)


def _router_kernel(res_ref, g_ref, wr_ref, idx_ref, gate_ref):
    h = _rms(res_ref[...], g_ref[...])
    w = wr_ref[...]
    hi = h.astype(BF16)
    lo = (h - hi.astype(F32)).astype(BF16)
    whi = w.astype(BF16)
    wlo = (w - whi.astype(F32)).astype(BF16)
    logits = (jnp.dot(hi, whi, preferred_element_type=F32)
              + jnp.dot(lo, whi, preferred_element_type=F32)
              + jnp.dot(hi, wlo, preferred_element_type=F32))
    lane = lax.broadcasted_iota(jnp.int32, logits.shape, 1)
    logits = jnp.where(lane < N_EXPERTS, logits, -jnp.inf)
    m1 = jnp.max(logits, axis=1, keepdims=True)
    i1 = jnp.min(jnp.where(logits == m1, lane, LANES), axis=1, keepdims=True)
    rest = jnp.where(lane == i1, -jnp.inf, logits)
    m2 = jnp.max(rest, axis=1, keepdims=True)
    i2 = jnp.min(jnp.where(rest == m2, lane, LANES), axis=1, keepdims=True)
    e = jnp.exp(m2 - m1)
    g1 = 1.0 / (1.0 + e)
    g2 = e / (1.0 + e)
    idx_ref[...] = jnp.where(lane == 0, i1, jnp.where(lane == 1, i2, 0))
    gate_ref[...] = jnp.where(lane == 0, g1, jnp.where(lane == 1, g2, 0.0))


def _router(res, g, wr):
    t = res.shape[0]
    return pl.pallas_call(
        _router_kernel,
        grid=(t // TOK_TILE,),
        in_specs=[pl.BlockSpec((TOK_TILE, D_MODEL), lambda i: (i, 0)),
                  pl.BlockSpec((1, D_MODEL), lambda i: (0, 0)),
                  pl.BlockSpec((D_MODEL, LANES), lambda i: (0, 0))],
        out_specs=[pl.BlockSpec((TOK_TILE, LANES), lambda i: (i, 0)),
                   pl.BlockSpec((TOK_TILE, LANES), lambda i: (i, 0))],
        out_shape=[jax.ShapeDtypeStruct((t, LANES), jnp.int32),
                   jax.ShapeDtypeStruct((t, LANES), F32)],
        compiler_params=_cp(("parallel",)),
        name="router",
    )(res, g, wr)


def _gather_copy(src_hbm, first, dst_ref, r, sem):
    first = pl.multiple_of(first, ROW_BLOCKS)
    dst = pl.multiple_of(r * ROW_BLOCKS, ROW_BLOCKS)
    return pltpu.make_async_copy(src_hbm.at[pl.ds(first, ROW_BLOCKS), :],
                                 dst_ref.at[pl.ds(dst, ROW_BLOCKS), :], sem)


def _expert_ffn_kernel(nf, te_ref, na_ref, src_ref, x_hbm, g_ref, w1_ref, w3_ref, w2_ref,
                       o_ref, xbuf_ref, h_ref, acc_ref, sem):
    r = pl.program_id(0)
    f = pl.program_id(1)
    n_act = na_ref[0]
    per_step = MOE_XROWS // nf
    last_tile = pl.num_programs(0) - 1

    def tile_copy(slot):
        return pltpu.make_async_copy(x_hbm.at[pl.ds(0, MOE_XROWS * ROW_BLOCKS), :],
                                     xbuf_ref.at[slot], sem.at[slot])

    def start_row(tile, slot, k):
        _gather_copy(x_hbm, src_ref[tile * MOE_XROWS + k], xbuf_ref.at[slot], k,
                     sem.at[slot]).start()

    @pl.when(r < n_act)
    def _():
        slot = lax.rem(r, MOE_XSLOTS)
        slot1 = lax.rem(r + 1, MOE_XSLOTS)
        slot2 = lax.rem(r + 2, MOE_XSLOTS)

        @pl.when((r == 0) & (f == 0))
        def _():
            def first(k, _):
                start_row(0, 0, k)
                start_row(jnp.minimum(1, last_tile), 1, k)
                return 0
            lax.fori_loop(0, MOE_XROWS, first, 0, unroll=8)

        @pl.when(f == 0)
        def _():
            tile_copy(slot).wait()
            x = _load_row_tiles(xbuf_ref.at[slot], MOE_TILE)
            h_ref[...] = _rms(x, g_ref[...]).astype(BF16)
            acc_ref[...] = jnp.zeros_like(acc_ref)

        ahead = jnp.minimum(r + 2, last_tile)
        for k in range(per_step):
            start_row(ahead, slot2, f * per_step + k)

        h = h_ref[...]
        a = jnp.dot(h, w1_ref[...].astype(BF16), preferred_element_type=F32)
        b = jnp.dot(h, w3_ref[...].astype(BF16), preferred_element_type=F32)
        hh = (a * jax.nn.sigmoid(a) * b).astype(BF16)
        acc_ref[...] += jnp.dot(hh, w2_ref[...].astype(BF16), preferred_element_type=F32)

        @pl.when(f == nf - 1)
        def _():
            _store_row_tiles(o_ref, acc_ref[...])

        @pl.when((f == nf - 1) & (r == n_act - 1))
        def _():
            tile_copy(slot1).wait()
            tile_copy(slot2).wait()


def _expert_ffn(tile_expert, n_active, src, x, g, w1, w3, w2):
    dff = w1.shape[2]
    nt = src.shape[0] // MOE_XROWS
    nf = dff // MOE_FCHUNK
    assert MOE_XROWS % nf == 0 and MOE_XROWS >= MOE_TILE

    def row(r, f, te, na, src):
        return jnp.minimum(r, na[0] - 1)

    def fch(r, f, te, na, src):
        return jnp.where(r < na[0], f, nf - 1)

    return pl.pallas_call(
        functools.partial(_expert_ffn_kernel, nf),
        grid_spec=pltpu.PrefetchScalarGridSpec(
            num_scalar_prefetch=3,
            grid=(nt, nf),
            in_specs=[
                pl.BlockSpec(memory_space=pl.ANY),
                pl.BlockSpec((1, D_MODEL), lambda r, f, te, na, src: (0, 0)),
                pl.BlockSpec((None, D_MODEL, MOE_FCHUNK),
                             lambda r, f, te, na, src: (te[r], 0, fch(r, f, te, na, src))),
                pl.BlockSpec((None, D_MODEL, MOE_FCHUNK),
                             lambda r, f, te, na, src: (te[r], 0, fch(r, f, te, na, src))),
                pl.BlockSpec((None, MOE_FCHUNK, D_MODEL),
                             lambda r, f, te, na, src: (te[r], fch(r, f, te, na, src), 0))],
            out_specs=pl.BlockSpec((MOE_TILE * ROW_BLOCKS, LANES),
                                   lambda r, f, te, na, src: (row(r, f, te, na, src), 0)),
            scratch_shapes=[pltpu.VMEM((MOE_XSLOTS, MOE_XROWS * ROW_BLOCKS, LANES), F32),
                            pltpu.VMEM((MOE_TILE, D_MODEL), BF16),
                            pltpu.VMEM((MOE_TILE, D_MODEL), F32),
                            pltpu.SemaphoreType.DMA((MOE_XSLOTS,))]),
        out_shape=jax.ShapeDtypeStruct((nt * MOE_TILE * ROW_BLOCKS, LANES), F32),
        compiler_params=_cp(("arbitrary", "arbitrary")),
        name="expert_ffn",
    )(tile_expert, n_active, src, x, g, w1, w3, w2)


def _combine_kernel(tile, final, pos_ref, res_ref, gate_ref, gf_ref, y_hbm, o_ref, ybuf_ref, sem):
    i = pl.program_id(0)
    n = pl.num_programs(0)
    slot = lax.rem(i, 2)

    def issue_tile(step, s):
        base = step * tile

        def issue(r, _):
            for k in range(2):
                _gather_copy(y_hbm, pos_ref[2 * (base + r) + k], ybuf_ref.at[s, k], r,
                             sem.at[s]).start(priority=k)
            return 0

        lax.fori_loop(0, tile, issue, 0, unroll=8)

    @pl.when(i == 0)
    def _():
        issue_tile(0, 0)

    @pl.when(i + 1 < n)
    def _():
        issue_tile(i + 1, 1 - slot)

    pltpu.make_async_copy(ybuf_ref.at[slot], ybuf_ref.at[slot], sem.at[slot]).wait()
    gate = gate_ref[...]
    out = (res_ref[...] + gate[:, 0:1] * _load_row_tiles(ybuf_ref.at[slot, 0], tile)
           + gate[:, 1:2] * _load_row_tiles(ybuf_ref.at[slot, 1], tile))
    o_ref[...] = _rms(out, gf_ref[...]) if final else out


def _combine(pos, res, gates, y, final_g=None):
    final = final_g is not None
    t = res.shape[0]
    n_rows = t - NB * N_META if final else t
    tile = 256 if final else COMBINE_TILE
    if not final:
        final_g = jnp.ones((1, D_MODEL), F32)
    return pl.pallas_call(
        functools.partial(_combine_kernel, tile, final),
        grid_spec=pltpu.PrefetchScalarGridSpec(
            num_scalar_prefetch=1,
            grid=(n_rows // tile,),
            in_specs=[pl.BlockSpec((tile, D_MODEL), lambda i, pos: (i, 0)),
                      pl.BlockSpec((tile, LANES), lambda i, pos: (i, 0)),
                      pl.BlockSpec((1, D_MODEL), lambda i, pos: (0, 0)),
                      pl.BlockSpec(memory_space=pl.ANY)],
            out_specs=pl.BlockSpec((tile, D_MODEL), lambda i, pos: (i, 0)),
            scratch_shapes=[pltpu.VMEM((2, 2, tile * ROW_BLOCKS, LANES), F32),
                            pltpu.SemaphoreType.DMA((2,))]),
        out_shape=jax.ShapeDtypeStruct((n_rows, D_MODEL), F32),
        compiler_params=_cp(("arbitrary",)),
        name="moe_combine",
    )(pos, res, gates, final_g, y)


def _moe(res, res_tiles, g, wr, w1, w3, w2, final_g=None):
    t = res.shape[0]
    wr = jnp.pad(wr.astype(F32), ((0, 0), (0, LANES - N_EXPERTS)))
    idx, gates = _router(res, g, wr)
    e_flat = idx[:, :2].reshape(-1)
    onehot = (e_flat[:, None] == jnp.arange(N_EXPERTS, dtype=jnp.int32)[None, :]).astype(jnp.int32)
    csum = jnp.cumsum(onehot, axis=0)
    rank = jnp.take_along_axis(csum, e_flat[:, None], axis=1)[:, 0] - 1
    counts = csum[-1]
    tiles = (counts + MOE_TILE - 1) // MOE_TILE
    tile_end = jnp.cumsum(tiles)
    starts = (tile_end - tiles) * MOE_TILE
    pos = (starts[e_flat] + rank).astype(jnp.int32)
    n_tiles = (2 * t + N_EXPERTS * (MOE_TILE - 1)) // MOE_TILE
    spos = (pos // MOE_TILE) * MOE_XROWS + pos % MOE_TILE
    src = jnp.zeros((n_tiles * MOE_XROWS,), jnp.int32).at[spos].set(
        (jnp.arange(2 * t, dtype=jnp.int32) // 2) * ROW_BLOCKS, unique_indices=True)
    n_active = tile_end[-1:].astype(jnp.int32)
    tile_ids = jnp.minimum(jnp.arange(n_tiles, dtype=jnp.int32), n_active[0] - 1)
    tile_expert = jnp.sum(tile_ids[:, None] >= tile_end[None, :], axis=1).astype(jnp.int32)

    y = _expert_ffn(tile_expert, n_active, src, res_tiles, g, w1, w3, w2)
    return _combine(pos * ROW_BLOCKS, res, gates, y, final_g)


def _final_norm_kernel(res_ref, g_ref, o_ref):
    o_ref[...] = _rms(res_ref[...], g_ref[...])


def _final_norm(res, g, n_rows):
    tile = 1024
    return pl.pallas_call(
        _final_norm_kernel,
        grid=(n_rows // tile,),
        in_specs=[pl.BlockSpec((tile, D_MODEL), lambda i: (i, 0)),
                  pl.BlockSpec((1, D_MODEL), lambda i: (0, 0))],
        out_specs=pl.BlockSpec((tile, D_MODEL), lambda i: (i, 0)),
        out_shape=jax.ShapeDtypeStruct((n_rows, D_MODEL), F32),
        compiler_params=_cp(("parallel",)),
        name="final_norm",
    )(res, g)


def kernel(x, meta_tokens, norm_mix_g, w_in, w_out, ssm_lambda_re, ssm_lambda_im, ssm_log_dt,
           ssm_b_re, ssm_b_im, ssm_c_re, ssm_c_im, ssm_d, ssm_w_glu, diff_lambda_q1,
           diff_lambda_k1, diff_lambda_q2, diff_lambda_k2, diff_subln_g, fox_forget_b,
           norm_ffn_g, dense_w1, dense_w3, dense_w2, moe_router, moe_w1, moe_w3, moe_w2,
           final_norm_g):
    bsz, seq, d = x.shape
    assert bsz == NB and d == D_MODEL and seq % ATT_TQ == 0
    depth = w_in.shape[0]
    tr = bsz * seq
    res = jnp.concatenate(
        [x.reshape(tr, d), jnp.tile(meta_tokens.astype(x.dtype), (bsz, 1))], axis=0)
    row = lambda v: v.reshape(1, -1).astype(F32)

    for l in range(depth):
        wl = w_in[l]
        w = jnp.concatenate([wl[:, :1280], wl[:, 1792:2304],
                             jnp.pad(wl[:, 2560:], ((0, 0), (0, LANES - FOX_HEADS)))], axis=1)
        wvt = jnp.concatenate([wl[:, 1280:1792], wl[:, 2304:2560]], axis=1).T
        zu, za, zg, vt, vtm = _inproj(res, row(norm_mix_g[l]), w.astype(BF16),
                                      wvt.astype(BF16))

        a, bd, cd, dskip = _s5_params(ssm_lambda_re[l], ssm_lambda_im[l], ssm_log_dt[l],
                                      ssm_b_re[l], ssm_b_im[l], ssm_c_re[l], ssm_c_im[l],
                                      ssm_d[l])
        ssm_out = _s5(zu, seq, a, bd, cd, dskip, ssm_w_glu[l].astype(BF16))

        lam_init = 0.8 - 0.6 * math.exp(-0.3 * l)
        lam = (jnp.exp(jnp.sum(diff_lambda_q1[l] * diff_lambda_k1[l]))
               - jnp.exp(jnp.sum(diff_lambda_q2[l] * diff_lambda_k2[l])) + lam_init)
        diff_out = _diff_attention(za, vt, vtm, seq, lam, lam_init, diff_subln_g[l])

        qa, ka, qa_m, ka_m = _fox_prep(zg, za, seq, fox_forget_b[l])
        fox_out = _fox_attention(za.shape[0], vt, vtm, seq, qa, ka, qa_m, ka_m)

        is_moe = l % 2 == 1
        res, *res_tiles = _outproj(res, ssm_out, diff_out, fox_out, w_out[l].astype(BF16),
                                   with_row_tiles=is_moe)

        if not is_moe:
            res = _dense_ffn(res, row(norm_ffn_g[l]), dense_w1[l // 2].astype(BF16),
                             dense_w3[l // 2].astype(BF16), dense_w2[l // 2].astype(BF16))
        else:
            res = _moe(res, res_tiles[0], row(norm_ffn_g[l]), moe_router[l // 2], moe_w1[l // 2],
                       moe_w3[l // 2], moe_w2[l // 2],
                       final_g=row(final_norm_g) if l == depth - 1 else None)

    if depth % 2 == 1:
        res = _final_norm(res, row(final_norm_g), tr)
    return res.reshape(bsz, seq, d)
```

```python
import functools
import math

import jax
import jax.numpy as jnp
from jax import lax
from jax.experimental import pallas as pl
from jax.experimental.pallas import tpu as pltpu

F32 = jnp.float32
BF16 = jnp.bfloat16
EPS = 1e-6

D_MODEL = 1024
N_META = 16
NB = 8
SSM_WIDTH = 256
SSM_GROUPS = 16
SSM_GROUP = 16
SSM_STATE = 64
SSM_COLS = 2 * SSM_GROUPS * SSM_STATE
DIFF_HEADS = 4
FOX_HEADS = 4
ATT_COLS = 1536
IN_PAD = 1920
N_EXPERTS = 8
LANES = 128
MASK_VALUE = -1e30
LOG2E = math.log2(math.e)
QK_SCALE = 64 ** -0.5 * LOG2E

VT_ROWS = 768
TOK_TILE = 688
IN_TILE = 384
FFN_TILE = 384
ATT_TQ = 256
ATT_TK = 256
S5_CHUNK = 256
MOE_TILE = 1024
MOE_FCHUNK = 512
MOE_SUB = 256
MOE_XROWS = 1029
MOE_XSLOTS = 3
ROW_BLOCKS = D_MODEL // LANES
COMBINE_TILE = 384

VMEM_LIMIT = 56 * 1024 * 1024


def _cp(sem):
    return pltpu.CompilerParams(dimension_semantics=sem, vmem_limit_bytes=VMEM_LIMIT)


def _rms(x, g):
    return x * lax.rsqrt(jnp.mean(x * x, axis=-1, keepdims=True) + EPS) * g


def _inproj_kernel(res_ref, g_ref, w_ref, wvt_ref, zu_ref, za_ref, zg_ref, vt_ref, vtm_ref):
    h = _rms(res_ref[...], g_ref[...]).astype(BF16)
    zu_ref[...] = jnp.dot(h, w_ref[:, 0:SSM_WIDTH], preferred_element_type=F32)
    for c in range(0, ATT_COLS, 256):
        za_ref[:, c:c + 256] = jnp.dot(
            h, w_ref[:, SSM_WIDTH + c:SSM_WIDTH + c + 256],
            preferred_element_type=F32).astype(BF16)
    zg_ref[...] = jnp.dot(h, w_ref[:, SSM_WIDTH + ATT_COLS:IN_PAD],
                          preferred_element_type=F32)
    vt = _nt_dot(wvt_ref[...], h).astype(BF16)
    for c in range(IN_TILE // LANES):
        vt_ref[c] = vt[:, LANES * c:LANES * (c + 1)]

    @pl.when(pl.program_id(0) == pl.num_programs(0) - 1)
    def _():
        first = IN_TILE - NB * N_META
        for b in range(NB):
            vtm_ref[b] = vt[:, first + N_META * b:first + N_META * (b + 1)]


def _inproj(res, g, w, wvt):
    t = res.shape[0]
    nblk = IN_TILE // LANES
    assert t % IN_TILE == 0 and NB * N_META <= IN_TILE
    return pl.pallas_call(
        _inproj_kernel,
        grid=(t // IN_TILE,),
        in_specs=[pl.BlockSpec((IN_TILE, D_MODEL), lambda i: (i, 0)),
                  pl.BlockSpec((1, D_MODEL), lambda i: (0, 0)),
                  pl.BlockSpec((D_MODEL, IN_PAD), lambda i: (0, 0)),
                  pl.BlockSpec((VT_ROWS, D_MODEL), lambda i: (0, 0))],
        out_specs=[pl.BlockSpec((IN_TILE, SSM_WIDTH), lambda i: (i, 0)),
                   pl.BlockSpec((IN_TILE, ATT_COLS), lambda i: (i, 0)),
                   pl.BlockSpec((IN_TILE, LANES), lambda i: (i, 0)),
                   pl.BlockSpec((nblk, VT_ROWS, LANES), lambda i: (i, 0, 0)),
                   pl.BlockSpec((NB, VT_ROWS, N_META), lambda i: (0, 0, 0))],
        out_shape=[jax.ShapeDtypeStruct((t, SSM_WIDTH), F32),
                   jax.ShapeDtypeStruct((t, ATT_COLS), BF16),
                   jax.ShapeDtypeStruct((t, LANES), F32),
                   jax.ShapeDtypeStruct((t // LANES, VT_ROWS, LANES), BF16),
                   jax.ShapeDtypeStruct((NB, VT_ROWS, N_META), BF16)],
        compiler_params=_cp(("arbitrary",)),
        name="inproj",
    )(res, g, w, wvt)


def _store_row_tiles(ref, x):
    n = x.shape[0]
    for j in range(ROW_BLOCKS):
        ref[pl.ds(j, n, stride=ROW_BLOCKS), :] = x[:, LANES * j:LANES * (j + 1)]


def _load_row_tiles(ref, n):
    return jnp.concatenate(
        [ref[pl.ds(j, n, stride=ROW_BLOCKS), :] for j in range(ROW_BLOCKS)], axis=1)


def _outproj_kernel(res_ref, s_ref, d_ref, f_ref, w_ref, o_ref, *tiled_ref):
    mixed = jnp.concatenate([s_ref[...], d_ref[...], f_ref[...]], axis=1)
    out = res_ref[...] + jnp.dot(mixed, w_ref[...], preferred_element_type=F32)
    o_ref[...] = out
    if tiled_ref:
        _store_row_tiles(tiled_ref[0], out)


def _outproj(res, ssm, diff, fox, w, with_row_tiles):
    t = res.shape[0]
    out_specs = [pl.BlockSpec((TOK_TILE, D_MODEL), lambda i: (i, 0))]
    out_shape = [jax.ShapeDtypeStruct((t, D_MODEL), F32)]
    if with_row_tiles:
        out_specs.append(pl.BlockSpec((TOK_TILE * ROW_BLOCKS, LANES), lambda i: (i, 0)))
        out_shape.append(jax.ShapeDtypeStruct((t * ROW_BLOCKS, LANES), F32))
    return pl.pallas_call(
        _outproj_kernel,
        grid=(t // TOK_TILE,),
        in_specs=[pl.BlockSpec((TOK_TILE, D_MODEL), lambda i: (i, 0)),
                  pl.BlockSpec((TOK_TILE, 256), lambda i: (i, 0)),
                  pl.BlockSpec((TOK_TILE, 512), lambda i: (i, 0)),
                  pl.BlockSpec((TOK_TILE, 256), lambda i: (i, 0)),
                  pl.BlockSpec((D_MODEL, D_MODEL), lambda i: (0, 0))],
        out_specs=out_specs,
        out_shape=out_shape,
        compiler_params=_cp(("parallel",)),
        name="outproj",
    )(res, ssm, diff, fox, w)


def _dense_ffn_kernel(res_ref, g_ref, w1_ref, w3_ref, w2_ref, o_ref):
    x = res_ref[...]
    h = _rms(x, g_ref[...]).astype(BF16)
    a = jnp.dot(h, w1_ref[...], preferred_element_type=F32)
    b = jnp.dot(h, w3_ref[...], preferred_element_type=F32)
    hh = (a * jax.nn.sigmoid(a) * b).astype(BF16)
    o_ref[...] = x + jnp.dot(hh, w2_ref[...], preferred_element_type=F32)


def _dense_ffn(res, g, w1, w3, w2):
    t = res.shape[0]
    dff = w1.shape[1]
    once = pl.Buffered(1)
    return pl.pallas_call(
        _dense_ffn_kernel,
        grid=(t // FFN_TILE,),
        in_specs=[pl.BlockSpec((FFN_TILE, D_MODEL), lambda i: (i, 0)),
                  pl.BlockSpec((1, D_MODEL), lambda i: (0, 0)),
                  pl.BlockSpec((D_MODEL, dff), lambda i: (0, 0), pipeline_mode=once),
                  pl.BlockSpec((D_MODEL, dff), lambda i: (0, 0), pipeline_mode=once),
                  pl.BlockSpec((dff, D_MODEL), lambda i: (0, 0), pipeline_mode=once)],
        out_specs=pl.BlockSpec((FFN_TILE, D_MODEL), lambda i: (i, 0)),
        out_shape=jax.ShapeDtypeStruct((t, D_MODEL), F32),
        compiler_params=_cp(("parallel",)),
        name="dense_ffn",
    )(res, g, w1, w3, w2)


def _gelu_tanh(x):
    c = math.sqrt(2.0 / math.pi)
    return 0.5 * x * (1.0 + jnp.tanh(c * (x + 0.044715 * (x * x * x))))


def _s5_chunk(lc, get_u, state_ref, ut_ref, bu_ref, ot_ref,
              a_ref, bd_ref, cd_ref, dskip_ref, wglu_ref):
    for b in range(NB):
        ub = get_u(b)
        for s in range(2):
            ut_ref[s, pl.ds(b, lc, stride=NB), :] = ub[:, LANES * s:LANES * (s + 1)]
    u_tm = jnp.concatenate([ut_ref[0], ut_ref[1]], axis=1)
    bu_ref[...] = jnp.dot(u_tm.astype(BF16), bd_ref[...], preferred_element_type=F32)

    half = SSM_COLS // 2
    a_re = a_ref[:, :half]
    a_im = a_ref[:, half:]

    def step(t, x):
        x_re, x_im = x
        r = pl.multiple_of(t * NB, NB)
        cur = bu_ref[pl.ds(r, NB), :]
        n_re = a_re * x_re - a_im * x_im + cur[:, :half]
        n_im = a_re * x_im + a_im * x_re + cur[:, half:]
        bu_ref[pl.ds(r, NB), :] = jnp.concatenate([n_re, n_im], axis=1)
        return n_re, n_im

    x_re, x_im = lax.fori_loop(0, lc, step,
                               (state_ref[:, :half], state_ref[:, half:]), unroll=4)
    state_ref[...] = jnp.concatenate([x_re, x_im], axis=1)

    y = jnp.dot(bu_ref[...].astype(BF16), cd_ref[...], preferred_element_type=F32)
    y = _gelu_tanh(y + dskip_ref[...] * u_tm)
    g = jnp.dot(y.astype(BF16), wglu_ref[...], preferred_element_type=F32)
    o = g[:, :SSM_WIDTH] * jax.nn.sigmoid(g[:, SSM_WIDTH:])
    ot_ref[0] = o[:, :LANES]
    ot_ref[1] = o[:, LANES:]


def _s5_read_out(ot_ref, b, lc):
    return jnp.concatenate(
        [ot_ref[s, pl.ds(b, lc, stride=NB), :] for s in range(2)], axis=1)


def _s5_meta_kernel(u_ref, a_ref, bd_ref, cd_ref, dskip_ref, wglu_ref,
                    o_ref, state_out_ref, state_ref, ut_ref, bu_ref, ot_ref):
    state_ref[...] = jnp.zeros_like(state_ref)
    _s5_chunk(N_META, lambda b: u_ref[b * N_META:(b + 1) * N_META, :],
              state_ref, ut_ref, bu_ref, ot_ref, a_ref, bd_ref, cd_ref, dskip_ref, wglu_ref)
    for b in range(NB):
        o_ref[b * N_META:(b + 1) * N_META, :] = _s5_read_out(ot_ref, b, N_META).astype(BF16)
    state_out_ref[...] = state_ref[...]


def _s5_real_kernel(*refs):
    u_refs = refs[:NB]
    (state_in_ref, a_ref, bd_ref, cd_ref, dskip_ref, wglu_ref, _flat_ref,
     o_ref, state_ref, ut_ref, bu_ref, ot_ref) = refs[NB:]
    c = pl.program_id(0)
    b = pl.program_id(1)

    @pl.when((c == 0) & (b == 0))
    def _():
        state_ref[...] = state_in_ref[...]

    @pl.when(b == 0)
    def _():
        _s5_chunk(S5_CHUNK, lambda bb: u_refs[bb][...],
                  state_ref, ut_ref, bu_ref, ot_ref, a_ref, bd_ref, cd_ref, dskip_ref, wglu_ref)

    o_ref[...] = _s5_read_out(ot_ref, b, S5_CHUNK).astype(BF16)


def _s5(zu, seq, a, bd, cd, dskip, wglu):
    t = zu.shape[0]
    tr = NB * seq
    nmeta_rows = NB * N_META
    const = lambda *_: (0, 0)
    par_specs = [pl.BlockSpec((NB, SSM_COLS), const),
                 pl.BlockSpec((SSM_WIDTH, SSM_COLS), const),
                 pl.BlockSpec((SSM_COLS, SSM_WIDTH), const),
                 pl.BlockSpec((1, SSM_WIDTH), const),
                 pl.BlockSpec((SSM_WIDTH, 2 * SSM_WIDTH), const)]

    def scratch(lc):
        return [pltpu.VMEM((NB, SSM_COLS), F32),
                pltpu.VMEM((2, lc * NB, LANES), F32),
                pltpu.VMEM((lc * NB, SSM_COLS), F32),
                pltpu.VMEM((2, lc * NB, LANES), F32)]

    meta_blk = tr // nmeta_rows
    flat, state = pl.pallas_call(
        _s5_meta_kernel,
        grid=(1,),
        in_specs=[pl.BlockSpec((nmeta_rows, SSM_WIDTH), lambda i: (meta_blk, 0))] + par_specs,
        out_specs=[pl.BlockSpec((nmeta_rows, SSM_WIDTH), lambda i: (meta_blk, 0)),
                   pl.BlockSpec((NB, SSM_COLS), const)],
        out_shape=[jax.ShapeDtypeStruct((t, SSM_WIDTH), BF16),
                   jax.ShapeDtypeStruct((NB, SSM_COLS), F32)],
        scratch_shapes=scratch(N_META),
        compiler_params=_cp(("arbitrary",)),
        name="s5_meta",
    )(zu, a, bd, cd, dskip, wglu)

    nc = seq // S5_CHUNK
    u_specs = [pl.BlockSpec((S5_CHUNK, SSM_WIDTH), lambda c, b, bb=bb: (bb * nc + c, 0))
               for bb in range(NB)]
    n_in = NB + 1 + len(par_specs)
    return pl.pallas_call(
        _s5_real_kernel,
        grid=(nc, NB),
        in_specs=u_specs + [pl.BlockSpec((NB, SSM_COLS), const)] + par_specs
        + [pl.BlockSpec(memory_space=pl.ANY)],
        out_specs=pl.BlockSpec((S5_CHUNK, SSM_WIDTH), lambda c, b: (b * nc + c, 0)),
        out_shape=jax.ShapeDtypeStruct((t, SSM_WIDTH), BF16),
        scratch_shapes=scratch(S5_CHUNK),
        input_output_aliases={n_in: 0},
        compiler_params=_cp(("arbitrary", "arbitrary")),
        name="s5_real",
    )(*([zu] * NB), state, a, bd, cd, dskip, wglu, flat)


def _s5_params(lam_re, lam_im, log_dt, b_re, b_im, c_re, c_im, d_skip):
    dt = jnp.exp(log_dt)[:, None]
    mag = jnp.exp(lam_re * dt)
    ab_re = mag * jnp.cos(lam_im * dt)
    ab_im = mag * jnp.sin(lam_im * dt)
    den = lam_re * lam_re + lam_im * lam_im
    nr = ab_re - 1.0
    ni = ab_im
    coef_re = ((nr * lam_re + ni * lam_im) / den)[..., None]
    coef_im = ((ni * lam_re - nr * lam_im) / den)[..., None]
    bb_re = coef_re * b_re - coef_im * b_im
    bb_im = coef_re * b_im + coef_im * b_re
    eye = jnp.eye(SSM_GROUPS, dtype=F32)
    half = SSM_COLS // 2
    bd = jnp.concatenate(
        [jnp.einsum('gnc,gh->gchn', m, eye).reshape(SSM_WIDTH, half) for m in (bb_re, bb_im)],
        axis=1).astype(BF16)
    cd = jnp.concatenate(
        [jnp.einsum('gcn,gh->gnhc', m, eye).reshape(half, SSM_WIDTH) for m in (c_re, -c_im)],
        axis=0).astype(BF16)
    a = jnp.concatenate([ab_re.reshape(1, half), ab_im.reshape(1, half)], axis=1)
    a = jnp.broadcast_to(a, (NB, SSM_COLS))
    return a, bd, cd, d_skip.reshape(1, SSM_WIDTH)


def _nt_dot(a, b):
    return lax.dot_general(a, b, (((1,), (1,)), ((), ())), preferred_element_type=F32)


def _osm_init(s, vt):
    m = jnp.max(s, axis=1, keepdims=True)
    p = jnp.exp2(s - m)
    l = jnp.sum(p, axis=1, keepdims=True)
    acc = _nt_dot(p.astype(BF16), vt)
    return m, l, acc


def _causal_sweep(qts, k_ref, k_cols, vt_ref, vt_rows, kms, vmts, qi,
                  m_ref, l_ref, acc_ref, al_ref, p_ref):
    n = len(qts)
    meta_scores = [jnp.dot(kms[i], qts[i], preferred_element_type=F32) for i in range(n)]
    meta_p = []
    for i in range(n):
        s = meta_scores[i]
        m = jnp.max(s, axis=0, keepdims=True)
        p = jnp.exp2(s - m)
        m_ref[i][...] = m
        l_ref[i][...] = jnp.sum(p, axis=0, keepdims=True)
        meta_p.append(p.astype(BF16))
        al_ref[i][...] = jnp.ones_like(al_ref[i])
        p_ref[i][...] = jnp.zeros_like(p_ref[i])
    for i in range(n):
        acc_ref[i][...] = jnp.dot(vmts[i], meta_p[i], preferred_element_type=F32)

    sub = ATT_TK // LANES

    def pending_pv(jp):
        for i in range(n):
            vtb = jnp.concatenate([vt_ref[jp * sub + c, vt_rows[i], :] for c in range(sub)],
                                  axis=1)
            acc_ref[i][...] = al_ref[i][...] * acc_ref[i][...] + jnp.dot(
                vtb, p_ref[i][...], preferred_element_type=F32)

    def chunk(j, masked):
        off = pl.multiple_of(j * ATT_TK, ATT_TK)
        if masked:
            krow = lax.broadcasted_iota(jnp.int32, (ATT_TK, ATT_TQ), 0)
            qcol = lax.broadcasted_iota(jnp.int32, (ATT_TK, ATT_TQ), 1)
            visible = krow <= qcol
        scores = []
        for i in range(n):
            kb = k_ref[pl.ds(off, ATT_TK), k_cols[i]]
            s = jnp.dot(kb, qts[i], preferred_element_type=F32)
            scores.append(jnp.where(visible, s, MASK_VALUE) if masked else s)
        pending_pv(jnp.maximum(j - 1, 0))
        for i in range(n):
            m_old = m_ref[i][...]
            m_new = jnp.maximum(m_old, jnp.max(scores[i], axis=0, keepdims=True))
            alpha = jnp.exp2(m_old - m_new)
            p = jnp.exp2(scores[i] - m_new)
            l_ref[i][...] = alpha * l_ref[i][...] + jnp.sum(p, axis=0, keepdims=True)
            m_ref[i][...] = m_new
            al_ref[i][...] = alpha
            p_ref[i][...] = p.astype(BF16)

    def body(j, carry):
        chunk(j, False)
        return carry

    lax.fori_loop(0, qi, body, 0)
    chunk(qi, True)
    pending_pv(qi)


def _meta_attend(q, km, vmt):
    n = km.shape[0]
    row = lax.broadcasted_iota(jnp.int32, (n, n), 0)
    col = lax.broadcasted_iota(jnp.int32, (n, n), 1)
    s = jnp.where(col <= row, _nt_dot(q, km), MASK_VALUE)
    _, l, acc = _osm_init(s, vmt)
    return l, acc


def _diff_split(q):
    lane = lax.broadcasted_iota(jnp.int32, q.shape, 1)
    q = (q.astype(F32) * QK_SCALE).astype(BF16)
    zero = jnp.zeros_like(q)
    return jnp.where(lane < 64, q, zero), jnp.where(lane >= 64, q, zero)


def _diff_finish(parts, lam, lam_init, g):
    (l1, acc1), (l2, acc2) = parts
    o = acc1 / l1 - lam * (acc2 / l2)
    return (_rms(o, g) * (1.0 - lam_init)).astype(BF16)


def _transpose_bf16(x):
    return x.astype(F32).T.astype(BF16)


def _map_scratch(n_maps, dv):
    return ([pltpu.VMEM((1, ATT_TQ), F32)] * (2 * n_maps)
            + [pltpu.VMEM((dv, ATT_TQ), F32)] * n_maps
            + [pltpu.VMEM((1, ATT_TQ), F32)] * n_maps
            + [pltpu.VMEM((ATT_TK, ATT_TQ), BF16)] * n_maps)


def _split_map_scratch(scratch, n_maps):
    return [scratch[k * n_maps:(k + 1) * n_maps] for k in range(5)]


def _diff_kernel(lam_init, lam_ref, q_ref, k_ref, vt_ref, km_ref, vmt_ref, g_ref, o_ref,
                 *scratch):
    m_ref, l_ref, acc_ref, al_ref, p_ref = _split_map_scratch(scratch, 2 * DIFF_HEADS)
    qts, k_cols, vt_rows, kms, vmts = [], [], [], [], []
    row = lax.broadcasted_iota(jnp.int32, (LANES, ATT_TQ), 0)
    for h in range(DIFF_HEADS):
        sl = slice(LANES * h, LANES * (h + 1))
        km = km_ref[:, sl]
        vmt = vmt_ref[sl, :]
        qt = (q_ref[:, sl].astype(F32) * QK_SCALE).T.astype(BF16)
        zero = jnp.zeros_like(qt)
        for qm in (jnp.where(row < 64, qt, zero), jnp.where(row >= 64, qt, zero)):
            qts.append(qm)
            k_cols.append(sl)
            vt_rows.append(sl)
            kms.append(km)
            vmts.append(vmt)
    _causal_sweep(qts, k_ref, k_cols, vt_ref, vt_rows, kms, vmts, pl.program_id(1),
                  m_ref, l_ref, acc_ref, al_ref, p_ref)
    lam = lam_ref[0]
    for h in range(DIFF_HEADS):
        o = (acc_ref[2 * h][...] / l_ref[2 * h][...]
             - lam * (acc_ref[2 * h + 1][...] / l_ref[2 * h + 1][...]))
        y = o * lax.rsqrt(jnp.mean(o * o, axis=0, keepdims=True) + EPS)
        o_ref[:, LANES * h:LANES * (h + 1)] = (
            y.T * g_ref[...] * (1.0 - lam_init)).astype(BF16)


def _diff_meta_kernel(lam_init, lam_ref, q_ref, km_ref, vmt_ref, g_ref, _flat_ref, o_ref):
    for h in range(DIFF_HEADS):
        sl = slice(LANES * h, LANES * (h + 1))
        q1, q2 = _diff_split(q_ref[:, sl])
        km = km_ref[:, sl]
        vmt = vmt_ref[sl, :]
        parts = [_meta_attend(q1, km, vmt), _meta_attend(q2, km, vmt)]
        o_ref[:, sl] = _diff_finish(parts, lam_ref[0], lam_init, g_ref[...])


def _diff_attention(za, vt, vtm, seq, lam, lam_init, g):
    t = za.shape[0]
    nq = seq // ATT_TQ
    mrow = NB * seq // N_META
    smem = pl.BlockSpec(memory_space=pltpu.SMEM)
    lam = lam.reshape(1).astype(F32)
    g = g.reshape(1, LANES)
    gspec = pl.BlockSpec((1, LANES), lambda *_: (0, 0))
    width = DIFF_HEADS * LANES
    n_maps = 2 * DIFF_HEADS
    flat = pl.pallas_call(
        functools.partial(_diff_kernel, lam_init),
        grid=(NB, nq),
        in_specs=[smem,
                  pl.BlockSpec((ATT_TQ, width), lambda b, i: (b * nq + i, 0)),
                  pl.BlockSpec((seq, width), lambda b, i: (b, 1)),
                  pl.BlockSpec((seq // LANES, width, LANES), lambda b, i: (b, 0, 0)),
                  pl.BlockSpec((N_META, width), lambda b, i: (mrow + b, 1)),
                  pl.BlockSpec((None, width, N_META), lambda b, i: (b, 0, 0)),
                  gspec],
        out_specs=pl.BlockSpec((ATT_TQ, width), lambda b, i: (b * nq + i, 0)),
        out_shape=jax.ShapeDtypeStruct((t, width), BF16),
        scratch_shapes=_map_scratch(n_maps, LANES),
        compiler_params=_cp(("parallel", "arbitrary")),
        name="diff_attn",
    )(lam, za, za, vt, za, vtm, g)
    return pl.pallas_call(
        functools.partial(_diff_meta_kernel, lam_init),
        grid=(NB,),
        in_specs=[smem,
                  pl.BlockSpec((N_META, width), lambda b: (mrow + b, 0)),
                  pl.BlockSpec((N_META, width), lambda b: (mrow + b, 1)),
                  pl.BlockSpec((None, width, N_META), lambda b: (b, 0, 0)),
                  gspec,
                  pl.BlockSpec(memory_space=pl.ANY)],
        out_specs=pl.BlockSpec((N_META, width), lambda b: (mrow + b, 0)),
        out_shape=jax.ShapeDtypeStruct((t, width), BF16),
        input_output_aliases={5: 0},
        compiler_params=_cp(("parallel",)),
        name="diff_attn_meta",
    )(lam, za, za, vtm, g, flat)


def _split3(c):
    hi = c.astype(BF16).astype(F32)
    r1 = c - hi
    mid = r1.astype(BF16).astype(F32)
    lo = (r1 - mid).astype(BF16).astype(F32)
    return hi, mid, lo


def _cumsum_rows(tri, lf):
    parts = jnp.concatenate(_split3(lf), axis=1).astype(BF16)
    r = jnp.dot(tri, parts, preferred_element_type=F32)
    return r[:, :LANES] + r[:, LANES:2 * LANES] + r[:, 2 * LANES:]


def _log_sigmoid(x):
    return jnp.minimum(x, 0.0) - jnp.log1p(jnp.exp(-jnp.abs(x)))


def _fox_augment(fq, fk, cum, qa_ref, ka_ref):
    n = fq.shape[0]
    lane = lax.broadcasted_iota(jnp.int32, (n, LANES), 1)
    for h in range(FOX_HEADS):
        pair = slice(LANES * (h // 2), LANES * (h // 2) + LANES)
        own = (lane // 64) == (h % 2)
        e0 = 64 * (1 - h % 2)
        hi, mid, lo = _split3(jnp.broadcast_to(cum[:, h:h + 1] * LOG2E, (n, LANES)))
        ones = (lane >= e0 + 3) & (lane < e0 + 6)
        q_extra = jnp.where(lane == e0, hi, jnp.where(lane == e0 + 1, mid, jnp.where(
            lane == e0 + 2, lo, jnp.where(ones, 1.0, 0.0))))
        ones = (lane >= e0) & (lane < e0 + 3)
        k_extra = jnp.where(lane == e0 + 3, -hi, jnp.where(lane == e0 + 4, -mid, jnp.where(
            lane == e0 + 5, -lo, jnp.where(ones, 1.0, 0.0))))
        q = fq[:, pair].astype(F32) * QK_SCALE
        k = fk[:, pair].astype(F32)
        qa_ref[:, LANES * h:LANES * (h + 1)] = jnp.where(own, q, q_extra).astype(BF16)
        ka_ref[:, LANES * h:LANES * (h + 1)] = jnp.where(own, k, k_extra).astype(BF16)


def _fox_prep_meta_kernel(zg_ref, fq_ref, fk_ref, fb_ref, qa_ref, ka_ref, carry_ref):
    n = NB * N_META
    lf = _log_sigmoid(zg_ref[...] + fb_ref[...])
    row = lax.broadcasted_iota(jnp.int32, (n, n), 0)
    col = lax.broadcasted_iota(jnp.int32, (n, n), 1)
    tri = ((col <= row) & (col // N_META == row // N_META)).astype(BF16)
    cum = _cumsum_rows(tri, lf)
    brow = lax.broadcasted_iota(jnp.int32, (NB, n), 0)
    bcol = lax.broadcasted_iota(jnp.int32, (NB, n), 1)
    carry_ref[...] = _cumsum_rows((bcol // N_META == brow).astype(BF16), lf)
    _fox_augment(fq_ref[...], fk_ref[...], cum, qa_ref, ka_ref)


def _fox_prep_kernel(zg_ref, fq_ref, fk_ref, fb_ref, carry_in_ref, qa_ref, ka_ref, carry_ref):
    b = pl.program_id(0)
    n = zg_ref.shape[0]

    @pl.when(pl.program_id(1) == 0)
    def _():
        carry_ref[...] = carry_in_ref[pl.ds(b, 1), :]

    lf = _log_sigmoid(zg_ref[...] + fb_ref[...])
    row = lax.broadcasted_iota(jnp.int32, (n, n), 0)
    col = lax.broadcasted_iota(jnp.int32, (n, n), 1)
    cum = _cumsum_rows((col <= row).astype(BF16), lf) + carry_ref[...]
    carry_ref[...] = cum[n - 1:n, :]
    _fox_augment(fq_ref[...], fk_ref[...], cum, qa_ref, ka_ref)


FOX_PREP_TILE = 1024


def _fox_prep(zg, za, seq, fb):
    tr = NB * seq
    nm = NB * N_META
    mblk = tr // nm
    fb = jnp.pad(fb.astype(F32), (0, LANES - FOX_HEADS)).reshape(1, LANES)
    fbspec = pl.BlockSpec((1, LANES), lambda *_: (0, 0))
    aug = FOX_HEADS * LANES
    qa_m, ka_m, carry = pl.pallas_call(
        _fox_prep_meta_kernel,
        grid=(1,),
        in_specs=[pl.BlockSpec((nm, LANES), lambda i: (mblk, 0)),
                  pl.BlockSpec((nm, 256), lambda i: (mblk, 4)),
                  pl.BlockSpec((nm, 256), lambda i: (mblk, 5)),
                  fbspec],
        out_specs=[pl.BlockSpec((nm, aug), lambda i: (0, 0)),
                   pl.BlockSpec((nm, aug), lambda i: (0, 0)),
                   pl.BlockSpec((NB, LANES), lambda i: (0, 0))],
        out_shape=[jax.ShapeDtypeStruct((nm, aug), BF16),
                   jax.ShapeDtypeStruct((nm, aug), BF16),
                   jax.ShapeDtypeStruct((NB, LANES), F32)],
        compiler_params=_cp(("arbitrary",)),
        name="fox_prep_meta",
    )(zg, za, za, fb)
    nc = seq // FOX_PREP_TILE
    qa, ka = pl.pallas_call(
        _fox_prep_kernel,
        grid=(NB, nc),
        in_specs=[pl.BlockSpec((FOX_PREP_TILE, LANES), lambda b, c: (b * nc + c, 0)),
                  pl.BlockSpec((FOX_PREP_TILE, 256), lambda b, c: (b * nc + c, 4)),
                  pl.BlockSpec((FOX_PREP_TILE, 256), lambda b, c: (b * nc + c, 5)),
                  fbspec,
                  pl.BlockSpec((NB, LANES), lambda b, c: (0, 0))],
        out_specs=[pl.BlockSpec((FOX_PREP_TILE, aug), lambda b, c: (b * nc + c, 0)),
                   pl.BlockSpec((FOX_PREP_TILE, aug), lambda b, c: (b * nc + c, 0))],
        out_shape=[jax.ShapeDtypeStruct((tr, aug), BF16),
                   jax.ShapeDtypeStruct((tr, aug), BF16)],
        scratch_shapes=[pltpu.VMEM((1, LANES), F32)],
        compiler_params=_cp(("parallel", "arbitrary")),
        name="fox_prep",
    )(zg, za, za, fb, carry)
    return qa, ka, qa_m, ka_m


def _fox_finish(parts):
    (l0, acc0), (l1, acc1) = parts
    lane = lax.broadcasted_iota(jnp.int32, acc0.shape, 1)
    return jnp.where(lane < 64, acc0 / l0, acc1 / l1).astype(BF16)


def _fox_kernel(q_ref, k_ref, vt_ref, km_ref, vmt_ref, o_ref, *scratch):
    m_ref, l_ref, acc_ref, al_ref, p_ref = _split_map_scratch(scratch, FOX_HEADS)
    hd = 64
    vmt = vmt_ref[...]
    qts, k_cols, vt_rows, kms, vmts = [], [], [], [], []
    for h in range(FOX_HEADS):
        sl = slice(LANES * h, LANES * (h + 1))
        qts.append(_transpose_bf16(q_ref[:, sl]))
        k_cols.append(sl)
        vt_rows.append(slice(hd * h, hd * (h + 1)))
        kms.append(km_ref[:, sl])
        vmts.append(vmt[hd * h:hd * (h + 1), :])
    _causal_sweep(qts, k_ref, k_cols, vt_ref, vt_rows, kms, vmts, pl.program_id(1),
                  m_ref, l_ref, acc_ref, al_ref, p_ref)
    for p in range(FOX_HEADS // 2):
        o = jnp.concatenate([acc_ref[2 * p][...] / l_ref[2 * p][...],
                             acc_ref[2 * p + 1][...] / l_ref[2 * p + 1][...]],
                            axis=0)
        o_ref[:, LANES * p:LANES * (p + 1)] = o.T.astype(BF16)


def _fox_meta_kernel(q_ref, km_ref, vmt_ref, _flat_ref, o_ref):
    for p in range(FOX_HEADS // 2):
        vmt = vmt_ref[LANES * p:LANES * (p + 1), :]
        parts = []
        for h in (2 * p, 2 * p + 1):
            sl = slice(LANES * h, LANES * (h + 1))
            parts.append(_meta_attend(q_ref[:, sl], km_ref[:, sl], vmt))
        o_ref[:, LANES * p:LANES * (p + 1)] = _fox_finish(parts)


def _fox_attention(t, vt, vtm, seq, qa, ka, qa_m, ka_m):
    nq = seq // ATT_TQ
    mrow = NB * seq // N_META
    aug = FOX_HEADS * LANES
    flat = pl.pallas_call(
        _fox_kernel,
        grid=(NB, nq),
        in_specs=[pl.BlockSpec((ATT_TQ, aug), lambda b, i: (b * nq + i, 0)),
                  pl.BlockSpec((seq, aug), lambda b, i: (b, 0)),
                  pl.BlockSpec((seq // LANES, 256, LANES), lambda b, i: (b, 2, 0)),
                  pl.BlockSpec((N_META, aug), lambda b, i: (b, 0)),
                  pl.BlockSpec((None, 256, N_META), lambda b, i: (b, 2, 0))],
        out_specs=pl.BlockSpec((ATT_TQ, 256), lambda b, i: (b * nq + i, 0)),
        out_shape=jax.ShapeDtypeStruct((t, 256), BF16),
        scratch_shapes=_map_scratch(FOX_HEADS, 64),
        compiler_params=_cp(("parallel", "arbitrary")),
        name="fox_attn",
    )(qa, ka, vt, ka_m, vtm)
    return pl.pallas_call(
        _fox_meta_kernel,
        grid=(NB,),
        in_specs=[pl.BlockSpec((N_META, aug), lambda b: (b, 0)),
                  pl.BlockSpec((N_META, aug), lambda b: (b, 0)),
                  pl.BlockSpec((None, 256, N_META), lambda b: (b, 2, 0)),
                  pl.BlockSpec(memory_space=pl.ANY)],
        out_specs=pl.BlockSpec((N_META, 256), lambda b: (mrow + b, 0)),
        out_shape=jax.ShapeDtypeStruct((t, 256), BF16),
        input_output_aliases={3: 0},
        compiler_params=_cp(("parallel",)),
        name="fox_attn_meta",
    )(qa_m, ka_m, vtm, flat)


def _router_kernel(res_ref, g_ref, wr_ref, idx_ref, gate_ref):
    h = _rms(res_ref[...], g_ref[...])
    w = wr_ref[...]
    hi = h.astype(BF16)
    lo = (h - hi.astype(F32)).astype(BF16)
    whi = w.astype(BF16)
    wlo = (w - whi.astype(F32)).astype(BF16)
    logits = (jnp.dot(hi, whi, preferred_element_type=F32)
              + jnp.dot(lo, whi, preferred_element_type=F32)
              + jnp.dot(hi, wlo, preferred_element_type=F32))
    lane = lax.broadcasted_iota(jnp.int32, logits.shape, 1)
    logits = jnp.where(lane < N_EXPERTS, logits, -jnp.inf)
    m1 = jnp.max(logits, axis=1, keepdims=True)
    i1 = jnp.min(jnp.where(logits == m1, lane, LANES), axis=1, keepdims=True)
    rest = jnp.where(lane == i1, -jnp.inf, logits)
    m2 = jnp.max(rest, axis=1, keepdims=True)
    i2 = jnp.min(jnp.where(rest == m2, lane, LANES), axis=1, keepdims=True)
    e = jnp.exp(m2 - m1)
    g1 = 1.0 / (1.0 + e)
    g2 = e / (1.0 + e)
    idx_ref[...] = jnp.where(lane == 0, i1, jnp.where(lane == 1, i2, 0))
    gate_ref[...] = jnp.where(lane == 0, g1, jnp.where(lane == 1, g2, 0.0))


def _router(res, g, wr):
    t = res.shape[0]
    return pl.pallas_call(
        _router_kernel,
        grid=(t // TOK_TILE,),
        in_specs=[pl.BlockSpec((TOK_TILE, D_MODEL), lambda i: (i, 0)),
                  pl.BlockSpec((1, D_MODEL), lambda i: (0, 0)),
                  pl.BlockSpec((D_MODEL, LANES), lambda i: (0, 0))],
        out_specs=[pl.BlockSpec((TOK_TILE, LANES), lambda i: (i, 0)),
                   pl.BlockSpec((TOK_TILE, LANES), lambda i: (i, 0))],
        out_shape=[jax.ShapeDtypeStruct((t, LANES), jnp.int32),
                   jax.ShapeDtypeStruct((t, LANES), F32)],
        compiler_params=_cp(("parallel",)),
        name="router",
    )(res, g, wr)


def _gather_copy(src_hbm, first, dst_ref, r, sem):
    first = pl.multiple_of(first, ROW_BLOCKS)
    dst = pl.multiple_of(r * ROW_BLOCKS, ROW_BLOCKS)
    return pltpu.make_async_copy(src_hbm.at[pl.ds(first, ROW_BLOCKS), :],
                                 dst_ref.at[pl.ds(dst, ROW_BLOCKS), :], sem)


def _expert_ffn_kernel(nf, te_ref, na_ref, nb_ref, src_ref, x_hbm, g_ref, w1_ref, w3_ref,
                       w2_ref, o_ref, xbuf_ref, h_ref, acc_ref, sem):
    r = pl.program_id(0)
    f = pl.program_id(1)
    n_act = na_ref[0]
    n_sub = MOE_TILE // MOE_SUB
    per_step = MOE_XROWS // nf
    last_tile = pl.num_programs(0) - 1

    def tile_copy(slot):
        return pltpu.make_async_copy(x_hbm.at[pl.ds(0, MOE_XROWS * ROW_BLOCKS), :],
                                     xbuf_ref.at[slot], sem.at[slot])

    def start_row(tile, slot, k):
        _gather_copy(x_hbm, src_ref[tile * MOE_XROWS + k], xbuf_ref.at[slot], k,
                     sem.at[slot]).start()

    @pl.when(r < n_act)
    def _():
        slot = lax.rem(r, MOE_XSLOTS)
        slot1 = lax.rem(r + 1, MOE_XSLOTS)
        slot2 = lax.rem(r + 2, MOE_XSLOTS)

        @pl.when((r == 0) & (f == 0))
        def _():
            def first(k, _):
                start_row(0, 0, k)
                start_row(jnp.minimum(1, last_tile), 1, k)
                return 0
            lax.fori_loop(0, MOE_XROWS, first, 0, unroll=8)

        @pl.when(f == 0)
        def _():
            tile_copy(slot).wait()
            x = _load_row_tiles(xbuf_ref.at[slot], MOE_TILE)
            h_ref[...] = _rms(x, g_ref[...]).astype(BF16)
            acc_ref[...] = jnp.zeros_like(acc_ref)

        ahead = jnp.minimum(r + 2, last_tile)

        def start_ahead():
            for k in range(per_step):
                start_row(ahead, slot2, f * per_step + k)

        def swiglu_rows(rows, w1, w3, w2):
            h = h_ref[rows, :]
            a = jnp.dot(h, w1, preferred_element_type=F32)
            b = jnp.dot(h, w3, preferred_element_type=F32)
            hh = (a * jax.nn.sigmoid(a) * b).astype(BF16)
            acc_ref[rows, :] += jnp.dot(hh, w2, preferred_element_type=F32)

        def weights():
            return (w1_ref[...].astype(BF16), w3_ref[...].astype(BF16),
                    w2_ref[...].astype(BF16))

        n_blocks = nb_ref[r]

        @pl.when(n_blocks == n_sub)
        def _():
            start_ahead()
            swiglu_rows(slice(None), *weights())

        @pl.when(n_blocks < n_sub)
        def _():
            start_ahead()
            w = weights()
            swiglu_rows(slice(0, MOE_SUB), *w)
            for sb in range(1, n_sub - 1):
                @pl.when(sb < n_blocks)
                def _():
                    swiglu_rows(slice(MOE_SUB * sb, MOE_SUB * (sb + 1)), *w)

        @pl.when(f == nf - 1)
        def _():
            _store_row_tiles(o_ref, acc_ref[...])

        @pl.when((f == nf - 1) & (r == n_act - 1))
        def _():
            tile_copy(slot1).wait()
            tile_copy(slot2).wait()


def _expert_ffn(tile_expert, n_active, tile_blocks, src, x, g, w1, w3, w2):
    dff = w1.shape[2]
    nt = src.shape[0] // MOE_XROWS
    nf = dff // MOE_FCHUNK
    assert MOE_XROWS % nf == 0 and MOE_XROWS >= MOE_TILE

    def row(r, f, te, na, *_):
        return jnp.minimum(r, na[0] - 1)

    def fch(r, f, te, na, *_):
        return jnp.where(r < na[0], f, nf - 1)

    return pl.pallas_call(
        functools.partial(_expert_ffn_kernel, nf),
        grid_spec=pltpu.PrefetchScalarGridSpec(
            num_scalar_prefetch=4,
            grid=(nt, nf),
            in_specs=[
                pl.BlockSpec(memory_space=pl.ANY),
                pl.BlockSpec((1, D_MODEL), lambda r, f, *_: (0, 0)),
                pl.BlockSpec((None, D_MODEL, MOE_FCHUNK),
                             lambda r, f, te, na, *_: (te[r], 0, fch(r, f, te, na))),
                pl.BlockSpec((None, D_MODEL, MOE_FCHUNK),
                             lambda r, f, te, na, *_: (te[r], 0, fch(r, f, te, na))),
                pl.BlockSpec((None, MOE_FCHUNK, D_MODEL),
                             lambda r, f, te, na, *_: (te[r], fch(r, f, te, na), 0))],
            out_specs=pl.BlockSpec((MOE_TILE * ROW_BLOCKS, LANES),
                                   lambda r, f, te, na, *_: (row(r, f, te, na), 0)),
            scratch_shapes=[pltpu.VMEM((MOE_XSLOTS, MOE_XROWS * ROW_BLOCKS, LANES), F32),
                            pltpu.VMEM((MOE_TILE, D_MODEL), BF16),
                            pltpu.VMEM((MOE_TILE, D_MODEL), F32),
                            pltpu.SemaphoreType.DMA((MOE_XSLOTS,))]),
        out_shape=jax.ShapeDtypeStruct((nt * MOE_TILE * ROW_BLOCKS, LANES), F32),
        compiler_params=_cp(("arbitrary", "arbitrary")),
        name="expert_ffn",
    )(tile_expert, n_active, tile_blocks, src, x, g, w1, w3, w2)


def _combine_kernel(tile, final, pos_ref, res_ref, gate_ref, gf_ref, y_hbm, o_ref, ybuf_ref, sem):
    i = pl.program_id(0)
    n = pl.num_programs(0)
    slot = lax.rem(i, 2)

    def issue_tile(step, s):
        base = step * tile

        def issue(r, _):
            for k in range(2):
                _gather_copy(y_hbm, pos_ref[2 * (base + r) + k], ybuf_ref.at[s, k], r,
                             sem.at[s]).start(priority=k)
            return 0

        lax.fori_loop(0, tile, issue, 0, unroll=8)

    @pl.when(i == 0)
    def _():
        issue_tile(0, 0)

    @pl.when(i + 1 < n)
    def _():
        issue_tile(i + 1, 1 - slot)

    pltpu.make_async_copy(ybuf_ref.at[slot], ybuf_ref.at[slot], sem.at[slot]).wait()
    gate = gate_ref[...]
    out = (res_ref[...] + gate[:, 0:1] * _load_row_tiles(ybuf_ref.at[slot, 0], tile)
           + gate[:, 1:2] * _load_row_tiles(ybuf_ref.at[slot, 1], tile))
    o_ref[...] = _rms(out, gf_ref[...]) if final else out


def _combine(pos, res, gates, y, final_g=None):
    final = final_g is not None
    t = res.shape[0]
    n_rows = t - NB * N_META if final else t
    tile = 256 if final else COMBINE_TILE
    if not final:
        final_g = jnp.ones((1, D_MODEL), F32)
    return pl.pallas_call(
        functools.partial(_combine_kernel, tile, final),
        grid_spec=pltpu.PrefetchScalarGridSpec(
            num_scalar_prefetch=1,
            grid=(n_rows // tile,),
            in_specs=[pl.BlockSpec((tile, D_MODEL), lambda i, pos: (i, 0)),
                      pl.BlockSpec((tile, LANES), lambda i, pos: (i, 0)),
                      pl.BlockSpec((1, D_MODEL), lambda i, pos: (0, 0)),
                      pl.BlockSpec(memory_space=pl.ANY)],
            out_specs=pl.BlockSpec((tile, D_MODEL), lambda i, pos: (i, 0)),
            scratch_shapes=[pltpu.VMEM((2, 2, tile * ROW_BLOCKS, LANES), F32),
                            pltpu.SemaphoreType.DMA((2,))]),
        out_shape=jax.ShapeDtypeStruct((n_rows, D_MODEL), F32),
        compiler_params=_cp(("arbitrary",)),
        name="moe_combine",
    )(pos, res, gates, final_g, y)


def _moe(res, res_tiles, g, wr, w1, w3, w2, final_g=None):
    t = res.shape[0]
    wr = jnp.pad(wr.astype(F32), ((0, 0), (0, LANES - N_EXPERTS)))
    idx, gates = _router(res, g, wr)
    e_flat = idx[:, :2].reshape(-1)
    onehot = (e_flat[:, None] == jnp.arange(N_EXPERTS, dtype=jnp.int32)[None, :]).astype(jnp.int32)
    csum = jnp.cumsum(onehot, axis=0)
    rank = jnp.take_along_axis(csum, e_flat[:, None], axis=1)[:, 0] - 1
    counts = csum[-1]
    tiles = (counts + MOE_TILE - 1) // MOE_TILE
    tile_end = jnp.cumsum(tiles)
    starts = (tile_end - tiles) * MOE_TILE
    pos = (starts[e_flat] + rank).astype(jnp.int32)
    n_tiles = (2 * t + N_EXPERTS * (MOE_TILE - 1)) // MOE_TILE
    spos = (pos // MOE_TILE) * MOE_XROWS + pos % MOE_TILE
    src = jnp.zeros((n_tiles * MOE_XROWS,), jnp.int32).at[spos].set(
        (jnp.arange(2 * t, dtype=jnp.int32) // 2) * ROW_BLOCKS, unique_indices=True)
    n_active = tile_end[-1:].astype(jnp.int32)
    tile_ids = jnp.minimum(jnp.arange(n_tiles, dtype=jnp.int32), n_active[0] - 1)
    tile_expert = jnp.sum(tile_ids[:, None] >= tile_end[None, :], axis=1).astype(jnp.int32)
    first_tile = (tile_end - tiles)[tile_expert]
    tile_rows = jnp.clip(counts[tile_expert] - (tile_ids - first_tile) * MOE_TILE, 1, MOE_TILE)
    tile_blocks = ((tile_rows + MOE_SUB - 1) // MOE_SUB).astype(jnp.int32)

    y = _expert_ffn(tile_expert, n_active, tile_blocks, src, res_tiles, g, w1, w3, w2)
    return _combine(pos * ROW_BLOCKS, res, gates, y, final_g)


def _final_norm_kernel(res_ref, g_ref, o_ref):
    o_ref[...] = _rms(res_ref[...], g_ref[...])


def _final_norm(res, g, n_rows):
    tile = 1024
    return pl.pallas_call(
        _final_norm_kernel,
        grid=(n_rows // tile,),
        in_specs=[pl.BlockSpec((tile, D_MODEL), lambda i: (i, 0)),
                  pl.BlockSpec((1, D_MODEL), lambda i: (0, 0))],
        out_specs=pl.BlockSpec((tile, D_MODEL), lambda i: (i, 0)),
        out_shape=jax.ShapeDtypeStruct((n_rows, D_MODEL), F32),
        compiler_params=_cp(("parallel",)),
        name="final_norm",
    )(res, g)


def kernel(x, meta_tokens, norm_mix_g, w_in, w_out, ssm_lambda_re, ssm_lambda_im, ssm_log_dt,
           ssm_b_re, ssm_b_im, ssm_c_re, ssm_c_im, ssm_d, ssm_w_glu, diff_lambda_q1,
           diff_lambda_k1, diff_lambda_q2, diff_lambda_k2, diff_subln_g, fox_forget_b,
           norm_ffn_g, dense_w1, dense_w3, dense_w2, moe_router, moe_w1, moe_w3, moe_w2,
           final_norm_g):
    bsz, seq, d = x.shape
    assert bsz == NB and d == D_MODEL and seq % ATT_TQ == 0
    depth = w_in.shape[0]
    tr = bsz * seq
    res = jnp.concatenate(
        [x.reshape(tr, d), jnp.tile(meta_tokens.astype(x.dtype), (bsz, 1))], axis=0)
    row = lambda v: v.reshape(1, -1).astype(F32)

    for l in range(depth):
        wl = w_in[l]
        w = jnp.concatenate([wl[:, :1280], wl[:, 1792:2304],
                             jnp.pad(wl[:, 2560:], ((0, 0), (0, LANES - FOX_HEADS)))], axis=1)
        wvt = jnp.concatenate([wl[:, 1280:1792], wl[:, 2304:2560]], axis=1).T
        zu, za, zg, vt, vtm = _inproj(res, row(norm_mix_g[l]), w.astype(BF16),
                                      wvt.astype(BF16))

        a, bd, cd, dskip = _s5_params(ssm_lambda_re[l], ssm_lambda_im[l], ssm_log_dt[l],
                                      ssm_b_re[l], ssm_b_im[l], ssm_c_re[l], ssm_c_im[l],
                                      ssm_d[l])
        ssm_out = _s5(zu, seq, a, bd, cd, dskip, ssm_w_glu[l].astype(BF16))

        lam_init = 0.8 - 0.6 * math.exp(-0.3 * l)
        lam = (jnp.exp(jnp.sum(diff_lambda_q1[l] * diff_lambda_k1[l]))
               - jnp.exp(jnp.sum(diff_lambda_q2[l] * diff_lambda_k2[l])) + lam_init)
        diff_out = _diff_attention(za, vt, vtm, seq, lam, lam_init, diff_subln_g[l])

        qa, ka, qa_m, ka_m = _fox_prep(zg, za, seq, fox_forget_b[l])
        fox_out = _fox_attention(za.shape[0], vt, vtm, seq, qa, ka, qa_m, ka_m)

        is_moe = l % 2 == 1
        res, *res_tiles = _outproj(res, ssm_out, diff_out, fox_out, w_out[l].astype(BF16),
                                   with_row_tiles=is_moe)

        if not is_moe:
            res = _dense_ffn(res, row(norm_ffn_g[l]), dense_w1[l // 2].astype(BF16),
                             dense_w3[l // 2].astype(BF16), dense_w2[l // 2].astype(BF16))
        else:
            res = _moe(res, res_tiles[0], row(norm_ffn_g[l]), moe_router[l // 2], moe_w1[l // 2],
                       moe_w3[l // 2], moe_w2[l // 2],
                       final_g=row(final_norm_g) if l == depth - 1 else None)

    if depth % 2 == 1:
        res = _final_norm(res, row(final_norm_g), tr)
    return res.reshape(bsz, seq, d)
```

```python
import functools
import math

import jax
import jax.numpy as jnp
from jax import lax
from jax.experimental import pallas as pl
from jax.experimental.pallas import tpu as pltpu

F32 = jnp.float32
BF16 = jnp.bfloat16
EPS = 1e-6

D_MODEL = 1024
N_META = 16
NB = 8
SSM_WIDTH = 256
SSM_GROUPS = 16
SSM_GROUP = 16
SSM_STATE = 64
SSM_COLS = 2 * SSM_GROUPS * SSM_STATE
DIFF_HEADS = 4
FOX_HEADS = 4
ATT_COLS = 1536
IN_PAD = 1920
N_EXPERTS = 8
LANES = 128
MASK_VALUE = -1e30
LOG2E = math.log2(math.e)
QK_SCALE = 64 ** -0.5 * LOG2E

VT_ROWS = 768
TOK_TILE = 688
IN_TILE = 384
FFN_TILE = 384
ATT_TQ = 256
ATT_TK = 256
S5_CHUNK = 256
MOE_TILE = 1024
MOE_FCHUNK = 512
MOE_SUB = 256
MOE_XROWS = 1029
MOE_XSLOTS = 3
ROW_BLOCKS = D_MODEL // LANES
COMBINE_TILE = 384

VMEM_LIMIT = 56 * 1024 * 1024


def _cp(sem):
    return pltpu.CompilerParams(dimension_semantics=sem, vmem_limit_bytes=VMEM_LIMIT)


def _rms(x, g):
    return x * lax.rsqrt(jnp.mean(x * x, axis=-1, keepdims=True) + EPS) * g


def _inproj_embed_kernel(x_ref, xtail_ref, meta_ref, g_ref, w_ref, wvt_ref,
                         res_ref, zu_ref, za_ref, zg_ref, vt_ref, vtm_ref):
    tail = jnp.concatenate([xtail_ref[...]] + [meta_ref[...]] * NB, axis=0)
    is_last = pl.program_id(0) == pl.num_programs(0) - 1
    x = jnp.where(is_last, tail, x_ref[...])
    res_ref[...] = x
    _inproj_body(x, g_ref, w_ref, wvt_ref, zu_ref, za_ref, zg_ref, vt_ref, vtm_ref)


def _inproj_kernel(res_ref, g_ref, w_ref, wvt_ref, zu_ref, za_ref, zg_ref, vt_ref, vtm_ref):
    _inproj_body(res_ref[...], g_ref, w_ref, wvt_ref, zu_ref, za_ref, zg_ref, vt_ref, vtm_ref)


def _inproj_body(x, g_ref, w_ref, wvt_ref, zu_ref, za_ref, zg_ref, vt_ref, vtm_ref):
    h = _rms(x, g_ref[...]).astype(BF16)
    zu_ref[...] = jnp.dot(h, w_ref[:, 0:SSM_WIDTH], preferred_element_type=F32)
    for c in range(0, ATT_COLS, 256):
        za_ref[:, c:c + 256] = jnp.dot(
            h, w_ref[:, SSM_WIDTH + c:SSM_WIDTH + c + 256],
            preferred_element_type=F32).astype(BF16)
    zg_ref[...] = jnp.dot(h, w_ref[:, SSM_WIDTH + ATT_COLS:IN_PAD],
                          preferred_element_type=F32)
    vt = _nt_dot(wvt_ref[...], h).astype(BF16)
    for c in range(IN_TILE // LANES):
        vt_ref[c] = vt[:, LANES * c:LANES * (c + 1)]

    @pl.when(pl.program_id(0) == pl.num_programs(0) - 1)
    def _():
        first = IN_TILE - NB * N_META
        for b in range(NB):
            vtm_ref[b] = vt[:, first + N_META * b:first + N_META * (b + 1)]


def _inproj(res, g, w, wvt, embed=None):
    nblk = IN_TILE // LANES
    const = lambda i: (0, 0)
    par_specs = [pl.BlockSpec((1, D_MODEL), const),
                 pl.BlockSpec((D_MODEL, IN_PAD), const),
                 pl.BlockSpec((VT_ROWS, D_MODEL), const)]
    tile = pl.BlockSpec((IN_TILE, D_MODEL), lambda i: (i, 0))
    if embed is None:
        t = res.shape[0]
        body, operands, in_specs = _inproj_kernel, (res,), [tile]
        extra_specs, extra_shapes = [], []
    else:
        x_flat, meta = embed
        tr = x_flat.shape[0]
        t = tr + NB * N_META
        n_full = tr // IN_TILE
        tail = tr - n_full * IN_TILE
        assert tail > 0 and tail + NB * N_META == IN_TILE and tr % tail == 0
        body, operands = _inproj_embed_kernel, (x_flat, x_flat, meta)
        in_specs = [pl.BlockSpec((IN_TILE, D_MODEL), lambda i: (jnp.minimum(i, n_full - 1), 0)),
                    pl.BlockSpec((tail, D_MODEL), lambda i: (tr // tail - 1, 0)),
                    pl.BlockSpec((N_META, D_MODEL), const)]
        extra_specs, extra_shapes = [tile], [jax.ShapeDtypeStruct((t, D_MODEL), F32)]
    assert t % IN_TILE == 0 and NB * N_META <= IN_TILE
    return pl.pallas_call(
        body,
        grid=(t // IN_TILE,),
        in_specs=in_specs + par_specs,
        out_specs=extra_specs + [
            pl.BlockSpec((IN_TILE, SSM_WIDTH), lambda i: (i, 0)),
            pl.BlockSpec((IN_TILE, ATT_COLS), lambda i: (i, 0)),
            pl.BlockSpec((IN_TILE, LANES), lambda i: (i, 0)),
            pl.BlockSpec((nblk, VT_ROWS, LANES), lambda i: (i, 0, 0)),
            pl.BlockSpec((NB, VT_ROWS, N_META), lambda i: (0, 0, 0))],
        out_shape=extra_shapes + [
            jax.ShapeDtypeStruct((t, SSM_WIDTH), F32),
            jax.ShapeDtypeStruct((t, ATT_COLS), BF16),
            jax.ShapeDtypeStruct((t, LANES), F32),
            jax.ShapeDtypeStruct((t // LANES, VT_ROWS, LANES), BF16),
            jax.ShapeDtypeStruct((NB, VT_ROWS, N_META), BF16)],
        compiler_params=_cp(("arbitrary",)),
        name="inproj",
    )(*operands, g, w, wvt)


def _store_row_tiles(ref, x):
    n = x.shape[0]
    for j in range(ROW_BLOCKS):
        ref[pl.ds(j, n, stride=ROW_BLOCKS), :] = x[:, LANES * j:LANES * (j + 1)]


def _load_row_tiles(ref, n):
    return jnp.concatenate(
        [ref[pl.ds(j, n, stride=ROW_BLOCKS), :] for j in range(ROW_BLOCKS)], axis=1)


def _outproj_kernel(res_ref, s_ref, d_ref, f_ref, w_ref, o_ref, *tiled_ref):
    mixed = jnp.concatenate([s_ref[...], d_ref[...], f_ref[...]], axis=1)
    out = res_ref[...] + jnp.dot(mixed, w_ref[...], preferred_element_type=F32)
    o_ref[...] = out
    if tiled_ref:
        _store_row_tiles(tiled_ref[0], out)


def _outproj(res, ssm, diff, fox, w, with_row_tiles):
    t = res.shape[0]
    out_specs = [pl.BlockSpec((TOK_TILE, D_MODEL), lambda i: (i, 0))]
    out_shape = [jax.ShapeDtypeStruct((t, D_MODEL), F32)]
    if with_row_tiles:
        out_specs.append(pl.BlockSpec((TOK_TILE * ROW_BLOCKS, LANES), lambda i: (i, 0)))
        out_shape.append(jax.ShapeDtypeStruct((t * ROW_BLOCKS, LANES), F32))
    return pl.pallas_call(
        _outproj_kernel,
        grid=(t // TOK_TILE,),
        in_specs=[pl.BlockSpec((TOK_TILE, D_MODEL), lambda i: (i, 0)),
                  pl.BlockSpec((TOK_TILE, 256), lambda i: (i, 0)),
                  pl.BlockSpec((TOK_TILE, 512), lambda i: (i, 0)),
                  pl.BlockSpec((TOK_TILE, 256), lambda i: (i, 0)),
                  pl.BlockSpec((D_MODEL, D_MODEL), lambda i: (0, 0))],
        out_specs=out_specs,
        out_shape=out_shape,
        compiler_params=_cp(("parallel",)),
        name="outproj",
    )(res, ssm, diff, fox, w)


def _dense_ffn_kernel(res_ref, g_ref, w1_ref, w3_ref, w2_ref, o_ref):
    x = res_ref[...]
    h = _rms(x, g_ref[...]).astype(BF16)
    a = jnp.dot(h, w1_ref[...], preferred_element_type=F32)
    b = jnp.dot(h, w3_ref[...], preferred_element_type=F32)
    hh = (a * jax.nn.sigmoid(a) * b).astype(BF16)
    o_ref[...] = x + jnp.dot(hh, w2_ref[...], preferred_element_type=F32)


def _dense_ffn(res, g, w1, w3, w2):
    t = res.shape[0]
    dff = w1.shape[1]
    once = pl.Buffered(1)
    return pl.pallas_call(
        _dense_ffn_kernel,
        grid=(t // FFN_TILE,),
        in_specs=[pl.BlockSpec((FFN_TILE, D_MODEL), lambda i: (i, 0)),
                  pl.BlockSpec((1, D_MODEL), lambda i: (0, 0)),
                  pl.BlockSpec((D_MODEL, dff), lambda i: (0, 0), pipeline_mode=once),
                  pl.BlockSpec((D_MODEL, dff), lambda i: (0, 0), pipeline_mode=once),
                  pl.BlockSpec((dff, D_MODEL), lambda i: (0, 0), pipeline_mode=once)],
        out_specs=pl.BlockSpec((FFN_TILE, D_MODEL), lambda i: (i, 0)),
        out_shape=jax.ShapeDtypeStruct((t, D_MODEL), F32),
        compiler_params=_cp(("parallel",)),
        name="dense_ffn",
    )(res, g, w1, w3, w2)


def _gelu_tanh(x):
    c = math.sqrt(2.0 / math.pi)
    return 0.5 * x * (1.0 + jnp.tanh(c * (x + 0.044715 * (x * x * x))))


def _s5_chunk(lc, get_u, state_ref, ut_ref, bu_ref, ot_ref,
              a_ref, bd_ref, cd_ref, dskip_ref, wglu_ref):
    for b in range(NB):
        ub = get_u(b)
        for s in range(2):
            ut_ref[s, pl.ds(b, lc, stride=NB), :] = ub[:, LANES * s:LANES * (s + 1)]
    u_tm = jnp.concatenate([ut_ref[0], ut_ref[1]], axis=1)
    bu_ref[...] = jnp.dot(u_tm.astype(BF16), bd_ref[...], preferred_element_type=F32)

    half = SSM_COLS // 2
    a_re = a_ref[:, :half]
    a_im = a_ref[:, half:]

    def step(t, x):
        x_re, x_im = x
        r = pl.multiple_of(t * NB, NB)
        cur = bu_ref[pl.ds(r, NB), :]
        n_re = a_re * x_re - a_im * x_im + cur[:, :half]
        n_im = a_re * x_im + a_im * x_re + cur[:, half:]
        bu_ref[pl.ds(r, NB), :] = jnp.concatenate([n_re, n_im], axis=1)
        return n_re, n_im

    x_re, x_im = lax.fori_loop(0, lc, step,
                               (state_ref[:, :half], state_ref[:, half:]), unroll=4)
    state_ref[...] = jnp.concatenate([x_re, x_im], axis=1)

    y = jnp.dot(bu_ref[...].astype(BF16), cd_ref[...], preferred_element_type=F32)
    y = _gelu_tanh(y + dskip_ref[...] * u_tm)
    g = jnp.dot(y.astype(BF16), wglu_ref[...], preferred_element_type=F32)
    o = g[:, :SSM_WIDTH] * jax.nn.sigmoid(g[:, SSM_WIDTH:])
    ot_ref[0] = o[:, :LANES]
    ot_ref[1] = o[:, LANES:]


def _s5_read_out(ot_ref, b, lc):
    return jnp.concatenate(
        [ot_ref[s, pl.ds(b, lc, stride=NB), :] for s in range(2)], axis=1)


def _s5_meta_kernel(u_ref, a_ref, bd_ref, cd_ref, dskip_ref, wglu_ref,
                    o_ref, state_out_ref, state_ref, ut_ref, bu_ref, ot_ref):
    state_ref[...] = jnp.zeros_like(state_ref)
    _s5_chunk(N_META, lambda b: u_ref[b * N_META:(b + 1) * N_META, :],
              state_ref, ut_ref, bu_ref, ot_ref, a_ref, bd_ref, cd_ref, dskip_ref, wglu_ref)
    for b in range(NB):
        o_ref[b * N_META:(b + 1) * N_META, :] = _s5_read_out(ot_ref, b, N_META).astype(BF16)
    state_out_ref[...] = state_ref[...]


def _s5_real_kernel(*refs):
    u_refs = refs[:NB]
    (state_in_ref, a_ref, bd_ref, cd_ref, dskip_ref, wglu_ref, _flat_ref,
     o_ref, state_ref, ut_ref, bu_ref, ot_ref) = refs[NB:]
    c = pl.program_id(0)
    b = pl.program_id(1)

    @pl.when((c == 0) & (b == 0))
    def _():
        state_ref[...] = state_in_ref[...]

    @pl.when(b == 0)
    def _():
        _s5_chunk(S5_CHUNK, lambda bb: u_refs[bb][...],
                  state_ref, ut_ref, bu_ref, ot_ref, a_ref, bd_ref, cd_ref, dskip_ref, wglu_ref)

    o_ref[...] = _s5_read_out(ot_ref, b, S5_CHUNK).astype(BF16)


def _s5(zu, seq, a, bd, cd, dskip, wglu):
    t = zu.shape[0]
    tr = NB * seq
    nmeta_rows = NB * N_META
    const = lambda *_: (0, 0)
    par_specs = [pl.BlockSpec((NB, SSM_COLS), const),
                 pl.BlockSpec((SSM_WIDTH, SSM_COLS), const),
                 pl.BlockSpec((SSM_COLS, SSM_WIDTH), const),
                 pl.BlockSpec((1, SSM_WIDTH), const),
                 pl.BlockSpec((SSM_WIDTH, 2 * SSM_WIDTH), const)]

    def scratch(lc):
        return [pltpu.VMEM((NB, SSM_COLS), F32),
                pltpu.VMEM((2, lc * NB, LANES), F32),
                pltpu.VMEM((lc * NB, SSM_COLS), F32),
                pltpu.VMEM((2, lc * NB, LANES), F32)]

    meta_blk = tr // nmeta_rows
    flat, state = pl.pallas_call(
        _s5_meta_kernel,
        grid=(1,),
        in_specs=[pl.BlockSpec((nmeta_rows, SSM_WIDTH), lambda i: (meta_blk, 0))] + par_specs,
        out_specs=[pl.BlockSpec((nmeta_rows, SSM_WIDTH), lambda i: (meta_blk, 0)),
                   pl.BlockSpec((NB, SSM_COLS), const)],
        out_shape=[jax.ShapeDtypeStruct((t, SSM_WIDTH), BF16),
                   jax.ShapeDtypeStruct((NB, SSM_COLS), F32)],
        scratch_shapes=scratch(N_META),
        compiler_params=_cp(("arbitrary",)),
        name="s5_meta",
    )(zu, a, bd, cd, dskip, wglu)

    nc = seq // S5_CHUNK
    u_specs = [pl.BlockSpec((S5_CHUNK, SSM_WIDTH), lambda c, b, bb=bb: (bb * nc + c, 0))
               for bb in range(NB)]
    n_in = NB + 1 + len(par_specs)
    return pl.pallas_call(
        _s5_real_kernel,
        grid=(nc, NB),
        in_specs=u_specs + [pl.BlockSpec((NB, SSM_COLS), const)] + par_specs
        + [pl.BlockSpec(memory_space=pl.ANY)],
        out_specs=pl.BlockSpec((S5_CHUNK, SSM_WIDTH), lambda c, b: (b * nc + c, 0)),
        out_shape=jax.ShapeDtypeStruct((t, SSM_WIDTH), BF16),
        scratch_shapes=scratch(S5_CHUNK),
        input_output_aliases={n_in: 0},
        compiler_params=_cp(("arbitrary", "arbitrary")),
        name="s5_real",
    )(*([zu] * NB), state, a, bd, cd, dskip, wglu, flat)


def _s5_params(lam_re, lam_im, log_dt, b_re, b_im, c_re, c_im, d_skip):
    dt = jnp.exp(log_dt)[:, None]
    mag = jnp.exp(lam_re * dt)
    ab_re = mag * jnp.cos(lam_im * dt)
    ab_im = mag * jnp.sin(lam_im * dt)
    den = lam_re * lam_re + lam_im * lam_im
    nr = ab_re - 1.0
    ni = ab_im
    coef_re = ((nr * lam_re + ni * lam_im) / den)[..., None]
    coef_im = ((ni * lam_re - nr * lam_im) / den)[..., None]
    bb_re = coef_re * b_re - coef_im * b_im
    bb_im = coef_re * b_im + coef_im * b_re
    eye = jnp.eye(SSM_GROUPS, dtype=F32)
    half = SSM_COLS // 2
    bd = jnp.concatenate(
        [jnp.einsum('gnc,gh->gchn', m, eye).reshape(SSM_WIDTH, half) for m in (bb_re, bb_im)],
        axis=1).astype(BF16)
    cd = jnp.concatenate(
        [jnp.einsum('gcn,gh->gnhc', m, eye).reshape(half, SSM_WIDTH) for m in (c_re, -c_im)],
        axis=0).astype(BF16)
    a = jnp.concatenate([ab_re.reshape(1, half), ab_im.reshape(1, half)], axis=1)
    a = jnp.broadcast_to(a, (NB, SSM_COLS))
    return a, bd, cd, d_skip.reshape(1, SSM_WIDTH)


def _nt_dot(a, b):
    return lax.dot_general(a, b, (((1,), (1,)), ((), ())), preferred_element_type=F32)


def _osm_init(s, vt):
    m = jnp.max(s, axis=1, keepdims=True)
    p = jnp.exp2(s - m)
    l = jnp.sum(p, axis=1, keepdims=True)
    acc = _nt_dot(p.astype(BF16), vt)
    return m, l, acc


def _causal_sweep(qts, k_ref, k_cols, vt_ref, vt_rows, kms, vmts, qi,
                  m_ref, l_ref, acc_ref, al_ref, p_ref):
    n = len(qts)
    meta_scores = [jnp.dot(kms[i], qts[i], preferred_element_type=F32) for i in range(n)]
    meta_p = []
    for i in range(n):
        s = meta_scores[i]
        m = jnp.max(s, axis=0, keepdims=True)
        p = jnp.exp2(s - m)
        m_ref[i][...] = m
        l_ref[i][...] = jnp.sum(p, axis=0, keepdims=True)
        meta_p.append(p.astype(BF16))
        al_ref[i][...] = jnp.ones_like(al_ref[i])
        p_ref[i][...] = jnp.zeros_like(p_ref[i])
    for i in range(n):
        acc_ref[i][...] = jnp.dot(vmts[i], meta_p[i], preferred_element_type=F32)

    sub = ATT_TK // LANES

    def pending_pv(jp):
        for i in range(n):
            vtb = jnp.concatenate([vt_ref[jp * sub + c, vt_rows[i], :] for c in range(sub)],
                                  axis=1)
            acc_ref[i][...] = al_ref[i][...] * acc_ref[i][...] + jnp.dot(
                vtb, p_ref[i][...], preferred_element_type=F32)

    def chunk(j, masked):
        off = pl.multiple_of(j * ATT_TK, ATT_TK)
        if masked:
            krow = lax.broadcasted_iota(jnp.int32, (ATT_TK, ATT_TQ), 0)
            qcol = lax.broadcasted_iota(jnp.int32, (ATT_TK, ATT_TQ), 1)
            visible = krow <= qcol
        scores = []
        for i in range(n):
            kb = k_ref[pl.ds(off, ATT_TK), k_cols[i]]
            s = jnp.dot(kb, qts[i], preferred_element_type=F32)
            scores.append(jnp.where(visible, s, MASK_VALUE) if masked else s)
        pending_pv(jnp.maximum(j - 1, 0))
        for i in range(n):
            m_old = m_ref[i][...]
            m_new = jnp.maximum(m_old, jnp.max(scores[i], axis=0, keepdims=True))
            alpha = jnp.exp2(m_old - m_new)
            p = jnp.exp2(scores[i] - m_new)
            l_ref[i][...] = alpha * l_ref[i][...] + jnp.sum(p, axis=0, keepdims=True)
            m_ref[i][...] = m_new
            al_ref[i][...] = alpha
            p_ref[i][...] = p.astype(BF16)

    def body(j, carry):
        chunk(j, False)
        return carry

    lax.fori_loop(0, qi, body, 0)
    chunk(qi, True)
    pending_pv(qi)


def _meta_attend(q, km, vmt):
    n = km.shape[0]
    row = lax.broadcasted_iota(jnp.int32, (n, n), 0)
    col = lax.broadcasted_iota(jnp.int32, (n, n), 1)
    s = jnp.where(col <= row, _nt_dot(q, km), MASK_VALUE)
    _, l, acc = _osm_init(s, vmt)
    return l, acc


def _diff_split(q):
    lane = lax.broadcasted_iota(jnp.int32, q.shape, 1)
    q = (q.astype(F32) * QK_SCALE).astype(BF16)
    zero = jnp.zeros_like(q)
    return jnp.where(lane < 64, q, zero), jnp.where(lane >= 64, q, zero)


def _diff_finish(parts, lam, lam_init, g):
    (l1, acc1), (l2, acc2) = parts
    o = acc1 / l1 - lam * (acc2 / l2)
    return (_rms(o, g) * (1.0 - lam_init)).astype(BF16)


def _transpose_bf16(x):
    return x.astype(F32).T.astype(BF16)


def _map_scratch(n_maps, dv):
    return ([pltpu.VMEM((1, ATT_TQ), F32)] * (2 * n_maps)
            + [pltpu.VMEM((dv, ATT_TQ), F32)] * n_maps
            + [pltpu.VMEM((1, ATT_TQ), F32)] * n_maps
            + [pltpu.VMEM((ATT_TK, ATT_TQ), BF16)] * n_maps)


def _split_map_scratch(scratch, n_maps):
    return [scratch[k * n_maps:(k + 1) * n_maps] for k in range(5)]


def _diff_kernel(lam_init, lam_ref, q_ref, k_ref, vt_ref, km_ref, vmt_ref, g_ref, o_ref,
                 *scratch):
    m_ref, l_ref, acc_ref, al_ref, p_ref = _split_map_scratch(scratch, 2 * DIFF_HEADS)
    qts, k_cols, vt_rows, kms, vmts = [], [], [], [], []
    row = lax.broadcasted_iota(jnp.int32, (LANES, ATT_TQ), 0)
    for h in range(DIFF_HEADS):
        sl = slice(LANES * h, LANES * (h + 1))
        km = km_ref[:, sl]
        vmt = vmt_ref[sl, :]
        qt = (q_ref[:, sl].astype(F32) * QK_SCALE).T.astype(BF16)
        zero = jnp.zeros_like(qt)
        for qm in (jnp.where(row < 64, qt, zero), jnp.where(row >= 64, qt, zero)):
            qts.append(qm)
            k_cols.append(sl)
            vt_rows.append(sl)
            kms.append(km)
            vmts.append(vmt)
    _causal_sweep(qts, k_ref, k_cols, vt_ref, vt_rows, kms, vmts, pl.program_id(1),
                  m_ref, l_ref, acc_ref, al_ref, p_ref)
    lam = lam_ref[0]
    for h in range(DIFF_HEADS):
        o = (acc_ref[2 * h][...] / l_ref[2 * h][...]
             - lam * (acc_ref[2 * h + 1][...] / l_ref[2 * h + 1][...]))
        y = o * lax.rsqrt(jnp.mean(o * o, axis=0, keepdims=True) + EPS)
        o_ref[:, LANES * h:LANES * (h + 1)] = (
            y.T * g_ref[...] * (1.0 - lam_init)).astype(BF16)


def _diff_meta_kernel(lam_init, lam_ref, q_ref, km_ref, vmt_ref, g_ref, _flat_ref, o_ref):
    for h in range(DIFF_HEADS):
        sl = slice(LANES * h, LANES * (h + 1))
        q1, q2 = _diff_split(q_ref[:, sl])
        km = km_ref[:, sl]
        vmt = vmt_ref[sl, :]
        parts = [_meta_attend(q1, km, vmt), _meta_attend(q2, km, vmt)]
        o_ref[:, sl] = _diff_finish(parts, lam_ref[0], lam_init, g_ref[...])


def _diff_attention(za, vt, vtm, seq, lam, lam_init, g):
    t = za.shape[0]
    nq = seq // ATT_TQ
    mrow = NB * seq // N_META
    smem = pl.BlockSpec(memory_space=pltpu.SMEM)
    lam = lam.reshape(1).astype(F32)
    g = g.reshape(1, LANES)
    gspec = pl.BlockSpec((1, LANES), lambda *_: (0, 0))
    width = DIFF_HEADS * LANES
    n_maps = 2 * DIFF_HEADS
    flat = pl.pallas_call(
        functools.partial(_diff_kernel, lam_init),
        grid=(NB, nq),
        in_specs=[smem,
                  pl.BlockSpec((ATT_TQ, width), lambda b, i: (b * nq + i, 0)),
                  pl.BlockSpec((seq, width), lambda b, i: (b, 1)),
                  pl.BlockSpec((seq // LANES, width, LANES), lambda b, i: (b, 0, 0)),
                  pl.BlockSpec((N_META, width), lambda b, i: (mrow + b, 1)),
                  pl.BlockSpec((None, width, N_META), lambda b, i: (b, 0, 0)),
                  gspec],
        out_specs=pl.BlockSpec((ATT_TQ, width), lambda b, i: (b * nq + i, 0)),
        out_shape=jax.ShapeDtypeStruct((t, width), BF16),
        scratch_shapes=_map_scratch(n_maps, LANES),
        compiler_params=_cp(("parallel", "arbitrary")),
        name="diff_attn",
    )(lam, za, za, vt, za, vtm, g)
    return pl.pallas_call(
        functools.partial(_diff_meta_kernel, lam_init),
        grid=(NB,),
        in_specs=[smem,
                  pl.BlockSpec((N_META, width), lambda b: (mrow + b, 0)),
                  pl.BlockSpec((N_META, width), lambda b: (mrow + b, 1)),
                  pl.BlockSpec((None, width, N_META), lambda b: (b, 0, 0)),
                  gspec,
                  pl.BlockSpec(memory_space=pl.ANY)],
        out_specs=pl.BlockSpec((N_META, width), lambda b: (mrow + b, 0)),
        out_shape=jax.ShapeDtypeStruct((t, width), BF16),
        input_output_aliases={5: 0},
        compiler_params=_cp(("parallel",)),
        name="diff_attn_meta",
    )(lam, za, za, vtm, g, flat)


def _split3(c):
    hi = c.astype(BF16).astype(F32)
    r1 = c - hi
    mid = r1.astype(BF16).astype(F32)
    lo = (r1 - mid).astype(BF16).astype(F32)
    return hi, mid, lo


def _cumsum_rows(tri, lf):
    parts = jnp.concatenate(_split3(lf), axis=1).astype(BF16)
    r = jnp.dot(tri, parts, preferred_element_type=F32)
    return r[:, :LANES] + r[:, LANES:2 * LANES] + r[:, 2 * LANES:]


def _log_sigmoid(x):
    return jnp.minimum(x, 0.0) - jnp.log1p(jnp.exp(-jnp.abs(x)))


def _fox_augment(fq, fk, cum, qa_ref, ka_ref):
    n = fq.shape[0]
    lane = lax.broadcasted_iota(jnp.int32, (n, LANES), 1)
    for h in range(FOX_HEADS):
        pair = slice(LANES * (h // 2), LANES * (h // 2) + LANES)
        own = (lane // 64) == (h % 2)
        e0 = 64 * (1 - h % 2)
        hi, mid, lo = _split3(jnp.broadcast_to(cum[:, h:h + 1] * LOG2E, (n, LANES)))
        ones = (lane >= e0 + 3) & (lane < e0 + 6)
        q_extra = jnp.where(lane == e0, hi, jnp.where(lane == e0 + 1, mid, jnp.where(
            lane == e0 + 2, lo, jnp.where(ones, 1.0, 0.0))))
        ones = (lane >= e0) & (lane < e0 + 3)
        k_extra = jnp.where(lane == e0 + 3, -hi, jnp.where(lane == e0 + 4, -mid, jnp.where(
            lane == e0 + 5, -lo, jnp.where(ones, 1.0, 0.0))))
        q = fq[:, pair].astype(F32) * QK_SCALE
        k = fk[:, pair].astype(F32)
        qa_ref[:, LANES * h:LANES * (h + 1)] = jnp.where(own, q, q_extra).astype(BF16)
        ka_ref[:, LANES * h:LANES * (h + 1)] = jnp.where(own, k, k_extra).astype(BF16)


def _fox_prep_meta_kernel(zg_ref, fq_ref, fk_ref, fb_ref, qa_ref, ka_ref, carry_ref):
    n = NB * N_META
    lf = _log_sigmoid(zg_ref[...] + fb_ref[...])
    row = lax.broadcasted_iota(jnp.int32, (n, n), 0)
    col = lax.broadcasted_iota(jnp.int32, (n, n), 1)
    tri = ((col <= row) & (col // N_META == row // N_META)).astype(BF16)
    cum = _cumsum_rows(tri, lf)
    brow = lax.broadcasted_iota(jnp.int32, (NB, n), 0)
    bcol = lax.broadcasted_iota(jnp.int32, (NB, n), 1)
    carry_ref[...] = _cumsum_rows((bcol // N_META == brow).astype(BF16), lf)
    _fox_augment(fq_ref[...], fk_ref[...], cum, qa_ref, ka_ref)


def _fox_prep_kernel(zg_ref, fq_ref, fk_ref, fb_ref, carry_in_ref, qa_ref, ka_ref, carry_ref):
    b = pl.program_id(0)
    n = zg_ref.shape[0]

    @pl.when(pl.program_id(1) == 0)
    def _():
        carry_ref[...] = carry_in_ref[pl.ds(b, 1), :]

    lf = _log_sigmoid(zg_ref[...] + fb_ref[...])
    row = lax.broadcasted_iota(jnp.int32, (n, n), 0)
    col = lax.broadcasted_iota(jnp.int32, (n, n), 1)
    cum = _cumsum_rows((col <= row).astype(BF16), lf) + carry_ref[...]
    carry_ref[...] = cum[n - 1:n, :]
    _fox_augment(fq_ref[...], fk_ref[...], cum, qa_ref, ka_ref)


FOX_PREP_TILE = 1024


def _fox_prep(zg, za, seq, fb):
    tr = NB * seq
    nm = NB * N_META
    mblk = tr // nm
    fb = jnp.pad(fb.astype(F32), (0, LANES - FOX_HEADS)).reshape(1, LANES)
    fbspec = pl.BlockSpec((1, LANES), lambda *_: (0, 0))
    aug = FOX_HEADS * LANES
    qa_m, ka_m, carry = pl.pallas_call(
        _fox_prep_meta_kernel,
        grid=(1,),
        in_specs=[pl.BlockSpec((nm, LANES), lambda i: (mblk, 0)),
                  pl.BlockSpec((nm, 256), lambda i: (mblk, 4)),
                  pl.BlockSpec((nm, 256), lambda i: (mblk, 5)),
                  fbspec],
        out_specs=[pl.BlockSpec((nm, aug), lambda i: (0, 0)),
                   pl.BlockSpec((nm, aug), lambda i: (0, 0)),
                   pl.BlockSpec((NB, LANES), lambda i: (0, 0))],
        out_shape=[jax.ShapeDtypeStruct((nm, aug), BF16),
                   jax.ShapeDtypeStruct((nm, aug), BF16),
                   jax.ShapeDtypeStruct((NB, LANES), F32)],
        compiler_params=_cp(("arbitrary",)),
        name="fox_prep_meta",
    )(zg, za, za, fb)
    nc = seq // FOX_PREP_TILE
    qa, ka = pl.pallas_call(
        _fox_prep_kernel,
        grid=(NB, nc),
        in_specs=[pl.BlockSpec((FOX_PREP_TILE, LANES), lambda b, c: (b * nc + c, 0)),
                  pl.BlockSpec((FOX_PREP_TILE, 256), lambda b, c: (b * nc + c, 4)),
                  pl.BlockSpec((FOX_PREP_TILE, 256), lambda b, c: (b * nc + c, 5)),
                  fbspec,
                  pl.BlockSpec((NB, LANES), lambda b, c: (0, 0))],
        out_specs=[pl.BlockSpec((FOX_PREP_TILE, aug), lambda b, c: (b * nc + c, 0)),
                   pl.BlockSpec((FOX_PREP_TILE, aug), lambda b, c: (b * nc + c, 0))],
        out_shape=[jax.ShapeDtypeStruct((tr, aug), BF16),
                   jax.ShapeDtypeStruct((tr, aug), BF16)],
        scratch_shapes=[pltpu.VMEM((1, LANES), F32)],
        compiler_params=_cp(("parallel", "arbitrary")),
        name="fox_prep",
    )(zg, za, za, fb, carry)
    return qa, ka, qa_m, ka_m


def _fox_finish(parts):
    (l0, acc0), (l1, acc1) = parts
    lane = lax.broadcasted_iota(jnp.int32, acc0.shape, 1)
    return jnp.where(lane < 64, acc0 / l0, acc1 / l1).astype(BF16)


def _fox_kernel(q_ref, k_ref, vt_ref, km_ref, vmt_ref, o_ref, *scratch):
    m_ref, l_ref, acc_ref, al_ref, p_ref = _split_map_scratch(scratch, FOX_HEADS)
    hd = 64
    vmt = vmt_ref[...]
    qts, k_cols, vt_rows, kms, vmts = [], [], [], [], []
    for h in range(FOX_HEADS):
        sl = slice(LANES * h, LANES * (h + 1))
        qts.append(_transpose_bf16(q_ref[:, sl]))
        k_cols.append(sl)
        vt_rows.append(slice(hd * h, hd * (h + 1)))
        kms.append(km_ref[:, sl])
        vmts.append(vmt[hd * h:hd * (h + 1), :])
    _causal_sweep(qts, k_ref, k_cols, vt_ref, vt_rows, kms, vmts, pl.program_id(1),
                  m_ref, l_ref, acc_ref, al_ref, p_ref)
    for p in range(FOX_HEADS // 2):
        o = jnp.concatenate([acc_ref[2 * p][...] / l_ref[2 * p][...],
                             acc_ref[2 * p + 1][...] / l_ref[2 * p + 1][...]],
                            axis=0)
        o_ref[:, LANES * p:LANES * (p + 1)] = o.T.astype(BF16)


def _fox_meta_kernel(q_ref, km_ref, vmt_ref, _flat_ref, o_ref):
    for p in range(FOX_HEADS // 2):
        vmt = vmt_ref[LANES * p:LANES * (p + 1), :]
        parts = []
        for h in (2 * p, 2 * p + 1):
            sl = slice(LANES * h, LANES * (h + 1))
            parts.append(_meta_attend(q_ref[:, sl], km_ref[:, sl], vmt))
        o_ref[:, LANES * p:LANES * (p + 1)] = _fox_finish(parts)


def _fox_attention(t, vt, vtm, seq, qa, ka, qa_m, ka_m):
    nq = seq // ATT_TQ
    mrow = NB * seq // N_META
    aug = FOX_HEADS * LANES
    flat = pl.pallas_call(
        _fox_kernel,
        grid=(NB, nq),
        in_specs=[pl.BlockSpec((ATT_TQ, aug), lambda b, i: (b * nq + i, 0)),
                  pl.BlockSpec((seq, aug), lambda b, i: (b, 0)),
                  pl.BlockSpec((seq // LANES, 256, LANES), lambda b, i: (b, 2, 0)),
                  pl.BlockSpec((N_META, aug), lambda b, i: (b, 0)),
                  pl.BlockSpec((None, 256, N_META), lambda b, i: (b, 2, 0))],
        out_specs=pl.BlockSpec((ATT_TQ, 256), lambda b, i: (b * nq + i, 0)),
        out_shape=jax.ShapeDtypeStruct((t, 256), BF16),
        scratch_shapes=_map_scratch(FOX_HEADS, 64),
        compiler_params=_cp(("parallel", "arbitrary")),
        name="fox_attn",
    )(qa, ka, vt, ka_m, vtm)
    return pl.pallas_call(
        _fox_meta_kernel,
        grid=(NB,),
        in_specs=[pl.BlockSpec((N_META, aug), lambda b: (b, 0)),
                  pl.BlockSpec((N_META, aug), lambda b: (b, 0)),
                  pl.BlockSpec((None, 256, N_META), lambda b: (b, 2, 0)),
                  pl.BlockSpec(memory_space=pl.ANY)],
        out_specs=pl.BlockSpec((N_META, 256), lambda b: (mrow + b, 0)),
        out_shape=jax.ShapeDtypeStruct((t, 256), BF16),
        input_output_aliases={3: 0},
        compiler_params=_cp(("parallel",)),
        name="fox_attn_meta",
    )(qa_m, ka_m, vtm, flat)


def _router_kernel(res_ref, g_ref, wr_ref, idx_ref, gate_ref):
    h = _rms(res_ref[...], g_ref[...])
    w = wr_ref[...]
    hi = h.astype(BF16)
    lo = (h - hi.astype(F32)).astype(BF16)
    whi = w.astype(BF16)
    wlo = (w - whi.astype(F32)).astype(BF16)
    logits = (jnp.dot(hi, whi, preferred_element_type=F32)
              + jnp.dot(lo, whi, preferred_element_type=F32)
              + jnp.dot(hi, wlo, preferred_element_type=F32))
    lane = lax.broadcasted_iota(jnp.int32, logits.shape, 1)
    logits = jnp.where(lane < N_EXPERTS, logits, -jnp.inf)
    m1 = jnp.max(logits, axis=1, keepdims=True)
    i1 = jnp.min(jnp.where(logits == m1, lane, LANES), axis=1, keepdims=True)
    rest = jnp.where(lane == i1, -jnp.inf, logits)
    m2 = jnp.max(rest, axis=1, keepdims=True)
    i2 = jnp.min(jnp.where(rest == m2, lane, LANES), axis=1, keepdims=True)
    e = jnp.exp(m2 - m1)
    g1 = 1.0 / (1.0 + e)
    g2 = e / (1.0 + e)
    idx_ref[...] = jnp.where(lane == 0, i1, jnp.where(lane == 1, i2, 0))
    gate_ref[...] = jnp.where(lane == 0, g1, jnp.where(lane == 1, g2, 0.0))


def _router(res, g, wr):
    t = res.shape[0]
    return pl.pallas_call(
        _router_kernel,
        grid=(t // TOK_TILE,),
        in_specs=[pl.BlockSpec((TOK_TILE, D_MODEL), lambda i: (i, 0)),
                  pl.BlockSpec((1, D_MODEL), lambda i: (0, 0)),
                  pl.BlockSpec((D_MODEL, LANES), lambda i: (0, 0))],
        out_specs=[pl.BlockSpec((TOK_TILE, LANES), lambda i: (i, 0)),
                   pl.BlockSpec((TOK_TILE, LANES), lambda i: (i, 0))],
        out_shape=[jax.ShapeDtypeStruct((t, LANES), jnp.int32),
                   jax.ShapeDtypeStruct((t, LANES), F32)],
        compiler_params=_cp(("parallel",)),
        name="router",
    )(res, g, wr)


def _gather_copy(src_hbm, first, dst_ref, r, sem):
    first = pl.multiple_of(first, ROW_BLOCKS)
    dst = pl.multiple_of(r * ROW_BLOCKS, ROW_BLOCKS)
    return pltpu.make_async_copy(src_hbm.at[pl.ds(first, ROW_BLOCKS), :],
                                 dst_ref.at[pl.ds(dst, ROW_BLOCKS), :], sem)


def _expert_ffn_kernel(nf, te_ref, na_ref, nb_ref, src_ref, x_hbm, g_ref, w1_ref, w3_ref,
                       w2_ref, o_ref, xbuf_ref, h_ref, acc_ref, sem):
    r = pl.program_id(0)
    f = pl.program_id(1)
    n_act = na_ref[0]
    n_sub = MOE_TILE // MOE_SUB
    per_step = MOE_XROWS // nf
    last_tile = pl.num_programs(0) - 1

    def tile_copy(slot):
        return pltpu.make_async_copy(x_hbm.at[pl.ds(0, MOE_XROWS * ROW_BLOCKS), :],
                                     xbuf_ref.at[slot], sem.at[slot])

    def start_row(tile, slot, k):
        _gather_copy(x_hbm, src_ref[tile * MOE_XROWS + k], xbuf_ref.at[slot], k,
                     sem.at[slot]).start()

    @pl.when(r < n_act)
    def _():
        slot = lax.rem(r, MOE_XSLOTS)
        slot1 = lax.rem(r + 1, MOE_XSLOTS)
        slot2 = lax.rem(r + 2, MOE_XSLOTS)

        @pl.when((r == 0) & (f == 0))
        def _():
            def first(k, _):
                start_row(0, 0, k)
                start_row(jnp.minimum(1, last_tile), 1, k)
                return 0
            lax.fori_loop(0, MOE_XROWS, first, 0, unroll=8)

        @pl.when(f == 0)
        def _():
            tile_copy(slot).wait()
            x = _load_row_tiles(xbuf_ref.at[slot], MOE_TILE)
            h_ref[...] = _rms(x, g_ref[...]).astype(BF16)
            acc_ref[...] = jnp.zeros_like(acc_ref)

        ahead = jnp.minimum(r + 2, last_tile)

        def start_ahead():
            for k in range(per_step):
                start_row(ahead, slot2, f * per_step + k)

        def swiglu_rows(rows, w1, w3, w2):
            h = h_ref[rows, :]
            a = jnp.dot(h, w1, preferred_element_type=F32)
            b = jnp.dot(h, w3, preferred_element_type=F32)
            hh = (a * jax.nn.sigmoid(a) * b).astype(BF16)
            acc_ref[rows, :] += jnp.dot(hh, w2, preferred_element_type=F32)

        def weights():
            return (w1_ref[...].astype(BF16), w3_ref[...].astype(BF16),
                    w2_ref[...].astype(BF16))

        n_blocks = nb_ref[r]

        @pl.when(n_blocks == n_sub)
        def _():
            start_ahead()
            swiglu_rows(slice(None), *weights())

        @pl.when(n_blocks < n_sub)
        def _():
            start_ahead()
            w = weights()
            swiglu_rows(slice(0, MOE_SUB), *w)
            for sb in range(1, n_sub - 1):
                @pl.when(sb < n_blocks)
                def _():
                    swiglu_rows(slice(MOE_SUB * sb, MOE_SUB * (sb + 1)), *w)

        @pl.when(f == nf - 1)
        def _():
            _store_row_tiles(o_ref, acc_ref[...])

        @pl.when((f == nf - 1) & (r == n_act - 1))
        def _():
            tile_copy(slot1).wait()
            tile_copy(slot2).wait()


def _expert_ffn(tile_expert, n_active, tile_blocks, src, x, g, w1, w3, w2):
    dff = w1.shape[2]
    nt = src.shape[0] // MOE_XROWS
    nf = dff // MOE_FCHUNK
    assert MOE_XROWS % nf == 0 and MOE_XROWS >= MOE_TILE

    def row(r, f, te, na, *_):
        return jnp.minimum(r, na[0] - 1)

    def fch(r, f, te, na, *_):
        return jnp.where(r < na[0], f, nf - 1)

    return pl.pallas_call(
        functools.partial(_expert_ffn_kernel, nf),
        grid_spec=pltpu.PrefetchScalarGridSpec(
            num_scalar_prefetch=4,
            grid=(nt, nf),
            in_specs=[
                pl.BlockSpec(memory_space=pl.ANY),
                pl.BlockSpec((1, D_MODEL), lambda r, f, *_: (0, 0)),
                pl.BlockSpec((None, D_MODEL, MOE_FCHUNK),
                             lambda r, f, te, na, *_: (te[r], 0, fch(r, f, te, na))),
                pl.BlockSpec((None, D_MODEL, MOE_FCHUNK),
                             lambda r, f, te, na, *_: (te[r], 0, fch(r, f, te, na))),
                pl.BlockSpec((None, MOE_FCHUNK, D_MODEL),
                             lambda r, f, te, na, *_: (te[r], fch(r, f, te, na), 0))],
            out_specs=pl.BlockSpec((MOE_TILE * ROW_BLOCKS, LANES),
                                   lambda r, f, te, na, *_: (row(r, f, te, na), 0)),
            scratch_shapes=[pltpu.VMEM((MOE_XSLOTS, MOE_XROWS * ROW_BLOCKS, LANES), F32),
                            pltpu.VMEM((MOE_TILE, D_MODEL), BF16),
                            pltpu.VMEM((MOE_TILE, D_MODEL), F32),
                            pltpu.SemaphoreType.DMA((MOE_XSLOTS,))]),
        out_shape=jax.ShapeDtypeStruct((nt * MOE_TILE * ROW_BLOCKS, LANES), F32),
        compiler_params=_cp(("arbitrary", "arbitrary")),
        name="expert_ffn",
    )(tile_expert, n_active, tile_blocks, src, x, g, w1, w3, w2)


def _combine_kernel(tile, final, pos_ref, res_ref, gate_ref, gf_ref, y_hbm, o_ref, ybuf_ref, sem):
    i = pl.program_id(0)
    n = pl.num_programs(0)
    slot = lax.rem(i, 2)

    def issue_tile(step, s):
        base = step * tile

        def issue(r, _):
            for k in range(2):
                _gather_copy(y_hbm, pos_ref[2 * (base + r) + k], ybuf_ref.at[s, k], r,
                             sem.at[s]).start(priority=k)
            return 0

        lax.fori_loop(0, tile, issue, 0, unroll=8)

    @pl.when(i == 0)
    def _():
        issue_tile(0, 0)

    @pl.when(i + 1 < n)
    def _():
        issue_tile(i + 1, 1 - slot)

    pltpu.make_async_copy(ybuf_ref.at[slot], ybuf_ref.at[slot], sem.at[slot]).wait()
    gate = gate_ref[...]
    out = (res_ref[...] + gate[:, 0:1] * _load_row_tiles(ybuf_ref.at[slot, 0], tile)
           + gate[:, 1:2] * _load_row_tiles(ybuf_ref.at[slot, 1], tile))
    o_ref[...] = _rms(out, gf_ref[...]) if final else out


def _combine(pos, res, gates, y, final_g=None):
    final = final_g is not None
    t = res.shape[0]
    n_rows = t - NB * N_META if final else t
    tile = 256 if final else COMBINE_TILE
    if not final:
        final_g = jnp.ones((1, D_MODEL), F32)
    return pl.pallas_call(
        functools.partial(_combine_kernel, tile, final),
        grid_spec=pltpu.PrefetchScalarGridSpec(
            num_scalar_prefetch=1,
            grid=(n_rows // tile,),
            in_specs=[pl.BlockSpec((tile, D_MODEL), lambda i, pos: (i, 0)),
                      pl.BlockSpec((tile, LANES), lambda i, pos: (i, 0)),
                      pl.BlockSpec((1, D_MODEL), lambda i, pos: (0, 0)),
                      pl.BlockSpec(memory_space=pl.ANY)],
            out_specs=pl.BlockSpec((tile, D_MODEL), lambda i, pos: (i, 0)),
            scratch_shapes=[pltpu.VMEM((2, 2, tile * ROW_BLOCKS, LANES), F32),
                            pltpu.SemaphoreType.DMA((2,))]),
        out_shape=jax.ShapeDtypeStruct((n_rows, D_MODEL), F32),
        compiler_params=_cp(("arbitrary",)),
        name="moe_combine",
    )(pos, res, gates, final_g, y)


def _invert_kernel(spos_ref, src_ref):
    def clear(i, _):
        src_ref[i] = 0
        return 0

    lax.fori_loop(0, src_ref.shape[0], clear, 0, unroll=8)

    def body(a, _):
        src_ref[spos_ref[a]] = lax.shift_right_logical(a, 1) * ROW_BLOCKS
        return 0

    lax.fori_loop(0, spos_ref.shape[0], body, 0, unroll=8)


def _invert_positions(spos, n):
    smem = pl.BlockSpec(memory_space=pltpu.SMEM)
    return pl.pallas_call(
        _invert_kernel,
        in_specs=[smem],
        out_specs=smem,
        out_shape=jax.ShapeDtypeStruct((n,), jnp.int32),
        name="moe_invert",
    )(spos)


def _moe(res, res_tiles, g, wr, w1, w3, w2, final_g=None):
    t = res.shape[0]
    wr = jnp.pad(wr.astype(F32), ((0, 0), (0, LANES - N_EXPERTS)))
    idx, gates = _router(res, g, wr)
    e_flat = idx[:, :2].reshape(-1)
    onehot = (e_flat[:, None] == jnp.arange(N_EXPERTS, dtype=jnp.int32)[None, :]).astype(jnp.int32)
    csum = jnp.cumsum(onehot, axis=0)
    rank = jnp.take_along_axis(csum, e_flat[:, None], axis=1)[:, 0] - 1
    counts = csum[-1]
    tiles = (counts + MOE_TILE - 1) // MOE_TILE
    tile_end = jnp.cumsum(tiles)
    starts = (tile_end - tiles) * MOE_TILE
    pos = (starts[e_flat] + rank).astype(jnp.int32)
    n_tiles = (2 * t + N_EXPERTS * (MOE_TILE - 1)) // MOE_TILE
    spos = (pos // MOE_TILE) * MOE_XROWS + pos % MOE_TILE
    src = _invert_positions(spos, n_tiles * MOE_XROWS)
    n_active = tile_end[-1:].astype(jnp.int32)
    tile_ids = jnp.minimum(jnp.arange(n_tiles, dtype=jnp.int32), n_active[0] - 1)
    tile_expert = jnp.sum(tile_ids[:, None] >= tile_end[None, :], axis=1).astype(jnp.int32)
    first_tile = (tile_end - tiles)[tile_expert]
    tile_rows = jnp.clip(counts[tile_expert] - (tile_ids - first_tile) * MOE_TILE, 1, MOE_TILE)
    tile_blocks = ((tile_rows + MOE_SUB - 1) // MOE_SUB).astype(jnp.int32)

    y = _expert_ffn(tile_expert, n_active, tile_blocks, src, res_tiles, g, w1, w3, w2)
    return _combine(pos * ROW_BLOCKS, res, gates, y, final_g)


def _final_norm_kernel(res_ref, g_ref, o_ref):
    o_ref[...] = _rms(res_ref[...], g_ref[...])


def _final_norm(res, g, n_rows):
    tile = 1024
    return pl.pallas_call(
        _final_norm_kernel,
        grid=(n_rows // tile,),
        in_specs=[pl.BlockSpec((tile, D_MODEL), lambda i: (i, 0)),
                  pl.BlockSpec((1, D_MODEL), lambda i: (0, 0))],
        out_specs=pl.BlockSpec((tile, D_MODEL), lambda i: (i, 0)),
        out_shape=jax.ShapeDtypeStruct((n_rows, D_MODEL), F32),
        compiler_params=_cp(("parallel",)),
        name="final_norm",
    )(res, g)


def kernel(x, meta_tokens, norm_mix_g, w_in, w_out, ssm_lambda_re, ssm_lambda_im, ssm_log_dt,
           ssm_b_re, ssm_b_im, ssm_c_re, ssm_c_im, ssm_d, ssm_w_glu, diff_lambda_q1,
           diff_lambda_k1, diff_lambda_q2, diff_lambda_k2, diff_subln_g, fox_forget_b,
           norm_ffn_g, dense_w1, dense_w3, dense_w2, moe_router, moe_w1, moe_w3, moe_w2,
           final_norm_g):
    bsz, seq, d = x.shape
    assert bsz == NB and d == D_MODEL and seq % ATT_TQ == 0
    depth = w_in.shape[0]
    tr = bsz * seq
    res = None
    row = lambda v: v.reshape(1, -1).astype(F32)

    for l in range(depth):
        wl = w_in[l]
        w = jnp.concatenate([wl[:, :1280], wl[:, 1792:2304],
                             jnp.pad(wl[:, 2560:], ((0, 0), (0, LANES - FOX_HEADS)))], axis=1)
        wvt = jnp.concatenate([wl[:, 1280:1792], wl[:, 2304:2560]], axis=1).T
        if l == 0:
            res, zu, za, zg, vt, vtm = _inproj(
                None, row(norm_mix_g[l]), w.astype(BF16), wvt.astype(BF16),
                embed=(x.reshape(tr, d), meta_tokens.astype(x.dtype)))
        else:
            zu, za, zg, vt, vtm = _inproj(res, row(norm_mix_g[l]), w.astype(BF16),
                                          wvt.astype(BF16))

        a, bd, cd, dskip = _s5_params(ssm_lambda_re[l], ssm_lambda_im[l], ssm_log_dt[l],
                                      ssm_b_re[l], ssm_b_im[l], ssm_c_re[l], ssm_c_im[l],
                                      ssm_d[l])
        ssm_out = _s5(zu, seq, a, bd, cd, dskip, ssm_w_glu[l].astype(BF16))

        lam_init = 0.8 - 0.6 * math.exp(-0.3 * l)
        lam = (jnp.exp(jnp.sum(diff_lambda_q1[l] * diff_lambda_k1[l]))
               - jnp.exp(jnp.sum(diff_lambda_q2[l] * diff_lambda_k2[l])) + lam_init)
        diff_out = _diff_attention(za, vt, vtm, seq, lam, lam_init, diff_subln_g[l])

        qa, ka, qa_m, ka_m = _fox_prep(zg, za, seq, fox_forget_b[l])
        fox_out = _fox_attention(za.shape[0], vt, vtm, seq, qa, ka, qa_m, ka_m)

        is_moe = l % 2 == 1
        res, *res_tiles = _outproj(res, ssm_out, diff_out, fox_out, w_out[l].astype(BF16),
                                   with_row_tiles=is_moe)

        if not is_moe:
            res = _dense_ffn(res, row(norm_ffn_g[l]), dense_w1[l // 2].astype(BF16),
                             dense_w3[l // 2].astype(BF16), dense_w2[l // 2].astype(BF16))
        else:
            res = _moe(res, res_tiles[0], row(norm_ffn_g[l]), moe_router[l // 2], moe_w1[l // 2],
                       moe_w3[l // 2], moe_w2[l // 2],
                       final_g=row(final_norm_g) if l == depth - 1 else None)

    if depth % 2 == 1:
        res = _final_norm(res, row(final_norm_g), tr)
    return res.reshape(bsz, seq, d)
```

```python
import functools
import math

import jax
import jax.numpy as jnp
from jax import lax
from jax.experimental import pallas as pl
from jax.experimental.pallas import tpu as pltpu

F32 = jnp.float32
BF16 = jnp.bfloat16
EPS = 1e-6

D_MODEL = 1024
N_META = 16
NB = 8
SSM_WIDTH = 256
SSM_GROUPS = 16
SSM_GROUP = 16
SSM_STATE = 64
SSM_COLS = 2 * SSM_GROUPS * SSM_STATE
DIFF_HEADS = 4
FOX_HEADS = 4
HEAD_DIM = 64
LANES = 128
DIFF_WIDTH = DIFF_HEADS * 2 * HEAD_DIM
FOX_WIDTH = FOX_HEADS * HEAD_DIM
COL_DQ = SSM_WIDTH
COL_DV = COL_DQ + 2 * DIFF_WIDTH
COL_FQ = COL_DV + DIFF_WIDTH
COL_FV = COL_FQ + 2 * FOX_WIDTH
COL_GATE = COL_FV + FOX_WIDTH
ATT_COLS = 2 * DIFF_WIDTH + 2 * FOX_WIDTH
IN_PAD = SSM_WIDTH + ATT_COLS + LANES
N_EXPERTS = 8
MASK_VALUE = -1e30
LOG2E = math.log2(math.e)
QK_SCALE = HEAD_DIM ** -0.5 * LOG2E

VT_ROWS = DIFF_WIDTH + FOX_WIDTH
TOK_TILE = 688
IN_TILE = 384
FFN_TILE = 384
ATT_TQ = 256
ATT_TK = 256
S5_CHUNK = 256
MOE_TILE = 1024
MOE_FCHUNK = 512
MOE_SUB = 256
MOE_XROWS = 1029
MOE_XSLOTS = 3
ROW_BLOCKS = D_MODEL // LANES
COMBINE_TILE = 384

VMEM_LIMIT = 56 * 1024 * 1024


def _cp(sem):
    return pltpu.CompilerParams(dimension_semantics=sem, vmem_limit_bytes=VMEM_LIMIT)


def _rms(x, g):
    return x * lax.rsqrt(jnp.mean(x * x, axis=-1, keepdims=True) + EPS) * g


def _inproj_embed_kernel(x_ref, xtail_ref, meta_ref, g_ref, w_ref, wvt_ref,
                         res_ref, zu_ref, za_ref, zg_ref, vt_ref, vtm_ref):
    tail = jnp.concatenate([xtail_ref[...]] + [meta_ref[...]] * NB, axis=0)
    is_last = pl.program_id(0) == pl.num_programs(0) - 1
    x = jnp.where(is_last, tail, x_ref[...])
    res_ref[...] = x
    _inproj_body(x, g_ref, w_ref, wvt_ref, zu_ref, za_ref, zg_ref, vt_ref, vtm_ref)


def _inproj_kernel(res_ref, g_ref, w_ref, wvt_ref, zu_ref, za_ref, zg_ref, vt_ref, vtm_ref):
    _inproj_body(res_ref[...], g_ref, w_ref, wvt_ref, zu_ref, za_ref, zg_ref, vt_ref, vtm_ref)


def _inproj_body(x, g_ref, w_ref, wvt_ref, zu_ref, za_ref, zg_ref, vt_ref, vtm_ref):
    h = _rms(x, g_ref[...]).astype(BF16)
    zu_ref[...] = jnp.dot(h, w_ref[:, 0:SSM_WIDTH], preferred_element_type=F32)
    for c in range(0, ATT_COLS, 256):
        za_ref[:, c:c + 256] = jnp.dot(
            h, w_ref[:, SSM_WIDTH + c:SSM_WIDTH + c + 256],
            preferred_element_type=F32).astype(BF16)
    zg_ref[...] = jnp.dot(h, w_ref[:, SSM_WIDTH + ATT_COLS:IN_PAD],
                          preferred_element_type=F32)
    vt = _nt_dot(wvt_ref[...], h).astype(BF16)
    for c in range(IN_TILE // LANES):
        vt_ref[c] = vt[:, LANES * c:LANES * (c + 1)]

    @pl.when(pl.program_id(0) == pl.num_programs(0) - 1)
    def _():
        first = IN_TILE - NB * N_META
        for b in range(NB):
            vtm_ref[b] = vt[:, first + N_META * b:first + N_META * (b + 1)]


def _inproj(res, g, w, wvt, embed=None):
    nblk = IN_TILE // LANES
    const = lambda i: (0, 0)
    par_specs = [pl.BlockSpec((1, D_MODEL), const),
                 pl.BlockSpec((D_MODEL, IN_PAD), const),
                 pl.BlockSpec((VT_ROWS, D_MODEL), const)]
    tile = pl.BlockSpec((IN_TILE, D_MODEL), lambda i: (i, 0))
    if embed is None:
        t = res.shape[0]
        body, operands, in_specs = _inproj_kernel, (res,), [tile]
        extra_specs, extra_shapes = [], []
    else:
        x_flat, meta = embed
        tr = x_flat.shape[0]
        t = tr + NB * N_META
        n_full = tr // IN_TILE
        tail = tr - n_full * IN_TILE
        assert tail > 0 and tail + NB * N_META == IN_TILE and tr % tail == 0
        body, operands = _inproj_embed_kernel, (x_flat, x_flat, meta)
        in_specs = [pl.BlockSpec((IN_TILE, D_MODEL), lambda i: (jnp.minimum(i, n_full - 1), 0)),
                    pl.BlockSpec((tail, D_MODEL), lambda i: (tr // tail - 1, 0)),
                    pl.BlockSpec((N_META, D_MODEL), const)]
        extra_specs, extra_shapes = [tile], [jax.ShapeDtypeStruct((t, D_MODEL), F32)]
    assert t % IN_TILE == 0 and NB * N_META <= IN_TILE
    return pl.pallas_call(
        body,
        grid=(t // IN_TILE,),
        in_specs=in_specs + par_specs,
        out_specs=extra_specs + [
            pl.BlockSpec((IN_TILE, SSM_WIDTH), lambda i: (i, 0)),
            pl.BlockSpec((IN_TILE, ATT_COLS), lambda i: (i, 0)),
            pl.BlockSpec((IN_TILE, LANES), lambda i: (i, 0)),
            pl.BlockSpec((nblk, VT_ROWS, LANES), lambda i: (i, 0, 0)),
            pl.BlockSpec((NB, VT_ROWS, N_META), lambda i: (0, 0, 0))],
        out_shape=extra_shapes + [
            jax.ShapeDtypeStruct((t, SSM_WIDTH), F32),
            jax.ShapeDtypeStruct((t, ATT_COLS), BF16),
            jax.ShapeDtypeStruct((t, LANES), F32),
            jax.ShapeDtypeStruct((t // LANES, VT_ROWS, LANES), BF16),
            jax.ShapeDtypeStruct((NB, VT_ROWS, N_META), BF16)],
        compiler_params=_cp(("arbitrary",)),
        name="inproj",
    )(*operands, g, w, wvt)


def _store_row_tiles(ref, x):
    n = x.shape[0]
    for j in range(ROW_BLOCKS):
        ref[pl.ds(j, n, stride=ROW_BLOCKS), :] = x[:, LANES * j:LANES * (j + 1)]


def _load_row_tiles(ref, n):
    return jnp.concatenate(
        [ref[pl.ds(j, n, stride=ROW_BLOCKS), :] for j in range(ROW_BLOCKS)], axis=1)


def _outproj_kernel(res_ref, s_ref, d_ref, f_ref, w_ref, o_ref, *tiled_ref):
    mixed = jnp.concatenate([s_ref[...], d_ref[...], f_ref[...]], axis=1)
    out = res_ref[...] + jnp.dot(mixed, w_ref[...], preferred_element_type=F32)
    o_ref[...] = out
    if tiled_ref:
        _store_row_tiles(tiled_ref[0], out)


def _outproj(res, ssm, diff, fox, w, with_row_tiles):
    t = res.shape[0]
    out_specs = [pl.BlockSpec((TOK_TILE, D_MODEL), lambda i: (i, 0))]
    out_shape = [jax.ShapeDtypeStruct((t, D_MODEL), F32)]
    if with_row_tiles:
        out_specs.append(pl.BlockSpec((TOK_TILE * ROW_BLOCKS, LANES), lambda i: (i, 0)))
        out_shape.append(jax.ShapeDtypeStruct((t * ROW_BLOCKS, LANES), F32))
    return pl.pallas_call(
        _outproj_kernel,
        grid=(t // TOK_TILE,),
        in_specs=[pl.BlockSpec((TOK_TILE, D_MODEL), lambda i: (i, 0)),
                  pl.BlockSpec((TOK_TILE, 256), lambda i: (i, 0)),
                  pl.BlockSpec((TOK_TILE, 512), lambda i: (i, 0)),
                  pl.BlockSpec((TOK_TILE, 256), lambda i: (i, 0)),
                  pl.BlockSpec((D_MODEL, D_MODEL), lambda i: (0, 0))],
        out_specs=out_specs,
        out_shape=out_shape,
        compiler_params=_cp(("parallel",)),
        name="outproj",
    )(res, ssm, diff, fox, w)


def _dense_ffn_kernel(res_ref, g_ref, w1_ref, w3_ref, w2_ref, o_ref):
    x = res_ref[...]
    h = _rms(x, g_ref[...]).astype(BF16)
    a = jnp.dot(h, w1_ref[...], preferred_element_type=F32)
    b = jnp.dot(h, w3_ref[...], preferred_element_type=F32)
    hh = (a * jax.nn.sigmoid(a) * b).astype(BF16)
    o_ref[...] = x + jnp.dot(hh, w2_ref[...], preferred_element_type=F32)


def _dense_ffn(res, g, w1, w3, w2):
    t = res.shape[0]
    dff = w1.shape[1]
    once = pl.Buffered(1)
    return pl.pallas_call(
        _dense_ffn_kernel,
        grid=(t // FFN_TILE,),
        in_specs=[pl.BlockSpec((FFN_TILE, D_MODEL), lambda i: (i, 0)),
                  pl.BlockSpec((1, D_MODEL), lambda i: (0, 0)),
                  pl.BlockSpec((D_MODEL, dff), lambda i: (0, 0), pipeline_mode=once),
                  pl.BlockSpec((D_MODEL, dff), lambda i: (0, 0), pipeline_mode=once),
                  pl.BlockSpec((dff, D_MODEL), lambda i: (0, 0), pipeline_mode=once)],
        out_specs=pl.BlockSpec((FFN_TILE, D_MODEL), lambda i: (i, 0)),
        out_shape=jax.ShapeDtypeStruct((t, D_MODEL), F32),
        compiler_params=_cp(("parallel",)),
        name="dense_ffn",
    )(res, g, w1, w3, w2)


def _gelu_tanh(x):
    c = math.sqrt(2.0 / math.pi)
    return 0.5 * x * (1.0 + jnp.tanh(c * (x + 0.044715 * (x * x * x))))


def _s5_chunk(lc, get_u, state_ref, ut_ref, bu_ref, ot_ref,
              a_ref, bd_ref, cd_ref, dskip_ref, wglu_ref):
    for b in range(NB):
        ub = get_u(b)
        for s in range(2):
            ut_ref[s, pl.ds(b, lc, stride=NB), :] = ub[:, LANES * s:LANES * (s + 1)]
    u_tm = jnp.concatenate([ut_ref[0], ut_ref[1]], axis=1)
    bu_ref[...] = jnp.dot(u_tm.astype(BF16), bd_ref[...], preferred_element_type=F32)

    half = SSM_COLS // 2
    a_re = a_ref[:, :half]
    a_im = a_ref[:, half:]

    def step(t, x):
        x_re, x_im = x
        r = pl.multiple_of(t * NB, NB)
        cur = bu_ref[pl.ds(r, NB), :]
        n_re = a_re * x_re - a_im * x_im + cur[:, :half]
        n_im = a_re * x_im + a_im * x_re + cur[:, half:]
        bu_ref[pl.ds(r, NB), :] = jnp.concatenate([n_re, n_im], axis=1)
        return n_re, n_im

    x_re, x_im = lax.fori_loop(0, lc, step,
                               (state_ref[:, :half], state_ref[:, half:]), unroll=4)
    state_ref[...] = jnp.concatenate([x_re, x_im], axis=1)

    y = jnp.dot(bu_ref[...].astype(BF16), cd_ref[...], preferred_element_type=F32)
    y = _gelu_tanh(y + dskip_ref[...] * u_tm)
    g = jnp.dot(y.astype(BF16), wglu_ref[...], preferred_element_type=F32)
    o = g[:, :SSM_WIDTH] * jax.nn.sigmoid(g[:, SSM_WIDTH:])
    ot_ref[0] = o[:, :LANES]
    ot_ref[1] = o[:, LANES:]


def _s5_read_out(ot_ref, b, lc):
    return jnp.concatenate(
        [ot_ref[s, pl.ds(b, lc, stride=NB), :] for s in range(2)], axis=1)


def _s5_meta_kernel(u_ref, a_ref, bd_ref, cd_ref, dskip_ref, wglu_ref,
                    o_ref, state_out_ref, state_ref, ut_ref, bu_ref, ot_ref):
    state_ref[...] = jnp.zeros_like(state_ref)
    _s5_chunk(N_META, lambda b: u_ref[b * N_META:(b + 1) * N_META, :],
              state_ref, ut_ref, bu_ref, ot_ref, a_ref, bd_ref, cd_ref, dskip_ref, wglu_ref)
    for b in range(NB):
        o_ref[b * N_META:(b + 1) * N_META, :] = _s5_read_out(ot_ref, b, N_META).astype(BF16)
    state_out_ref[...] = state_ref[...]


def _s5_real_kernel(*refs):
    u_refs = refs[:NB]
    (state_in_ref, a_ref, bd_ref, cd_ref, dskip_ref, wglu_ref, _flat_ref,
     o_ref, state_ref, ut_ref, bu_ref, ot_ref) = refs[NB:]
    c = pl.program_id(0)
    b = pl.program_id(1)

    @pl.when((c == 0) & (b == 0))
    def _():
        state_ref[...] = state_in_ref[...]

    @pl.when(b == 0)
    def _():
        _s5_chunk(S5_CHUNK, lambda bb: u_refs[bb][...],
                  state_ref, ut_ref, bu_ref, ot_ref, a_ref, bd_ref, cd_ref, dskip_ref, wglu_ref)

    o_ref[...] = _s5_read_out(ot_ref, b, S5_CHUNK).astype(BF16)


def _s5(zu, seq, a, bd, cd, dskip, wglu):
    t = zu.shape[0]
    tr = NB * seq
    nmeta_rows = NB * N_META
    const = lambda *_: (0, 0)
    par_specs = [pl.BlockSpec((NB, SSM_COLS), const),
                 pl.BlockSpec((SSM_WIDTH, SSM_COLS), const),
                 pl.BlockSpec((SSM_COLS, SSM_WIDTH), const),
                 pl.BlockSpec((1, SSM_WIDTH), const),
                 pl.BlockSpec((SSM_WIDTH, 2 * SSM_WIDTH), const)]

    def scratch(lc):
        return [pltpu.VMEM((NB, SSM_COLS), F32),
                pltpu.VMEM((2, lc * NB, LANES), F32),
                pltpu.VMEM((lc * NB, SSM_COLS), F32),
                pltpu.VMEM((2, lc * NB, LANES), F32)]

    meta_blk = tr // nmeta_rows
    flat, state = pl.pallas_call(
        _s5_meta_kernel,
        grid=(1,),
        in_specs=[pl.BlockSpec((nmeta_rows, SSM_WIDTH), lambda i: (meta_blk, 0))] + par_specs,
        out_specs=[pl.BlockSpec((nmeta_rows, SSM_WIDTH), lambda i: (meta_blk, 0)),
                   pl.BlockSpec((NB, SSM_COLS), const)],
        out_shape=[jax.ShapeDtypeStruct((t, SSM_WIDTH), BF16),
                   jax.ShapeDtypeStruct((NB, SSM_COLS), F32)],
        scratch_shapes=scratch(N_META),
        compiler_params=_cp(("arbitrary",)),
        name="s5_meta",
    )(zu, a, bd, cd, dskip, wglu)

    nc = seq // S5_CHUNK
    u_specs = [pl.BlockSpec((S5_CHUNK, SSM_WIDTH), lambda c, b, bb=bb: (bb * nc + c, 0))
               for bb in range(NB)]
    n_in = NB + 1 + len(par_specs)
    return pl.pallas_call(
        _s5_real_kernel,
        grid=(nc, NB),
        in_specs=u_specs + [pl.BlockSpec((NB, SSM_COLS), const)] + par_specs
        + [pl.BlockSpec(memory_space=pl.ANY)],
        out_specs=pl.BlockSpec((S5_CHUNK, SSM_WIDTH), lambda c, b: (b * nc + c, 0)),
        out_shape=jax.ShapeDtypeStruct((t, SSM_WIDTH), BF16),
        scratch_shapes=scratch(S5_CHUNK),
        input_output_aliases={n_in: 0},
        compiler_params=_cp(("arbitrary", "arbitrary")),
        name="s5_real",
    )(*([zu] * NB), state, a, bd, cd, dskip, wglu, flat)


def _s5_params(lam_re, lam_im, log_dt, b_re, b_im, c_re, c_im, d_skip):
    dt = jnp.exp(log_dt)[:, None]
    mag = jnp.exp(lam_re * dt)
    ab_re = mag * jnp.cos(lam_im * dt)
    ab_im = mag * jnp.sin(lam_im * dt)
    den = lam_re * lam_re + lam_im * lam_im
    nr = ab_re - 1.0
    ni = ab_im
    coef_re = ((nr * lam_re + ni * lam_im) / den)[..., None]
    coef_im = ((ni * lam_re - nr * lam_im) / den)[..., None]
    bb_re = coef_re * b_re - coef_im * b_im
    bb_im = coef_re * b_im + coef_im * b_re
    eye = jnp.eye(SSM_GROUPS, dtype=F32)
    half = SSM_COLS // 2
    bd = jnp.concatenate(
        [jnp.einsum('gnc,gh->gchn', m, eye).reshape(SSM_WIDTH, half) for m in (bb_re, bb_im)],
        axis=1).astype(BF16)
    cd = jnp.concatenate(
        [jnp.einsum('gcn,gh->gnhc', m, eye).reshape(half, SSM_WIDTH) for m in (c_re, -c_im)],
        axis=0).astype(BF16)
    a = jnp.concatenate([ab_re.reshape(1, half), ab_im.reshape(1, half)], axis=1)
    a = jnp.broadcast_to(a, (NB, SSM_COLS))
    return a, bd, cd, d_skip.reshape(1, SSM_WIDTH)


def _nt_dot(a, b):
    return lax.dot_general(a, b, (((1,), (1,)), ((), ())), preferred_element_type=F32)


def _osm_init(s, vt):
    m = jnp.max(s, axis=1, keepdims=True)
    p = jnp.exp2(s - m)
    l = jnp.sum(p, axis=1, keepdims=True)
    acc = _nt_dot(p.astype(BF16), vt)
    return m, l, acc


def _causal_sweep(qts, k_ref, k_cols, vt_ref, vt_rows, kms, vmts, qi,
                  m_ref, l_ref, acc_ref, al_ref, p_ref):
    n = len(qts)
    meta_scores = [jnp.dot(kms[i], qts[i], preferred_element_type=F32) for i in range(n)]
    meta_p = []
    for i in range(n):
        s = meta_scores[i]
        m = jnp.max(s, axis=0, keepdims=True)
        p = jnp.exp2(s - m)
        m_ref[i][...] = m
        l_ref[i][...] = jnp.sum(p, axis=0, keepdims=True)
        meta_p.append(p.astype(BF16))
        al_ref[i][...] = jnp.ones_like(al_ref[i])
        p_ref[i][...] = jnp.zeros_like(p_ref[i])
    for i in range(n):
        acc_ref[i][...] = jnp.dot(vmts[i], meta_p[i], preferred_element_type=F32)

    sub = ATT_TK // LANES

    def pending_pv(jp):
        for i in range(n):
            vtb = jnp.concatenate([vt_ref[jp * sub + c, vt_rows[i], :] for c in range(sub)],
                                  axis=1)
            acc_ref[i][...] = al_ref[i][...] * acc_ref[i][...] + jnp.dot(
                vtb, p_ref[i][...], preferred_element_type=F32)

    def chunk(j, masked):
        off = pl.multiple_of(j * ATT_TK, ATT_TK)
        if masked:
            krow = lax.broadcasted_iota(jnp.int32, (ATT_TK, ATT_TQ), 0)
            qcol = lax.broadcasted_iota(jnp.int32, (ATT_TK, ATT_TQ), 1)
            visible = krow <= qcol
        scores = []
        for i in range(n):
            kb = k_ref[pl.ds(off, ATT_TK), k_cols[i]]
            s = jnp.dot(kb, qts[i], preferred_element_type=F32)
            scores.append(jnp.where(visible, s, MASK_VALUE) if masked else s)
        pending_pv(jnp.maximum(j - 1, 0))
        for i in range(n):
            m_old = m_ref[i][...]
            m_new = jnp.maximum(m_old, jnp.max(scores[i], axis=0, keepdims=True))
            alpha = jnp.exp2(m_old - m_new)
            p = jnp.exp2(scores[i] - m_new)
            l_ref[i][...] = alpha * l_ref[i][...] + jnp.sum(p, axis=0, keepdims=True)
            m_ref[i][...] = m_new
            al_ref[i][...] = alpha
            p_ref[i][...] = p.astype(BF16)

    def body(j, carry):
        chunk(j, False)
        return carry

    lax.fori_loop(0, qi, body, 0)
    chunk(qi, True)
    pending_pv(qi)


def _meta_attend(q, km, vmt):
    n = km.shape[0]
    row = lax.broadcasted_iota(jnp.int32, (n, n), 0)
    col = lax.broadcasted_iota(jnp.int32, (n, n), 1)
    s = jnp.where(col <= row, _nt_dot(q, km), MASK_VALUE)
    _, l, acc = _osm_init(s, vmt)
    return l, acc


def _diff_split(q):
    lane = lax.broadcasted_iota(jnp.int32, q.shape, 1)
    q = (q.astype(F32) * QK_SCALE).astype(BF16)
    zero = jnp.zeros_like(q)
    return jnp.where(lane < HEAD_DIM, q, zero), jnp.where(lane >= HEAD_DIM, q, zero)


def _diff_finish(parts, lam, lam_init, g):
    (l1, acc1), (l2, acc2) = parts
    o = acc1 / l1 - lam * (acc2 / l2)
    return (_rms(o, g) * (1.0 - lam_init)).astype(BF16)


def _transpose_bf16(x):
    return x.astype(F32).T.astype(BF16)


def _map_scratch(n_maps, dv):
    return ([pltpu.VMEM((1, ATT_TQ), F32)] * (2 * n_maps)
            + [pltpu.VMEM((dv, ATT_TQ), F32)] * n_maps
            + [pltpu.VMEM((1, ATT_TQ), F32)] * n_maps
            + [pltpu.VMEM((ATT_TK, ATT_TQ), BF16)] * n_maps)


def _split_map_scratch(scratch, n_maps):
    return [scratch[k * n_maps:(k + 1) * n_maps] for k in range(5)]


def _diff_kernel(lam_init, lam_ref, q_ref, k_ref, vt_ref, km_ref, vmt_ref, g_ref, o_ref,
                 *scratch):
    m_ref, l_ref, acc_ref, al_ref, p_ref = _split_map_scratch(scratch, 2 * DIFF_HEADS)
    qts, k_cols, vt_rows, kms, vmts = [], [], [], [], []
    row = lax.broadcasted_iota(jnp.int32, (LANES, ATT_TQ), 0)
    for h in range(DIFF_HEADS):
        sl = slice(LANES * h, LANES * (h + 1))
        km = km_ref[:, sl]
        vmt = vmt_ref[sl, :]
        qt = (q_ref[:, sl].astype(F32) * QK_SCALE).T.astype(BF16)
        zero = jnp.zeros_like(qt)
        for qm in (jnp.where(row < HEAD_DIM, qt, zero), jnp.where(row >= HEAD_DIM, qt, zero)):
            qts.append(qm)
            k_cols.append(sl)
            vt_rows.append(sl)
            kms.append(km)
            vmts.append(vmt)
    _causal_sweep(qts, k_ref, k_cols, vt_ref, vt_rows, kms, vmts, pl.program_id(1),
                  m_ref, l_ref, acc_ref, al_ref, p_ref)
    lam = lam_ref[0]
    for h in range(DIFF_HEADS):
        o = (acc_ref[2 * h][...] / l_ref[2 * h][...]
             - lam * (acc_ref[2 * h + 1][...] / l_ref[2 * h + 1][...]))
        y = o * lax.rsqrt(jnp.mean(o * o, axis=0, keepdims=True) + EPS)
        o_ref[:, LANES * h:LANES * (h + 1)] = (
            y.T * g_ref[...] * (1.0 - lam_init)).astype(BF16)


def _diff_meta_kernel(lam_init, lam_ref, q_ref, km_ref, vmt_ref, g_ref, _flat_ref, o_ref):
    for h in range(DIFF_HEADS):
        sl = slice(LANES * h, LANES * (h + 1))
        q1, q2 = _diff_split(q_ref[:, sl])
        km = km_ref[:, sl]
        vmt = vmt_ref[sl, :]
        parts = [_meta_attend(q1, km, vmt), _meta_attend(q2, km, vmt)]
        o_ref[:, sl] = _diff_finish(parts, lam_ref[0], lam_init, g_ref[...])


def _diff_attention(za, vt, vtm, seq, lam, lam_init, g):
    t = za.shape[0]
    nq = seq // ATT_TQ
    mrow = NB * seq // N_META
    smem = pl.BlockSpec(memory_space=pltpu.SMEM)
    lam = lam.reshape(1).astype(F32)
    g = g.reshape(1, LANES)
    gspec = pl.BlockSpec((1, LANES), lambda *_: (0, 0))
    width = DIFF_HEADS * LANES
    n_maps = 2 * DIFF_HEADS
    flat = pl.pallas_call(
        functools.partial(_diff_kernel, lam_init),
        grid=(NB, nq),
        in_specs=[smem,
                  pl.BlockSpec((ATT_TQ, width), lambda b, i: (b * nq + i, 0)),
                  pl.BlockSpec((seq, width), lambda b, i: (b, 1)),
                  pl.BlockSpec((seq // LANES, width, LANES), lambda b, i: (b, 0, 0)),
                  pl.BlockSpec((N_META, width), lambda b, i: (mrow + b, 1)),
                  pl.BlockSpec((None, width, N_META), lambda b, i: (b, 0, 0)),
                  gspec],
        out_specs=pl.BlockSpec((ATT_TQ, width), lambda b, i: (b * nq + i, 0)),
        out_shape=jax.ShapeDtypeStruct((t, width), BF16),
        scratch_shapes=_map_scratch(n_maps, LANES),
        compiler_params=_cp(("parallel", "arbitrary")),
        name="diff_attn",
    )(lam, za, za, vt, za, vtm, g)
    return pl.pallas_call(
        functools.partial(_diff_meta_kernel, lam_init),
        grid=(NB,),
        in_specs=[smem,
                  pl.BlockSpec((N_META, width), lambda b: (mrow + b, 0)),
                  pl.BlockSpec((N_META, width), lambda b: (mrow + b, 1)),
                  pl.BlockSpec((None, width, N_META), lambda b: (b, 0, 0)),
                  gspec,
                  pl.BlockSpec(memory_space=pl.ANY)],
        out_specs=pl.BlockSpec((N_META, width), lambda b: (mrow + b, 0)),
        out_shape=jax.ShapeDtypeStruct((t, width), BF16),
        input_output_aliases={5: 0},
        compiler_params=_cp(("parallel",)),
        name="diff_attn_meta",
    )(lam, za, za, vtm, g, flat)


def _split3(c):
    hi = c.astype(BF16).astype(F32)
    r1 = c - hi
    mid = r1.astype(BF16).astype(F32)
    lo = (r1 - mid).astype(BF16).astype(F32)
    return hi, mid, lo


def _cumsum_rows(tri, lf):
    parts = jnp.concatenate(_split3(lf), axis=1).astype(BF16)
    r = jnp.dot(tri, parts, preferred_element_type=F32)
    return r[:, :LANES] + r[:, LANES:2 * LANES] + r[:, 2 * LANES:]


def _log_sigmoid(x):
    return jnp.minimum(x, 0.0) - jnp.log1p(jnp.exp(-jnp.abs(x)))


def _fox_augment(fq, fk, cum, qa_ref, ka_ref):
    n = fq.shape[0]
    lane = lax.broadcasted_iota(jnp.int32, (n, LANES), 1)
    for h in range(FOX_HEADS):
        pair = slice(LANES * (h // 2), LANES * (h // 2) + LANES)
        own = (lane // HEAD_DIM) == (h % 2)
        e0 = HEAD_DIM * (1 - h % 2)
        hi, mid, lo = _split3(jnp.broadcast_to(cum[:, h:h + 1] * LOG2E, (n, LANES)))
        ones = (lane >= e0 + 3) & (lane < e0 + 6)
        q_extra = jnp.where(lane == e0, hi, jnp.where(lane == e0 + 1, mid, jnp.where(
            lane == e0 + 2, lo, jnp.where(ones, 1.0, 0.0))))
        ones = (lane >= e0) & (lane < e0 + 3)
        k_extra = jnp.where(lane == e0 + 3, -hi, jnp.where(lane == e0 + 4, -mid, jnp.where(
            lane == e0 + 5, -lo, jnp.where(ones, 1.0, 0.0))))
        q = fq[:, pair].astype(F32) * QK_SCALE
        k = fk[:, pair].astype(F32)
        qa_ref[:, LANES * h:LANES * (h + 1)] = jnp.where(own, q, q_extra).astype(BF16)
        ka_ref[:, LANES * h:LANES * (h + 1)] = jnp.where(own, k, k_extra).astype(BF16)


def _fox_prep_meta_kernel(zg_ref, fq_ref, fk_ref, fb_ref, qa_ref, ka_ref, carry_ref):
    n = NB * N_META
    lf = _log_sigmoid(zg_ref[...] + fb_ref[...])
    row = lax.broadcasted_iota(jnp.int32, (n, n), 0)
    col = lax.broadcasted_iota(jnp.int32, (n, n), 1)
    tri = ((col <= row) & (col // N_META == row // N_META)).astype(BF16)
    cum = _cumsum_rows(tri, lf)
    brow = lax.broadcasted_iota(jnp.int32, (NB, n), 0)
    bcol = lax.broadcasted_iota(jnp.int32, (NB, n), 1)
    carry_ref[...] = _cumsum_rows((bcol // N_META == brow).astype(BF16), lf)
    _fox_augment(fq_ref[...], fk_ref[...], cum, qa_ref, ka_ref)


def _fox_prep_kernel(zg_ref, fq_ref, fk_ref, fb_ref, carry_in_ref, qa_ref, ka_ref, carry_ref):
    b = pl.program_id(0)
    n = zg_ref.shape[0]

    @pl.when(pl.program_id(1) == 0)
    def _():
        carry_ref[...] = carry_in_ref[pl.ds(b, 1), :]

    lf = _log_sigmoid(zg_ref[...] + fb_ref[...])
    row = lax.broadcasted_iota(jnp.int32, (n, n), 0)
    col = lax.broadcasted_iota(jnp.int32, (n, n), 1)
    cum = _cumsum_rows((col <= row).astype(BF16), lf) + carry_ref[...]
    carry_ref[...] = cum[n - 1:n, :]
    _fox_augment(fq_ref[...], fk_ref[...], cum, qa_ref, ka_ref)


FOX_PREP_TILE = 1024


def _fox_prep(zg, za, seq, fb):
    tr = NB * seq
    nm = NB * N_META
    mblk = tr // nm
    fb = jnp.pad(fb.astype(F32), (0, LANES - FOX_HEADS)).reshape(1, LANES)
    fbspec = pl.BlockSpec((1, LANES), lambda *_: (0, 0))
    aug = FOX_HEADS * LANES
    qa_m, ka_m, carry = pl.pallas_call(
        _fox_prep_meta_kernel,
        grid=(1,),
        in_specs=[pl.BlockSpec((nm, LANES), lambda i: (mblk, 0)),
                  pl.BlockSpec((nm, 256), lambda i: (mblk, 4)),
                  pl.BlockSpec((nm, 256), lambda i: (mblk, 5)),
                  fbspec],
        out_specs=[pl.BlockSpec((nm, aug), lambda i: (0, 0)),
                   pl.BlockSpec((nm, aug), lambda i: (0, 0)),
                   pl.BlockSpec((NB, LANES), lambda i: (0, 0))],
        out_shape=[jax.ShapeDtypeStruct((nm, aug), BF16),
                   jax.ShapeDtypeStruct((nm, aug), BF16),
                   jax.ShapeDtypeStruct((NB, LANES), F32)],
        compiler_params=_cp(("arbitrary",)),
        name="fox_prep_meta",
    )(zg, za, za, fb)
    nc = seq // FOX_PREP_TILE
    qa, ka = pl.pallas_call(
        _fox_prep_kernel,
        grid=(NB, nc),
        in_specs=[pl.BlockSpec((FOX_PREP_TILE, LANES), lambda b, c: (b * nc + c, 0)),
                  pl.BlockSpec((FOX_PREP_TILE, 256), lambda b, c: (b * nc + c, 4)),
                  pl.BlockSpec((FOX_PREP_TILE, 256), lambda b, c: (b * nc + c, 5)),
                  fbspec,
                  pl.BlockSpec((NB, LANES), lambda b, c: (0, 0))],
        out_specs=[pl.BlockSpec((FOX_PREP_TILE, aug), lambda b, c: (b * nc + c, 0)),
                   pl.BlockSpec((FOX_PREP_TILE, aug), lambda b, c: (b * nc + c, 0))],
        out_shape=[jax.ShapeDtypeStruct((tr, aug), BF16),
                   jax.ShapeDtypeStruct((tr, aug), BF16)],
        scratch_shapes=[pltpu.VMEM((1, LANES), F32)],
        compiler_params=_cp(("parallel", "arbitrary")),
        name="fox_prep",
    )(zg, za, za, fb, carry)
    return qa, ka, qa_m, ka_m


def _fox_finish(parts):
    (l0, acc0), (l1, acc1) = parts
    lane = lax.broadcasted_iota(jnp.int32, acc0.shape, 1)
    return jnp.where(lane < HEAD_DIM, acc0 / l0, acc1 / l1).astype(BF16)


def _fox_kernel(q_ref, k_ref, vt_ref, km_ref, vmt_ref, o_ref, *scratch):
    m_ref, l_ref, acc_ref, al_ref, p_ref = _split_map_scratch(scratch, FOX_HEADS)
    hd = HEAD_DIM
    vmt = vmt_ref[...]
    qts, k_cols, vt_rows, kms, vmts = [], [], [], [], []
    for h in range(FOX_HEADS):
        sl = slice(LANES * h, LANES * (h + 1))
        qts.append(_transpose_bf16(q_ref[:, sl]))
        k_cols.append(sl)
        vt_rows.append(slice(hd * h, hd * (h + 1)))
        kms.append(km_ref[:, sl])
        vmts.append(vmt[hd * h:hd * (h + 1), :])
    _causal_sweep(qts, k_ref, k_cols, vt_ref, vt_rows, kms, vmts, pl.program_id(1),
                  m_ref, l_ref, acc_ref, al_ref, p_ref)
    for p in range(FOX_HEADS // 2):
        o = jnp.concatenate([acc_ref[2 * p][...] / l_ref[2 * p][...],
                             acc_ref[2 * p + 1][...] / l_ref[2 * p + 1][...]],
                            axis=0)
        o_ref[:, LANES * p:LANES * (p + 1)] = o.T.astype(BF16)


def _fox_meta_kernel(q_ref, km_ref, vmt_ref, _flat_ref, o_ref):
    for p in range(FOX_HEADS // 2):
        vmt = vmt_ref[LANES * p:LANES * (p + 1), :]
        parts = []
        for h in (2 * p, 2 * p + 1):
            sl = slice(LANES * h, LANES * (h + 1))
            parts.append(_meta_attend(q_ref[:, sl], km_ref[:, sl], vmt))
        o_ref[:, LANES * p:LANES * (p + 1)] = _fox_finish(parts)


def _fox_attention(t, vt, vtm, seq, qa, ka, qa_m, ka_m):
    nq = seq // ATT_TQ
    mrow = NB * seq // N_META
    aug = FOX_HEADS * LANES
    flat = pl.pallas_call(
        _fox_kernel,
        grid=(NB, nq),
        in_specs=[pl.BlockSpec((ATT_TQ, aug), lambda b, i: (b * nq + i, 0)),
                  pl.BlockSpec((seq, aug), lambda b, i: (b, 0)),
                  pl.BlockSpec((seq // LANES, 256, LANES), lambda b, i: (b, 2, 0)),
                  pl.BlockSpec((N_META, aug), lambda b, i: (b, 0)),
                  pl.BlockSpec((None, 256, N_META), lambda b, i: (b, 2, 0))],
        out_specs=pl.BlockSpec((ATT_TQ, 256), lambda b, i: (b * nq + i, 0)),
        out_shape=jax.ShapeDtypeStruct((t, 256), BF16),
        scratch_shapes=_map_scratch(FOX_HEADS, HEAD_DIM),
        compiler_params=_cp(("parallel", "arbitrary")),
        name="fox_attn",
    )(qa, ka, vt, ka_m, vtm)
    return pl.pallas_call(
        _fox_meta_kernel,
        grid=(NB,),
        in_specs=[pl.BlockSpec((N_META, aug), lambda b: (b, 0)),
                  pl.BlockSpec((N_META, aug), lambda b: (b, 0)),
                  pl.BlockSpec((None, 256, N_META), lambda b: (b, 2, 0)),
                  pl.BlockSpec(memory_space=pl.ANY)],
        out_specs=pl.BlockSpec((N_META, 256), lambda b: (mrow + b, 0)),
        out_shape=jax.ShapeDtypeStruct((t, 256), BF16),
        input_output_aliases={3: 0},
        compiler_params=_cp(("parallel",)),
        name="fox_attn_meta",
    )(qa_m, ka_m, vtm, flat)


def _router_kernel(res_ref, g_ref, wr_ref, idx_ref, gate_ref):
    h = _rms(res_ref[...], g_ref[...])
    w = wr_ref[...]
    hi = h.astype(BF16)
    lo = (h - hi.astype(F32)).astype(BF16)
    whi = w.astype(BF16)
    wlo = (w - whi.astype(F32)).astype(BF16)
    logits = (jnp.dot(hi, whi, preferred_element_type=F32)
              + jnp.dot(lo, whi, preferred_element_type=F32)
              + jnp.dot(hi, wlo, preferred_element_type=F32))
    lane = lax.broadcasted_iota(jnp.int32, logits.shape, 1)
    logits = jnp.where(lane < N_EXPERTS, logits, -jnp.inf)
    m1 = jnp.max(logits, axis=1, keepdims=True)
    i1 = jnp.min(jnp.where(logits == m1, lane, LANES), axis=1, keepdims=True)
    rest = jnp.where(lane == i1, -jnp.inf, logits)
    m2 = jnp.max(rest, axis=1, keepdims=True)
    i2 = jnp.min(jnp.where(rest == m2, lane, LANES), axis=1, keepdims=True)
    e = jnp.exp(m2 - m1)
    g1 = 1.0 / (1.0 + e)
    g2 = e / (1.0 + e)
    idx_ref[...] = jnp.where(lane == 0, i1, jnp.where(lane == 1, i2, 0))
    gate_ref[...] = jnp.where(lane == 0, g1, jnp.where(lane == 1, g2, 0.0))


def _router(res, g, wr):
    t = res.shape[0]
    return pl.pallas_call(
        _router_kernel,
        grid=(t // TOK_TILE,),
        in_specs=[pl.BlockSpec((TOK_TILE, D_MODEL), lambda i: (i, 0)),
                  pl.BlockSpec((1, D_MODEL), lambda i: (0, 0)),
                  pl.BlockSpec((D_MODEL, LANES), lambda i: (0, 0))],
        out_specs=[pl.BlockSpec((TOK_TILE, LANES), lambda i: (i, 0)),
                   pl.BlockSpec((TOK_TILE, LANES), lambda i: (i, 0))],
        out_shape=[jax.ShapeDtypeStruct((t, LANES), jnp.int32),
                   jax.ShapeDtypeStruct((t, LANES), F32)],
        compiler_params=_cp(("parallel",)),
        name="router",
    )(res, g, wr)


def _gather_copy(src_hbm, first, dst_ref, r, sem):
    first = pl.multiple_of(first, ROW_BLOCKS)
    dst = pl.multiple_of(r * ROW_BLOCKS, ROW_BLOCKS)
    return pltpu.make_async_copy(src_hbm.at[pl.ds(first, ROW_BLOCKS), :],
                                 dst_ref.at[pl.ds(dst, ROW_BLOCKS), :], sem)


def _expert_ffn_kernel(nf, te_ref, na_ref, nb_ref, src_ref, x_hbm, g_ref, w1_ref, w3_ref,
                       w2_ref, o_ref, xbuf_ref, h_ref, acc_ref, sem):
    r = pl.program_id(0)
    f = pl.program_id(1)
    n_act = na_ref[0]
    n_sub = MOE_TILE // MOE_SUB
    per_step = MOE_XROWS // nf
    last_tile = pl.num_programs(0) - 1

    def tile_copy(slot):
        return pltpu.make_async_copy(x_hbm.at[pl.ds(0, MOE_XROWS * ROW_BLOCKS), :],
                                     xbuf_ref.at[slot], sem.at[slot])

    def start_row(tile, slot, k):
        _gather_copy(x_hbm, src_ref[tile * MOE_XROWS + k], xbuf_ref.at[slot], k,
                     sem.at[slot]).start()

    @pl.when(r < n_act)
    def _():
        slot = lax.rem(r, MOE_XSLOTS)
        slot1 = lax.rem(r + 1, MOE_XSLOTS)
        slot2 = lax.rem(r + 2, MOE_XSLOTS)

        @pl.when((r == 0) & (f == 0))
        def _():
            def first(k, _):
                start_row(0, 0, k)
                start_row(jnp.minimum(1, last_tile), 1, k)
                return 0
            lax.fori_loop(0, MOE_XROWS, first, 0, unroll=8)

        @pl.when(f == 0)
        def _():
            tile_copy(slot).wait()
            x = _load_row_tiles(xbuf_ref.at[slot], MOE_TILE)
            h_ref[...] = _rms(x, g_ref[...]).astype(BF16)
            acc_ref[...] = jnp.zeros_like(acc_ref)

        ahead = jnp.minimum(r + 2, last_tile)

        def start_ahead():
            for k in range(per_step):
                start_row(ahead, slot2, f * per_step + k)

        def swiglu_rows(rows, w1, w3, w2):
            h = h_ref[rows, :]
            a = jnp.dot(h, w1, preferred_element_type=F32)
            b = jnp.dot(h, w3, preferred_element_type=F32)
            hh = (a * jax.nn.sigmoid(a) * b).astype(BF16)
            acc_ref[rows, :] += jnp.dot(hh, w2, preferred_element_type=F32)

        def weights():
            return (w1_ref[...].astype(BF16), w3_ref[...].astype(BF16),
                    w2_ref[...].astype(BF16))

        n_blocks = nb_ref[r]

        @pl.when(n_blocks == n_sub)
        def _():
            start_ahead()
            swiglu_rows(slice(None), *weights())

        @pl.when(n_blocks < n_sub)
        def _():
            start_ahead()
            w = weights()
            swiglu_rows(slice(0, MOE_SUB), *w)
            for sb in range(1, n_sub - 1):
                @pl.when(sb < n_blocks)
                def _():
                    swiglu_rows(slice(MOE_SUB * sb, MOE_SUB * (sb + 1)), *w)

        @pl.when(f == nf - 1)
        def _():
            _store_row_tiles(o_ref, acc_ref[...])

        @pl.when((f == nf - 1) & (r == n_act - 1))
        def _():
            tile_copy(slot1).wait()
            tile_copy(slot2).wait()


def _expert_ffn(tile_expert, n_active, tile_blocks, src, x, g, w1, w3, w2):
    dff = w1.shape[2]
    nt = src.shape[0] // MOE_XROWS
    nf = dff // MOE_FCHUNK
    assert MOE_XROWS % nf == 0 and MOE_XROWS >= MOE_TILE

    def row(r, f, te, na, *_):
        return jnp.minimum(r, na[0] - 1)

    def fch(r, f, te, na, *_):
        return jnp.where(r < na[0], f, nf - 1)

    return pl.pallas_call(
        functools.partial(_expert_ffn_kernel, nf),
        grid_spec=pltpu.PrefetchScalarGridSpec(
            num_scalar_prefetch=4,
            grid=(nt, nf),
            in_specs=[
                pl.BlockSpec(memory_space=pl.ANY),
                pl.BlockSpec((1, D_MODEL), lambda r, f, *_: (0, 0)),
                pl.BlockSpec((None, D_MODEL, MOE_FCHUNK),
                             lambda r, f, te, na, *_: (te[r], 0, fch(r, f, te, na))),
                pl.BlockSpec((None, D_MODEL, MOE_FCHUNK),
                             lambda r, f, te, na, *_: (te[r], 0, fch(r, f, te, na))),
                pl.BlockSpec((None, MOE_FCHUNK, D_MODEL),
                             lambda r, f, te, na, *_: (te[r], fch(r, f, te, na), 0))],
            out_specs=pl.BlockSpec((MOE_TILE * ROW_BLOCKS, LANES),
                                   lambda r, f, te, na, *_: (row(r, f, te, na), 0)),
            scratch_shapes=[pltpu.VMEM((MOE_XSLOTS, MOE_XROWS * ROW_BLOCKS, LANES), F32),
                            pltpu.VMEM((MOE_TILE, D_MODEL), BF16),
                            pltpu.VMEM((MOE_TILE, D_MODEL), F32),
                            pltpu.SemaphoreType.DMA((MOE_XSLOTS,))]),
        out_shape=jax.ShapeDtypeStruct((nt * MOE_TILE * ROW_BLOCKS, LANES), F32),
        compiler_params=_cp(("arbitrary", "arbitrary")),
        name="expert_ffn",
    )(tile_expert, n_active, tile_blocks, src, x, g, w1, w3, w2)


def _combine_kernel(tile, final, pos_ref, res_ref, gate_ref, gf_ref, y_hbm, o_ref, ybuf_ref, sem):
    i = pl.program_id(0)
    n = pl.num_programs(0)
    slot = lax.rem(i, 2)

    def issue_tile(step, s):
        base = step * tile

        def issue(r, _):
            for k in range(2):
                _gather_copy(y_hbm, pos_ref[2 * (base + r) + k], ybuf_ref.at[s, k], r,
                             sem.at[s]).start(priority=k)
            return 0

        lax.fori_loop(0, tile, issue, 0, unroll=8)

    @pl.when(i == 0)
    def _():
        issue_tile(0, 0)

    @pl.when(i + 1 < n)
    def _():
        issue_tile(i + 1, 1 - slot)

    pltpu.make_async_copy(ybuf_ref.at[slot], ybuf_ref.at[slot], sem.at[slot]).wait()
    gate = gate_ref[...]
    out = (res_ref[...] + gate[:, 0:1] * _load_row_tiles(ybuf_ref.at[slot, 0], tile)
           + gate[:, 1:2] * _load_row_tiles(ybuf_ref.at[slot, 1], tile))
    o_ref[...] = _rms(out, gf_ref[...]) if final else out


def _combine(pos, res, gates, y, final_g=None):
    final = final_g is not None
    t = res.shape[0]
    n_rows = t - NB * N_META if final else t
    tile = 256 if final else COMBINE_TILE
    if not final:
        final_g = jnp.ones((1, D_MODEL), F32)
    return pl.pallas_call(
        functools.partial(_combine_kernel, tile, final),
        grid_spec=pltpu.PrefetchScalarGridSpec(
            num_scalar_prefetch=1,
            grid=(n_rows // tile,),
            in_specs=[pl.BlockSpec((tile, D_MODEL), lambda i, pos: (i, 0)),
                      pl.BlockSpec((tile, LANES), lambda i, pos: (i, 0)),
                      pl.BlockSpec((1, D_MODEL), lambda i, pos: (0, 0)),
                      pl.BlockSpec(memory_space=pl.ANY)],
            out_specs=pl.BlockSpec((tile, D_MODEL), lambda i, pos: (i, 0)),
            scratch_shapes=[pltpu.VMEM((2, 2, tile * ROW_BLOCKS, LANES), F32),
                            pltpu.SemaphoreType.DMA((2,))]),
        out_shape=jax.ShapeDtypeStruct((n_rows, D_MODEL), F32),
        compiler_params=_cp(("arbitrary",)),
        name="moe_combine",
    )(pos, res, gates, final_g, y)


def _invert_kernel(spos_ref, src_ref):
    def clear(i, _):
        src_ref[i] = 0
        return 0

    lax.fori_loop(0, src_ref.shape[0], clear, 0, unroll=8)

    def body(a, _):
        src_ref[spos_ref[a]] = lax.shift_right_logical(a, 1) * ROW_BLOCKS
        return 0

    lax.fori_loop(0, spos_ref.shape[0], body, 0, unroll=8)


def _invert_positions(spos, n):
    smem = pl.BlockSpec(memory_space=pltpu.SMEM)
    return pl.pallas_call(
        _invert_kernel,
        in_specs=[smem],
        out_specs=smem,
        out_shape=jax.ShapeDtypeStruct((n,), jnp.int32),
        name="moe_invert",
    )(spos)


def _moe(res, res_tiles, g, wr, w1, w3, w2, final_g=None):
    t = res.shape[0]
    wr = jnp.pad(wr.astype(F32), ((0, 0), (0, LANES - N_EXPERTS)))
    idx, gates = _router(res, g, wr)
    e_flat = idx[:, :2].reshape(-1)
    onehot = (e_flat[:, None] == jnp.arange(N_EXPERTS, dtype=jnp.int32)[None, :]).astype(jnp.int32)
    csum = jnp.cumsum(onehot, axis=0)
    rank = jnp.take_along_axis(csum, e_flat[:, None], axis=1)[:, 0] - 1
    counts = csum[-1]
    tiles = (counts + MOE_TILE - 1) // MOE_TILE
    tile_end = jnp.cumsum(tiles)
    starts = (tile_end - tiles) * MOE_TILE
    pos = (starts[e_flat] + rank).astype(jnp.int32)
    n_tiles = (2 * t + N_EXPERTS * (MOE_TILE - 1)) // MOE_TILE
    spos = (pos // MOE_TILE) * MOE_XROWS + pos % MOE_TILE
    src = _invert_positions(spos, n_tiles * MOE_XROWS)
    n_active = tile_end[-1:].astype(jnp.int32)
    tile_ids = jnp.minimum(jnp.arange(n_tiles, dtype=jnp.int32), n_active[0] - 1)
    tile_expert = jnp.sum(tile_ids[:, None] >= tile_end[None, :], axis=1).astype(jnp.int32)
    first_tile = (tile_end - tiles)[tile_expert]
    tile_rows = jnp.clip(counts[tile_expert] - (tile_ids - first_tile) * MOE_TILE, 1, MOE_TILE)
    tile_blocks = ((tile_rows + MOE_SUB - 1) // MOE_SUB).astype(jnp.int32)

    y = _expert_ffn(tile_expert, n_active, tile_blocks, src, res_tiles, g, w1, w3, w2)
    return _combine(pos * ROW_BLOCKS, res, gates, y, final_g)


def _final_norm_kernel(res_ref, g_ref, o_ref):
    o_ref[...] = _rms(res_ref[...], g_ref[...])


def _final_norm(res, g, n_rows):
    tile = 1024
    return pl.pallas_call(
        _final_norm_kernel,
        grid=(n_rows // tile,),
        in_specs=[pl.BlockSpec((tile, D_MODEL), lambda i: (i, 0)),
                  pl.BlockSpec((1, D_MODEL), lambda i: (0, 0))],
        out_specs=pl.BlockSpec((tile, D_MODEL), lambda i: (i, 0)),
        out_shape=jax.ShapeDtypeStruct((n_rows, D_MODEL), F32),
        compiler_params=_cp(("parallel",)),
        name="final_norm",
    )(res, g)


def kernel(x, meta_tokens, norm_mix_g, w_in, w_out, ssm_lambda_re, ssm_lambda_im, ssm_log_dt,
           ssm_b_re, ssm_b_im, ssm_c_re, ssm_c_im, ssm_d, ssm_w_glu, diff_lambda_q1,
           diff_lambda_k1, diff_lambda_q2, diff_lambda_k2, diff_subln_g, fox_forget_b,
           norm_ffn_g, dense_w1, dense_w3, dense_w2, moe_router, moe_w1, moe_w3, moe_w2,
           final_norm_g):
    bsz, seq, d = x.shape
    assert bsz == NB and d == D_MODEL and seq % ATT_TQ == 0
    depth = w_in.shape[0]
    tr = bsz * seq
    res = None
    row = lambda v: v.reshape(1, -1).astype(F32)

    for l in range(depth):
        wl = w_in[l]
        w = jnp.concatenate([wl[:, :COL_DV], wl[:, COL_FQ:COL_FV],
                             jnp.pad(wl[:, COL_GATE:], ((0, 0), (0, LANES - FOX_HEADS)))], axis=1)
        wvt = jnp.concatenate([wl[:, COL_DV:COL_FQ], wl[:, COL_FV:COL_GATE]], axis=1).T
        if l == 0:
            res, zu, za, zg, vt, vtm = _inproj(
                None, row(norm_mix_g[l]), w.astype(BF16), wvt.astype(BF16),
                embed=(x.reshape(tr, d), meta_tokens.astype(x.dtype)))
        else:
            zu, za, zg, vt, vtm = _inproj(res, row(norm_mix_g[l]), w.astype(BF16),
                                          wvt.astype(BF16))

        a, bd, cd, dskip = _s5_params(ssm_lambda_re[l], ssm_lambda_im[l], ssm_log_dt[l],
                                      ssm_b_re[l], ssm_b_im[l], ssm_c_re[l], ssm_c_im[l],
                                      ssm_d[l])
        ssm_out = _s5(zu, seq, a, bd, cd, dskip, ssm_w_glu[l].astype(BF16))

        lam_init = 0.8 - 0.6 * math.exp(-0.3 * l)
        lam = (jnp.exp(jnp.sum(diff_lambda_q1[l] * diff_lambda_k1[l]))
               - jnp.exp(jnp.sum(diff_lambda_q2[l] * diff_lambda_k2[l])) + lam_init)
        diff_out = _diff_attention(za, vt, vtm, seq, lam, lam_init, diff_subln_g[l])

        qa, ka, qa_m, ka_m = _fox_prep(zg, za, seq, fox_forget_b[l])
        fox_out = _fox_attention(za.shape[0], vt, vtm, seq, qa, ka, qa_m, ka_m)

        is_moe = l % 2 == 1
        res, *res_tiles = _outproj(res, ssm_out, diff_out, fox_out, w_out[l].astype(BF16),
                                   with_row_tiles=is_moe)

        if not is_moe:
            res = _dense_ffn(res, row(norm_ffn_g[l]), dense_w1[l // 2].astype(BF16),
                             dense_w3[l // 2].astype(BF16), dense_w2[l // 2].astype(BF16))
        else:
            res = _moe(res, res_tiles[0], row(norm_ffn_g[l]), moe_router[l // 2], moe_w1[l // 2],
                       moe_w3[l // 2], moe_w2[l // 2],
                       final_g=row(final_norm_g) if l == depth - 1 else None)

    if depth % 2 == 1:
        res = _final_norm(res, row(final_norm_g), tr)
    return res.reshape(bsz, seq, d)
```

```python
import functools
import math

import jax
import jax.numpy as jnp
from jax import lax
from jax.experimental import pallas as pl
from jax.experimental.pallas import tpu as pltpu

F32 = jnp.float32
BF16 = jnp.bfloat16
EPS = 1e-6

D_MODEL = 1024
N_META = 16
NB = 8
SSM_WIDTH = 256
SSM_GROUPS = 16
SSM_GROUP = 16
SSM_STATE = 64
SSM_COLS = 2 * SSM_GROUPS * SSM_STATE
DIFF_HEADS = 4
FOX_HEADS = 4
HEAD_DIM = 64
LANES = 128
DIFF_WIDTH = DIFF_HEADS * 2 * HEAD_DIM
FOX_WIDTH = FOX_HEADS * HEAD_DIM
COL_DQ = SSM_WIDTH
COL_DV = COL_DQ + 2 * DIFF_WIDTH
COL_FQ = COL_DV + DIFF_WIDTH
COL_FV = COL_FQ + 2 * FOX_WIDTH
COL_GATE = COL_FV + FOX_WIDTH
ATT_COLS = 2 * DIFF_WIDTH + 2 * FOX_WIDTH
IN_PAD = SSM_WIDTH + ATT_COLS + LANES
N_EXPERTS = 8
MASK_VALUE = -1e30
LOG2E = math.log2(math.e)
QK_SCALE = HEAD_DIM ** -0.5 * LOG2E

VT_ROWS = DIFF_WIDTH + FOX_WIDTH
TOK_TILE = 688
IN_TILE = 384
FFN_TILE = 688
ATT_TQ = 256
ATT_TK = 256
S5_CHUNK = 256
MOE_TILE = 1024
MOE_FCHUNK = 512
MOE_SUB = 256
MOE_XROWS = 1029
MOE_XSLOTS = 3
ROW_BLOCKS = D_MODEL // LANES
COMBINE_TILE = 384

VMEM_LIMIT = 56 * 1024 * 1024


def _cp(sem):
    return pltpu.CompilerParams(dimension_semantics=sem, vmem_limit_bytes=VMEM_LIMIT)


def _rms(x, g):
    return x * lax.rsqrt(jnp.mean(x * x, axis=-1, keepdims=True) + EPS) * g


def _inproj_embed_kernel(x_ref, xtail_ref, meta_ref, g_ref, w_ref, wvt_ref,
                         res_ref, zu_ref, za_ref, zg_ref, vt_ref, vtm_ref):
    tail = jnp.concatenate([xtail_ref[...]] + [meta_ref[...]] * NB, axis=0)
    is_last = pl.program_id(0) == pl.num_programs(0) - 1
    x = jnp.where(is_last, tail, x_ref[...])
    res_ref[...] = x
    _inproj_body(x, g_ref, w_ref, wvt_ref, zu_ref, za_ref, zg_ref, vt_ref, vtm_ref)


def _inproj_kernel(res_ref, g_ref, w_ref, wvt_ref, zu_ref, za_ref, zg_ref, vt_ref, vtm_ref):
    _inproj_body(res_ref[...], g_ref, w_ref, wvt_ref, zu_ref, za_ref, zg_ref, vt_ref, vtm_ref)


def _inproj_body(x, g_ref, w_ref, wvt_ref, zu_ref, za_ref, zg_ref, vt_ref, vtm_ref):
    h = _rms(x, g_ref[...]).astype(BF16)
    zu_ref[...] = jnp.dot(h, w_ref[:, 0:SSM_WIDTH], preferred_element_type=F32)
    for c in range(0, ATT_COLS, 256):
        za_ref[:, c:c + 256] = jnp.dot(
            h, w_ref[:, SSM_WIDTH + c:SSM_WIDTH + c + 256],
            preferred_element_type=F32).astype(BF16)
    zg_ref[...] = jnp.dot(h, w_ref[:, SSM_WIDTH + ATT_COLS:IN_PAD],
                          preferred_element_type=F32)
    vt = _nt_dot(wvt_ref[...], h).astype(BF16)
    for c in range(IN_TILE // LANES):
        vt_ref[c] = vt[:, LANES * c:LANES * (c + 1)]

    @pl.when(pl.program_id(0) == pl.num_programs(0) - 1)
    def _():
        first = IN_TILE - NB * N_META
        for b in range(NB):
            vtm_ref[b] = vt[:, first + N_META * b:first + N_META * (b + 1)]


def _inproj(res, g, w, wvt, embed=None):
    nblk = IN_TILE // LANES
    const = lambda i: (0, 0)
    par_specs = [pl.BlockSpec((1, D_MODEL), const),
                 pl.BlockSpec((D_MODEL, IN_PAD), const),
                 pl.BlockSpec((VT_ROWS, D_MODEL), const)]
    tile = pl.BlockSpec((IN_TILE, D_MODEL), lambda i: (i, 0))
    if embed is None:
        t = res.shape[0]
        body, operands, in_specs = _inproj_kernel, (res,), [tile]
        extra_specs, extra_shapes = [], []
    else:
        x_flat, meta = embed
        tr = x_flat.shape[0]
        t = tr + NB * N_META
        n_full = tr // IN_TILE
        tail = tr - n_full * IN_TILE
        assert tail > 0 and tail + NB * N_META == IN_TILE and tr % tail == 0
        body, operands = _inproj_embed_kernel, (x_flat, x_flat, meta)
        in_specs = [pl.BlockSpec((IN_TILE, D_MODEL), lambda i: (jnp.minimum(i, n_full - 1), 0)),
                    pl.BlockSpec((tail, D_MODEL), lambda i: (tr // tail - 1, 0)),
                    pl.BlockSpec((N_META, D_MODEL), const)]
        extra_specs, extra_shapes = [tile], [jax.ShapeDtypeStruct((t, D_MODEL), F32)]
    assert t % IN_TILE == 0 and NB * N_META <= IN_TILE
    return pl.pallas_call(
        body,
        grid=(t // IN_TILE,),
        in_specs=in_specs + par_specs,
        out_specs=extra_specs + [
            pl.BlockSpec((IN_TILE, SSM_WIDTH), lambda i: (i, 0)),
            pl.BlockSpec((IN_TILE, ATT_COLS), lambda i: (i, 0)),
            pl.BlockSpec((IN_TILE, LANES), lambda i: (i, 0)),
            pl.BlockSpec((nblk, VT_ROWS, LANES), lambda i: (i, 0, 0)),
            pl.BlockSpec((NB, VT_ROWS, N_META), lambda i: (0, 0, 0))],
        out_shape=extra_shapes + [
            jax.ShapeDtypeStruct((t, SSM_WIDTH), F32),
            jax.ShapeDtypeStruct((t, ATT_COLS), BF16),
            jax.ShapeDtypeStruct((t, LANES), F32),
            jax.ShapeDtypeStruct((t // LANES, VT_ROWS, LANES), BF16),
            jax.ShapeDtypeStruct((NB, VT_ROWS, N_META), BF16)],
        compiler_params=_cp(("arbitrary",)),
        name="inproj",
    )(*operands, g, w, wvt)


def _store_row_tiles(ref, x):
    n = x.shape[0]
    for j in range(ROW_BLOCKS):
        ref[pl.ds(j, n, stride=ROW_BLOCKS), :] = x[:, LANES * j:LANES * (j + 1)]


def _load_row_tiles(ref, n):
    return jnp.concatenate(
        [ref[pl.ds(j, n, stride=ROW_BLOCKS), :] for j in range(ROW_BLOCKS)], axis=1)


def _outproj_kernel(res_ref, s_ref, d_ref, f_ref, w_ref, o_ref, *tiled_ref):
    mixed = jnp.concatenate([s_ref[...], d_ref[...], f_ref[...]], axis=1)
    out = res_ref[...] + jnp.dot(mixed, w_ref[...], preferred_element_type=F32)
    o_ref[...] = out
    if tiled_ref:
        _store_row_tiles(tiled_ref[0], out)


def _outproj(res, ssm, diff, fox, w, with_row_tiles):
    t = res.shape[0]
    out_specs = [pl.BlockSpec((TOK_TILE, D_MODEL), lambda i: (i, 0))]
    out_shape = [jax.ShapeDtypeStruct((t, D_MODEL), F32)]
    if with_row_tiles:
        out_specs.append(pl.BlockSpec((TOK_TILE * ROW_BLOCKS, LANES), lambda i: (i, 0)))
        out_shape.append(jax.ShapeDtypeStruct((t * ROW_BLOCKS, LANES), F32))
    return pl.pallas_call(
        _outproj_kernel,
        grid=(t // TOK_TILE,),
        in_specs=[pl.BlockSpec((TOK_TILE, D_MODEL), lambda i: (i, 0)),
                  pl.BlockSpec((TOK_TILE, 256), lambda i: (i, 0)),
                  pl.BlockSpec((TOK_TILE, 512), lambda i: (i, 0)),
                  pl.BlockSpec((TOK_TILE, 256), lambda i: (i, 0)),
                  pl.BlockSpec((D_MODEL, D_MODEL), lambda i: (0, 0))],
        out_specs=out_specs,
        out_shape=out_shape,
        compiler_params=_cp(("parallel",)),
        name="outproj",
    )(res, ssm, diff, fox, w)


def _dense_ffn_kernel(res_ref, g_ref, w1_ref, w3_ref, w2_ref, o_ref):
    x = res_ref[...]
    h = _rms(x, g_ref[...]).astype(BF16)
    a = jnp.dot(h, w1_ref[...], preferred_element_type=F32)
    b = jnp.dot(h, w3_ref[...], preferred_element_type=F32)
    hh = (a * jax.nn.sigmoid(a) * b).astype(BF16)
    o_ref[...] = x + jnp.dot(hh, w2_ref[...], preferred_element_type=F32)


def _dense_ffn(res, g, w1, w3, w2):
    t = res.shape[0]
    dff = w1.shape[1]
    once = pl.Buffered(1)
    return pl.pallas_call(
        _dense_ffn_kernel,
        grid=(t // FFN_TILE,),
        in_specs=[pl.BlockSpec((FFN_TILE, D_MODEL), lambda i: (i, 0)),
                  pl.BlockSpec((1, D_MODEL), lambda i: (0, 0)),
                  pl.BlockSpec((D_MODEL, dff), lambda i: (0, 0), pipeline_mode=once),
                  pl.BlockSpec((D_MODEL, dff), lambda i: (0, 0), pipeline_mode=once),
                  pl.BlockSpec((dff, D_MODEL), lambda i: (0, 0), pipeline_mode=once)],
        out_specs=pl.BlockSpec((FFN_TILE, D_MODEL), lambda i: (i, 0)),
        out_shape=jax.ShapeDtypeStruct((t, D_MODEL), F32),
        compiler_params=_cp(("parallel",)),
        name="dense_ffn",
    )(res, g, w1, w3, w2)


def _gelu_tanh(x):
    c = math.sqrt(2.0 / math.pi)
    return 0.5 * x * (1.0 + jnp.tanh(c * (x + 0.044715 * (x * x * x))))


def _s5_chunk(lc, get_u, state_ref, ut_ref, bu_ref, ot_ref,
              a_ref, bd_ref, cd_ref, dskip_ref, wglu_ref):
    for b in range(NB):
        ub = get_u(b)
        for s in range(2):
            ut_ref[s, pl.ds(b, lc, stride=NB), :] = ub[:, LANES * s:LANES * (s + 1)]
    u_tm = jnp.concatenate([ut_ref[0], ut_ref[1]], axis=1)
    bu_ref[...] = jnp.dot(u_tm.astype(BF16), bd_ref[...], preferred_element_type=F32)

    half = SSM_COLS // 2
    a_re = a_ref[:, :half]
    a_im = a_ref[:, half:]

    def step(t, x):
        x_re, x_im = x
        r = pl.multiple_of(t * NB, NB)
        cur = bu_ref[pl.ds(r, NB), :]
        n_re = a_re * x_re - a_im * x_im + cur[:, :half]
        n_im = a_re * x_im + a_im * x_re + cur[:, half:]
        bu_ref[pl.ds(r, NB), :] = jnp.concatenate([n_re, n_im], axis=1)
        return n_re, n_im

    x_re, x_im = lax.fori_loop(0, lc, step,
                               (state_ref[:, :half], state_ref[:, half:]), unroll=4)
    state_ref[...] = jnp.concatenate([x_re, x_im], axis=1)

    y = jnp.dot(bu_ref[...].astype(BF16), cd_ref[...], preferred_element_type=F32)
    y = _gelu_tanh(y + dskip_ref[...] * u_tm)
    g = jnp.dot(y.astype(BF16), wglu_ref[...], preferred_element_type=F32)
    o = g[:, :SSM_WIDTH] * jax.nn.sigmoid(g[:, SSM_WIDTH:])
    ot_ref[0] = o[:, :LANES]
    ot_ref[1] = o[:, LANES:]


def _s5_read_out(ot_ref, b, lc):
    return jnp.concatenate(
        [ot_ref[s, pl.ds(b, lc, stride=NB), :] for s in range(2)], axis=1)


def _s5_meta_kernel(u_ref, a_ref, bd_ref, cd_ref, dskip_ref, wglu_ref,
                    o_ref, state_out_ref, state_ref, ut_ref, bu_ref, ot_ref):
    state_ref[...] = jnp.zeros_like(state_ref)
    _s5_chunk(N_META, lambda b: u_ref[b * N_META:(b + 1) * N_META, :],
              state_ref, ut_ref, bu_ref, ot_ref, a_ref, bd_ref, cd_ref, dskip_ref, wglu_ref)
    for b in range(NB):
        o_ref[b * N_META:(b + 1) * N_META, :] = _s5_read_out(ot_ref, b, N_META).astype(BF16)
    state_out_ref[...] = state_ref[...]


def _s5_real_kernel(*refs):
    u_refs = refs[:NB]
    (state_in_ref, a_ref, bd_ref, cd_ref, dskip_ref, wglu_ref, _flat_ref,
     o_ref, state_ref, ut_ref, bu_ref, ot_ref) = refs[NB:]
    c = pl.program_id(0)
    b = pl.program_id(1)

    @pl.when((c == 0) & (b == 0))
    def _():
        state_ref[...] = state_in_ref[...]

    @pl.when(b == 0)
    def _():
        _s5_chunk(S5_CHUNK, lambda bb: u_refs[bb][...],
                  state_ref, ut_ref, bu_ref, ot_ref, a_ref, bd_ref, cd_ref, dskip_ref, wglu_ref)

    o_ref[...] = _s5_read_out(ot_ref, b, S5_CHUNK).astype(BF16)


def _s5(zu, seq, a, bd, cd, dskip, wglu):
    t = zu.shape[0]
    tr = NB * seq
    nmeta_rows = NB * N_META
    const = lambda *_: (0, 0)
    par_specs = [pl.BlockSpec((NB, SSM_COLS), const),
                 pl.BlockSpec((SSM_WIDTH, SSM_COLS), const),
                 pl.BlockSpec((SSM_COLS, SSM_WIDTH), const),
                 pl.BlockSpec((1, SSM_WIDTH), const),
                 pl.BlockSpec((SSM_WIDTH, 2 * SSM_WIDTH), const)]

    def scratch(lc):
        return [pltpu.VMEM((NB, SSM_COLS), F32),
                pltpu.VMEM((2, lc * NB, LANES), F32),
                pltpu.VMEM((lc * NB, SSM_COLS), F32),
                pltpu.VMEM((2, lc * NB, LANES), F32)]

    meta_blk = tr // nmeta_rows
    flat, state = pl.pallas_call(
        _s5_meta_kernel,
        grid=(1,),
        in_specs=[pl.BlockSpec((nmeta_rows, SSM_WIDTH), lambda i: (meta_blk, 0))] + par_specs,
        out_specs=[pl.BlockSpec((nmeta_rows, SSM_WIDTH), lambda i: (meta_blk, 0)),
                   pl.BlockSpec((NB, SSM_COLS), const)],
        out_shape=[jax.ShapeDtypeStruct((t, SSM_WIDTH), BF16),
                   jax.ShapeDtypeStruct((NB, SSM_COLS), F32)],
        scratch_shapes=scratch(N_META),
        compiler_params=_cp(("arbitrary",)),
        name="s5_meta",
    )(zu, a, bd, cd, dskip, wglu)

    nc = seq // S5_CHUNK
    u_specs = [pl.BlockSpec((S5_CHUNK, SSM_WIDTH), lambda c, b, bb=bb: (bb * nc + c, 0))
               for bb in range(NB)]
    n_in = NB + 1 + len(par_specs)
    return pl.pallas_call(
        _s5_real_kernel,
        grid=(nc, NB),
        in_specs=u_specs + [pl.BlockSpec((NB, SSM_COLS), const)] + par_specs
        + [pl.BlockSpec(memory_space=pl.ANY)],
        out_specs=pl.BlockSpec((S5_CHUNK, SSM_WIDTH), lambda c, b: (b * nc + c, 0)),
        out_shape=jax.ShapeDtypeStruct((t, SSM_WIDTH), BF16),
        scratch_shapes=scratch(S5_CHUNK),
        input_output_aliases={n_in: 0},
        compiler_params=_cp(("arbitrary", "arbitrary")),
        name="s5_real",
    )(*([zu] * NB), state, a, bd, cd, dskip, wglu, flat)


def _s5_params(lam_re, lam_im, log_dt, b_re, b_im, c_re, c_im, d_skip):
    dt = jnp.exp(log_dt)[:, None]
    mag = jnp.exp(lam_re * dt)
    ab_re = mag * jnp.cos(lam_im * dt)
    ab_im = mag * jnp.sin(lam_im * dt)
    den = lam_re * lam_re + lam_im * lam_im
    nr = ab_re - 1.0
    ni = ab_im
    coef_re = ((nr * lam_re + ni * lam_im) / den)[..., None]
    coef_im = ((ni * lam_re - nr * lam_im) / den)[..., None]
    bb_re = coef_re * b_re - coef_im * b_im
    bb_im = coef_re * b_im + coef_im * b_re
    eye = jnp.eye(SSM_GROUPS, dtype=F32)
    half = SSM_COLS // 2
    bd = jnp.concatenate(
        [jnp.einsum('gnc,gh->gchn', m, eye).reshape(SSM_WIDTH, half) for m in (bb_re, bb_im)],
        axis=1).astype(BF16)
    cd = jnp.concatenate(
        [jnp.einsum('gcn,gh->gnhc', m, eye).reshape(half, SSM_WIDTH) for m in (c_re, -c_im)],
        axis=0).astype(BF16)
    a = jnp.concatenate([ab_re.reshape(1, half), ab_im.reshape(1, half)], axis=1)
    a = jnp.broadcast_to(a, (NB, SSM_COLS))
    return a, bd, cd, d_skip.reshape(1, SSM_WIDTH)


def _nt_dot(a, b):
    return lax.dot_general(a, b, (((1,), (1,)), ((), ())), preferred_element_type=F32)


def _osm_init(s, vt):
    m = jnp.max(s, axis=1, keepdims=True)
    p = jnp.exp2(s - m)
    l = jnp.sum(p, axis=1, keepdims=True)
    acc = _nt_dot(p.astype(BF16), vt)
    return m, l, acc


def _causal_sweep(qts, k_ref, k_cols, vt_ref, vt_rows, kms, vmts, qi,
                  m_ref, l_ref, acc_ref, al_ref, p_ref):
    n = len(qts)
    meta_scores = [jnp.dot(kms[i], qts[i], preferred_element_type=F32) for i in range(n)]
    meta_p = []
    for i in range(n):
        s = meta_scores[i]
        m = jnp.max(s, axis=0, keepdims=True)
        p = jnp.exp2(s - m)
        m_ref[i][...] = m
        l_ref[i][...] = jnp.sum(p, axis=0, keepdims=True)
        meta_p.append(p.astype(BF16))
        al_ref[i][...] = jnp.ones_like(al_ref[i])
        p_ref[i][...] = jnp.zeros_like(p_ref[i])
    for i in range(n):
        acc_ref[i][...] = jnp.dot(vmts[i], meta_p[i], preferred_element_type=F32)

    sub = ATT_TK // LANES

    def pending_pv(jp):
        for i in range(n):
            vtb = jnp.concatenate([vt_ref[jp * sub + c, vt_rows[i], :] for c in range(sub)],
                                  axis=1)
            acc_ref[i][...] = al_ref[i][...] * acc_ref[i][...] + jnp.dot(
                vtb, p_ref[i][...], preferred_element_type=F32)

    def chunk(j, masked):
        off = pl.multiple_of(j * ATT_TK, ATT_TK)
        if masked:
            krow = lax.broadcasted_iota(jnp.int32, (ATT_TK, ATT_TQ), 0)
            qcol = lax.broadcasted_iota(jnp.int32, (ATT_TK, ATT_TQ), 1)
            visible = krow <= qcol
        scores = []
        for i in range(n):
            kb = k_ref[pl.ds(off, ATT_TK), k_cols[i]]
            s = jnp.dot(kb, qts[i], preferred_element_type=F32)
            scores.append(jnp.where(visible, s, MASK_VALUE) if masked else s)
        pending_pv(jnp.maximum(j - 1, 0))
        for i in range(n):
            m_old = m_ref[i][...]
            m_new = jnp.maximum(m_old, jnp.max(scores[i], axis=0, keepdims=True))
            alpha = jnp.exp2(m_old - m_new)
            p = jnp.exp2(scores[i] - m_new)
            l_ref[i][...] = alpha * l_ref[i][...] + jnp.sum(p, axis=0, keepdims=True)
            m_ref[i][...] = m_new
            al_ref[i][...] = alpha
            p_ref[i][...] = p.astype(BF16)

    def body(j, carry):
        chunk(j, False)
        return carry

    lax.fori_loop(0, qi, body, 0)
    chunk(qi, True)
    pending_pv(qi)


def _meta_attend(q, km, vmt):
    n = km.shape[0]
    row = lax.broadcasted_iota(jnp.int32, (n, n), 0)
    col = lax.broadcasted_iota(jnp.int32, (n, n), 1)
    s = jnp.where(col <= row, _nt_dot(q, km), MASK_VALUE)
    _, l, acc = _osm_init(s, vmt)
    return l, acc


def _diff_split(q):
    lane = lax.broadcasted_iota(jnp.int32, q.shape, 1)
    q = (q.astype(F32) * QK_SCALE).astype(BF16)
    zero = jnp.zeros_like(q)
    return jnp.where(lane < HEAD_DIM, q, zero), jnp.where(lane >= HEAD_DIM, q, zero)


def _diff_finish(parts, lam, lam_init, g):
    (l1, acc1), (l2, acc2) = parts
    o = acc1 / l1 - lam * (acc2 / l2)
    return (_rms(o, g) * (1.0 - lam_init)).astype(BF16)


def _transpose_bf16(x):
    return x.astype(F32).T.astype(BF16)


def _map_scratch(n_maps, dv):
    return ([pltpu.VMEM((1, ATT_TQ), F32)] * (2 * n_maps)
            + [pltpu.VMEM((dv, ATT_TQ), F32)] * n_maps
            + [pltpu.VMEM((1, ATT_TQ), F32)] * n_maps
            + [pltpu.VMEM((ATT_TK, ATT_TQ), BF16)] * n_maps)


def _split_map_scratch(scratch, n_maps):
    return [scratch[k * n_maps:(k + 1) * n_maps] for k in range(5)]


def _diff_kernel(lam_init, lam_ref, q_ref, k_ref, vt_ref, km_ref, vmt_ref, g_ref, o_ref,
                 *scratch):
    m_ref, l_ref, acc_ref, al_ref, p_ref = _split_map_scratch(scratch, 2 * DIFF_HEADS)
    qts, k_cols, vt_rows, kms, vmts = [], [], [], [], []
    row = lax.broadcasted_iota(jnp.int32, (LANES, ATT_TQ), 0)
    for h in range(DIFF_HEADS):
        sl = slice(LANES * h, LANES * (h + 1))
        km = km_ref[:, sl]
        vmt = vmt_ref[sl, :]
        qt = (q_ref[:, sl].astype(F32) * QK_SCALE).T.astype(BF16)
        zero = jnp.zeros_like(qt)
        for qm in (jnp.where(row < HEAD_DIM, qt, zero), jnp.where(row >= HEAD_DIM, qt, zero)):
            qts.append(qm)
            k_cols.append(sl)
            vt_rows.append(sl)
            kms.append(km)
            vmts.append(vmt)
    _causal_sweep(qts, k_ref, k_cols, vt_ref, vt_rows, kms, vmts, pl.program_id(1),
                  m_ref, l_ref, acc_ref, al_ref, p_ref)
    lam = lam_ref[0]
    for h in range(DIFF_HEADS):
        o = (acc_ref[2 * h][...] / l_ref[2 * h][...]
             - lam * (acc_ref[2 * h + 1][...] / l_ref[2 * h + 1][...]))
        y = o * lax.rsqrt(jnp.mean(o * o, axis=0, keepdims=True) + EPS)
        o_ref[:, LANES * h:LANES * (h + 1)] = (
            y.T * g_ref[...] * (1.0 - lam_init)).astype(BF16)


def _diff_meta_kernel(lam_init, lam_ref, q_ref, km_ref, vmt_ref, g_ref, _flat_ref, o_ref):
    for h in range(DIFF_HEADS):
        sl = slice(LANES * h, LANES * (h + 1))
        q1, q2 = _diff_split(q_ref[:, sl])
        km = km_ref[:, sl]
        vmt = vmt_ref[sl, :]
        parts = [_meta_attend(q1, km, vmt), _meta_attend(q2, km, vmt)]
        o_ref[:, sl] = _diff_finish(parts, lam_ref[0], lam_init, g_ref[...])


def _diff_attention(za, vt, vtm, seq, lam, lam_init, g):
    t = za.shape[0]
    nq = seq // ATT_TQ
    mrow = NB * seq // N_META
    smem = pl.BlockSpec(memory_space=pltpu.SMEM)
    lam = lam.reshape(1).astype(F32)
    g = g.reshape(1, LANES)
    gspec = pl.BlockSpec((1, LANES), lambda *_: (0, 0))
    width = DIFF_HEADS * LANES
    n_maps = 2 * DIFF_HEADS
    flat = pl.pallas_call(
        functools.partial(_diff_kernel, lam_init),
        grid=(NB, nq),
        in_specs=[smem,
                  pl.BlockSpec((ATT_TQ, width), lambda b, i: (b * nq + i, 0)),
                  pl.BlockSpec((seq, width), lambda b, i: (b, 1)),
                  pl.BlockSpec((seq // LANES, width, LANES), lambda b, i: (b, 0, 0)),
                  pl.BlockSpec((N_META, width), lambda b, i: (mrow + b, 1)),
                  pl.BlockSpec((None, width, N_META), lambda b, i: (b, 0, 0)),
                  gspec],
        out_specs=pl.BlockSpec((ATT_TQ, width), lambda b, i: (b * nq + i, 0)),
        out_shape=jax.ShapeDtypeStruct((t, width), BF16),
        scratch_shapes=_map_scratch(n_maps, LANES),
        compiler_params=_cp(("parallel", "arbitrary")),
        name="diff_attn",
    )(lam, za, za, vt, za, vtm, g)
    return pl.pallas_call(
        functools.partial(_diff_meta_kernel, lam_init),
        grid=(NB,),
        in_specs=[smem,
                  pl.BlockSpec((N_META, width), lambda b: (mrow + b, 0)),
                  pl.BlockSpec((N_META, width), lambda b: (mrow + b, 1)),
                  pl.BlockSpec((None, width, N_META), lambda b: (b, 0, 0)),
                  gspec,
                  pl.BlockSpec(memory_space=pl.ANY)],
        out_specs=pl.BlockSpec((N_META, width), lambda b: (mrow + b, 0)),
        out_shape=jax.ShapeDtypeStruct((t, width), BF16),
        input_output_aliases={5: 0},
        compiler_params=_cp(("parallel",)),
        name="diff_attn_meta",
    )(lam, za, za, vtm, g, flat)


def _split3(c):
    hi = c.astype(BF16).astype(F32)
    r1 = c - hi
    mid = r1.astype(BF16).astype(F32)
    lo = (r1 - mid).astype(BF16).astype(F32)
    return hi, mid, lo


def _cumsum_rows(tri, lf):
    parts = jnp.concatenate(_split3(lf), axis=1).astype(BF16)
    r = jnp.dot(tri, parts, preferred_element_type=F32)
    return r[:, :LANES] + r[:, LANES:2 * LANES] + r[:, 2 * LANES:]


def _log_sigmoid(x):
    return jnp.minimum(x, 0.0) - jnp.log1p(jnp.exp(-jnp.abs(x)))


def _fox_augment(fq, fk, cum, qa_ref, ka_ref):
    n = fq.shape[0]
    lane = lax.broadcasted_iota(jnp.int32, (n, LANES), 1)
    for h in range(FOX_HEADS):
        pair = slice(LANES * (h // 2), LANES * (h // 2) + LANES)
        own = (lane // HEAD_DIM) == (h % 2)
        e0 = HEAD_DIM * (1 - h % 2)
        hi, mid, lo = _split3(jnp.broadcast_to(cum[:, h:h + 1] * LOG2E, (n, LANES)))
        ones = (lane >= e0 + 3) & (lane < e0 + 6)
        q_extra = jnp.where(lane == e0, hi, jnp.where(lane == e0 + 1, mid, jnp.where(
            lane == e0 + 2, lo, jnp.where(ones, 1.0, 0.0))))
        ones = (lane >= e0) & (lane < e0 + 3)
        k_extra = jnp.where(lane == e0 + 3, -hi, jnp.where(lane == e0 + 4, -mid, jnp.where(
            lane == e0 + 5, -lo, jnp.where(ones, 1.0, 0.0))))
        q = fq[:, pair].astype(F32) * QK_SCALE
        k = fk[:, pair].astype(F32)
        qa_ref[:, LANES * h:LANES * (h + 1)] = jnp.where(own, q, q_extra).astype(BF16)
        ka_ref[:, LANES * h:LANES * (h + 1)] = jnp.where(own, k, k_extra).astype(BF16)


def _fox_prep_meta_kernel(zg_ref, fq_ref, fk_ref, fb_ref, qa_ref, ka_ref, carry_ref):
    n = NB * N_META
    lf = _log_sigmoid(zg_ref[...] + fb_ref[...])
    row = lax.broadcasted_iota(jnp.int32, (n, n), 0)
    col = lax.broadcasted_iota(jnp.int32, (n, n), 1)
    tri = ((col <= row) & (col // N_META == row // N_META)).astype(BF16)
    cum = _cumsum_rows(tri, lf)
    brow = lax.broadcasted_iota(jnp.int32, (NB, n), 0)
    bcol = lax.broadcasted_iota(jnp.int32, (NB, n), 1)
    carry_ref[...] = _cumsum_rows((bcol // N_META == brow).astype(BF16), lf)
    _fox_augment(fq_ref[...], fk_ref[...], cum, qa_ref, ka_ref)


def _fox_prep_kernel(zg_ref, fq_ref, fk_ref, fb_ref, carry_in_ref, qa_ref, ka_ref, carry_ref):
    b = pl.program_id(0)
    n = zg_ref.shape[0]

    @pl.when(pl.program_id(1) == 0)
    def _():
        carry_ref[...] = carry_in_ref[pl.ds(b, 1), :]

    lf = _log_sigmoid(zg_ref[...] + fb_ref[...])
    row = lax.broadcasted_iota(jnp.int32, (n, n), 0)
    col = lax.broadcasted_iota(jnp.int32, (n, n), 1)
    cum = _cumsum_rows((col <= row).astype(BF16), lf) + carry_ref[...]
    carry_ref[...] = cum[n - 1:n, :]
    _fox_augment(fq_ref[...], fk_ref[...], cum, qa_ref, ka_ref)


FOX_PREP_TILE = 1024


def _fox_prep(zg, za, seq, fb):
    tr = NB * seq
    nm = NB * N_META
    mblk = tr // nm
    fb = jnp.pad(fb.astype(F32), (0, LANES - FOX_HEADS)).reshape(1, LANES)
    fbspec = pl.BlockSpec((1, LANES), lambda *_: (0, 0))
    aug = FOX_HEADS * LANES
    qa_m, ka_m, carry = pl.pallas_call(
        _fox_prep_meta_kernel,
        grid=(1,),
        in_specs=[pl.BlockSpec((nm, LANES), lambda i: (mblk, 0)),
                  pl.BlockSpec((nm, 256), lambda i: (mblk, 4)),
                  pl.BlockSpec((nm, 256), lambda i: (mblk, 5)),
                  fbspec],
        out_specs=[pl.BlockSpec((nm, aug), lambda i: (0, 0)),
                   pl.BlockSpec((nm, aug), lambda i: (0, 0)),
                   pl.BlockSpec((NB, LANES), lambda i: (0, 0))],
        out_shape=[jax.ShapeDtypeStruct((nm, aug), BF16),
                   jax.ShapeDtypeStruct((nm, aug), BF16),
                   jax.ShapeDtypeStruct((NB, LANES), F32)],
        compiler_params=_cp(("arbitrary",)),
        name="fox_prep_meta",
    )(zg, za, za, fb)
    nc = seq // FOX_PREP_TILE
    qa, ka = pl.pallas_call(
        _fox_prep_kernel,
        grid=(NB, nc),
        in_specs=[pl.BlockSpec((FOX_PREP_TILE, LANES), lambda b, c: (b * nc + c, 0)),
                  pl.BlockSpec((FOX_PREP_TILE, 256), lambda b, c: (b * nc + c, 4)),
                  pl.BlockSpec((FOX_PREP_TILE, 256), lambda b, c: (b * nc + c, 5)),
                  fbspec,
                  pl.BlockSpec((NB, LANES), lambda b, c: (0, 0))],
        out_specs=[pl.BlockSpec((FOX_PREP_TILE, aug), lambda b, c: (b * nc + c, 0)),
                   pl.BlockSpec((FOX_PREP_TILE, aug), lambda b, c: (b * nc + c, 0))],
        out_shape=[jax.ShapeDtypeStruct((tr, aug), BF16),
                   jax.ShapeDtypeStruct((tr, aug), BF16)],
        scratch_shapes=[pltpu.VMEM((1, LANES), F32)],
        compiler_params=_cp(("parallel", "arbitrary")),
        name="fox_prep",
    )(zg, za, za, fb, carry)
    return qa, ka, qa_m, ka_m


def _fox_finish(parts):
    (l0, acc0), (l1, acc1) = parts
    lane = lax.broadcasted_iota(jnp.int32, acc0.shape, 1)
    return jnp.where(lane < HEAD_DIM, acc0 / l0, acc1 / l1).astype(BF16)


def _fox_kernel(q_ref, k_ref, vt_ref, km_ref, vmt_ref, o_ref, *scratch):
    m_ref, l_ref, acc_ref, al_ref, p_ref = _split_map_scratch(scratch, FOX_HEADS)
    hd = HEAD_DIM
    vmt = vmt_ref[...]
    qts, k_cols, vt_rows, kms, vmts = [], [], [], [], []
    for h in range(FOX_HEADS):
        sl = slice(LANES * h, LANES * (h + 1))
        qts.append(_transpose_bf16(q_ref[:, sl]))
        k_cols.append(sl)
        vt_rows.append(slice(hd * h, hd * (h + 1)))
        kms.append(km_ref[:, sl])
        vmts.append(vmt[hd * h:hd * (h + 1), :])
    _causal_sweep(qts, k_ref, k_cols, vt_ref, vt_rows, kms, vmts, pl.program_id(1),
                  m_ref, l_ref, acc_ref, al_ref, p_ref)
    for p in range(FOX_HEADS // 2):
        o = jnp.concatenate([acc_ref[2 * p][...] / l_ref[2 * p][...],
                             acc_ref[2 * p + 1][...] / l_ref[2 * p + 1][...]],
                            axis=0)
        o_ref[:, LANES * p:LANES * (p + 1)] = o.T.astype(BF16)


def _fox_meta_kernel(q_ref, km_ref, vmt_ref, _flat_ref, o_ref):
    for p in range(FOX_HEADS // 2):
        vmt = vmt_ref[LANES * p:LANES * (p + 1), :]
        parts = []
        for h in (2 * p, 2 * p + 1):
            sl = slice(LANES * h, LANES * (h + 1))
            parts.append(_meta_attend(q_ref[:, sl], km_ref[:, sl], vmt))
        o_ref[:, LANES * p:LANES * (p + 1)] = _fox_finish(parts)


def _fox_attention(t, vt, vtm, seq, qa, ka, qa_m, ka_m):
    nq = seq // ATT_TQ
    mrow = NB * seq // N_META
    aug = FOX_HEADS * LANES
    flat = pl.pallas_call(
        _fox_kernel,
        grid=(NB, nq),
        in_specs=[pl.BlockSpec((ATT_TQ, aug), lambda b, i: (b * nq + i, 0)),
                  pl.BlockSpec((seq, aug), lambda b, i: (b, 0)),
                  pl.BlockSpec((seq // LANES, 256, LANES), lambda b, i: (b, 2, 0)),
                  pl.BlockSpec((N_META, aug), lambda b, i: (b, 0)),
                  pl.BlockSpec((None, 256, N_META), lambda b, i: (b, 2, 0))],
        out_specs=pl.BlockSpec((ATT_TQ, 256), lambda b, i: (b * nq + i, 0)),
        out_shape=jax.ShapeDtypeStruct((t, 256), BF16),
        scratch_shapes=_map_scratch(FOX_HEADS, HEAD_DIM),
        compiler_params=_cp(("parallel", "arbitrary")),
        name="fox_attn",
    )(qa, ka, vt, ka_m, vtm)
    return pl.pallas_call(
        _fox_meta_kernel,
        grid=(NB,),
        in_specs=[pl.BlockSpec((N_META, aug), lambda b: (b, 0)),
                  pl.BlockSpec((N_META, aug), lambda b: (b, 0)),
                  pl.BlockSpec((None, 256, N_META), lambda b: (b, 2, 0)),
                  pl.BlockSpec(memory_space=pl.ANY)],
        out_specs=pl.BlockSpec((N_META, 256), lambda b: (mrow + b, 0)),
        out_shape=jax.ShapeDtypeStruct((t, 256), BF16),
        input_output_aliases={3: 0},
        compiler_params=_cp(("parallel",)),
        name="fox_attn_meta",
    )(qa_m, ka_m, vtm, flat)


def _router_kernel(res_ref, g_ref, wr_ref, idx_ref, gate_ref):
    h = _rms(res_ref[...], g_ref[...])
    w = wr_ref[...]
    hi = h.astype(BF16)
    lo = (h - hi.astype(F32)).astype(BF16)
    whi = w.astype(BF16)
    wlo = (w - whi.astype(F32)).astype(BF16)
    logits = (jnp.dot(hi, whi, preferred_element_type=F32)
              + jnp.dot(lo, whi, preferred_element_type=F32)
              + jnp.dot(hi, wlo, preferred_element_type=F32))
    lane = lax.broadcasted_iota(jnp.int32, logits.shape, 1)
    logits = jnp.where(lane < N_EXPERTS, logits, -jnp.inf)
    m1 = jnp.max(logits, axis=1, keepdims=True)
    i1 = jnp.min(jnp.where(logits == m1, lane, LANES), axis=1, keepdims=True)
    rest = jnp.where(lane == i1, -jnp.inf, logits)
    m2 = jnp.max(rest, axis=1, keepdims=True)
    i2 = jnp.min(jnp.where(rest == m2, lane, LANES), axis=1, keepdims=True)
    e = jnp.exp(m2 - m1)
    g1 = 1.0 / (1.0 + e)
    g2 = e / (1.0 + e)
    idx_ref[...] = jnp.where(lane == 0, i1, jnp.where(lane == 1, i2, 0))
    gate_ref[...] = jnp.where(lane == 0, g1, jnp.where(lane == 1, g2, 0.0))


def _router(res, g, wr):
    t = res.shape[0]
    return pl.pallas_call(
        _router_kernel,
        grid=(t // TOK_TILE,),
        in_specs=[pl.BlockSpec((TOK_TILE, D_MODEL), lambda i: (i, 0)),
                  pl.BlockSpec((1, D_MODEL), lambda i: (0, 0)),
                  pl.BlockSpec((D_MODEL, LANES), lambda i: (0, 0))],
        out_specs=[pl.BlockSpec((TOK_TILE, LANES), lambda i: (i, 0)),
                   pl.BlockSpec((TOK_TILE, LANES), lambda i: (i, 0))],
        out_shape=[jax.ShapeDtypeStruct((t, LANES), jnp.int32),
                   jax.ShapeDtypeStruct((t, LANES), F32)],
        compiler_params=_cp(("parallel",)),
        name="router",
    )(res, g, wr)


def _gather_copy(src_hbm, first, dst_ref, r, sem):
    first = pl.multiple_of(first, ROW_BLOCKS)
    dst = pl.multiple_of(r * ROW_BLOCKS, ROW_BLOCKS)
    return pltpu.make_async_copy(src_hbm.at[pl.ds(first, ROW_BLOCKS), :],
                                 dst_ref.at[pl.ds(dst, ROW_BLOCKS), :], sem)


def _expert_ffn_kernel(nf, te_ref, na_ref, nb_ref, src_ref, x_hbm, g_ref, w1_ref, w3_ref,
                       w2_ref, o_ref, xbuf_ref, h_ref, acc_ref, sem):
    r = pl.program_id(0)
    f = pl.program_id(1)
    n_act = na_ref[0]
    n_sub = MOE_TILE // MOE_SUB
    per_step = MOE_XROWS // nf
    last_tile = pl.num_programs(0) - 1

    def tile_copy(slot):
        return pltpu.make_async_copy(x_hbm.at[pl.ds(0, MOE_XROWS * ROW_BLOCKS), :],
                                     xbuf_ref.at[slot], sem.at[slot])

    def start_row(tile, slot, k):
        _gather_copy(x_hbm, src_ref[tile * MOE_XROWS + k], xbuf_ref.at[slot], k,
                     sem.at[slot]).start()

    @pl.when(r < n_act)
    def _():
        slot = lax.rem(r, MOE_XSLOTS)
        slot1 = lax.rem(r + 1, MOE_XSLOTS)
        slot2 = lax.rem(r + 2, MOE_XSLOTS)

        @pl.when((r == 0) & (f == 0))
        def _():
            def first(k, _):
                start_row(0, 0, k)
                start_row(jnp.minimum(1, last_tile), 1, k)
                return 0
            lax.fori_loop(0, MOE_XROWS, first, 0, unroll=8)

        @pl.when(f == 0)
        def _():
            tile_copy(slot).wait()
            x = _load_row_tiles(xbuf_ref.at[slot], MOE_TILE)
            h_ref[...] = _rms(x, g_ref[...]).astype(BF16)
            acc_ref[...] = jnp.zeros_like(acc_ref)

        ahead = jnp.minimum(r + 2, last_tile)

        def start_ahead():
            for k in range(per_step):
                start_row(ahead, slot2, f * per_step + k)

        def swiglu_rows(rows, w1, w3, w2):
            h = h_ref[rows, :]
            a = jnp.dot(h, w1, preferred_element_type=F32)
            b = jnp.dot(h, w3, preferred_element_type=F32)
            hh = (a * jax.nn.sigmoid(a) * b).astype(BF16)
            acc_ref[rows, :] += jnp.dot(hh, w2, preferred_element_type=F32)

        def weights():
            return (w1_ref[...].astype(BF16), w3_ref[...].astype(BF16),
                    w2_ref[...].astype(BF16))

        n_blocks = nb_ref[r]

        @pl.when(n_blocks == n_sub)
        def _():
            start_ahead()
            swiglu_rows(slice(None), *weights())

        @pl.when(n_blocks < n_sub)
        def _():
            start_ahead()
            w = weights()
            swiglu_rows(slice(0, MOE_SUB), *w)
            for sb in range(1, n_sub - 1):
                @pl.when(sb < n_blocks)
                def _():
                    swiglu_rows(slice(MOE_SUB * sb, MOE_SUB * (sb + 1)), *w)

        @pl.when(f == nf - 1)
        def _():
            _store_row_tiles(o_ref, acc_ref[...])

        @pl.when((f == nf - 1) & (r == n_act - 1))
        def _():
            tile_copy(slot1).wait()
            tile_copy(slot2).wait()


def _expert_ffn(tile_expert, n_active, tile_blocks, src, x, g, w1, w3, w2):
    dff = w1.shape[2]
    nt = src.shape[0] // MOE_XROWS
    nf = dff // MOE_FCHUNK
    assert MOE_XROWS % nf == 0 and MOE_XROWS >= MOE_TILE

    def row(r, f, te, na, *_):
        return jnp.minimum(r, na[0] - 1)

    def fch(r, f, te, na, *_):
        return jnp.where(r < na[0], f, nf - 1)

    return pl.pallas_call(
        functools.partial(_expert_ffn_kernel, nf),
        grid_spec=pltpu.PrefetchScalarGridSpec(
            num_scalar_prefetch=4,
            grid=(nt, nf),
            in_specs=[
                pl.BlockSpec(memory_space=pl.ANY),
                pl.BlockSpec((1, D_MODEL), lambda r, f, *_: (0, 0)),
                pl.BlockSpec((None, D_MODEL, MOE_FCHUNK),
                             lambda r, f, te, na, *_: (te[r], 0, fch(r, f, te, na))),
                pl.BlockSpec((None, D_MODEL, MOE_FCHUNK),
                             lambda r, f, te, na, *_: (te[r], 0, fch(r, f, te, na))),
                pl.BlockSpec((None, MOE_FCHUNK, D_MODEL),
                             lambda r, f, te, na, *_: (te[r], fch(r, f, te, na), 0))],
            out_specs=pl.BlockSpec((MOE_TILE * ROW_BLOCKS, LANES),
                                   lambda r, f, te, na, *_: (row(r, f, te, na), 0)),
            scratch_shapes=[pltpu.VMEM((MOE_XSLOTS, MOE_XROWS * ROW_BLOCKS, LANES), F32),
                            pltpu.VMEM((MOE_TILE, D_MODEL), BF16),
                            pltpu.VMEM((MOE_TILE, D_MODEL), F32),
                            pltpu.SemaphoreType.DMA((MOE_XSLOTS,))]),
        out_shape=jax.ShapeDtypeStruct((nt * MOE_TILE * ROW_BLOCKS, LANES), F32),
        compiler_params=_cp(("arbitrary", "arbitrary")),
        name="expert_ffn",
    )(tile_expert, n_active, tile_blocks, src, x, g, w1, w3, w2)


def _combine_kernel(tile, final, pos_ref, res_ref, gate_ref, gf_ref, y_hbm, o_ref, ybuf_ref, sem):
    i = pl.program_id(0)
    n = pl.num_programs(0)
    slot = lax.rem(i, 2)

    def issue_tile(step, s):
        base = step * tile

        def issue(r, _):
            for k in range(2):
                _gather_copy(y_hbm, pos_ref[2 * (base + r) + k], ybuf_ref.at[s, k], r,
                             sem.at[s]).start(priority=k)
            return 0

        lax.fori_loop(0, tile, issue, 0, unroll=8)

    @pl.when(i == 0)
    def _():
        issue_tile(0, 0)

    @pl.when(i + 1 < n)
    def _():
        issue_tile(i + 1, 1 - slot)

    pltpu.make_async_copy(ybuf_ref.at[slot], ybuf_ref.at[slot], sem.at[slot]).wait()
    gate = gate_ref[...]
    out = (res_ref[...] + gate[:, 0:1] * _load_row_tiles(ybuf_ref.at[slot, 0], tile)
           + gate[:, 1:2] * _load_row_tiles(ybuf_ref.at[slot, 1], tile))
    o_ref[...] = _rms(out, gf_ref[...]) if final else out


def _combine(pos, res, gates, y, final_g=None):
    final = final_g is not None
    t = res.shape[0]
    n_rows = t - NB * N_META if final else t
    tile = 512 if final else COMBINE_TILE
    if not final:
        final_g = jnp.ones((1, D_MODEL), F32)
    return pl.pallas_call(
        functools.partial(_combine_kernel, tile, final),
        grid_spec=pltpu.PrefetchScalarGridSpec(
            num_scalar_prefetch=1,
            grid=(n_rows // tile,),
            in_specs=[pl.BlockSpec((tile, D_MODEL), lambda i, pos: (i, 0)),
                      pl.BlockSpec((tile, LANES), lambda i, pos: (i, 0)),
                      pl.BlockSpec((1, D_MODEL), lambda i, pos: (0, 0)),
                      pl.BlockSpec(memory_space=pl.ANY)],
            out_specs=pl.BlockSpec((tile, D_MODEL), lambda i, pos: (i, 0)),
            scratch_shapes=[pltpu.VMEM((2, 2, tile * ROW_BLOCKS, LANES), F32),
                            pltpu.SemaphoreType.DMA((2,))]),
        out_shape=jax.ShapeDtypeStruct((n_rows, D_MODEL), F32),
        compiler_params=_cp(("arbitrary",)),
        name="moe_combine",
    )(pos, res, gates, final_g, y)


def _invert_kernel(spos_ref, src_ref):
    def clear(i, _):
        src_ref[i] = 0
        return 0

    lax.fori_loop(0, src_ref.shape[0], clear, 0, unroll=8)

    def body(a, _):
        src_ref[spos_ref[a]] = lax.shift_right_logical(a, 1) * ROW_BLOCKS
        return 0

    lax.fori_loop(0, spos_ref.shape[0], body, 0, unroll=8)


def _invert_positions(spos, n):
    smem = pl.BlockSpec(memory_space=pltpu.SMEM)
    return pl.pallas_call(
        _invert_kernel,
        in_specs=[smem],
        out_specs=smem,
        out_shape=jax.ShapeDtypeStruct((n,), jnp.int32),
        name="moe_invert",
    )(spos)


def _moe(res, res_tiles, g, wr, w1, w3, w2, final_g=None):
    t = res.shape[0]
    wr = jnp.pad(wr.astype(F32), ((0, 0), (0, LANES - N_EXPERTS)))
    idx, gates = _router(res, g, wr)
    e_flat = idx[:, :2].reshape(-1)
    onehot = (e_flat[:, None] == jnp.arange(N_EXPERTS, dtype=jnp.int32)[None, :]).astype(jnp.int32)
    csum = jnp.cumsum(onehot, axis=0)
    rank = jnp.take_along_axis(csum, e_flat[:, None], axis=1)[:, 0] - 1
    counts = csum[-1]
    tiles = (counts + MOE_TILE - 1) // MOE_TILE
    tile_end = jnp.cumsum(tiles)
    starts = (tile_end - tiles) * MOE_TILE
    pos = (starts[e_flat] + rank).astype(jnp.int32)
    n_tiles = (2 * t + N_EXPERTS * (MOE_TILE - 1)) // MOE_TILE
    spos = (pos // MOE_TILE) * MOE_XROWS + pos % MOE_TILE
    src = _invert_positions(spos, n_tiles * MOE_XROWS)
    n_active = tile_end[-1:].astype(jnp.int32)
    tile_ids = jnp.minimum(jnp.arange(n_tiles, dtype=jnp.int32), n_active[0] - 1)
    tile_expert = jnp.sum(tile_ids[:, None] >= tile_end[None, :], axis=1).astype(jnp.int32)
    first_tile = (tile_end - tiles)[tile_expert]
    tile_rows = jnp.clip(counts[tile_expert] - (tile_ids - first_tile) * MOE_TILE, 1, MOE_TILE)
    tile_blocks = ((tile_rows + MOE_SUB - 1) // MOE_SUB).astype(jnp.int32)

    y = _expert_ffn(tile_expert, n_active, tile_blocks, src, res_tiles, g, w1, w3, w2)
    return _combine(pos * ROW_BLOCKS, res, gates, y, final_g)


def _final_norm_kernel(res_ref, g_ref, o_ref):
    o_ref[...] = _rms(res_ref[...], g_ref[...])


def _final_norm(res, g, n_rows):
    tile = 1024
    return pl.pallas_call(
        _final_norm_kernel,
        grid=(n_rows // tile,),
        in_specs=[pl.BlockSpec((tile, D_MODEL), lambda i: (i, 0)),
                  pl.BlockSpec((1, D_MODEL), lambda i: (0, 0))],
        out_specs=pl.BlockSpec((tile, D_MODEL), lambda i: (i, 0)),
        out_shape=jax.ShapeDtypeStruct((n_rows, D_MODEL), F32),
        compiler_params=_cp(("parallel",)),
        name="final_norm",
    )(res, g)


def kernel(x, meta_tokens, norm_mix_g, w_in, w_out, ssm_lambda_re, ssm_lambda_im, ssm_log_dt,
           ssm_b_re, ssm_b_im, ssm_c_re, ssm_c_im, ssm_d, ssm_w_glu, diff_lambda_q1,
           diff_lambda_k1, diff_lambda_q2, diff_lambda_k2, diff_subln_g, fox_forget_b,
           norm_ffn_g, dense_w1, dense_w3, dense_w2, moe_router, moe_w1, moe_w3, moe_w2,
           final_norm_g):
    bsz, seq, d = x.shape
    assert bsz == NB and d == D_MODEL and seq % ATT_TQ == 0
    depth = w_in.shape[0]
    tr = bsz * seq
    res = None
    row = lambda v: v.reshape(1, -1).astype(F32)

    for l in range(depth):
        wl = w_in[l]
        w = jnp.concatenate([wl[:, :COL_DV], wl[:, COL_FQ:COL_FV],
                             jnp.pad(wl[:, COL_GATE:], ((0, 0), (0, LANES - FOX_HEADS)))], axis=1)
        wvt = jnp.concatenate([wl[:, COL_DV:COL_FQ], wl[:, COL_FV:COL_GATE]], axis=1).T
        if l == 0:
            res, zu, za, zg, vt, vtm = _inproj(
                None, row(norm_mix_g[l]), w.astype(BF16), wvt.astype(BF16),
                embed=(x.reshape(tr, d), meta_tokens.astype(x.dtype)))
        else:
            zu, za, zg, vt, vtm = _inproj(res, row(norm_mix_g[l]), w.astype(BF16),
                                          wvt.astype(BF16))

        a, bd, cd, dskip = _s5_params(ssm_lambda_re[l], ssm_lambda_im[l], ssm_log_dt[l],
                                      ssm_b_re[l], ssm_b_im[l], ssm_c_re[l], ssm_c_im[l],
                                      ssm_d[l])
        ssm_out = _s5(zu, seq, a, bd, cd, dskip, ssm_w_glu[l].astype(BF16))

        lam_init = 0.8 - 0.6 * math.exp(-0.3 * l)
        lam = (jnp.exp(jnp.sum(diff_lambda_q1[l] * diff_lambda_k1[l]))
               - jnp.exp(jnp.sum(diff_lambda_q2[l] * diff_lambda_k2[l])) + lam_init)
        diff_out = _diff_attention(za, vt, vtm, seq, lam, lam_init, diff_subln_g[l])

        qa, ka, qa_m, ka_m = _fox_prep(zg, za, seq, fox_forget_b[l])
        fox_out = _fox_attention(za.shape[0], vt, vtm, seq, qa, ka, qa_m, ka_m)

        is_moe = l % 2 == 1
        res, *res_tiles = _outproj(res, ssm_out, diff_out, fox_out, w_out[l].astype(BF16),
                                   with_row_tiles=is_moe)

        if not is_moe:
            res = _dense_ffn(res, row(norm_ffn_g[l]), dense_w1[l // 2].astype(BF16),
                             dense_w3[l // 2].astype(BF16), dense_w2[l // 2].astype(BF16))
        else:
            res = _moe(res, res_tiles[0], row(norm_ffn_g[l]), moe_router[l // 2], moe_w1[l // 2],
                       moe_w3[l // 2], moe_w2[l // 2],
                       final_g=row(final_norm_g) if l == depth - 1 else None)

    if depth % 2 == 1:
        res = _final_norm(res, row(final_norm_g), tr)
    return res.reshape(bsz, seq, d)
```
